```python
import math
import jax
import jax.numpy as jnp
from jax import lax
import numpy as np

D_MODEL = 1024
BATCH = 4
SEQ = 4096
DEPTH = 2

HEAD_DIM = 64
NSA_HEADS = 8
NSA_KV_GROUPS = 2
SWA_HEADS = 8
SWA_KV_GROUPS = 2
FOX_HEADS = 8
MIX_WIDTH = 8 * HEAD_DIM
N_BRANCH = 3
Q_BLOCK = 128
CMP_LEN = 32
CMP_STRIDE = 16
CMP_HIDDEN = 256
SLC_LEN = 64
SLC_TOPK = 8
NSA_WINDOW = 256
SWA_WINDOW = 128
N_BUCKETS = 32
MAX_DISTANCE = 128
D_FF = 256 * ((8 * D_MODEL // 3 + 255) // 256)
CONV_WIDTH = 3
RMS_EPS = 1e-6
NEG_INF = -1e30
FORCE_SCORE = 1e9
N_QK_GAINS = 6

KV_A = NSA_KV_GROUPS * HEAD_DIM
KV_B = SWA_KV_GROUPS * HEAD_DIM
IN_SPLITS = (
    ('a_q', NSA_HEADS * HEAD_DIM), ('a_kc', KV_A), ('a_vc', KV_A), ('a_ks', KV_A), ('a_vs', KV_A),
    ('a_kw', KV_A), ('a_vw', KV_A), ('a_gate', NSA_HEADS * 3),
    ('b_q', SWA_HEADS * HEAD_DIM), ('b_k', KV_B), ('b_v', KV_B),
    ('c_q', FOX_HEADS * HEAD_DIM), ('c_k', FOX_HEADS * HEAD_DIM), ('c_v', FOX_HEADS * HEAD_DIM),
    ('c_f', FOX_HEADS),
    ('merge', N_BRANCH * D_MODEL),
)
IN_WIDTH = sum(w for _, w in IN_SPLITS)

kernel_name = 'hybrid_nsa_swa_fox_convffn'


def rms_norm(x, g):
    xf = x.astype(jnp.float32)
    return (xf * lax.rsqrt(jnp.mean(xf * xf, axis=-1, keepdims=True) + RMS_EPS)).astype(x.dtype) * g


def split_columns(h):
    cols, off = {}, 0
    for name, w in IN_SPLITS:
        cols[name] = h[..., off:off + w]
        off += w
    return cols


def to_heads(z, n_heads):
    b, t, _ = z.shape
    return z.reshape(b, t, n_heads, HEAD_DIM).transpose(0, 2, 1, 3)


def from_heads(z):
    b, h, t, d = z.shape
    return z.transpose(0, 2, 1, 3).reshape(b, t, h * d)


def t5_bucket(dist):
    max_exact = N_BUCKETS // 2
    d = jnp.maximum(dist, 0)
    log_ratio = jnp.log(jnp.maximum(d, 1).astype(jnp.float32) / max_exact) / math.log(MAX_DISTANCE / max_exact)
    large = jnp.minimum(max_exact + (log_ratio * (N_BUCKETS - max_exact)).astype(jnp.int32), N_BUCKETS - 1)
    return jnp.where(d < max_exact, d, large)


def banded_kv(z, n_prev):
    b, g, t, d = z.shape
    nb = t // Q_BLOCK
    zb = jnp.pad(z.reshape(b, g, nb, Q_BLOCK, d), ((0, 0), (0, 0), (n_prev, 0), (0, 0), (0, 0)))
    return jnp.concatenate([zb[:, :, j:j + nb] for j in range(n_prev + 1)], axis=3)


def banded_attention(q, k, v, window, tab, sinks):
    b, g, r, t, d = q.shape
    nb = t // Q_BLOCK
    n_prev = -(-(window - 1) // Q_BLOCK)
    n_keys = (n_prev + 1) * Q_BLOCK
    kb, vb = banded_kv(k, n_prev), banded_kv(v, n_prev)
    qb = q.reshape(b, g, r, nb, Q_BLOCK, d)
    qi = jnp.arange(Q_BLOCK)[:, None]
    ki = jnp.arange(n_keys)[None, :]
    dist = n_prev * Q_BLOCK + qi - ki
    s_pos = (jnp.arange(nb)[:, None, None] - n_prev) * Q_BLOCK + ki[None]
    valid = (dist >= 0) & (dist < window) & (s_pos >= 0)
    bias = jnp.transpose(tab[t5_bucket(dist)], (2, 3, 0, 1))[:, :, None]
    logits = jnp.einsum('bgrnqd,bgnkd->bgrnqk', qb, kb).astype(jnp.float32) * (HEAD_DIM ** -0.5) + bias
    logits = jnp.where(valid, logits, NEG_INF)
    if sinks is None:
        probs = jax.nn.softmax(logits, axis=-1)
    else:
        sink_col = jnp.broadcast_to(sinks.astype(jnp.float32).reshape(g, r, 1, 1, 1), logits.shape[:-1] + (1,))
        probs = jax.nn.softmax(jnp.concatenate([logits, sink_col], axis=-1), axis=-1)[..., :-1]
    out = jnp.einsum('bgrnqk,bgnkd->bgrnqd', probs.astype(v.dtype), vb)
    return out.reshape(b, g, r, t, d)


def nsa_mixer(cols, q_g, k_g, cmp_pos, cmp_w1, cmp_w2, tab_flat):
    b, t, _ = cols['a_q'].shape
    G, R = NSA_KV_GROUPS, NSA_HEADS // NSA_KV_GROUPS
    scale = HEAD_DIM ** -0.5
    tab = tab_flat.reshape(N_BUCKETS, G, R)
    q = rms_norm(to_heads(cols['a_q'], NSA_HEADS), q_g).reshape(b, G, R, t, HEAD_DIM)
    kc, vc = to_heads(cols['a_kc'], G), to_heads(cols['a_vc'], G)
    ks, vs = rms_norm(to_heads(cols['a_ks'], G), k_g), to_heads(cols['a_vs'], G)
    kw, vw = rms_norm(to_heads(cols['a_kw'], G), k_g), to_heads(cols['a_vw'], G)
    pos_t = jnp.arange(t)

    n_cmp = (t - CMP_LEN) // CMP_STRIDE + 1
    win_idx = jnp.arange(n_cmp)[:, None] * CMP_STRIDE + jnp.arange(CMP_LEN)[None, :]

    def compress(z, which):
        blk = z[:, :, win_idx] + cmp_pos[which]
        hid = jax.nn.gelu(blk.reshape(b, G, n_cmp, CMP_LEN * HEAD_DIM) @ cmp_w1[which])
        return hid @ cmp_w2[which]

    k_cmp = rms_norm(compress(kc, 0), k_g)
    v_cmp = compress(vc, 1)
    blk_end = jnp.arange(n_cmp) * CMP_STRIDE + CMP_LEN - 1
    dist_c = pos_t[:, None] - blk_end[None, :]
    valid_c = dist_c >= 0
    bias_c = jnp.transpose(tab[t5_bucket(dist_c)], (2, 3, 0, 1))
    logits_c = jnp.einsum('bgrtd,bgnd->bgrtn', q, k_cmp).astype(jnp.float32) * scale + bias_c
    logits_c = jnp.where(valid_c, logits_c, NEG_INF)
    p_cmp = jax.nn.softmax(logits_c, axis=-1) * jnp.any(valid_c, axis=-1)[:, None]
    o_cmp = jnp.einsum('bgrtn,bgnd->bgrtd', p_cmp.astype(v_cmp.dtype), v_cmp)

    n_slc = t // SLC_LEN
    n_top = min(SLC_TOPK, n_slc)
    c_start = jnp.arange(n_cmp) * CMP_STRIDE
    s_start = jnp.arange(n_slc) * SLC_LEN
    overlap = ((c_start[:, None] < s_start[None, :] + SLC_LEN)
               & (c_start[:, None] + CMP_LEN > s_start[None, :])).astype(jnp.float32)
    importance = jnp.einsum('bgtn,nj->bgtj', p_cmp.sum(axis=2), overlap)
    cur = pos_t // SLC_LEN
    jblk = jnp.arange(n_slc)[None, :]
    valid_s = jblk <= cur[:, None]
    forced = (jblk == 0) | (jblk == cur[:, None]) | (jblk == cur[:, None] - 1)
    score = jnp.where(valid_s, jnp.where(forced, FORCE_SCORE, importance), NEG_INF)
    top_score, top_idx = lax.top_k(score, n_top)
    top_ok = top_score > NEG_INF / 2

    nb = t // Q_BLOCK
    ks_blk = ks.reshape(b, G, n_slc, SLC_LEN, HEAD_DIM)
    vs_blk = vs.reshape(b, G, n_slc, SLC_LEN, HEAD_DIM)
    q_b = jnp.moveaxis(q.reshape(b, G, R, nb, Q_BLOCK, HEAD_DIM), 3, 0)
    idx_b = jnp.moveaxis(top_idx.reshape(b, G, nb, Q_BLOCK, n_top), 2, 0)
    ok_b = jnp.moveaxis(top_ok.reshape(b, G, nb, Q_BLOCK, n_top), 2, 0)
    gather = jax.vmap(jax.vmap(lambda blocks, ix: blocks[ix]))
    bias_lookup = jax.vmap(lambda tb, bk: tb[bk], in_axes=(1, 1), out_axes=1)
    n_sel = n_top * SLC_LEN

    def selected_block(args):
        i, qi, ix, ok = args
        kg = gather(ks_blk, ix).reshape(b, G, Q_BLOCK, n_sel, HEAD_DIM)
        vg = gather(vs_blk, ix).reshape(b, G, Q_BLOCK, n_sel, HEAD_DIM)
        t_pos = i * Q_BLOCK + jnp.arange(Q_BLOCK)
        s_pos = (ix[..., None] * SLC_LEN + jnp.arange(SLC_LEN)).reshape(b, G, Q_BLOCK, n_sel)
        dist = t_pos[:, None] - s_pos
        valid = (dist >= 0) & jnp.repeat(ok, SLC_LEN, axis=-1)
        bias = jnp.moveaxis(bias_lookup(tab, t5_bucket(dist)), -1, 2)
        logits = jnp.einsum('bgrqd,bgqkd->bgrqk', qi, kg).astype(jnp.float32) * scale + bias
        logits = jnp.where(valid[:, :, None], logits, NEG_INF)
        probs = jax.nn.softmax(logits, axis=-1)
        return jnp.einsum('bgrqk,bgqkd->bgrqd', probs.astype(vg.dtype), vg)

    o_slc = lax.map(selected_block, (jnp.arange(nb), q_b, idx_b, ok_b))
    o_slc = jnp.moveaxis(o_slc, 0, 3).reshape(b, G, R, t, HEAD_DIM)

    o_win = banded_attention(q, kw, vw, NSA_WINDOW, tab, None)

    gate = jax.nn.sigmoid(cols['a_gate']).reshape(b, t, NSA_HEADS, 3).transpose(0, 2, 1, 3)
    gate = gate.reshape(b, G, R, t, 3)
    o = gate[..., 0:1] * o_cmp + gate[..., 1:2] * o_slc + gate[..., 2:3] * o_win
    return from_heads(o.reshape(b, NSA_HEADS, t, HEAD_DIM))


def swa_mixer(cols, q_g, k_g, sinks, tab_flat):
    b, t, _ = cols['b_q'].shape
    G, R = SWA_KV_GROUPS, SWA_HEADS // SWA_KV_GROUPS
    q = rms_norm(to_heads(cols['b_q'], SWA_HEADS), q_g).reshape(b, G, R, t, HEAD_DIM)
    k = rms_norm(to_heads(cols['b_k'], G), k_g)
    v = to_heads(cols['b_v'], G)
    o = banded_attention(q, k, v, SWA_WINDOW, tab_flat.reshape(N_BUCKETS, G, R), sinks.reshape(G, R))
    return from_heads(o.reshape(b, SWA_HEADS, t, HEAD_DIM))


def fox_mixer(cols, q_g, k_g, forget_bias):
    b, t, _ = cols['c_q'].shape
    q = rms_norm(to_heads(cols['c_q'], FOX_HEADS), q_g)
    k = rms_norm(to_heads(cols['c_k'], FOX_HEADS), k_g)
    v = to_heads(cols['c_v'], FOX_HEADS)
    log_f = jax.nn.log_sigmoid((cols['c_f'] + forget_bias).astype(jnp.float32))
    c = jnp.cumsum(log_f, axis=1).transpose(0, 2, 1)
    nb = t // Q_BLOCK
    q_b = jnp.moveaxis(q.reshape(b, FOX_HEADS, nb, Q_BLOCK, HEAD_DIM), 2, 0)
    c_b = jnp.moveaxis(c.reshape(b, FOX_HEADS, nb, Q_BLOCK), 2, 0)
    key_pos = jnp.arange(t)

    def query_block(args):
        i, qi, ci = args
        t_pos = i * Q_BLOCK + jnp.arange(Q_BLOCK)
        logits = (jnp.einsum('bhqd,bhsd->bhqs', qi, k).astype(jnp.float32) * (HEAD_DIM ** -0.5)
                  + ci[..., None] - c[:, :, None, :])
        logits = jnp.where(t_pos[:, None] >= key_pos[None, :], logits, NEG_INF)
        probs = jax.nn.softmax(logits, axis=-1)
        return jnp.einsum('bhqs,bhsd->bhqd', probs.astype(v.dtype), v)

    o = lax.map(query_block, (jnp.arange(nb), q_b, c_b))
    o = jnp.moveaxis(o, 0, 2).reshape(b, FOX_HEADS, t, HEAD_DIM)
    return from_heads(o)


def conv_ffn(h, w_gate, w_up, conv_w, conv_b, w_down):
    a = h @ w_gate
    a = lax.conv_general_dilated(a, conv_w[:, None, :], window_strides=(1,),
                                 padding=[(CONV_WIDTH - 1, 0)],
                                 dimension_numbers=('NWC', 'WIO', 'NWC'),
                                 feature_group_count=a.shape[-1]) + conv_b
    return (jax.nn.gelu(a) * (h @ w_up)) @ w_down


def setup_inputs(seed: int = 0) -> dict:
    key = jax.random.key(seed)
    ks = jax.random.split(key, 19)
    L, D, F, d = DEPTH, D_MODEL, D_FF, HEAD_DIM
    nrm = jax.random.normal
    return {
        'x': nrm(ks[0], (BATCH, SEQ, D), jnp.float32),
        'rel_bias': 0.5 * nrm(ks[1], (N_BUCKETS, NSA_HEADS + SWA_HEADS), jnp.float32),
        'norm_mix': 1.0 + 0.05 * nrm(ks[2], (L, D), jnp.float32),
        'norm_ffn': 1.0 + 0.05 * nrm(ks[3], (L, D), jnp.float32),
        'w_in': nrm(ks[4], (L, D, IN_WIDTH), jnp.float32) * D ** -0.5,
        'forget_bias': 4.0 + 0.5 * nrm(ks[5], (L, FOX_HEADS), jnp.float32),
        'qk_gain': 1.0 + 0.05 * nrm(ks[6], (L, N_QK_GAINS, d), jnp.float32),
        'cmp_pos': 0.1 * nrm(ks[7], (L, 2, CMP_LEN, d), jnp.float32),
        'cmp_w1': nrm(ks[8], (L, 2, CMP_LEN * d, CMP_HIDDEN), jnp.float32) * (CMP_LEN * d) ** -0.5,
        'cmp_w2': nrm(ks[9], (L, 2, CMP_HIDDEN, d), jnp.float32) * CMP_HIDDEN ** -0.5,
        'sinks': nrm(ks[10], (L, SWA_HEADS), jnp.float32),
        'w_branch': nrm(ks[11], (L, N_BRANCH, MIX_WIDTH, D), jnp.float32) * MIX_WIDTH ** -0.5,
        'w_out': nrm(ks[12], (L, D, D), jnp.float32) * D ** -0.5,
        'w_gate': nrm(ks[13], (L, D, F), jnp.float32) * D ** -0.5,
        'w_up': nrm(ks[14], (L, D, F), jnp.float32) * D ** -0.5,
        'conv_w': nrm(ks[15], (L, CONV_WIDTH, F), jnp.float32) * CONV_WIDTH ** -0.5,
        'conv_b': 0.02 * nrm(ks[16], (L, F), jnp.float32),
        'w_down': nrm(ks[17], (L, F, D), jnp.float32) * F ** -0.5,
    }


def reference(x, rel_bias, norm_mix, norm_ffn, w_in, forget_bias, qk_gain, cmp_pos, cmp_w1, cmp_w2,
              sinks, w_branch, w_out, w_gate, w_up, conv_w, conv_b, w_down):
    b, t, _ = x.shape
    for l in range(DEPTH):
        h = rms_norm(x, norm_mix[l])
        cols = split_columns(h @ w_in[l])
        o_a = nsa_mixer(cols, qk_gain[l, 0], qk_gain[l, 1], cmp_pos[l], cmp_w1[l], cmp_w2[l],
                        rel_bias[:, :NSA_HEADS])
        o_b = swa_mixer(cols, qk_gain[l, 2], qk_gain[l, 3], sinks[l], rel_bias[:, NSA_HEADS:])
        o_c = fox_mixer(cols, qk_gain[l, 4], qk_gain[l, 5], forget_bias[l])
        branches = jnp.stack([o_a, o_b, o_c], axis=2)
        proj = jnp.einsum('btnm,nmd->btnd', branches, w_branch[l])
        gates = jax.nn.sigmoid(cols['merge']).reshape(b, t, N_BRANCH, D_MODEL)
        x = x + (gates * proj).sum(axis=2) @ w_out[l]
        x = x + conv_ffn(rms_norm(x, norm_ffn[l]), w_gate[l], w_up[l], conv_w[l], conv_b[l], w_down[l])
    return x
```

```python
import functools
import math

import numpy as np
import jax
import jax.numpy as jnp
from jax import lax
from jax.experimental import pallas as pl
from jax.experimental.pallas import tpu as pltpu

F32 = jnp.float32
BF16 = jnp.bfloat16

HEAD_DIM = 64
LANES = 128
N_HEADS = 8
KV_GROUPS = 2
HEADS_PER_GROUP = N_HEADS // KV_GROUPS
N_QBLK = N_HEADS * HEAD_DIM // LANES
MIX_WIDTH = N_HEADS * HEAD_DIM
N_BRANCH = 3
Q_BLOCK = 128
CMP_LEN = 32
CMP_STRIDE = 16
CMP_HIDDEN = 256
SLC_LEN = 64
SLC_TOPK = 8
NSA_WINDOW = 256
SWA_WINDOW = 128
N_BUCKETS = 32
MAX_DISTANCE = 128
CONV_WIDTH = 3
RMS_EPS = 1e-6
NEG = -1e30
FORCE_SCORE = 1e9
SLC_TILE = 256
VMEM_LIMIT = 48 * 1024 * 1024

CB_AQ, CB_BQ, CB_CQ, CB_CK, CB_CV = 0, 4, 8, 12, 16
CB_AKC, CB_AVC, CB_AKS, CB_AVS, CB_AKW, CB_AVW, CB_BK, CB_BV = 20, 21, 22, 23, 24, 25, 26, 27
N_CB = 28
N_GATE_COLS = N_HEADS * N_BRANCH


def _cparams(*sem):
    return pltpu.CompilerParams(dimension_semantics=sem, vmem_limit_bytes=VMEM_LIMIT)


def _split_hi_lo(v):
    hi = v.astype(BF16)
    return hi, (v - hi.astype(F32)).astype(BF16)


def _dot(a, b):
    return jnp.dot(a, b, preferred_element_type=F32)


def _dot_nt(a, b):
    return lax.dot_general(a, b, (((1,), (1,)), ((), ())), preferred_element_type=F32)


def _dot2(v, m):
    hi, lo = _split_hi_lo(v)
    return _dot(hi, m) + _dot(lo, m)


def _gelu_tanh(x):
    return x * (0.5 * (1.0 + jnp.tanh(math.sqrt(2.0 / math.pi) * (x + 0.044715 * (x * x * x)))))


def _head_rms(y, bd, gain):
    ms = _dot2(y * y, bd) * (1.0 / HEAD_DIM)
    return (y * lax.rsqrt(ms + RMS_EPS)) * gain


def _lane_half_mask(g):
    return (lax.broadcasted_iota(jnp.int32, (1, LANES), 1) // HEAD_DIM) == g


def _stack_group_queries(q_ref, g):
    mask = _lane_half_mask(g)
    return jnp.concatenate([jnp.where(mask, q_ref[p, 0], jnp.zeros_like(q_ref[p, 0]))
                            for p in range(N_QBLK)], axis=0)


def _merge_group_outputs(o_ref, o_g0, o_g1):
    low = _lane_half_mask(0)
    for p in range(N_QBLK):
        rows = slice(p * Q_BLOCK, (p + 1) * Q_BLOCK)
        o_ref[p, 0] = jnp.where(low, o_g0[rows], o_g1[rows]).astype(o_ref.dtype)


def _rmsnorm_kernel(x_ref, g_ref, o_ref):
    x = x_ref[...]
    ms = jnp.mean(x * x, axis=-1, keepdims=True)
    o_ref[...] = ((x * lax.rsqrt(ms + RMS_EPS)) * g_ref[...]).astype(o_ref.dtype)


def _rmsnorm(x2, g, tm=512):
    bt, d = x2.shape
    return pl.pallas_call(
        _rmsnorm_kernel,
        grid=(bt // tm,),
        in_specs=[pl.BlockSpec((tm, d), lambda i: (i, 0)), pl.BlockSpec((1, d), lambda i: (0, 0))],
        out_specs=pl.BlockSpec((tm, d), lambda i: (i, 0)),
        out_shape=jax.ShapeDtypeStruct((bt, d), BF16),
        compiler_params=_cparams("parallel"),
        name="rmsnorm",
    )(x2, g.reshape(1, d))


def _inproj_qkv_kernel(h_ref, w_ref, aux_ref, bd_ref, o_ref, *, n_sub):
    y = _dot(h_ref[...], w_ref[...])
    for c in range(n_sub):
        cols = slice(c * LANES, (c + 1) * LANES)
        yb = y[:, cols]
        normed = _head_rms(yb, bd_ref[...], aux_ref[0:1, cols])
        o_ref[c] = jnp.where(aux_ref[1:2, cols] > 0.5, normed, yb).astype(o_ref.dtype)


def _inproj_qkv(h, w, aux, bd, tm=1024, tn=512):
    bt, d = h.shape
    n = w.shape[1]
    n_sub = tn // LANES
    return pl.pallas_call(
        functools.partial(_inproj_qkv_kernel, n_sub=n_sub),
        grid=(bt // tm, n // tn),
        in_specs=[pl.BlockSpec((tm, d), lambda i, j: (i, 0)),
                  pl.BlockSpec((d, tn), lambda i, j: (0, j)),
                  pl.BlockSpec((8, tn), lambda i, j: (0, j)),
                  pl.BlockSpec((LANES, LANES), lambda i, j: (0, 0))],
        out_specs=pl.BlockSpec((n_sub, tm, LANES), lambda i, j: (j, i, 0)),
        out_shape=jax.ShapeDtypeStruct((n // LANES, bt, LANES), BF16),
        compiler_params=_cparams("parallel", "arbitrary"),
        name="inproj_qkv",
    )(h, w, aux, bd)


def _inproj_gates_kernel(h_ref, w_ref, eg_ref, raw_ref, ga_ref):
    y = _dot(h_ref[...], w_ref[...])
    raw_ref[...] = y
    ga = _dot2(jax.nn.sigmoid(y), eg_ref[...])
    for n in range(N_BRANCH):
        ga_ref[n] = ga[:, n * MIX_WIDTH:(n + 1) * MIX_WIDTH].astype(ga_ref.dtype)


def _inproj_gates(h, w, eg, tm=1024):
    bt, d = h.shape
    return pl.pallas_call(
        _inproj_gates_kernel,
        grid=(bt // tm,),
        in_specs=[pl.BlockSpec((tm, d), lambda i: (i, 0)),
                  pl.BlockSpec((d, LANES), lambda i: (0, 0)),
                  pl.BlockSpec((LANES, N_BRANCH * MIX_WIDTH), lambda i: (0, 0))],
        out_specs=[pl.BlockSpec((tm, LANES), lambda i: (i, 0)),
                   pl.BlockSpec((N_BRANCH, tm, MIX_WIDTH), lambda i: (0, i, 0))],
        out_shape=[jax.ShapeDtypeStruct((bt, LANES), F32),
                   jax.ShapeDtypeStruct((N_BRANCH, bt, MIX_WIDTH), BF16)],
        compiler_params=_cparams("parallel"),
        name="inproj_gates",
    )(h, w, eg)


def _inproj_merge_kernel(h_ref, w_ref, o_ref):
    o_ref[0] = jax.nn.sigmoid(_dot(h_ref[...], w_ref[...])).astype(o_ref.dtype)


def _inproj_merge(h, w, d_model, tm=1024, tn=512):
    bt, d = h.shape
    n = w.shape[1]
    per = d_model // tn
    return pl.pallas_call(
        _inproj_merge_kernel,
        grid=(bt // tm, n // tn),
        in_specs=[pl.BlockSpec((tm, d), lambda i, j: (i, 0)),
                  pl.BlockSpec((d, tn), lambda i, j: (0, j))],
        out_specs=pl.BlockSpec((1, tm, tn), lambda i, j: (j // per, i, j % per)),
        out_shape=jax.ShapeDtypeStruct((n // d_model, bt, d_model), BF16),
        compiler_params=_cparams("parallel", "arbitrary"),
        name="inproj_merge",
    )(h, w)


def _compress_kernel(c_ref, pos_ref, w1_ref, w2_ref, aux_ref, bd_ref, o_ref):
    n_chunks = c_ref.shape[3]
    y = jnp.zeros((n_chunks, LANES), F32)
    for g in range(KV_GROUPS):
        c = c_ref[0, 0, g].astype(F32)
        a = _dot((c + pos_ref[0, 0]).astype(BF16), w1_ref[0, 0])
        b = _dot((c + pos_ref[0, 1]).astype(BF16), w1_ref[0, 1])
        hid = _gelu_tanh(a + pltpu.roll(b, n_chunks - 1, 0))
        y = y + _dot(hid.astype(BF16), w2_ref[0, g])
    normed = _head_rms(y, bd_ref[...], aux_ref[0, 0:1, :])
    o_ref[0, 0] = jnp.where(aux_ref[0, 1:2, :] > 0.5, normed, y).astype(o_ref.dtype)


def _compress(chunks, pos, w1, w2p, aux, bd):
    _, b, g, n_chunks, cw = chunks.shape
    return pl.pallas_call(
        _compress_kernel,
        grid=(2, b),
        in_specs=[pl.BlockSpec((1, 1, g, n_chunks, cw), lambda w, i: (w, i, 0, 0, 0)),
                  pl.BlockSpec((1, 2, 1, cw), lambda w, i: (w, 0, 0, 0)),
                  pl.BlockSpec((1, 2, cw, CMP_HIDDEN), lambda w, i: (w, 0, 0, 0)),
                  pl.BlockSpec((1, g, CMP_HIDDEN, LANES), lambda w, i: (w, 0, 0, 0)),
                  pl.BlockSpec((1, 8, LANES), lambda w, i: (w, 0, 0)),
                  pl.BlockSpec((LANES, LANES), lambda w, i: (0, 0))],
        out_specs=pl.BlockSpec((1, 1, n_chunks, LANES), lambda w, i: (w, i, 0, 0)),
        out_shape=jax.ShapeDtypeStruct((2, b, n_chunks, LANES), BF16),
        compiler_params=_cparams("parallel", "parallel"),
        name="nsa_compress",
    )(chunks, pos, w1, w2p, aux, bd)


def _cmp_select_kernel(q_ref, kc_ref, vc_ref, bias_ref, ovt_ref, o_ref, sel_ref, *, n_slc, n_top):
    i = pl.program_id(1)
    ncp = kc_ref.shape[2]
    row_t = i * Q_BLOCK + lax.broadcasted_iota(jnp.int32, (Q_BLOCK, 1), 0)
    row_valid = jnp.concatenate([row_t >= CMP_LEN - 1] * HEADS_PER_GROUP, axis=0)
    jrow = lax.broadcasted_iota(jnp.int32, (n_slc, Q_BLOCK), 0)
    cur = (i * Q_BLOCK + lax.broadcasted_iota(jnp.int32, (n_slc, Q_BLOCK), 1)) // SLC_LEN
    block_valid = jrow <= cur

    def force(imp):
        forced = jnp.where(jrow == cur, FORCE_SCORE, jnp.where(jrow == cur - 1, FORCE_SCORE, imp))
        return jnp.where(jrow == 0, FORCE_SCORE, forced)
    outs = []
    for g in range(KV_GROUPS):
        qop = _stack_group_queries(q_ref, g)
        rows = HEADS_PER_GROUP * Q_BLOCK
        s = _dot_nt(qop, kc_ref[0, 0]) + bias_ref[g * HEADS_PER_GROUP:(g + 1) * HEADS_PER_GROUP].reshape(rows, ncp)
        e = jnp.exp(s - jnp.max(s, axis=-1, keepdims=True))
        inv = jnp.where(row_valid, 1.0 / jnp.sum(e, axis=-1, keepdims=True), 0.0)
        p = e * inv
        outs.append(_dot(p.astype(BF16), vc_ref[0, 0]))
        psum = p[0:Q_BLOCK]
        for r in range(1, HEADS_PER_GROUP):
            psum = psum + p[r * Q_BLOCK:(r + 1) * Q_BLOCK]
        hi, lo = _split_hi_lo(psum)
        imp_t = (_dot_nt(ovt_ref[...], hi) + _dot_nt(ovt_ref[...], lo))[0:n_slc]
        score = jnp.where(block_valid, force(imp_t), NEG)
        groups = [score[a:a + 8] for a in range(0, n_slc, 8)]
        ranks = [jnp.zeros((8, Q_BLOCK), F32) for _ in groups]
        sub = lax.broadcasted_iota(jnp.int32, (8, Q_BLOCK), 0)
        for j in range(n_slc):
            other = score[j:j + 1, :]
            for a, blk in enumerate(groups):
                ge = jnp.where(other >= blk, 1.0, 0.0)
                if 8 * a > j:
                    ranks[a] = ranks[a] + ge
                elif 8 * a + 7 <= j:
                    ranks[a] = ranks[a] + jnp.where(other > blk, 1.0, 0.0)
                else:
                    ranks[a] = ranks[a] + jnp.where(sub > j - 8 * a, ge, jnp.where(other > blk, 1.0, 0.0))
        rank = jnp.concatenate(ranks, axis=0)
        sel_bias = jnp.where(rank < n_top, jnp.where(score > NEG / 2, 0.0, NEG), NEG)
        if n_slc < LANES:
            sel_bias = jnp.concatenate([sel_bias, jnp.full((LANES - n_slc, Q_BLOCK), NEG, F32)], axis=0)
        sel_ref[0, g] = sel_bias.T.astype(sel_ref.dtype)
    _merge_group_outputs(o_ref, outs[0], outs[1])


def _cmp_select(qkv, cmp_kv, bias_c, ovt, n_slc):
    _, b, t, _ = qkv.shape
    nt = t // Q_BLOCK
    ncp = cmp_kv.shape[2]
    return pl.pallas_call(
        functools.partial(_cmp_select_kernel, n_slc=n_slc, n_top=min(SLC_TOPK, n_slc)),
        grid=(b, nt),
        in_specs=[pl.BlockSpec((N_QBLK, 1, Q_BLOCK, LANES), lambda bi, i: (CB_AQ // N_QBLK, bi, i, 0)),
                  pl.BlockSpec((1, 1, ncp, LANES), lambda bi, i: (0, bi, 0, 0)),
                  pl.BlockSpec((1, 1, ncp, LANES), lambda bi, i: (1, bi, 0, 0)),
                  pl.BlockSpec((N_HEADS, Q_BLOCK, ncp), lambda bi, i: (0, i, 0)),
                  pl.BlockSpec((LANES, ncp), lambda bi, i: (0, 0))],
        out_specs=[pl.BlockSpec((N_QBLK, 1, Q_BLOCK, LANES), lambda bi, i: (0, bi, i, 0)),
                   pl.BlockSpec((1, KV_GROUPS, Q_BLOCK, LANES), lambda bi, i: (bi, 0, i, 0))],
        out_shape=[jax.ShapeDtypeStruct((N_QBLK, b, t, LANES), BF16),
                   jax.ShapeDtypeStruct((b, KV_GROUPS, t, LANES), BF16)],
        compiler_params=_cparams("parallel", "parallel"),
        name="nsa_cmp_select",
    )(qkv, cmp_kv, cmp_kv, bias_c, ovt)


def _slc_kernel(q_ref, k_ref, v_ref, sel_ref, e_ref, nb_ref, o_ref):
    i = pl.program_id(1)
    par = i % 2
    diag = i // 2
    rows = HEADS_PER_GROUP * Q_BLOCK
    outs = []
    for g in range(KV_GROUPS):
        qop = _stack_group_queries(q_ref, g)
        sel = sel_ref[0, g]

        def step(j, carry, bias=None, pen=None):
            m, l, acc = carry
            start = pl.multiple_of(j * SLC_TILE, SLC_TILE)
            k = k_ref[0, 0, pl.ds(start, SLC_TILE), :]
            v = v_ref[0, 0, pl.ds(start, SLC_TILE), :]
            s = _dot_nt(qop, k).reshape(HEADS_PER_GROUP, Q_BLOCK, SLC_TILE)
            blk_bias = _dot(sel, e_ref[j])
            if pen is not None:
                blk_bias = blk_bias + pen
            s = s + blk_bias[None]
            if bias is not None:
                s = s + bias
            s = s.reshape(rows, SLC_TILE)
            m_new = jnp.maximum(m, jnp.max(s, axis=-1, keepdims=True))
            alpha = jnp.exp(m - m_new)
            p = jnp.exp(s - m_new)
            l = alpha * l + jnp.sum(p, axis=-1, keepdims=True)
            acc = alpha * acc + _dot(p.astype(BF16), v)
            return m_new, l, acc

        carry = (jnp.full((rows, 1), NEG, F32), jnp.zeros((rows, 1), F32), jnp.zeros((rows, LANES), F32))
        carry = lax.fori_loop(0, jnp.maximum(diag - 1, 0), step, carry)
        hs = slice(g * HEADS_PER_GROUP, (g + 1) * HEADS_PER_GROUP)
        pen = jnp.where(diag >= 1, 0.0, NEG)
        carry = step(jnp.maximum(diag - 1, 0), carry, bias=nb_ref[par, 0, hs], pen=pen)
        m, l, acc = step(diag, carry, bias=nb_ref[par, 1, hs])
        outs.append(acc / l)
    _merge_group_outputs(o_ref, outs[0], outs[1])


def _slc_attention(qkv, sel, e3, near_bias):
    _, b, t, _ = qkv.shape
    nt = t // Q_BLOCK
    return pl.pallas_call(
        _slc_kernel,
        grid=(b, nt),
        in_specs=[pl.BlockSpec((N_QBLK, 1, Q_BLOCK, LANES), lambda bi, i: (CB_AQ // N_QBLK, bi, i, 0)),
                  pl.BlockSpec((1, 1, t, LANES), lambda bi, i: (CB_AKS, bi, 0, 0)),
                  pl.BlockSpec((1, 1, t, LANES), lambda bi, i: (CB_AVS, bi, 0, 0)),
                  pl.BlockSpec((1, KV_GROUPS, Q_BLOCK, LANES), lambda bi, i: (bi, 0, i, 0)),
                  pl.BlockSpec(e3.shape, lambda bi, i: (0, 0, 0)),
                  pl.BlockSpec(near_bias.shape, lambda bi, i: (0, 0, 0, 0, 0))],
        out_specs=pl.BlockSpec((N_QBLK, 1, Q_BLOCK, LANES), lambda bi, i: (0, bi, i, 0)),
        out_shape=jax.ShapeDtypeStruct((N_QBLK, b, t, LANES), BF16),
        compiler_params=_cparams("parallel", "parallel"),
        name="nsa_slc_attention",
    )(qkv, qkv, qkv, sel, e3, near_bias)


def _banded_kernel(*refs, n_blk, use_sinks):
    q_ref = refs[0]
    k_refs = refs[1:1 + n_blk]
    v_refs = refs[1 + n_blk:1 + 2 * n_blk]
    bias_ref, sink_ref, o_ref = refs[1 + 2 * n_blk:]
    i = pl.program_id(1)
    n_keys = n_blk * Q_BLOCK
    rows = HEADS_PER_GROUP * Q_BLOCK
    key_pos = (i - (n_blk - 1)) * Q_BLOCK + lax.broadcasted_iota(jnp.int32, (1, n_keys), 1)
    pad_pen = jnp.where(key_pos >= 0, 0.0, NEG)
    kcat = jnp.concatenate([r[0, 0] for r in k_refs], axis=0)
    vcat = jnp.concatenate([r[0, 0] for r in v_refs], axis=0)
    outs = []
    for g in range(KV_GROUPS):
        qop = _stack_group_queries(q_ref, g)
        hs = slice(g * HEADS_PER_GROUP, (g + 1) * HEADS_PER_GROUP)
        s = _dot_nt(qop, kcat) + bias_ref[hs].reshape(rows, n_keys) + pad_pen
        m = jnp.max(s, axis=-1, keepdims=True)
        if use_sinks:
            sink = jnp.concatenate([jnp.broadcast_to(sink_ref[h:h + 1, 0:1], (Q_BLOCK, 1))
                                    for h in range(hs.start, hs.stop)], axis=0)
            m = jnp.maximum(m, sink)
        e = jnp.exp(s - m)
        l = jnp.sum(e, axis=-1, keepdims=True)
        if use_sinks:
            l = l + jnp.exp(sink - m)
        outs.append(_dot(e.astype(BF16), vcat) / l)
    _merge_group_outputs(o_ref, outs[0], outs[1])


def _banded_attention(qkv, cb_q, cb_k, cb_v, bias, sinks, use_sinks):
    _, b, t, _ = qkv.shape
    nt = t // Q_BLOCK
    n_blk = bias.shape[2] // Q_BLOCK

    def kv_spec(cb, back):
        return pl.BlockSpec((1, 1, Q_BLOCK, LANES), lambda bi, i: (cb, bi, jnp.maximum(i - back, 0), 0))

    backs = [n_blk - 1 - jb for jb in range(n_blk)]
    return pl.pallas_call(
        functools.partial(_banded_kernel, n_blk=n_blk, use_sinks=use_sinks),
        grid=(b, nt),
        in_specs=([pl.BlockSpec((N_QBLK, 1, Q_BLOCK, LANES), lambda bi, i: (cb_q // N_QBLK, bi, i, 0))]
                  + [kv_spec(cb_k, back) for back in backs] + [kv_spec(cb_v, back) for back in backs]
                  + [pl.BlockSpec(bias.shape, lambda bi, i: (0, 0, 0)),
                     pl.BlockSpec(sinks.shape, lambda bi, i: (0, 0))]),
        out_specs=pl.BlockSpec((N_QBLK, 1, Q_BLOCK, LANES), lambda bi, i: (0, bi, i, 0)),
        out_shape=jax.ShapeDtypeStruct((N_QBLK, b, t, LANES), BF16),
        compiler_params=_cparams("parallel", "parallel"),
        name="banded_attention",
    )(qkv, *([qkv] * (2 * n_blk)), bias, sinks)


def _fox_cumsum_kernel(f_ref, fb_ref, o_ref, *, chunk):
    x = f_ref[0] + fb_ref[...]
    logf = jnp.minimum(x, 0.0) - jnp.log(1.0 + jnp.exp(-jnp.abs(x)))
    t = x.shape[1]
    upper = jnp.where(lax.broadcasted_iota(jnp.int32, (chunk, chunk), 0)
                      <= lax.broadcasted_iota(jnp.int32, (chunk, chunk), 1), 1.0, 0.0).astype(BF16)
    carry = jnp.zeros((x.shape[0], 1), F32)
    for c in range(t // chunk):
        blk = logf[:, c * chunk:(c + 1) * chunk]
        hi, rest = blk.astype(BF16), blk - blk.astype(BF16).astype(F32)
        mid, lo = _split_hi_lo(rest)
        cs = (_dot(hi, upper) + _dot(mid, upper) + _dot(lo, upper)) + carry
        o_ref[0, :, c * chunk:(c + 1) * chunk] = cs
        carry = cs[:, chunk - 1:chunk]


def _fox_cumsum(f_t, fb, chunk=512):
    b, h, t = f_t.shape
    chunk = min(chunk, t)
    return pl.pallas_call(
        functools.partial(_fox_cumsum_kernel, chunk=chunk),
        grid=(b,),
        in_specs=[pl.BlockSpec((1, h, t), lambda i: (i, 0, 0)), pl.BlockSpec((h, 1), lambda i: (0, 0))],
        out_specs=pl.BlockSpec((1, h, t), lambda i: (i, 0, 0)),
        out_shape=jax.ShapeDtypeStruct((b, h, t), F32),
        compiler_params=_cparams("parallel"),
        name="fox_cumsum",
    )(f_t, fb)


def _fox_kernel(q_ref, k_ref, v_ref, cq_ref, ck_ref, o_ref, *, tq):
    i = pl.program_id(2)
    q = q_ref[0, 0]
    causal_pen = jnp.where(lax.broadcasted_iota(jnp.int32, (tq, tq), 0)
                           >= lax.broadcasted_iota(jnp.int32, (tq, tq), 1), 0.0, NEG)
    outs = []
    for hh in range(2):
        qop = jnp.where(_lane_half_mask(hh), q, jnp.zeros_like(q))
        cq = cq_ref[0, 0][:, hh:hh + 1]

        def step(j, carry, pen=None):
            m, l, acc = carry
            start = pl.multiple_of(j * tq, tq)
            k = k_ref[0, 0, pl.ds(start, tq), :]
            v = v_ref[0, 0, pl.ds(start, tq), :]
            s = _dot_nt(qop, k) + cq - ck_ref[0, 0, hh, pl.ds(j, 1), :]
            if pen is not None:
                s = s + pen
            m_new = jnp.maximum(m, jnp.max(s, axis=-1, keepdims=True))
            alpha = jnp.exp(m - m_new)
            p = jnp.exp(s - m_new)
            l = alpha * l + jnp.sum(p, axis=-1, keepdims=True)
            acc = alpha * acc + _dot(p.astype(BF16), v)
            return m_new, l, acc

        carry = (jnp.full((tq, 1), NEG, F32), jnp.zeros((tq, 1), F32), jnp.zeros((tq, LANES), F32))
        carry = lax.fori_loop(0, i, step, carry)
        m, l, acc = step(i, carry, pen=causal_pen)
        outs.append(acc / l)
    o_ref[0, 0] = jnp.where(_lane_half_mask(0), outs[0], outs[1]).astype(o_ref.dtype)


def _fox_attention(qkv, c_col, c_row, tq):
    _, b, t, _ = qkv.shape
    nq = t // tq
    pairs = N_QBLK
    return pl.pallas_call(
        functools.partial(_fox_kernel, tq=tq),
        grid=(b, pairs, nq),
        in_specs=[pl.BlockSpec((1, 1, tq, LANES), lambda bi, p, i: (CB_CQ + p, bi, i, 0)),
                  pl.BlockSpec((1, 1, t, LANES), lambda bi, p, i: (CB_CK + p, bi, 0, 0)),
                  pl.BlockSpec((1, 1, t, LANES), lambda bi, p, i: (CB_CV + p, bi, 0, 0)),
                  pl.BlockSpec((1, 1, tq, 2), lambda bi, p, i: (bi, p, i, 0)),
                  pl.BlockSpec((1, 1, 2, nq, tq), lambda bi, p, i: (bi, p, 0, 0, 0))],
        out_specs=pl.BlockSpec((1, 1, tq, LANES), lambda bi, p, i: (p, bi, i, 0)),
        out_shape=jax.ShapeDtypeStruct((pairs, b, t, LANES), BF16),
        compiler_params=_cparams("parallel", "parallel", "parallel"),
        name="fox_attention",
    )(qkv, qkv, qkv, c_col, c_row)


def _merge_kernel(ocmp_ref, oslc_ref, owin_ref, ob_ref, oc_ref, ga_ref, mg_ref, wb_ref, wo_ref, x_ref, o_ref):
    def cat(ref):
        return jnp.concatenate([ref[p] for p in range(N_QBLK)], axis=-1)

    oa = (ga_ref[0].astype(F32) * cat(ocmp_ref).astype(F32)
          + ga_ref[1].astype(F32) * cat(oslc_ref).astype(F32)
          + ga_ref[2].astype(F32) * cat(owin_ref).astype(F32)).astype(BF16)
    mix = (mg_ref[0].astype(F32) * _dot(oa, wb_ref[0])
           + mg_ref[1].astype(F32) * _dot(cat(ob_ref), wb_ref[1])
           + mg_ref[2].astype(F32) * _dot(cat(oc_ref), wb_ref[2]))
    o_ref[...] = x_ref[...] + _dot(mix.astype(BF16), wo_ref[...])


def _merge(ocmp, oslc, owin, ob, oc, ga, mg, wb, wo, x2, tm=512):
    bt, d = x2.shape
    o_spec = pl.BlockSpec((N_QBLK, tm, LANES), lambda i: (0, i, 0))
    return pl.pallas_call(
        _merge_kernel,
        grid=(bt // tm,),
        in_specs=[o_spec] * 5 + [pl.BlockSpec((N_BRANCH, tm, MIX_WIDTH), lambda i: (0, i, 0)),
                                 pl.BlockSpec((N_BRANCH, tm, d), lambda i: (0, i, 0)),
                                 pl.BlockSpec(wb.shape, lambda i: (0, 0, 0)),
                                 pl.BlockSpec(wo.shape, lambda i: (0, 0)),
                                 pl.BlockSpec((tm, d), lambda i: (i, 0))],
        out_specs=pl.BlockSpec((tm, d), lambda i: (i, 0)),
        out_shape=jax.ShapeDtypeStruct((bt, d), F32),
        compiler_params=_cparams("parallel"),
        name="branch_merge",
    )(ocmp, oslc, owin, ob, oc, ga, mg, wb, wo, x2)


def _ffn_kernel(x_ref, xh_ref, g_ref, wg_ref, wu_ref, cw_ref, cb_ref, wd_ref, o_ref,
                h_scr, hh_scr, a_scr, acc_scr, *, tm, tiles_per_seq):
    i = pl.program_id(0)
    f = pl.program_id(1)

    def norm(x):
        ms = jnp.mean(x * x, axis=-1, keepdims=True)
        return ((x * lax.rsqrt(ms + RMS_EPS)) * g_ref[...]).astype(BF16)

    @pl.when(f == 0)
    def _():
        h_scr[...] = norm(x_ref[...])
        hh_scr[...] = norm(xh_ref[...])
        acc_scr[...] = jnp.zeros_like(acc_scr)

    a = _dot(h_scr[...], wg_ref[...])
    halo = _dot(hh_scr[...], wg_ref[...]) * jnp.where(i % tiles_per_seq == 0, 0.0, 1.0)
    a_scr[0:8, :] = halo
    a_scr[8:8 + tm, :] = a
    conv = (cw_ref[0:1, :] * a_scr[6:6 + tm, :] + cw_ref[1:2, :] * a_scr[7:7 + tm, :]
            + cw_ref[2:3, :] * a + cb_ref[...])
    act = _gelu_tanh(conv) * _dot(h_scr[...], wu_ref[...])
    acc_scr[...] += _dot(act.astype(BF16), wd_ref[...])

    @pl.when(f == pl.num_programs(1) - 1)
    def _():
        o_ref[...] = x_ref[...] + acc_scr[...]


def _ffn(x2, g, wg, wu, cw, cb, wd, t, tm=1024, tf=256):
    bt, d = x2.shape
    ff = wg.shape[1]
    halo_blocks = tm // 8
    return pl.pallas_call(
        functools.partial(_ffn_kernel, tm=tm, tiles_per_seq=t // tm),
        grid=(bt // tm, ff // tf),
        in_specs=[pl.BlockSpec((tm, d), lambda i, f: (i, 0)),
                  pl.BlockSpec((8, d), lambda i, f: (jnp.maximum(i * halo_blocks - 1, 0), 0)),
                  pl.BlockSpec((1, d), lambda i, f: (0, 0)),
                  pl.BlockSpec((d, tf), lambda i, f: (0, f)),
                  pl.BlockSpec((d, tf), lambda i, f: (0, f)),
                  pl.BlockSpec((CONV_WIDTH, tf), lambda i, f: (0, f)),
                  pl.BlockSpec((1, tf), lambda i, f: (0, f)),
                  pl.BlockSpec((tf, d), lambda i, f: (f, 0))],
        out_specs=pl.BlockSpec((tm, d), lambda i, f: (i, 0)),
        out_shape=jax.ShapeDtypeStruct((bt, d), F32),
        scratch_shapes=[pltpu.VMEM((tm, d), BF16), pltpu.VMEM((8, d), BF16),
                        pltpu.VMEM((tm + 8, tf), F32), pltpu.VMEM((tm, d), F32)],
        compiler_params=_cparams("parallel", "arbitrary"),
        name="conv_ffn",
    )(x2, x2, g.reshape(1, d), wg, wu, cw, cb.reshape(1, ff), wd)


def _t5_bucket_np(dist):
    max_exact = N_BUCKETS // 2
    d = np.maximum(dist, 0)
    ratio = np.log(np.maximum(d, 1) / max_exact) / math.log(MAX_DISTANCE / max_exact)
    large = np.minimum(max_exact + (ratio * (N_BUCKETS - max_exact)).astype(np.int64), N_BUCKETS - 1)
    return np.where(d < max_exact, d, large)


def _bias_from_dist(tab_t, dist, valid, shift_far=False):
    vals = jnp.take(tab_t, jnp.asarray(_t5_bucket_np(dist).reshape(-1), jnp.int32), axis=1)
    vals = vals.reshape((tab_t.shape[0],) + dist.shape)
    if shift_far:
        vals = vals - tab_t[:, N_BUCKETS - 1].reshape((-1,) + (1,) * dist.ndim)
    return jnp.where(jnp.asarray(valid)[None], vals, NEG)


def _static_tables(t):
    ncp = t // CMP_STRIDE
    n_cmp = (t - CMP_LEN) // CMP_STRIDE + 1
    n_slc = t // SLC_LEN
    lane = np.arange(LANES)
    bd = (lane[:, None] // HEAD_DIM == lane[None, :] // HEAD_DIM).astype(np.float32)
    eg = np.zeros((LANES, N_BRANCH * MIX_WIDTH), np.float32)
    for h in range(N_HEADS):
        base = (h % N_QBLK) * LANES + (h // N_QBLK) * HEAD_DIM
        for n in range(N_BRANCH):
            eg[h * N_BRANCH + n, n * MIX_WIDTH + base:n * MIX_WIDTH + base + HEAD_DIM] = 1.0
    c_start = np.arange(n_cmp) * CMP_STRIDE
    s_start = np.arange(n_slc) * SLC_LEN
    overlap = ((c_start[:, None] < s_start[None, :] + SLC_LEN) & (c_start[:, None] + CMP_LEN > s_start[None, :]))
    ovt = np.zeros((LANES, ncp), np.float32)
    ovt[:n_slc, :n_cmp] = overlap.T
    e3 = np.zeros((t // SLC_TILE, LANES, SLC_TILE), np.float32)
    for j in range(t // SLC_TILE):
        key_blk = (j * SLC_TILE + np.arange(SLC_TILE)) // SLC_LEN
        e3[j, key_blk, np.arange(SLC_TILE)] = 1.0
    as_bf16 = lambda a: jnp.asarray(a, BF16)
    return dict(bd=as_bf16(bd), eg=as_bf16(eg), ovt=as_bf16(ovt), e3=as_bf16(e3), n_slc=n_slc)


def _bias_tables(rel_bias, t):
    n_cmp = (t - CMP_LEN) // CMP_STRIDE + 1
    ncp = t // CMP_STRIDE
    tab_a = rel_bias[:, :N_HEADS].T
    tab_b = rel_bias[:, N_HEADS:].T
    r = np.arange(Q_BLOCK)[:, None]

    def band(window):
        n_prev = -(-(window - 1) // Q_BLOCK)
        dist = n_prev * Q_BLOCK + r - np.arange((n_prev + 1) * Q_BLOCK)[None, :]
        return dist, (dist >= 0) & (dist < window)

    dist_c = np.arange(t)[:, None] - (np.arange(ncp)[None, :] * CMP_STRIDE + CMP_LEN - 1)
    valid_c = (dist_c >= 0) & (np.arange(ncp)[None, :] < n_cmp)
    c = np.arange(SLC_TILE)[None, :]
    dist_n = np.stack([np.stack([SLC_TILE * (1 - which) + Q_BLOCK * par + r - c for which in range(2)])
                       for par in range(2)])
    near = _bias_from_dist(tab_a, dist_n, dist_n >= 0, shift_far=True)
    return dict(win=_bias_from_dist(tab_a, *band(NSA_WINDOW)),
                swa=_bias_from_dist(tab_b, *band(SWA_WINDOW)),
                cmp=_bias_from_dist(tab_a, dist_c, valid_c),
                near=jnp.transpose(near, (1, 2, 0, 3, 4)))


def _pair_cols(w):
    parts = []
    for p in range(N_QBLK):
        parts += [w[..., p * HEAD_DIM:(p + 1) * HEAD_DIM],
                  w[..., (N_QBLK + p) * HEAD_DIM:(N_QBLK + p + 1) * HEAD_DIM]]
    return jnp.concatenate(parts, axis=-1)


def _prep_layer(w_in, qk_gain, w_branch):
    kv = KV_GROUPS * HEAD_DIM
    widths = [('a_q', MIX_WIDTH), ('a_kc', kv), ('a_vc', kv), ('a_ks', kv), ('a_vs', kv), ('a_kw', kv), ('a_vw', kv),
              ('a_gate', N_GATE_COLS), ('b_q', MIX_WIDTH), ('b_k', kv), ('b_v', kv),
              ('c_q', MIX_WIDTH), ('c_k', MIX_WIDTH), ('c_v', MIX_WIDTH), ('c_f', N_HEADS),
              ('merge', N_BRANCH * w_in.shape[0])]
    cols, off = {}, 0
    for name, w in widths:
        cols[name] = w_in[:, off:off + w]
        off += w
    scale = HEAD_DIM ** -0.5
    tile = lambda gvec, n: jnp.tile(gvec, n)
    zeros = lambda n: jnp.zeros((n,), F32)
    ones = lambda n: jnp.ones((n,), F32)
    pieces = [(_pair_cols(cols['a_q']), tile(qk_gain[0] * scale, N_HEADS), ones(MIX_WIDTH)),
              (_pair_cols(cols['b_q']), tile(qk_gain[2] * scale, N_HEADS), ones(MIX_WIDTH)),
              (cols['c_q'], tile(qk_gain[4] * scale, N_HEADS), ones(MIX_WIDTH)),
              (cols['c_k'], tile(qk_gain[5], N_HEADS), ones(MIX_WIDTH)),
              (cols['c_v'], zeros(MIX_WIDTH), zeros(MIX_WIDTH)),
              (cols['a_kc'], zeros(kv), zeros(kv)), (cols['a_vc'], zeros(kv), zeros(kv)),
              (cols['a_ks'], tile(qk_gain[1], KV_GROUPS), ones(kv)), (cols['a_vs'], zeros(kv), zeros(kv)),
              (cols['a_kw'], tile(qk_gain[1], KV_GROUPS), ones(kv)), (cols['a_vw'], zeros(kv), zeros(kv)),
              (cols['b_k'], tile(qk_gain[3], KV_GROUPS), ones(kv)), (cols['b_v'], zeros(kv), zeros(kv))]
    w_qkv = jnp.concatenate([p[0] for p in pieces], axis=1).astype(BF16)
    n_qkv = w_qkv.shape[1]
    aux = jnp.zeros((8, n_qkv), F32)
    aux = aux.at[0].set(jnp.concatenate([p[1] for p in pieces])).at[1].set(jnp.concatenate([p[2] for p in pieces]))
    d = w_in.shape[0]
    w_gates = jnp.concatenate([cols['a_gate'], cols['c_f'],
                               jnp.zeros((d, LANES - N_GATE_COLS - N_HEADS), F32)], axis=1).astype(BF16)
    wb = jnp.stack([_pair_cols(w_branch[0].T).T, _pair_cols(w_branch[1].T).T, w_branch[2]]).astype(BF16)
    return w_qkv, aux, w_gates, cols['merge'].astype(BF16), wb


def _prep_compress(cmp_pos, cmp_w1, cmp_w2, k_gain):
    half = CMP_STRIDE * HEAD_DIM
    pos = cmp_pos.reshape(2, 2, 1, half)
    w1 = cmp_w1.reshape(2, 2, half, CMP_HIDDEN).astype(BF16)
    z = jnp.zeros_like(cmp_w2)
    w2p = jnp.stack([jnp.concatenate([cmp_w2, z], axis=-1), jnp.concatenate([z, cmp_w2], axis=-1)], axis=1).astype(BF16)
    aux = jnp.zeros((2, 8, LANES), F32)
    aux = aux.at[0, 0].set(jnp.tile(k_gain, KV_GROUPS)).at[0, 1].set(1.0)
    return pos, w1, w2p, aux


def kernel(x, rel_bias, norm_mix, norm_ffn, w_in, forget_bias, qk_gain, cmp_pos, cmp_w1, cmp_w2, sinks, w_branch, w_out, w_gate, w_up, conv_w, conv_b, w_down):
    b, t, d = x.shape
    bt = b * t
    depth = w_in.shape[0]
    tabs = _static_tables(t)
    biases = _bias_tables(rel_bias, t)
    n_chunks = t // CMP_STRIDE
    fox_tq = min(512, t)
    x2 = x.reshape(bt, d)
    for l in range(depth):
        w_qkv, aux, w_gates, w_merge, wb = _prep_layer(w_in[l], qk_gain[l], w_branch[l])
        h = _rmsnorm(x2, norm_mix[l])
        qkv = _inproj_qkv(h, w_qkv, aux, tabs['bd']).reshape(N_CB, b, t, LANES)
        graw, ga = _inproj_gates(h, w_gates, tabs['eg'])
        mg = _inproj_merge(h, w_merge, d)

        chunks = jnp.stack([qkv[CB_AKC], qkv[CB_AVC]]).reshape(2, b, n_chunks, CMP_STRIDE, KV_GROUPS, HEAD_DIM)
        chunks = jnp.transpose(chunks, (0, 1, 4, 2, 3, 5)).reshape(2, b, KV_GROUPS, n_chunks, CMP_STRIDE * HEAD_DIM)
        cmp_kv = _compress(chunks, *_prep_compress(cmp_pos[l], cmp_w1[l], cmp_w2[l], qk_gain[l, 1]), tabs['bd'])
        o_cmp, sel = _cmp_select(qkv, cmp_kv, biases['cmp'], tabs['ovt'], tabs['n_slc'])
        o_slc = _slc_attention(qkv, sel, tabs['e3'], biases['near'])
        no_sinks = jnp.zeros((N_HEADS, LANES), F32)
        o_win = _banded_attention(qkv, CB_AQ, CB_AKW, CB_AVW, biases['win'], no_sinks, False)

        sink_tab = jnp.broadcast_to(sinks[l][:, None], (N_HEADS, LANES))
        o_b = _banded_attention(qkv, CB_BQ, CB_BK, CB_BV, biases['swa'], sink_tab, True)

        f_t = jnp.transpose(graw[:, N_GATE_COLS:N_GATE_COLS + N_HEADS].reshape(b, t, N_HEADS), (0, 2, 1))
        c_row = _fox_cumsum(f_t, forget_bias[l].reshape(N_HEADS, 1))
        c_col = jnp.transpose(c_row.reshape(b, N_QBLK, 2, t), (0, 1, 3, 2))
        o_c = _fox_attention(qkv, c_col, c_row.reshape(b, N_QBLK, 2, t // fox_tq, fox_tq), fox_tq)

        flat = lambda o: o.reshape(N_QBLK, bt, LANES)
        x2 = _merge(flat(o_cmp), flat(o_slc), flat(o_win), flat(o_b), flat(o_c), ga, mg, wb,
                    w_out[l].astype(BF16), x2)
        x2 = _ffn(x2, norm_ffn[l], w_gate[l].astype(BF16), w_up[l].astype(BF16), conv_w[l], conv_b[l],
                  w_down[l].astype(BF16), t)
    return x2.reshape(b, t, d)
```

```python
import functools
import math

import numpy as np
import jax
import jax.numpy as jnp
from jax import lax
from jax.experimental import pallas as pl
from jax.experimental.pallas import tpu as pltpu

F32 = jnp.float32
BF16 = jnp.bfloat16

HEAD_DIM = 64
LANES = 128
N_HEADS = 8
KV_GROUPS = 2
HEADS_PER_GROUP = N_HEADS // KV_GROUPS
N_QBLK = N_HEADS * HEAD_DIM // LANES
MIX_WIDTH = N_HEADS * HEAD_DIM
N_BRANCH = 3
Q_BLOCK = 128
CMP_LEN = 32
CMP_STRIDE = 16
CMP_HIDDEN = 256
SLC_LEN = 64
SLC_TOPK = 8
NSA_WINDOW = 256
SWA_WINDOW = 128
N_BUCKETS = 32
MAX_DISTANCE = 128
CONV_WIDTH = 3
RMS_EPS = 1e-6
NEG = -1e30
FORCE_SCORE = 1e9
SLC_TILE = 256
VMEM_LIMIT = 48 * 1024 * 1024

CB_AQ, CB_BQ, CB_CQ, CB_CK, CB_CV = 0, 4, 8, 12, 16
CB_AKC, CB_AVC, CB_AKS, CB_AVS, CB_AKW, CB_AVW, CB_BK, CB_BV = 20, 21, 22, 23, 24, 25, 26, 27
N_CB = 28
N_GATE_COLS = N_HEADS * N_BRANCH


def _cparams(*sem):
    return pltpu.CompilerParams(dimension_semantics=sem, vmem_limit_bytes=VMEM_LIMIT)


def _split_hi_lo(v):
    hi = v.astype(BF16)
    return hi, (v - hi.astype(F32)).astype(BF16)


def _dot(a, b):
    return jnp.dot(a, b, preferred_element_type=F32)


def _dot_nt(a, b):
    return lax.dot_general(a, b, (((1,), (1,)), ((), ())), preferred_element_type=F32)


def _dot2(v, m):
    hi, lo = _split_hi_lo(v)
    return _dot(hi, m) + _dot(lo, m)


def _gelu_tanh(x):
    return x * (0.5 * (1.0 + jnp.tanh(math.sqrt(2.0 / math.pi) * (x + 0.044715 * (x * x * x)))))


def _head_rms(y, bd, gain):
    ms = _dot2(y * y, bd) * (1.0 / HEAD_DIM)
    return (y * lax.rsqrt(ms + RMS_EPS)) * gain


def _lane_half_mask(g):
    return (lax.broadcasted_iota(jnp.int32, (1, LANES), 1) // HEAD_DIM) == g


def _stack_group_queries(q_ref, g):
    mask = _lane_half_mask(g)
    return jnp.concatenate([jnp.where(mask, q_ref[p, 0], jnp.zeros_like(q_ref[p, 0]))
                            for p in range(N_QBLK)], axis=0)


def _merge_group_outputs(o_ref, o_g0, o_g1):
    low = _lane_half_mask(0)
    for p in range(N_QBLK):
        rows = slice(p * Q_BLOCK, (p + 1) * Q_BLOCK)
        o_ref[p, 0] = jnp.where(low, o_g0[rows], o_g1[rows]).astype(o_ref.dtype)


def _rmsnorm_kernel(x_ref, g_ref, o_ref):
    x = x_ref[...]
    ms = jnp.mean(x * x, axis=-1, keepdims=True)
    o_ref[...] = ((x * lax.rsqrt(ms + RMS_EPS)) * g_ref[...]).astype(o_ref.dtype)


def _rmsnorm(x2, g, tm=512):
    bt, d = x2.shape
    return pl.pallas_call(
        _rmsnorm_kernel,
        grid=(bt // tm,),
        in_specs=[pl.BlockSpec((tm, d), lambda i: (i, 0)), pl.BlockSpec((1, d), lambda i: (0, 0))],
        out_specs=pl.BlockSpec((tm, d), lambda i: (i, 0)),
        out_shape=jax.ShapeDtypeStruct((bt, d), BF16),
        compiler_params=_cparams("parallel"),
        name="rmsnorm",
    )(x2, g.reshape(1, d))


def _inproj_qkv_kernel(h_ref, w_ref, aux_ref, bd_ref, o_ref, *, n_sub):
    y = _dot(h_ref[...], w_ref[...])
    for c in range(n_sub):
        cols = slice(c * LANES, (c + 1) * LANES)
        yb = y[:, cols]
        normed = _head_rms(yb, bd_ref[...], aux_ref[0:1, cols])
        o_ref[c] = jnp.where(aux_ref[1:2, cols] > 0.5, normed, yb).astype(o_ref.dtype)


def _inproj_qkv(h, w, aux, bd, tm=1024, tn=512):
    bt, d = h.shape
    n = w.shape[1]
    n_sub = tn // LANES
    return pl.pallas_call(
        functools.partial(_inproj_qkv_kernel, n_sub=n_sub),
        grid=(bt // tm, n // tn),
        in_specs=[pl.BlockSpec((tm, d), lambda i, j: (i, 0)),
                  pl.BlockSpec((d, tn), lambda i, j: (0, j)),
                  pl.BlockSpec((8, tn), lambda i, j: (0, j)),
                  pl.BlockSpec((LANES, LANES), lambda i, j: (0, 0))],
        out_specs=pl.BlockSpec((n_sub, tm, LANES), lambda i, j: (j, i, 0)),
        out_shape=jax.ShapeDtypeStruct((n // LANES, bt, LANES), BF16),
        compiler_params=_cparams("parallel", "arbitrary"),
        name="inproj_qkv",
    )(h, w, aux, bd)


def _inproj_gates_kernel(h_ref, w_ref, eg_ref, raw_ref, ga_ref):
    y = _dot(h_ref[...], w_ref[...])
    raw_ref[...] = y
    ga = _dot2(jax.nn.sigmoid(y), eg_ref[...])
    for n in range(N_BRANCH):
        ga_ref[n] = ga[:, n * MIX_WIDTH:(n + 1) * MIX_WIDTH].astype(ga_ref.dtype)


def _inproj_gates(h, w, eg, tm=1024):
    bt, d = h.shape
    return pl.pallas_call(
        _inproj_gates_kernel,
        grid=(bt // tm,),
        in_specs=[pl.BlockSpec((tm, d), lambda i: (i, 0)),
                  pl.BlockSpec((d, LANES), lambda i: (0, 0)),
                  pl.BlockSpec((LANES, N_BRANCH * MIX_WIDTH), lambda i: (0, 0))],
        out_specs=[pl.BlockSpec((tm, LANES), lambda i: (i, 0)),
                   pl.BlockSpec((N_BRANCH, tm, MIX_WIDTH), lambda i: (0, i, 0))],
        out_shape=[jax.ShapeDtypeStruct((bt, LANES), F32),
                   jax.ShapeDtypeStruct((N_BRANCH, bt, MIX_WIDTH), BF16)],
        compiler_params=_cparams("parallel"),
        name="inproj_gates",
    )(h, w, eg)


def _inproj_merge_kernel(h_ref, w_ref, o_ref):
    o_ref[0] = jax.nn.sigmoid(_dot(h_ref[...], w_ref[...])).astype(o_ref.dtype)


def _inproj_merge(h, w, d_model, tm=1024, tn=512):
    bt, d = h.shape
    n = w.shape[1]
    per = d_model // tn
    return pl.pallas_call(
        _inproj_merge_kernel,
        grid=(bt // tm, n // tn),
        in_specs=[pl.BlockSpec((tm, d), lambda i, j: (i, 0)),
                  pl.BlockSpec((d, tn), lambda i, j: (0, j))],
        out_specs=pl.BlockSpec((1, tm, tn), lambda i, j: (j // per, i, j % per)),
        out_shape=jax.ShapeDtypeStruct((n // d_model, bt, d_model), BF16),
        compiler_params=_cparams("parallel", "arbitrary"),
        name="inproj_merge",
    )(h, w)


def _compress_kernel(c_ref, pos_ref, w1_ref, w2_ref, aux_ref, bd_ref, o_ref):
    n_chunks = c_ref.shape[3]
    y = jnp.zeros((n_chunks, LANES), F32)
    for g in range(KV_GROUPS):
        c = c_ref[0, 0, g].astype(F32)
        a = _dot((c + pos_ref[0, 0]).astype(BF16), w1_ref[0, 0])
        b = _dot((c + pos_ref[0, 1]).astype(BF16), w1_ref[0, 1])
        hid = _gelu_tanh(a + pltpu.roll(b, n_chunks - 1, 0))
        y = y + _dot(hid.astype(BF16), w2_ref[0, g])
    normed = _head_rms(y, bd_ref[...], aux_ref[0, 0:1, :])
    o_ref[0, 0] = jnp.where(aux_ref[0, 1:2, :] > 0.5, normed, y).astype(o_ref.dtype)


def _compress(chunks, pos, w1, w2p, aux, bd):
    _, b, g, n_chunks, cw = chunks.shape
    return pl.pallas_call(
        _compress_kernel,
        grid=(2, b),
        in_specs=[pl.BlockSpec((1, 1, g, n_chunks, cw), lambda w, i: (w, i, 0, 0, 0)),
                  pl.BlockSpec((1, 2, 1, cw), lambda w, i: (w, 0, 0, 0)),
                  pl.BlockSpec((1, 2, cw, CMP_HIDDEN), lambda w, i: (w, 0, 0, 0)),
                  pl.BlockSpec((1, g, CMP_HIDDEN, LANES), lambda w, i: (w, 0, 0, 0)),
                  pl.BlockSpec((1, 8, LANES), lambda w, i: (w, 0, 0)),
                  pl.BlockSpec((LANES, LANES), lambda w, i: (0, 0))],
        out_specs=pl.BlockSpec((1, 1, n_chunks, LANES), lambda w, i: (w, i, 0, 0)),
        out_shape=jax.ShapeDtypeStruct((2, b, n_chunks, LANES), BF16),
        compiler_params=_cparams("parallel", "parallel"),
        name="nsa_compress",
    )(chunks, pos, w1, w2p, aux, bd)


def _cmp_select_kernel(q_ref, kc_ref, vc_ref, bias_ref, ovt_ref, o_ref, sel_ref, *, n_slc, n_top):
    i = pl.program_id(1)
    ncp = kc_ref.shape[2]
    row_t = i * Q_BLOCK + lax.broadcasted_iota(jnp.int32, (Q_BLOCK, 1), 0)
    row_valid = jnp.concatenate([row_t >= CMP_LEN - 1] * HEADS_PER_GROUP, axis=0)
    jrow = lax.broadcasted_iota(jnp.int32, (n_slc, Q_BLOCK), 0)
    cur = (i * Q_BLOCK + lax.broadcasted_iota(jnp.int32, (n_slc, Q_BLOCK), 1)) // SLC_LEN
    block_valid = jrow <= cur

    def force(imp):
        forced = jnp.where(jrow == cur, FORCE_SCORE, jnp.where(jrow == cur - 1, FORCE_SCORE, imp))
        return jnp.where(jrow == 0, FORCE_SCORE, forced)
    outs = []
    for g in range(KV_GROUPS):
        qop = _stack_group_queries(q_ref, g)
        rows = HEADS_PER_GROUP * Q_BLOCK
        s = _dot_nt(qop, kc_ref[0, 0]) + bias_ref[g * HEADS_PER_GROUP:(g + 1) * HEADS_PER_GROUP].reshape(rows, ncp)
        e = jnp.exp(s - jnp.max(s, axis=-1, keepdims=True))
        inv = jnp.where(row_valid, 1.0 / jnp.sum(e, axis=-1, keepdims=True), 0.0)
        p = e * inv
        outs.append(_dot(p.astype(BF16), vc_ref[0, 0]))
        psum = p[0:Q_BLOCK]
        for r in range(1, HEADS_PER_GROUP):
            psum = psum + p[r * Q_BLOCK:(r + 1) * Q_BLOCK]
        hi, lo = _split_hi_lo(psum)
        imp_t = (_dot_nt(ovt_ref[...], hi) + _dot_nt(ovt_ref[...], lo))[0:n_slc]
        score = jnp.where(block_valid, force(imp_t), NEG)
        groups = [score[a:a + 8] for a in range(0, n_slc, 8)]
        ranks = [jnp.zeros((8, Q_BLOCK), F32) for _ in groups]
        sub = lax.broadcasted_iota(jnp.int32, (8, Q_BLOCK), 0)
        for j in range(n_slc):
            other = score[j:j + 1, :]
            for a, blk in enumerate(groups):
                ge = jnp.where(other >= blk, 1.0, 0.0)
                if 8 * a > j:
                    ranks[a] = ranks[a] + ge
                elif 8 * a + 7 <= j:
                    ranks[a] = ranks[a] + jnp.where(other > blk, 1.0, 0.0)
                else:
                    ranks[a] = ranks[a] + jnp.where(sub > j - 8 * a, ge, jnp.where(other > blk, 1.0, 0.0))
        rank = jnp.concatenate(ranks, axis=0)
        sel_bias = jnp.where(rank < n_top, jnp.where(score > NEG / 2, 0.0, NEG), NEG)
        if n_slc < LANES:
            sel_bias = jnp.concatenate([sel_bias, jnp.full((LANES - n_slc, Q_BLOCK), NEG, F32)], axis=0)
        sel_ref[0, g] = sel_bias.T.astype(sel_ref.dtype)
    _merge_group_outputs(o_ref, outs[0], outs[1])


def _cmp_select(qkv, cmp_kv, bias_c, ovt, n_slc):
    _, b, t, _ = qkv.shape
    nt = t // Q_BLOCK
    ncp = cmp_kv.shape[2]
    return pl.pallas_call(
        functools.partial(_cmp_select_kernel, n_slc=n_slc, n_top=min(SLC_TOPK, n_slc)),
        grid=(b, nt),
        in_specs=[pl.BlockSpec((N_QBLK, 1, Q_BLOCK, LANES), lambda bi, i: (CB_AQ // N_QBLK, bi, i, 0)),
                  pl.BlockSpec((1, 1, ncp, LANES), lambda bi, i: (0, bi, 0, 0)),
                  pl.BlockSpec((1, 1, ncp, LANES), lambda bi, i: (1, bi, 0, 0)),
                  pl.BlockSpec((N_HEADS, Q_BLOCK, ncp), lambda bi, i: (0, i, 0)),
                  pl.BlockSpec((LANES, ncp), lambda bi, i: (0, 0))],
        out_specs=[pl.BlockSpec((N_QBLK, 1, Q_BLOCK, LANES), lambda bi, i: (0, bi, i, 0)),
                   pl.BlockSpec((1, KV_GROUPS, Q_BLOCK, LANES), lambda bi, i: (bi, 0, i, 0))],
        out_shape=[jax.ShapeDtypeStruct((N_QBLK, b, t, LANES), BF16),
                   jax.ShapeDtypeStruct((b, KV_GROUPS, t, LANES), BF16)],
        compiler_params=_cparams("parallel", "parallel"),
        name="nsa_cmp_select",
    )(qkv, cmp_kv, cmp_kv, bias_c, ovt)


def _slc_kernel(q_ref, k_ref, v_ref, sel_ref, e_ref, nb_ref, o_ref):
    i = pl.program_id(1)
    par = i % 2
    diag = i // 2
    rows = HEADS_PER_GROUP * Q_BLOCK
    outs = []
    for g in range(KV_GROUPS):
        qop = _stack_group_queries(q_ref, g)
        sel = sel_ref[0, g]

        def step(j, carry, bias=None, pen=None):
            m, l, acc = carry
            start = pl.multiple_of(j * SLC_TILE, SLC_TILE)
            k = k_ref[0, 0, pl.ds(start, SLC_TILE), :]
            v = v_ref[0, 0, pl.ds(start, SLC_TILE), :]
            s = _dot_nt(qop, k).reshape(HEADS_PER_GROUP, Q_BLOCK, SLC_TILE)
            blk_bias = _dot(sel, e_ref[j])
            if pen is not None:
                blk_bias = blk_bias + pen
            s = s + blk_bias[None]
            if bias is not None:
                s = s + bias
            s = s.reshape(rows, SLC_TILE)
            m_new = jnp.maximum(m, jnp.max(s, axis=-1, keepdims=True))
            alpha = jnp.exp(m - m_new)
            p = jnp.exp(s - m_new)
            l = alpha * l + jnp.sum(p, axis=-1, keepdims=True)
            acc = alpha * acc + _dot(p.astype(BF16), v)
            return m_new, l, acc

        carry = (jnp.full((rows, 1), NEG, F32), jnp.zeros((rows, 1), F32), jnp.zeros((rows, LANES), F32))
        carry = lax.fori_loop(0, jnp.maximum(diag - 1, 0), step, carry)
        hs = slice(g * HEADS_PER_GROUP, (g + 1) * HEADS_PER_GROUP)
        pen = jnp.where(diag >= 1, 0.0, NEG)
        carry = step(jnp.maximum(diag - 1, 0), carry, bias=nb_ref[par, 0, hs], pen=pen)
        m, l, acc = step(diag, carry, bias=nb_ref[par, 1, hs])
        outs.append(acc / l)
    _merge_group_outputs(o_ref, outs[0], outs[1])


def _slc_attention(qkv, sel, e3, near_bias):
    _, b, t, _ = qkv.shape
    nt = t // Q_BLOCK
    return pl.pallas_call(
        _slc_kernel,
        grid=(b, nt),
        in_specs=[pl.BlockSpec((N_QBLK, 1, Q_BLOCK, LANES), lambda bi, i: (CB_AQ // N_QBLK, bi, i, 0)),
                  pl.BlockSpec((1, 1, t, LANES), lambda bi, i: (CB_AKS, bi, 0, 0)),
                  pl.BlockSpec((1, 1, t, LANES), lambda bi, i: (CB_AVS, bi, 0, 0)),
                  pl.BlockSpec((1, KV_GROUPS, Q_BLOCK, LANES), lambda bi, i: (bi, 0, i, 0)),
                  pl.BlockSpec(e3.shape, lambda bi, i: (0, 0, 0)),
                  pl.BlockSpec(near_bias.shape, lambda bi, i: (0, 0, 0, 0, 0))],
        out_specs=pl.BlockSpec((N_QBLK, 1, Q_BLOCK, LANES), lambda bi, i: (0, bi, i, 0)),
        out_shape=jax.ShapeDtypeStruct((N_QBLK, b, t, LANES), BF16),
        compiler_params=_cparams("parallel", "parallel"),
        name="nsa_slc_attention",
    )(qkv, qkv, qkv, sel, e3, near_bias)


def _banded_kernel(*refs, n_blk, use_sinks):
    q_ref = refs[0]
    k_refs = refs[1:1 + n_blk]
    v_refs = refs[1 + n_blk:1 + 2 * n_blk]
    bias_ref, sink_ref, o_ref = refs[1 + 2 * n_blk:]
    i = pl.program_id(1)
    n_keys = n_blk * Q_BLOCK
    rows = HEADS_PER_GROUP * Q_BLOCK
    key_pos = (i - (n_blk - 1)) * Q_BLOCK + lax.broadcasted_iota(jnp.int32, (1, n_keys), 1)
    pad_pen = jnp.where(key_pos >= 0, 0.0, NEG)
    kcat = jnp.concatenate([r[0, 0] for r in k_refs], axis=0)
    vcat = jnp.concatenate([r[0, 0] for r in v_refs], axis=0)
    outs = []
    for g in range(KV_GROUPS):
        qop = _stack_group_queries(q_ref, g)
        hs = slice(g * HEADS_PER_GROUP, (g + 1) * HEADS_PER_GROUP)
        s = _dot_nt(qop, kcat) + bias_ref[hs].reshape(rows, n_keys) + pad_pen
        m = jnp.max(s, axis=-1, keepdims=True)
        if use_sinks:
            sink = jnp.concatenate([jnp.broadcast_to(sink_ref[h:h + 1, 0:1], (Q_BLOCK, 1))
                                    for h in range(hs.start, hs.stop)], axis=0)
            m = jnp.maximum(m, sink)
        e = jnp.exp(s - m)
        l = jnp.sum(e, axis=-1, keepdims=True)
        if use_sinks:
            l = l + jnp.exp(sink - m)
        outs.append(_dot(e.astype(BF16), vcat) / l)
    _merge_group_outputs(o_ref, outs[0], outs[1])


def _banded_attention(qkv, cb_q, cb_k, cb_v, bias, sinks, use_sinks):
    _, b, t, _ = qkv.shape
    nt = t // Q_BLOCK
    n_blk = bias.shape[2] // Q_BLOCK

    def kv_spec(cb, back):
        return pl.BlockSpec((1, 1, Q_BLOCK, LANES), lambda bi, i: (cb, bi, jnp.maximum(i - back, 0), 0))

    backs = [n_blk - 1 - jb for jb in range(n_blk)]
    return pl.pallas_call(
        functools.partial(_banded_kernel, n_blk=n_blk, use_sinks=use_sinks),
        grid=(b, nt),
        in_specs=([pl.BlockSpec((N_QBLK, 1, Q_BLOCK, LANES), lambda bi, i: (cb_q // N_QBLK, bi, i, 0))]
                  + [kv_spec(cb_k, back) for back in backs] + [kv_spec(cb_v, back) for back in backs]
                  + [pl.BlockSpec(bias.shape, lambda bi, i: (0, 0, 0)),
                     pl.BlockSpec(sinks.shape, lambda bi, i: (0, 0))]),
        out_specs=pl.BlockSpec((N_QBLK, 1, Q_BLOCK, LANES), lambda bi, i: (0, bi, i, 0)),
        out_shape=jax.ShapeDtypeStruct((N_QBLK, b, t, LANES), BF16),
        compiler_params=_cparams("parallel", "parallel"),
        name="banded_attention",
    )(qkv, *([qkv] * (2 * n_blk)), bias, sinks)


def _fox_cumsum_kernel(f_ref, fb_ref, o_ref, *, chunk):
    x = f_ref[0] + fb_ref[...]
    logf = jnp.minimum(x, 0.0) - jnp.log(1.0 + jnp.exp(-jnp.abs(x)))
    t = x.shape[1]
    upper = jnp.where(lax.broadcasted_iota(jnp.int32, (chunk, chunk), 0)
                      <= lax.broadcasted_iota(jnp.int32, (chunk, chunk), 1), 1.0, 0.0).astype(BF16)
    carry = jnp.zeros((x.shape[0], 1), F32)
    for c in range(t // chunk):
        blk = logf[:, c * chunk:(c + 1) * chunk]
        hi, rest = blk.astype(BF16), blk - blk.astype(BF16).astype(F32)
        mid, lo = _split_hi_lo(rest)
        cs = (_dot(hi, upper) + _dot(mid, upper) + _dot(lo, upper)) + carry
        o_ref[0, :, c * chunk:(c + 1) * chunk] = cs
        carry = cs[:, chunk - 1:chunk]


def _fox_cumsum(f_t, fb, chunk=512):
    b, h, t = f_t.shape
    chunk = min(chunk, t)
    return pl.pallas_call(
        functools.partial(_fox_cumsum_kernel, chunk=chunk),
        grid=(b,),
        in_specs=[pl.BlockSpec((1, h, t), lambda i: (i, 0, 0)), pl.BlockSpec((h, 1), lambda i: (0, 0))],
        out_specs=pl.BlockSpec((1, h, t), lambda i: (i, 0, 0)),
        out_shape=jax.ShapeDtypeStruct((b, h, t), F32),
        compiler_params=_cparams("parallel"),
        name="fox_cumsum",
    )(f_t, fb)


def _fox_kernel(q_ref, k_ref, v_ref, cq_ref, ck_ref, o_ref, *, tq):
    i = pl.program_id(2)
    q = q_ref[0, 0]
    causal_pen = jnp.where(lax.broadcasted_iota(jnp.int32, (tq, tq), 0)
                           >= lax.broadcasted_iota(jnp.int32, (tq, tq), 1), 0.0, NEG)
    outs = []
    for hh in range(2):
        qop = jnp.where(_lane_half_mask(hh), q, jnp.zeros_like(q))
        cq = cq_ref[0, 0][:, hh:hh + 1]

        def step(j, carry, pen=None):
            m, l, acc = carry
            start = pl.multiple_of(j * tq, tq)
            k = k_ref[0, 0, pl.ds(start, tq), :]
            v = v_ref[0, 0, pl.ds(start, tq), :]
            s = _dot_nt(qop, k) + cq - ck_ref[0, 0, hh, pl.ds(j, 1), :]
            if pen is not None:
                s = s + pen
            m_new = jnp.maximum(m, jnp.max(s, axis=-1, keepdims=True))
            alpha = jnp.exp(m - m_new)
            p = jnp.exp(s - m_new)
            l = alpha * l + jnp.sum(p, axis=-1, keepdims=True)
            acc = alpha * acc + _dot(p.astype(BF16), v)
            return m_new, l, acc

        carry = (jnp.full((tq, 1), NEG, F32), jnp.zeros((tq, 1), F32), jnp.zeros((tq, LANES), F32))
        carry = lax.fori_loop(0, i, step, carry)
        m, l, acc = step(i, carry, pen=causal_pen)
        outs.append(acc / l)
    o_ref[0, 0] = jnp.where(_lane_half_mask(0), outs[0], outs[1]).astype(o_ref.dtype)


def _fox_attention(qkv, c_col, c_row, tq):
    _, b, t, _ = qkv.shape
    nq = t // tq
    pairs = N_QBLK
    return pl.pallas_call(
        functools.partial(_fox_kernel, tq=tq),
        grid=(b, pairs, nq),
        in_specs=[pl.BlockSpec((1, 1, tq, LANES), lambda bi, p, i: (CB_CQ + p, bi, i, 0)),
                  pl.BlockSpec((1, 1, t, LANES), lambda bi, p, i: (CB_CK + p, bi, 0, 0)),
                  pl.BlockSpec((1, 1, t, LANES), lambda bi, p, i: (CB_CV + p, bi, 0, 0)),
                  pl.BlockSpec((1, 1, tq, 2), lambda bi, p, i: (bi, p, i, 0)),
                  pl.BlockSpec((1, 1, 2, nq, tq), lambda bi, p, i: (bi, p, 0, 0, 0))],
        out_specs=pl.BlockSpec((1, 1, tq, LANES), lambda bi, p, i: (p, bi, i, 0)),
        out_shape=jax.ShapeDtypeStruct((pairs, b, t, LANES), BF16),
        compiler_params=_cparams("parallel", "parallel", "parallel"),
        name="fox_attention",
    )(qkv, qkv, qkv, c_col, c_row)


def _merge_kernel(ocmp_ref, oslc_ref, owin_ref, ob_ref, oc_ref, ga_ref, mg_ref, wb_ref, wo_ref, x_ref, o_ref):
    def cat(ref):
        return jnp.concatenate([ref[p] for p in range(N_QBLK)], axis=-1)

    oa = (ga_ref[0].astype(F32) * cat(ocmp_ref).astype(F32)
          + ga_ref[1].astype(F32) * cat(oslc_ref).astype(F32)
          + ga_ref[2].astype(F32) * cat(owin_ref).astype(F32)).astype(BF16)
    mix = (mg_ref[0].astype(F32) * _dot(oa, wb_ref[0])
           + mg_ref[1].astype(F32) * _dot(cat(ob_ref), wb_ref[1])
           + mg_ref[2].astype(F32) * _dot(cat(oc_ref), wb_ref[2]))
    o_ref[...] = x_ref[...] + _dot(mix.astype(BF16), wo_ref[...])


def _merge(ocmp, oslc, owin, ob, oc, ga, mg, wb, wo, x2, tm=512):
    bt, d = x2.shape
    o_spec = pl.BlockSpec((N_QBLK, tm, LANES), lambda i: (0, i, 0))
    return pl.pallas_call(
        _merge_kernel,
        grid=(bt // tm,),
        in_specs=[o_spec] * 5 + [pl.BlockSpec((N_BRANCH, tm, MIX_WIDTH), lambda i: (0, i, 0)),
                                 pl.BlockSpec((N_BRANCH, tm, d), lambda i: (0, i, 0)),
                                 pl.BlockSpec(wb.shape, lambda i: (0, 0, 0)),
                                 pl.BlockSpec(wo.shape, lambda i: (0, 0)),
                                 pl.BlockSpec((tm, d), lambda i: (i, 0))],
        out_specs=pl.BlockSpec((tm, d), lambda i: (i, 0)),
        out_shape=jax.ShapeDtypeStruct((bt, d), F32),
        compiler_params=_cparams("parallel"),
        name="branch_merge",
    )(ocmp, oslc, owin, ob, oc, ga, mg, wb, wo, x2)


def _ffn_kernel(x_ref, xh_ref, g_ref, wg_ref, wu_ref, cw_ref, cb_ref, wd_ref, o_ref,
                h_scr, hh_scr, a_scr, acc_scr, *, tm, tiles_per_seq):
    i = pl.program_id(0)
    f = pl.program_id(1)

    def norm(x):
        ms = jnp.mean(x * x, axis=-1, keepdims=True)
        return ((x * lax.rsqrt(ms + RMS_EPS)) * g_ref[...]).astype(BF16)

    @pl.when(f == 0)
    def _():
        h_scr[...] = norm(x_ref[...])
        hh_scr[...] = norm(xh_ref[...])
        acc_scr[...] = jnp.zeros_like(acc_scr)

    a = _dot(h_scr[...], wg_ref[...])
    halo = _dot(hh_scr[...], wg_ref[...]) * jnp.where(i % tiles_per_seq == 0, 0.0, 1.0)
    a_scr[0:8, :] = halo
    a_scr[8:8 + tm, :] = a
    conv = (cw_ref[0:1, :] * a_scr[6:6 + tm, :] + cw_ref[1:2, :] * a_scr[7:7 + tm, :]
            + cw_ref[2:3, :] * a + cb_ref[...])
    act = _gelu_tanh(conv) * _dot(h_scr[...], wu_ref[...])
    acc_scr[...] += _dot(act.astype(BF16), wd_ref[...])

    @pl.when(f == pl.num_programs(1) - 1)
    def _():
        o_ref[...] = x_ref[...] + acc_scr[...]


def _ffn(x2, g, wg, wu, cw, cb, wd, t, tm=1024, tf=256):
    bt, d = x2.shape
    ff = wg.shape[1]
    halo_blocks = tm // 8
    return pl.pallas_call(
        functools.partial(_ffn_kernel, tm=tm, tiles_per_seq=t // tm),
        grid=(bt // tm, ff // tf),
        in_specs=[pl.BlockSpec((tm, d), lambda i, f: (i, 0)),
                  pl.BlockSpec((8, d), lambda i, f: (jnp.maximum(i * halo_blocks - 1, 0), 0)),
                  pl.BlockSpec((1, d), lambda i, f: (0, 0)),
                  pl.BlockSpec((d, tf), lambda i, f: (0, f)),
                  pl.BlockSpec((d, tf), lambda i, f: (0, f)),
                  pl.BlockSpec((CONV_WIDTH, tf), lambda i, f: (0, f)),
                  pl.BlockSpec((1, tf), lambda i, f: (0, f)),
                  pl.BlockSpec((tf, d), lambda i, f: (f, 0))],
        out_specs=pl.BlockSpec((tm, d), lambda i, f: (i, 0)),
        out_shape=jax.ShapeDtypeStruct((bt, d), F32),
        scratch_shapes=[pltpu.VMEM((tm, d), BF16), pltpu.VMEM((8, d), BF16),
                        pltpu.VMEM((tm + 8, tf), F32), pltpu.VMEM((tm, d), F32)],
        compiler_params=_cparams("parallel", "arbitrary"),
        name="conv_ffn",
    )(x2, x2, g.reshape(1, d), wg, wu, cw, cb.reshape(1, ff), wd)


def _t5_bucket_np(dist):
    max_exact = N_BUCKETS // 2
    d = np.maximum(dist, 0)
    ratio = np.log(np.maximum(d, 1) / max_exact) / math.log(MAX_DISTANCE / max_exact)
    large = np.minimum(max_exact + (ratio * (N_BUCKETS - max_exact)).astype(np.int64), N_BUCKETS - 1)
    return np.where(d < max_exact, d, large)


def _expand_kernel(tab_ref, idx_ref, o_ref):
    tab = tab_ref[...]
    onehot = jnp.where(lax.broadcasted_iota(jnp.int32, (tab.shape[1], idx_ref.shape[1]), 0) == idx_ref[...],
                       1.0, 0.0).astype(BF16)
    hi, rest = tab.astype(BF16), tab - tab.astype(BF16).astype(F32)
    mid, lo = _split_hi_lo(rest)
    o_ref[...] = _dot(hi, onehot) + _dot(mid, onehot) + _dot(lo, onehot)


def _bias_from_dist(tab_t, dist, valid, chunk=4096):
    heads = tab_t.shape[0]
    tab = jnp.zeros((heads, LANES), F32).at[:, :N_BUCKETS].set(tab_t).at[:, N_BUCKETS].set(NEG)
    idx = np.where(valid, _t5_bucket_np(dist), N_BUCKETS).reshape(1, -1).astype(np.int32)
    n = idx.shape[1]
    chunk = min(chunk, n)
    out = pl.pallas_call(
        _expand_kernel,
        grid=(n // chunk,),
        in_specs=[pl.BlockSpec((heads, LANES), lambda i: (0, 0)), pl.BlockSpec((1, chunk), lambda i: (0, i))],
        out_specs=pl.BlockSpec((heads, chunk), lambda i: (0, i)),
        out_shape=jax.ShapeDtypeStruct((heads, n), F32),
        compiler_params=_cparams("parallel"),
        name="rel_bias_expand",
    )(tab, jnp.asarray(idx))
    return out.reshape((heads,) + dist.shape)


def _cmp_bias_kernel(g_ref, o_ref):
    i = pl.program_id(0)
    ncp = o_ref.shape[2]
    for h in range(o_ref.shape[0]):
        o_ref[h] = pltpu.roll(g_ref[h], i * (Q_BLOCK // CMP_STRIDE), 1)[:, :ncp]


def _cmp_bias(g_tab, t):
    heads, _, width = g_tab.shape
    ncp = t // CMP_STRIDE
    return pl.pallas_call(
        _cmp_bias_kernel,
        grid=(t // Q_BLOCK,),
        in_specs=[pl.BlockSpec(g_tab.shape, lambda i: (0, 0, 0))],
        out_specs=pl.BlockSpec((heads, Q_BLOCK, ncp), lambda i: (0, i, 0)),
        out_shape=jax.ShapeDtypeStruct((heads, t, ncp), F32),
        compiler_params=_cparams("parallel"),
        name="cmp_bias_build",
    )(g_tab)


def _static_tables(t):
    ncp = t // CMP_STRIDE
    n_cmp = (t - CMP_LEN) // CMP_STRIDE + 1
    n_slc = t // SLC_LEN
    lane = np.arange(LANES)
    bd = (lane[:, None] // HEAD_DIM == lane[None, :] // HEAD_DIM).astype(np.float32)
    eg = np.zeros((LANES, N_BRANCH * MIX_WIDTH), np.float32)
    for h in range(N_HEADS):
        base = (h % N_QBLK) * LANES + (h // N_QBLK) * HEAD_DIM
        for n in range(N_BRANCH):
            eg[h * N_BRANCH + n, n * MIX_WIDTH + base:n * MIX_WIDTH + base + HEAD_DIM] = 1.0
    c_start = np.arange(n_cmp) * CMP_STRIDE
    s_start = np.arange(n_slc) * SLC_LEN
    overlap = ((c_start[:, None] < s_start[None, :] + SLC_LEN) & (c_start[:, None] + CMP_LEN > s_start[None, :]))
    ovt = np.zeros((LANES, ncp), np.float32)
    ovt[:n_slc, :n_cmp] = overlap.T
    e3 = np.zeros((t // SLC_TILE, LANES, SLC_TILE), np.float32)
    for j in range(t // SLC_TILE):
        key_blk = (j * SLC_TILE + np.arange(SLC_TILE)) // SLC_LEN
        e3[j, key_blk, np.arange(SLC_TILE)] = 1.0
    as_bf16 = lambda a: jnp.asarray(a, BF16)
    return dict(bd=as_bf16(bd), eg=as_bf16(eg), ovt=as_bf16(ovt), e3=as_bf16(e3), n_slc=n_slc)


def _bias_tables(rel_bias, t):
    n_cmp = (t - CMP_LEN) // CMP_STRIDE + 1
    ncp = t // CMP_STRIDE
    tab_a = rel_bias[:, :N_HEADS].T
    tab_b = rel_bias[:, N_HEADS:].T
    r = np.arange(Q_BLOCK)[:, None]

    def band(window):
        n_prev = -(-(window - 1) // Q_BLOCK)
        dist = n_prev * Q_BLOCK + r - np.arange((n_prev + 1) * Q_BLOCK)[None, :]
        return dist, (dist >= 0) & (dist < window)

    m = np.arange(2 * ncp)
    m = np.where(m < ncp, m, m - 2 * ncp)[None, :]
    dist_c = r - (m * CMP_STRIDE + CMP_LEN - 1)
    assert n_cmp == ncp - 1 and (dist_c[:, m[0] == Q_BLOCK // CMP_STRIDE - 1] < 0).all()
    c = np.arange(SLC_TILE)[None, :]
    dist_n = np.stack([np.stack([SLC_TILE * (1 - which) + Q_BLOCK * par + r - c for which in range(2)])
                       for par in range(2)])
    near = _bias_from_dist(tab_a - tab_a[:, N_BUCKETS - 1:], dist_n, dist_n >= 0)
    return dict(win=_bias_from_dist(tab_a, *band(NSA_WINDOW)),
                swa=_bias_from_dist(tab_b, *band(SWA_WINDOW)),
                cmp=_cmp_bias(_bias_from_dist(tab_a, dist_c, dist_c >= 0), t),
                near=jnp.transpose(near, (1, 2, 0, 3, 4)))


def _pair_cols(w):
    parts = []
    for p in range(N_QBLK):
        parts += [w[..., p * HEAD_DIM:(p + 1) * HEAD_DIM],
                  w[..., (N_QBLK + p) * HEAD_DIM:(N_QBLK + p + 1) * HEAD_DIM]]
    return jnp.concatenate(parts, axis=-1)


def _prep_layer(w_in, qk_gain, w_branch):
    kv = KV_GROUPS * HEAD_DIM
    widths = [('a_q', MIX_WIDTH), ('a_kc', kv), ('a_vc', kv), ('a_ks', kv), ('a_vs', kv), ('a_kw', kv), ('a_vw', kv),
              ('a_gate', N_GATE_COLS), ('b_q', MIX_WIDTH), ('b_k', kv), ('b_v', kv),
              ('c_q', MIX_WIDTH), ('c_k', MIX_WIDTH), ('c_v', MIX_WIDTH), ('c_f', N_HEADS),
              ('merge', N_BRANCH * w_in.shape[0])]
    cols, off = {}, 0
    for name, w in widths:
        cols[name] = w_in[:, off:off + w]
        off += w
    scale = HEAD_DIM ** -0.5
    tile = lambda gvec, n: jnp.tile(gvec, n)
    zeros = lambda n: jnp.zeros((n,), F32)
    ones = lambda n: jnp.ones((n,), F32)
    pieces = [(_pair_cols(cols['a_q']), tile(qk_gain[0] * scale, N_HEADS), ones(MIX_WIDTH)),
              (_pair_cols(cols['b_q']), tile(qk_gain[2] * scale, N_HEADS), ones(MIX_WIDTH)),
              (cols['c_q'], tile(qk_gain[4] * scale, N_HEADS), ones(MIX_WIDTH)),
              (cols['c_k'], tile(qk_gain[5], N_HEADS), ones(MIX_WIDTH)),
              (cols['c_v'], zeros(MIX_WIDTH), zeros(MIX_WIDTH)),
              (cols['a_kc'], zeros(kv), zeros(kv)), (cols['a_vc'], zeros(kv), zeros(kv)),
              (cols['a_ks'], tile(qk_gain[1], KV_GROUPS), ones(kv)), (cols['a_vs'], zeros(kv), zeros(kv)),
              (cols['a_kw'], tile(qk_gain[1], KV_GROUPS), ones(kv)), (cols['a_vw'], zeros(kv), zeros(kv)),
              (cols['b_k'], tile(qk_gain[3], KV_GROUPS), ones(kv)), (cols['b_v'], zeros(kv), zeros(kv))]
    w_qkv = jnp.concatenate([p[0] for p in pieces], axis=1).astype(BF16)
    n_qkv = w_qkv.shape[1]
    aux = jnp.zeros((8, n_qkv), F32)
    aux = aux.at[0].set(jnp.concatenate([p[1] for p in pieces])).at[1].set(jnp.concatenate([p[2] for p in pieces]))
    d = w_in.shape[0]
    w_gates = jnp.concatenate([cols['a_gate'], cols['c_f'],
                               jnp.zeros((d, LANES - N_GATE_COLS - N_HEADS), F32)], axis=1).astype(BF16)
    wb = jnp.stack([_pair_cols(w_branch[0].T).T, _pair_cols(w_branch[1].T).T, w_branch[2]]).astype(BF16)
    return w_qkv, aux, w_gates, cols['merge'].astype(BF16), wb


def _prep_compress(cmp_pos, cmp_w1, cmp_w2, k_gain):
    half = CMP_STRIDE * HEAD_DIM
    pos = cmp_pos.reshape(2, 2, 1, half)
    w1 = cmp_w1.reshape(2, 2, half, CMP_HIDDEN).astype(BF16)
    z = jnp.zeros_like(cmp_w2)
    w2p = jnp.stack([jnp.concatenate([cmp_w2, z], axis=-1), jnp.concatenate([z, cmp_w2], axis=-1)], axis=1).astype(BF16)
    aux = jnp.zeros((2, 8, LANES), F32)
    aux = aux.at[0, 0].set(jnp.tile(k_gain, KV_GROUPS)).at[0, 1].set(1.0)
    return pos, w1, w2p, aux


def kernel(x, rel_bias, norm_mix, norm_ffn, w_in, forget_bias, qk_gain, cmp_pos, cmp_w1, cmp_w2, sinks, w_branch, w_out, w_gate, w_up, conv_w, conv_b, w_down):
    b, t, d = x.shape
    bt = b * t
    depth = w_in.shape[0]
    tabs = _static_tables(t)
    biases = _bias_tables(rel_bias, t)
    n_chunks = t // CMP_STRIDE
    fox_tq = min(512, t)
    x2 = x.reshape(bt, d)
    for l in range(depth):
        w_qkv, aux, w_gates, w_merge, wb = _prep_layer(w_in[l], qk_gain[l], w_branch[l])
        h = _rmsnorm(x2, norm_mix[l])
        qkv = _inproj_qkv(h, w_qkv, aux, tabs['bd']).reshape(N_CB, b, t, LANES)
        graw, ga = _inproj_gates(h, w_gates, tabs['eg'])
        mg = _inproj_merge(h, w_merge, d)

        chunks = jnp.stack([qkv[CB_AKC], qkv[CB_AVC]]).reshape(2, b, n_chunks, CMP_STRIDE, KV_GROUPS, HEAD_DIM)
        chunks = jnp.transpose(chunks, (0, 1, 4, 2, 3, 5)).reshape(2, b, KV_GROUPS, n_chunks, CMP_STRIDE * HEAD_DIM)
        cmp_kv = _compress(chunks, *_prep_compress(cmp_pos[l], cmp_w1[l], cmp_w2[l], qk_gain[l, 1]), tabs['bd'])
        o_cmp, sel = _cmp_select(qkv, cmp_kv, biases['cmp'], tabs['ovt'], tabs['n_slc'])
        o_slc = _slc_attention(qkv, sel, tabs['e3'], biases['near'])
        no_sinks = jnp.zeros((N_HEADS, LANES), F32)
        o_win = _banded_attention(qkv, CB_AQ, CB_AKW, CB_AVW, biases['win'], no_sinks, False)

        sink_tab = jnp.broadcast_to(sinks[l][:, None], (N_HEADS, LANES))
        o_b = _banded_attention(qkv, CB_BQ, CB_BK, CB_BV, biases['swa'], sink_tab, True)

        f_t = jnp.transpose(graw[:, N_GATE_COLS:N_GATE_COLS + N_HEADS].reshape(b, t, N_HEADS), (0, 2, 1))
        c_row = _fox_cumsum(f_t, forget_bias[l].reshape(N_HEADS, 1))
        c_col = jnp.transpose(c_row.reshape(b, N_QBLK, 2, t), (0, 1, 3, 2))
        o_c = _fox_attention(qkv, c_col, c_row.reshape(b, N_QBLK, 2, t // fox_tq, fox_tq), fox_tq)

        flat = lambda o: o.reshape(N_QBLK, bt, LANES)
        x2 = _merge(flat(o_cmp), flat(o_slc), flat(o_win), flat(o_b), flat(o_c), ga, mg, wb,
                    w_out[l].astype(BF16), x2)
        x2 = _ffn(x2, norm_ffn[l], w_gate[l].astype(BF16), w_up[l].astype(BF16), conv_w[l], conv_b[l],
                  w_down[l].astype(BF16), t)
    return x2.reshape(b, t, d)
```

```python
import functools
import math

import numpy as np
import jax
import jax.numpy as jnp
from jax import lax
from jax.experimental import pallas as pl
from jax.experimental.pallas import tpu as pltpu

F32 = jnp.float32
BF16 = jnp.bfloat16

HEAD_DIM = 64
LANES = 128
N_HEADS = 8
KV_GROUPS = 2
HEADS_PER_GROUP = N_HEADS // KV_GROUPS
N_QBLK = N_HEADS * HEAD_DIM // LANES
MIX_WIDTH = N_HEADS * HEAD_DIM
N_BRANCH = 3
Q_BLOCK = 128
CMP_LEN = 32
CMP_STRIDE = 16
CMP_HIDDEN = 256
SLC_LEN = 64
SLC_TOPK = 8
NSA_WINDOW = 256
SWA_WINDOW = 128
N_BUCKETS = 32
MAX_DISTANCE = 128
CONV_WIDTH = 3
RMS_EPS = 1e-6
LOG2E = math.log2(math.e)
NEG = -1e30
FORCE_SCORE = 1e9
SLC_TILE = 256
VMEM_LIMIT = 48 * 1024 * 1024

CB_AQ, CB_BQ, CB_CQ, CB_CK, CB_CV = 0, 4, 8, 12, 16
CB_AKC, CB_AVC, CB_AKS, CB_AVS, CB_AKW, CB_AVW, CB_BK, CB_BV = 20, 21, 22, 23, 24, 25, 26, 27
N_CB = 28
N_GATE_COLS = N_HEADS * N_BRANCH


def _cparams(*sem):
    return pltpu.CompilerParams(dimension_semantics=sem, vmem_limit_bytes=VMEM_LIMIT)


def _split_hi_lo(v):
    hi = v.astype(BF16)
    return hi, (v - hi.astype(F32)).astype(BF16)


def _dot(a, b):
    return jnp.dot(a, b, preferred_element_type=F32)


def _dot_nt(a, b):
    return lax.dot_general(a, b, (((1,), (1,)), ((), ())), preferred_element_type=F32)


def _dot2(v, m):
    hi, lo = _split_hi_lo(v)
    return _dot(hi, m) + _dot(lo, m)


def _gelu_tanh(x):
    return x * (0.5 * (1.0 + jnp.tanh(math.sqrt(2.0 / math.pi) * (x + 0.044715 * (x * x * x)))))


def _head_rms(y, bd, gain):
    ms = _dot2(y * y, bd) * (1.0 / HEAD_DIM)
    return (y * lax.rsqrt(ms + RMS_EPS)) * gain


def _lane_half_mask(g):
    return (lax.broadcasted_iota(jnp.int32, (1, LANES), 1) // HEAD_DIM) == g


def _stack_group_queries(q_ref, g):
    mask = _lane_half_mask(g)
    return jnp.concatenate([jnp.where(mask, q_ref[p, 0], jnp.zeros_like(q_ref[p, 0]))
                            for p in range(N_QBLK)], axis=0)


def _merge_group_outputs(o_ref, o_g0, o_g1):
    low = _lane_half_mask(0)
    for p in range(N_QBLK):
        rows = slice(p * Q_BLOCK, (p + 1) * Q_BLOCK)
        o_ref[p, 0] = jnp.where(low, o_g0[rows], o_g1[rows]).astype(o_ref.dtype)


def _rmsnorm_kernel(x_ref, g_ref, o_ref):
    x = x_ref[...]
    ms = jnp.mean(x * x, axis=-1, keepdims=True)
    o_ref[...] = ((x * lax.rsqrt(ms + RMS_EPS)) * g_ref[...]).astype(o_ref.dtype)


def _rmsnorm(x2, g, tm=512):
    bt, d = x2.shape
    return pl.pallas_call(
        _rmsnorm_kernel,
        grid=(bt // tm,),
        in_specs=[pl.BlockSpec((tm, d), lambda i: (i, 0)), pl.BlockSpec((1, d), lambda i: (0, 0))],
        out_specs=pl.BlockSpec((tm, d), lambda i: (i, 0)),
        out_shape=jax.ShapeDtypeStruct((bt, d), BF16),
        compiler_params=_cparams("parallel"),
        name="rmsnorm",
    )(x2, g.reshape(1, d))


def _inproj_qkv_kernel(h_ref, w_ref, aux_ref, bd_ref, o_ref, *, n_sub):
    y = _dot(h_ref[...], w_ref[...])
    for c in range(n_sub):
        cols = slice(c * LANES, (c + 1) * LANES)
        yb = y[:, cols]
        normed = _head_rms(yb, bd_ref[...], aux_ref[0:1, cols])
        o_ref[c] = jnp.where(aux_ref[1:2, cols] > 0.5, normed, yb).astype(o_ref.dtype)


def _inproj_qkv(h, w, aux, bd, tm=1024, tn=512):
    bt, d = h.shape
    n = w.shape[1]
    n_sub = tn // LANES
    return pl.pallas_call(
        functools.partial(_inproj_qkv_kernel, n_sub=n_sub),
        grid=(bt // tm, n // tn),
        in_specs=[pl.BlockSpec((tm, d), lambda i, j: (i, 0)),
                  pl.BlockSpec((d, tn), lambda i, j: (0, j)),
                  pl.BlockSpec((8, tn), lambda i, j: (0, j)),
                  pl.BlockSpec((LANES, LANES), lambda i, j: (0, 0))],
        out_specs=pl.BlockSpec((n_sub, tm, LANES), lambda i, j: (j, i, 0)),
        out_shape=jax.ShapeDtypeStruct((n // LANES, bt, LANES), BF16),
        compiler_params=_cparams("parallel", "arbitrary"),
        name="inproj_qkv",
    )(h, w, aux, bd)


def _inproj_gates_kernel(h_ref, w_ref, eg_ref, raw_ref, ga_ref):
    y = _dot(h_ref[...], w_ref[...])
    raw_ref[...] = y
    ga = _dot2(jax.nn.sigmoid(y), eg_ref[...])
    for n in range(N_BRANCH):
        ga_ref[n] = ga[:, n * MIX_WIDTH:(n + 1) * MIX_WIDTH].astype(ga_ref.dtype)


def _inproj_gates(h, w, eg, tm=1024):
    bt, d = h.shape
    return pl.pallas_call(
        _inproj_gates_kernel,
        grid=(bt // tm,),
        in_specs=[pl.BlockSpec((tm, d), lambda i: (i, 0)),
                  pl.BlockSpec((d, LANES), lambda i: (0, 0)),
                  pl.BlockSpec((LANES, N_BRANCH * MIX_WIDTH), lambda i: (0, 0))],
        out_specs=[pl.BlockSpec((tm, LANES), lambda i: (i, 0)),
                   pl.BlockSpec((N_BRANCH, tm, MIX_WIDTH), lambda i: (0, i, 0))],
        out_shape=[jax.ShapeDtypeStruct((bt, LANES), F32),
                   jax.ShapeDtypeStruct((N_BRANCH, bt, MIX_WIDTH), BF16)],
        compiler_params=_cparams("parallel"),
        name="inproj_gates",
    )(h, w, eg)


def _inproj_merge_kernel(h_ref, w_ref, o_ref):
    o_ref[0] = jax.nn.sigmoid(_dot(h_ref[...], w_ref[...])).astype(o_ref.dtype)


def _inproj_merge(h, w, d_model, tm=1024, tn=512):
    bt, d = h.shape
    n = w.shape[1]
    per = d_model // tn
    return pl.pallas_call(
        _inproj_merge_kernel,
        grid=(bt // tm, n // tn),
        in_specs=[pl.BlockSpec((tm, d), lambda i, j: (i, 0)),
                  pl.BlockSpec((d, tn), lambda i, j: (0, j))],
        out_specs=pl.BlockSpec((1, tm, tn), lambda i, j: (j // per, i, j % per)),
        out_shape=jax.ShapeDtypeStruct((n // d_model, bt, d_model), BF16),
        compiler_params=_cparams("parallel", "arbitrary"),
        name="inproj_merge",
    )(h, w)


def _compress_kernel(c_ref, pos_ref, w1_ref, w2_ref, aux_ref, bd_ref, o_ref):
    n_chunks = c_ref.shape[3]
    y = jnp.zeros((n_chunks, LANES), F32)
    for g in range(KV_GROUPS):
        c = c_ref[0, 0, g].astype(F32)
        a = _dot((c + pos_ref[0, 0]).astype(BF16), w1_ref[0, 0])
        b = _dot((c + pos_ref[0, 1]).astype(BF16), w1_ref[0, 1])
        hid = _gelu_tanh(a + pltpu.roll(b, n_chunks - 1, 0))
        y = y + _dot(hid.astype(BF16), w2_ref[0, g])
    normed = _head_rms(y, bd_ref[...], aux_ref[0, 0:1, :])
    o_ref[0, 0] = jnp.where(aux_ref[0, 1:2, :] > 0.5, normed, y).astype(o_ref.dtype)


def _compress(chunks, pos, w1, w2p, aux, bd):
    _, b, g, n_chunks, cw = chunks.shape
    return pl.pallas_call(
        _compress_kernel,
        grid=(2, b),
        in_specs=[pl.BlockSpec((1, 1, g, n_chunks, cw), lambda w, i: (w, i, 0, 0, 0)),
                  pl.BlockSpec((1, 2, 1, cw), lambda w, i: (w, 0, 0, 0)),
                  pl.BlockSpec((1, 2, cw, CMP_HIDDEN), lambda w, i: (w, 0, 0, 0)),
                  pl.BlockSpec((1, g, CMP_HIDDEN, LANES), lambda w, i: (w, 0, 0, 0)),
                  pl.BlockSpec((1, 8, LANES), lambda w, i: (w, 0, 0)),
                  pl.BlockSpec((LANES, LANES), lambda w, i: (0, 0))],
        out_specs=pl.BlockSpec((1, 1, n_chunks, LANES), lambda w, i: (w, i, 0, 0)),
        out_shape=jax.ShapeDtypeStruct((2, b, n_chunks, LANES), BF16),
        compiler_params=_cparams("parallel", "parallel"),
        name="nsa_compress",
    )(chunks, pos, w1, w2p, aux, bd)


def _cmp_select_kernel(q_ref, kc_ref, vc_ref, bias_ref, ovt_ref, o_ref, qa_ref, *, n_slc, n_top):
    i = pl.program_id(1)
    ncp = kc_ref.shape[2]
    row_t = i * Q_BLOCK + lax.broadcasted_iota(jnp.int32, (Q_BLOCK, 1), 0)
    row_valid = jnp.concatenate([row_t >= CMP_LEN - 1] * HEADS_PER_GROUP, axis=0)
    jrow = lax.broadcasted_iota(jnp.int32, (n_slc, Q_BLOCK), 0)
    cur = (i * Q_BLOCK + lax.broadcasted_iota(jnp.int32, (n_slc, Q_BLOCK), 1)) // SLC_LEN
    block_valid = jrow <= cur

    def force(imp):
        forced = jnp.where(jrow == cur, FORCE_SCORE, jnp.where(jrow == cur - 1, FORCE_SCORE, imp))
        return jnp.where(jrow == 0, FORCE_SCORE, forced)
    outs = []
    for g in range(KV_GROUPS):
        qop = _stack_group_queries(q_ref, g)
        rows = HEADS_PER_GROUP * Q_BLOCK
        s = _dot_nt(qop, kc_ref[0, 0]) + bias_ref[g * HEADS_PER_GROUP:(g + 1) * HEADS_PER_GROUP].reshape(rows, ncp)
        e = jnp.exp2(s - jnp.max(s, axis=-1, keepdims=True))
        inv = jnp.where(row_valid, 1.0 / jnp.sum(e, axis=-1, keepdims=True), 0.0)
        p = e * inv
        outs.append(_dot(p.astype(BF16), vc_ref[0, 0]))
        psum = p[0:Q_BLOCK]
        for r in range(1, HEADS_PER_GROUP):
            psum = psum + p[r * Q_BLOCK:(r + 1) * Q_BLOCK]
        hi, lo = _split_hi_lo(psum)
        imp_t = (_dot_nt(ovt_ref[...], hi) + _dot_nt(ovt_ref[...], lo))[0:n_slc]
        score = jnp.where(block_valid, force(imp_t), NEG)
        groups = [score[a:a + 8] for a in range(0, n_slc, 8)]
        ranks = [jnp.zeros((8, Q_BLOCK), F32) for _ in groups]
        sub = lax.broadcasted_iota(jnp.int32, (8, Q_BLOCK), 0)
        for j in range(n_slc):
            other = score[j:j + 1, :]
            for a, blk in enumerate(groups):
                ge = jnp.where(other >= blk, 1.0, 0.0)
                if 8 * a > j:
                    ranks[a] = ranks[a] + ge
                elif 8 * a + 7 <= j:
                    ranks[a] = ranks[a] + jnp.where(other > blk, 1.0, 0.0)
                else:
                    ranks[a] = ranks[a] + jnp.where(sub > j - 8 * a, ge, jnp.where(other > blk, 1.0, 0.0))
        rank = jnp.concatenate(ranks, axis=0)
        sel_bias = jnp.where(rank < n_top, jnp.where(score > NEG / 2, 0.0, NEG), NEG)
        if n_slc < HEAD_DIM:
            sel_bias = jnp.concatenate([sel_bias, jnp.full((HEAD_DIM - n_slc, Q_BLOCK), NEG, F32)], axis=0)
        sel_t = jnp.concatenate([sel_bias, sel_bias], axis=0).T.astype(qa_ref.dtype)
        own_half = _lane_half_mask(g)
        for p in range(N_QBLK):
            qa_ref[g * HEADS_PER_GROUP + p, 0] = jnp.where(own_half, q_ref[p, 0], sel_t)
    _merge_group_outputs(o_ref, outs[0], outs[1])


def _cmp_select(qkv, cmp_kv, bias_c, ovt, n_slc):
    _, b, t, _ = qkv.shape
    nt = t // Q_BLOCK
    ncp = cmp_kv.shape[2]
    assert n_slc <= HEAD_DIM
    return pl.pallas_call(
        functools.partial(_cmp_select_kernel, n_slc=n_slc, n_top=min(SLC_TOPK, n_slc)),
        grid=(b, nt),
        in_specs=[pl.BlockSpec((N_QBLK, 1, Q_BLOCK, LANES), lambda bi, i: (CB_AQ // N_QBLK, bi, i, 0)),
                  pl.BlockSpec((1, 1, ncp, LANES), lambda bi, i: (0, bi, 0, 0)),
                  pl.BlockSpec((1, 1, ncp, LANES), lambda bi, i: (1, bi, 0, 0)),
                  pl.BlockSpec((N_HEADS, Q_BLOCK, ncp), lambda bi, i: (0, i, 0)),
                  pl.BlockSpec((LANES, ncp), lambda bi, i: (0, 0))],
        out_specs=[pl.BlockSpec((N_QBLK, 1, Q_BLOCK, LANES), lambda bi, i: (0, bi, i, 0)),
                   pl.BlockSpec((N_HEADS, 1, Q_BLOCK, LANES), lambda bi, i: (0, bi, i, 0))],
        out_shape=[jax.ShapeDtypeStruct((N_QBLK, b, t, LANES), BF16),
                   jax.ShapeDtypeStruct((N_HEADS, b, t, LANES), BF16)],
        compiler_params=_cparams("parallel", "parallel"),
        name="nsa_cmp_select",
    )(qkv, cmp_kv, cmp_kv, bias_c, ovt)


def _flash_update(s, m, acc, v):
    m_new = jnp.maximum(m, jnp.max(s, axis=-1, keepdims=True))
    p = jnp.exp2(s - m_new)
    return m_new, jnp.exp2(m - m_new) * acc + _dot(p.astype(BF16), v)


def _normalise(acc):
    return acc / pltpu.roll(acc, HEAD_DIM, 1)


def _slc_kernel(qa_ref, k_ref, v_ref, nb_ref, o_ref):
    i = pl.program_id(1)
    par = i % 2
    diag = i // 2
    rows = HEADS_PER_GROUP * Q_BLOCK
    outs = []
    for g in range(KV_GROUPS):
        hs = slice(g * HEADS_PER_GROUP, (g + 1) * HEADS_PER_GROUP)
        qop = qa_ref[hs, 0].reshape(rows, LANES)

        def step(j, carry, bias=None, pen=None):
            start = pl.multiple_of(j * SLC_TILE, SLC_TILE)
            s = _dot_nt(qop, k_ref[g, 0, pl.ds(start, SLC_TILE), :])
            if bias is not None:
                s = (s.reshape(HEADS_PER_GROUP, Q_BLOCK, SLC_TILE) + bias).reshape(rows, SLC_TILE)
            if pen is not None:
                s = s + pen
            return _flash_update(s, *carry, v_ref[g, 0, pl.ds(start, SLC_TILE), :])

        carry = (jnp.full((rows, 1), NEG, F32), jnp.zeros((rows, LANES), F32))
        carry = lax.fori_loop(0, jnp.maximum(diag - 1, 0), step, carry)
        pen = jnp.where(diag >= 1, 0.0, NEG)
        carry = step(jnp.maximum(diag - 1, 0), carry, bias=nb_ref[par, 0, hs], pen=pen)
        _, acc = step(diag, carry, bias=nb_ref[par, 1, hs])
        outs.append(_normalise(acc))
    _merge_group_outputs(o_ref, outs[0], outs[1])


def _slc_attention(qa, ka, va, near_bias):
    _, b, t, _ = qa.shape
    nt = t // Q_BLOCK
    return pl.pallas_call(
        _slc_kernel,
        grid=(b, nt),
        in_specs=[pl.BlockSpec((N_HEADS, 1, Q_BLOCK, LANES), lambda bi, i: (0, bi, i, 0)),
                  pl.BlockSpec((KV_GROUPS, 1, t, LANES), lambda bi, i: (0, bi, 0, 0)),
                  pl.BlockSpec((KV_GROUPS, 1, t, LANES), lambda bi, i: (0, bi, 0, 0)),
                  pl.BlockSpec(near_bias.shape, lambda bi, i: (0, 0, 0, 0, 0))],
        out_specs=pl.BlockSpec((N_QBLK, 1, Q_BLOCK, LANES), lambda bi, i: (0, bi, i, 0)),
        out_shape=jax.ShapeDtypeStruct((N_QBLK, b, t, LANES), BF16),
        compiler_params=_cparams("parallel", "parallel"),
        name="nsa_slc_attention",
    )(qa, ka, va, near_bias)


def _banded_kernel(*refs, n_blk, use_sinks):
    q_ref = refs[0]
    k_refs = refs[1:1 + n_blk]
    v_refs = refs[1 + n_blk:1 + 2 * n_blk]
    bias_ref, sink_ref, o_ref = refs[1 + 2 * n_blk:]
    i = pl.program_id(1)
    n_keys = n_blk * Q_BLOCK
    rows = HEADS_PER_GROUP * Q_BLOCK
    key_pos = (i - (n_blk - 1)) * Q_BLOCK + lax.broadcasted_iota(jnp.int32, (1, n_keys), 1)
    pad_pen = jnp.where(key_pos >= 0, 0.0, NEG)
    kcat = jnp.concatenate([r[0, 0] for r in k_refs], axis=0)
    vcat = jnp.concatenate([r[0, 0] for r in v_refs], axis=0)
    outs = []
    for g in range(KV_GROUPS):
        qop = _stack_group_queries(q_ref, g)
        hs = slice(g * HEADS_PER_GROUP, (g + 1) * HEADS_PER_GROUP)
        s = _dot_nt(qop, kcat) + bias_ref[hs].reshape(rows, n_keys) + pad_pen
        m = jnp.max(s, axis=-1, keepdims=True)
        if use_sinks:
            sink = jnp.concatenate([jnp.broadcast_to(sink_ref[h:h + 1, 0:1], (Q_BLOCK, 1))
                                    for h in range(hs.start, hs.stop)], axis=0)
            m = jnp.maximum(m, sink)
        e = jnp.exp2(s - m)
        l = jnp.sum(e, axis=-1, keepdims=True)
        if use_sinks:
            l = l + jnp.exp2(sink - m)
        outs.append(_dot(e.astype(BF16), vcat) / l)
    _merge_group_outputs(o_ref, outs[0], outs[1])


def _banded_attention(qkv, cb_q, cb_k, cb_v, bias, sinks, use_sinks):
    _, b, t, _ = qkv.shape
    nt = t // Q_BLOCK
    n_blk = bias.shape[2] // Q_BLOCK

    def kv_spec(cb, back):
        return pl.BlockSpec((1, 1, Q_BLOCK, LANES), lambda bi, i: (cb, bi, jnp.maximum(i - back, 0), 0))

    backs = [n_blk - 1 - jb for jb in range(n_blk)]
    return pl.pallas_call(
        functools.partial(_banded_kernel, n_blk=n_blk, use_sinks=use_sinks),
        grid=(b, nt),
        in_specs=([pl.BlockSpec((N_QBLK, 1, Q_BLOCK, LANES), lambda bi, i: (cb_q // N_QBLK, bi, i, 0))]
                  + [kv_spec(cb_k, back) for back in backs] + [kv_spec(cb_v, back) for back in backs]
                  + [pl.BlockSpec(bias.shape, lambda bi, i: (0, 0, 0)),
                     pl.BlockSpec(sinks.shape, lambda bi, i: (0, 0))]),
        out_specs=pl.BlockSpec((N_QBLK, 1, Q_BLOCK, LANES), lambda bi, i: (0, bi, i, 0)),
        out_shape=jax.ShapeDtypeStruct((N_QBLK, b, t, LANES), BF16),
        compiler_params=_cparams("parallel", "parallel"),
        name="banded_attention",
    )(qkv, *([qkv] * (2 * n_blk)), bias, sinks)


def _fox_cumsum_kernel(f_ref, fb_ref, o_ref, *, chunk):
    x = f_ref[0] + fb_ref[...]
    logf = jnp.minimum(x, 0.0) - jnp.log(1.0 + jnp.exp(-jnp.abs(x)))
    t = x.shape[1]
    upper = jnp.where(lax.broadcasted_iota(jnp.int32, (chunk, chunk), 0)
                      <= lax.broadcasted_iota(jnp.int32, (chunk, chunk), 1), 1.0, 0.0).astype(BF16)
    carry = jnp.zeros((x.shape[0], 1), F32)
    for c in range(t // chunk):
        blk = logf[:, c * chunk:(c + 1) * chunk]
        hi, rest = blk.astype(BF16), blk - blk.astype(BF16).astype(F32)
        mid, lo = _split_hi_lo(rest)
        cs = (_dot(hi, upper) + _dot(mid, upper) + _dot(lo, upper)) + carry
        o_ref[0, :, c * chunk:(c + 1) * chunk] = cs
        carry = cs[:, chunk - 1:chunk]


def _fox_cumsum(f_t, fb, chunk=512):
    b, h, t = f_t.shape
    chunk = min(chunk, t)
    return pl.pallas_call(
        functools.partial(_fox_cumsum_kernel, chunk=chunk),
        grid=(b,),
        in_specs=[pl.BlockSpec((1, h, t), lambda i: (i, 0, 0)), pl.BlockSpec((h, 1), lambda i: (0, 0))],
        out_specs=pl.BlockSpec((1, h, t), lambda i: (i, 0, 0)),
        out_shape=jax.ShapeDtypeStruct((b, h, t), F32),
        compiler_params=_cparams("parallel"),
        name="fox_cumsum",
    )(f_t, fb)


def _fox_kernel(q_ref, k_ref, v_ref, o_ref, *, tq):
    i = pl.program_id(2)
    causal_pen = jnp.where(lax.broadcasted_iota(jnp.int32, (tq, tq), 0)
                           >= lax.broadcasted_iota(jnp.int32, (tq, tq), 1), 0.0, NEG)
    halves = [pl.ds(hh * LANES, LANES) for hh in range(2)]
    qs = [q_ref[0, 0, :, lanes] for lanes in halves]

    def step(j, carry, pen=None):
        start = pl.multiple_of(j * tq, tq)
        new = []
        for hh, lanes in enumerate(halves):
            s = _dot_nt(qs[hh], k_ref[0, 0, pl.ds(start, tq), lanes])
            if pen is not None:
                s = s + pen
            new.append(_flash_update(s, *carry[hh], v_ref[0, 0, pl.ds(start, tq), lanes]))
        return tuple(new)

    init = (jnp.full((tq, 1), NEG, F32), jnp.zeros((tq, LANES), F32))
    carry = lax.fori_loop(0, i, step, (init, init))
    (_, acc0), (_, acc1) = step(i, carry, pen=causal_pen)
    o_ref[0, 0] = jnp.where(_lane_half_mask(0), _normalise(acc0),
                            pltpu.roll(_normalise(acc1), HEAD_DIM, 1)).astype(o_ref.dtype)


def _fox_attention(qa, ka, va, tq):
    pairs, b, t, width = qa.shape
    return pl.pallas_call(
        functools.partial(_fox_kernel, tq=tq),
        grid=(b, pairs, t // tq),
        in_specs=[pl.BlockSpec((1, 1, tq, width), lambda bi, p, i: (p, bi, i, 0)),
                  pl.BlockSpec((1, 1, t, width), lambda bi, p, i: (p, bi, 0, 0)),
                  pl.BlockSpec((1, 1, t, width), lambda bi, p, i: (p, bi, 0, 0))],
        out_specs=pl.BlockSpec((1, 1, tq, LANES), lambda bi, p, i: (p, bi, i, 0)),
        out_shape=jax.ShapeDtypeStruct((pairs, b, t, LANES), BF16),
        compiler_params=_cparams("parallel", "parallel", "parallel"),
        name="fox_attention",
    )(qa, ka, va)


def _merge_kernel(ocmp_ref, oslc_ref, owin_ref, ob_ref, oc_ref, ga_ref, mg_ref, wb_ref, wo_ref, x_ref, o_ref):
    def cat(ref):
        return jnp.concatenate([ref[p] for p in range(N_QBLK)], axis=-1)

    oa = (ga_ref[0].astype(F32) * cat(ocmp_ref).astype(F32)
          + ga_ref[1].astype(F32) * cat(oslc_ref).astype(F32)
          + ga_ref[2].astype(F32) * cat(owin_ref).astype(F32)).astype(BF16)
    mix = (mg_ref[0].astype(F32) * _dot(oa, wb_ref[0])
           + mg_ref[1].astype(F32) * _dot(cat(ob_ref), wb_ref[1])
           + mg_ref[2].astype(F32) * _dot(cat(oc_ref), wb_ref[2]))
    o_ref[...] = x_ref[...] + _dot(mix.astype(BF16), wo_ref[...])


def _merge(ocmp, oslc, owin, ob, oc, ga, mg, wb, wo, x2, tm=512):
    bt, d = x2.shape
    o_spec = pl.BlockSpec((N_QBLK, tm, LANES), lambda i: (0, i, 0))
    return pl.pallas_call(
        _merge_kernel,
        grid=(bt // tm,),
        in_specs=[o_spec] * 5 + [pl.BlockSpec((N_BRANCH, tm, MIX_WIDTH), lambda i: (0, i, 0)),
                                 pl.BlockSpec((N_BRANCH, tm, d), lambda i: (0, i, 0)),
                                 pl.BlockSpec(wb.shape, lambda i: (0, 0, 0)),
                                 pl.BlockSpec(wo.shape, lambda i: (0, 0)),
                                 pl.BlockSpec((tm, d), lambda i: (i, 0))],
        out_specs=pl.BlockSpec((tm, d), lambda i: (i, 0)),
        out_shape=jax.ShapeDtypeStruct((bt, d), F32),
        compiler_params=_cparams("parallel"),
        name="branch_merge",
    )(ocmp, oslc, owin, ob, oc, ga, mg, wb, wo, x2)


def _ffn_kernel(x_ref, xh_ref, g_ref, wg_ref, wu_ref, cw_ref, cb_ref, wd_ref, o_ref,
                h_scr, hh_scr, a_scr, acc_scr, *, tm, tiles_per_seq):
    i = pl.program_id(0)
    f = pl.program_id(1)

    def norm(x):
        ms = jnp.mean(x * x, axis=-1, keepdims=True)
        return ((x * lax.rsqrt(ms + RMS_EPS)) * g_ref[...]).astype(BF16)

    @pl.when(f == 0)
    def _():
        h_scr[...] = norm(x_ref[...])
        hh_scr[...] = norm(xh_ref[...])
        acc_scr[...] = jnp.zeros_like(acc_scr)

    a = _dot(h_scr[...], wg_ref[...])
    halo = _dot(hh_scr[...], wg_ref[...]) * jnp.where(i % tiles_per_seq == 0, 0.0, 1.0)
    a_scr[0:8, :] = halo
    a_scr[8:8 + tm, :] = a
    conv = (cw_ref[0:1, :] * a_scr[6:6 + tm, :] + cw_ref[1:2, :] * a_scr[7:7 + tm, :]
            + cw_ref[2:3, :] * a + cb_ref[...])
    act = _gelu_tanh(conv) * _dot(h_scr[...], wu_ref[...])
    acc_scr[...] += _dot(act.astype(BF16), wd_ref[...])

    @pl.when(f == pl.num_programs(1) - 1)
    def _():
        o_ref[...] = x_ref[...] + acc_scr[...]


def _ffn(x2, g, wg, wu, cw, cb, wd, t, tm=1024, tf=256):
    bt, d = x2.shape
    ff = wg.shape[1]
    halo_blocks = tm // 8
    return pl.pallas_call(
        functools.partial(_ffn_kernel, tm=tm, tiles_per_seq=t // tm),
        grid=(bt // tm, ff // tf),
        in_specs=[pl.BlockSpec((tm, d), lambda i, f: (i, 0)),
                  pl.BlockSpec((8, d), lambda i, f: (jnp.maximum(i * halo_blocks - 1, 0), 0)),
                  pl.BlockSpec((1, d), lambda i, f: (0, 0)),
                  pl.BlockSpec((d, tf), lambda i, f: (0, f)),
                  pl.BlockSpec((d, tf), lambda i, f: (0, f)),
                  pl.BlockSpec((CONV_WIDTH, tf), lambda i, f: (0, f)),
                  pl.BlockSpec((1, tf), lambda i, f: (0, f)),
                  pl.BlockSpec((tf, d), lambda i, f: (f, 0))],
        out_specs=pl.BlockSpec((tm, d), lambda i, f: (i, 0)),
        out_shape=jax.ShapeDtypeStruct((bt, d), F32),
        scratch_shapes=[pltpu.VMEM((tm, d), BF16), pltpu.VMEM((8, d), BF16),
                        pltpu.VMEM((tm + 8, tf), F32), pltpu.VMEM((tm, d), F32)],
        compiler_params=_cparams("parallel", "arbitrary"),
        name="conv_ffn",
    )(x2, x2, g.reshape(1, d), wg, wu, cw, cb.reshape(1, ff), wd)


def _t5_bucket_np(dist):
    max_exact = N_BUCKETS // 2
    d = np.maximum(dist, 0)
    ratio = np.log(np.maximum(d, 1) / max_exact) / math.log(MAX_DISTANCE / max_exact)
    large = np.minimum(max_exact + (ratio * (N_BUCKETS - max_exact)).astype(np.int64), N_BUCKETS - 1)
    return np.where(d < max_exact, d, large)


def _expand_kernel(tab_ref, idx_ref, o_ref):
    tab = tab_ref[...]
    onehot = jnp.where(lax.broadcasted_iota(jnp.int32, (tab.shape[1], idx_ref.shape[1]), 0) == idx_ref[...],
                       1.0, 0.0).astype(BF16)
    hi, rest = tab.astype(BF16), tab - tab.astype(BF16).astype(F32)
    mid, lo = _split_hi_lo(rest)
    o_ref[...] = _dot(hi, onehot) + _dot(mid, onehot) + _dot(lo, onehot)


def _bias_from_dist(tab_t, dist, valid, chunk=4096):
    heads = tab_t.shape[0]
    tab = jnp.zeros((heads, LANES), F32).at[:, :N_BUCKETS].set(tab_t).at[:, N_BUCKETS].set(NEG)
    idx = np.where(valid, _t5_bucket_np(dist), N_BUCKETS).reshape(1, -1).astype(np.int32)
    n = idx.shape[1]
    chunk = min(chunk, n)
    out = pl.pallas_call(
        _expand_kernel,
        grid=(n // chunk,),
        in_specs=[pl.BlockSpec((heads, LANES), lambda i: (0, 0)), pl.BlockSpec((1, chunk), lambda i: (0, i))],
        out_specs=pl.BlockSpec((heads, chunk), lambda i: (0, i)),
        out_shape=jax.ShapeDtypeStruct((heads, n), F32),
        compiler_params=_cparams("parallel"),
        name="rel_bias_expand",
    )(tab, jnp.asarray(idx))
    return out.reshape((heads,) + dist.shape)


def _cmp_bias_kernel(g_ref, o_ref):
    i = pl.program_id(0)
    ncp = o_ref.shape[2]
    for h in range(o_ref.shape[0]):
        o_ref[h] = pltpu.roll(g_ref[h], i * (Q_BLOCK // CMP_STRIDE), 1)[:, :ncp]


def _cmp_bias(g_tab, t):
    heads, _, width = g_tab.shape
    ncp = t // CMP_STRIDE
    return pl.pallas_call(
        _cmp_bias_kernel,
        grid=(t // Q_BLOCK,),
        in_specs=[pl.BlockSpec(g_tab.shape, lambda i: (0, 0, 0))],
        out_specs=pl.BlockSpec((heads, Q_BLOCK, ncp), lambda i: (0, i, 0)),
        out_shape=jax.ShapeDtypeStruct((heads, t, ncp), F32),
        compiler_params=_cparams("parallel"),
        name="cmp_bias_build",
    )(g_tab)


def _static_tables(t):
    ncp = t // CMP_STRIDE
    n_cmp = (t - CMP_LEN) // CMP_STRIDE + 1
    n_slc = t // SLC_LEN
    lane = np.arange(LANES)
    bd = (lane[:, None] // HEAD_DIM == lane[None, :] // HEAD_DIM).astype(np.float32)
    eg = np.zeros((LANES, N_BRANCH * MIX_WIDTH), np.float32)
    for h in range(N_HEADS):
        base = (h % N_QBLK) * LANES + (h // N_QBLK) * HEAD_DIM
        for n in range(N_BRANCH):
            eg[h * N_BRANCH + n, n * MIX_WIDTH + base:n * MIX_WIDTH + base + HEAD_DIM] = 1.0
    c_start = np.arange(n_cmp) * CMP_STRIDE
    s_start = np.arange(n_slc) * SLC_LEN
    overlap = ((c_start[:, None] < s_start[None, :] + SLC_LEN) & (c_start[:, None] + CMP_LEN > s_start[None, :]))
    ovt = np.zeros((LANES, ncp), np.float32)
    ovt[:n_slc, :n_cmp] = overlap.T
    key_block = np.zeros((t, HEAD_DIM), np.float32)
    key_block[np.arange(t), np.arange(t) // SLC_LEN] = 1.0
    as_bf16 = lambda a: jnp.asarray(a, BF16)
    return dict(bd=as_bf16(bd), eg=as_bf16(eg), ovt=as_bf16(ovt), key_block=as_bf16(key_block), n_slc=n_slc)


def _bias_tables(rel_bias, t):
    n_cmp = (t - CMP_LEN) // CMP_STRIDE + 1
    ncp = t // CMP_STRIDE
    tab_a = rel_bias[:, :N_HEADS].T * LOG2E
    tab_b = rel_bias[:, N_HEADS:].T * LOG2E
    r = np.arange(Q_BLOCK)[:, None]

    def band(window):
        n_prev = -(-(window - 1) // Q_BLOCK)
        dist = n_prev * Q_BLOCK + r - np.arange((n_prev + 1) * Q_BLOCK)[None, :]
        return dist, (dist >= 0) & (dist < window)

    m = np.arange(2 * ncp)
    m = np.where(m < ncp, m, m - 2 * ncp)[None, :]
    dist_c = r - (m * CMP_STRIDE + CMP_LEN - 1)
    assert n_cmp == ncp - 1 and (dist_c[:, m[0] == Q_BLOCK // CMP_STRIDE - 1] < 0).all()
    c = np.arange(SLC_TILE)[None, :]
    dist_n = np.stack([np.stack([SLC_TILE * (1 - which) + Q_BLOCK * par + r - c for which in range(2)])
                       for par in range(2)])
    near = _bias_from_dist(tab_a - tab_a[:, N_BUCKETS - 1:], dist_n, dist_n >= 0)
    return dict(win=_bias_from_dist(tab_a, *band(NSA_WINDOW)),
                swa=_bias_from_dist(tab_b, *band(SWA_WINDOW)),
                cmp=_cmp_bias(_bias_from_dist(tab_a, dist_c, dist_c >= 0), t),
                near=jnp.transpose(near, (1, 2, 0, 3, 4)))


def _pair_cols(w):
    parts = []
    for p in range(N_QBLK):
        parts += [w[..., p * HEAD_DIM:(p + 1) * HEAD_DIM],
                  w[..., (N_QBLK + p) * HEAD_DIM:(N_QBLK + p + 1) * HEAD_DIM]]
    return jnp.concatenate(parts, axis=-1)


def _prep_layer(w_in, qk_gain, w_branch):
    kv = KV_GROUPS * HEAD_DIM
    widths = [('a_q', MIX_WIDTH), ('a_kc', kv), ('a_vc', kv), ('a_ks', kv), ('a_vs', kv), ('a_kw', kv), ('a_vw', kv),
              ('a_gate', N_GATE_COLS), ('b_q', MIX_WIDTH), ('b_k', kv), ('b_v', kv),
              ('c_q', MIX_WIDTH), ('c_k', MIX_WIDTH), ('c_v', MIX_WIDTH), ('c_f', N_HEADS),
              ('merge', N_BRANCH * w_in.shape[0])]
    cols, off = {}, 0
    for name, w in widths:
        cols[name] = w_in[:, off:off + w]
        off += w
    scale = HEAD_DIM ** -0.5 * LOG2E
    tile = lambda gvec, n: jnp.tile(gvec, n)
    zeros = lambda n: jnp.zeros((n,), F32)
    ones = lambda n: jnp.ones((n,), F32)
    pieces = [(_pair_cols(cols['a_q']), tile(qk_gain[0] * scale, N_HEADS), ones(MIX_WIDTH)),
              (_pair_cols(cols['b_q']), tile(qk_gain[2] * scale, N_HEADS), ones(MIX_WIDTH)),
              (cols['c_q'], tile(qk_gain[4] * scale, N_HEADS), ones(MIX_WIDTH)),
              (cols['c_k'], tile(qk_gain[5], N_HEADS), ones(MIX_WIDTH)),
              (cols['c_v'], zeros(MIX_WIDTH), zeros(MIX_WIDTH)),
              (cols['a_kc'], zeros(kv), zeros(kv)), (cols['a_vc'], zeros(kv), zeros(kv)),
              (cols['a_ks'], tile(qk_gain[1], KV_GROUPS), ones(kv)), (cols['a_vs'], zeros(kv), zeros(kv)),
              (cols['a_kw'], tile(qk_gain[1], KV_GROUPS), ones(kv)), (cols['a_vw'], zeros(kv), zeros(kv)),
              (cols['b_k'], tile(qk_gain[3], KV_GROUPS), ones(kv)), (cols['b_v'], zeros(kv), zeros(kv))]
    w_qkv = jnp.concatenate([p[0] for p in pieces], axis=1).astype(BF16)
    n_qkv = w_qkv.shape[1]
    aux = jnp.zeros((8, n_qkv), F32)
    aux = aux.at[0].set(jnp.concatenate([p[1] for p in pieces])).at[1].set(jnp.concatenate([p[2] for p in pieces]))
    d = w_in.shape[0]
    w_gates = jnp.concatenate([cols['a_gate'], cols['c_f'],
                               jnp.zeros((d, LANES - N_GATE_COLS - N_HEADS), F32)], axis=1).astype(BF16)
    wb = jnp.stack([_pair_cols(w_branch[0].T).T, _pair_cols(w_branch[1].T).T, w_branch[2]]).astype(BF16)
    return w_qkv, aux, w_gates, cols['merge'].astype(BF16), wb


def _prep_compress(cmp_pos, cmp_w1, cmp_w2, k_gain):
    half = CMP_STRIDE * HEAD_DIM
    pos = cmp_pos.reshape(2, 2, 1, half)
    w1 = cmp_w1.reshape(2, 2, half, CMP_HIDDEN).astype(BF16)
    z = jnp.zeros_like(cmp_w2)
    w2p = jnp.stack([jnp.concatenate([cmp_w2, z], axis=-1), jnp.concatenate([z, cmp_w2], axis=-1)], axis=1).astype(BF16)
    aux = jnp.zeros((2, 8, LANES), F32)
    aux = aux.at[0, 0].set(jnp.tile(k_gain, KV_GROUPS)).at[0, 1].set(1.0)
    return pos, w1, w2p, aux


def _slc_keys_values(ks, vs, key_block):
    b, t, _ = ks.shape
    hot = jnp.broadcast_to(key_block[None], (b, t, HEAD_DIM))
    ones = jnp.ones((b, t, HEAD_DIM), ks.dtype)
    lo, hi = slice(0, HEAD_DIM), slice(HEAD_DIM, LANES)
    ka = jnp.stack([jnp.concatenate([ks[..., lo], hot], -1), jnp.concatenate([hot, ks[..., hi]], -1)])
    va = jnp.stack([jnp.concatenate([vs[..., lo], ones], -1), jnp.concatenate([ones, vs[..., hi]], -1)])
    return ka, va


def _fox_operands(qkv, c_row):
    _, b, t, _ = qkv.shape
    c2 = jnp.transpose(c_row, (0, 2, 1)) * LOG2E
    round_bf16 = lambda v: lax.reduce_precision(v, exponent_bits=8, mantissa_bits=7)
    hi = round_bf16(c2)
    mid = round_bf16(c2 - hi)
    lo = (c2 - hi - mid).astype(BF16)
    hi, mid = hi.astype(BF16), mid.astype(BF16)
    one = jnp.ones_like(hi)
    pad = jnp.zeros((b, t, N_HEADS, HEAD_DIM - 6), BF16)
    extra_q = jnp.concatenate([jnp.stack([hi, mid, lo, one, one, one], -1), pad], -1)
    extra_k = jnp.concatenate([jnp.stack([one, one, one, -hi, -mid, -lo], -1), pad], -1)
    extra_v = jnp.ones((b, t, N_HEADS, HEAD_DIM), BF16)

    def widen(cb, extra):
        x = qkv[cb:cb + N_QBLK].reshape(N_QBLK, b, t, 2, HEAD_DIM)
        e = jnp.transpose(extra.reshape(b, t, N_QBLK, 2, HEAD_DIM), (2, 0, 1, 3, 4))
        return jnp.concatenate([x, e], -1).reshape(N_QBLK, b, t, 2 * LANES)

    return widen(CB_CQ, extra_q), widen(CB_CK, extra_k), widen(CB_CV, extra_v)


def kernel(x, rel_bias, norm_mix, norm_ffn, w_in, forget_bias, qk_gain, cmp_pos, cmp_w1, cmp_w2, sinks, w_branch, w_out, w_gate, w_up, conv_w, conv_b, w_down):
    b, t, d = x.shape
    bt = b * t
    depth = w_in.shape[0]
    tabs = _static_tables(t)
    biases = _bias_tables(rel_bias, t)
    n_chunks = t // CMP_STRIDE
    fox_tq = min(512, t)
    x2 = x.reshape(bt, d)
    for l in range(depth):
        w_qkv, aux, w_gates, w_merge, wb = _prep_layer(w_in[l], qk_gain[l], w_branch[l])
        h = _rmsnorm(x2, norm_mix[l])
        qkv = _inproj_qkv(h, w_qkv, aux, tabs['bd']).reshape(N_CB, b, t, LANES)
        graw, ga = _inproj_gates(h, w_gates, tabs['eg'])
        mg = _inproj_merge(h, w_merge, d)

        chunks = jnp.stack([qkv[CB_AKC], qkv[CB_AVC]]).reshape(2, b, n_chunks, CMP_STRIDE, KV_GROUPS, HEAD_DIM)
        chunks = jnp.transpose(chunks, (0, 1, 4, 2, 3, 5)).reshape(2, b, KV_GROUPS, n_chunks, CMP_STRIDE * HEAD_DIM)
        cmp_kv = _compress(chunks, *_prep_compress(cmp_pos[l], cmp_w1[l], cmp_w2[l], qk_gain[l, 1]), tabs['bd'])
        o_cmp, qa = _cmp_select(qkv, cmp_kv, biases['cmp'], tabs['ovt'], tabs['n_slc'])
        o_slc = _slc_attention(qa, *_slc_keys_values(qkv[CB_AKS], qkv[CB_AVS], tabs['key_block']), biases['near'])
        no_sinks = jnp.zeros((N_HEADS, LANES), F32)
        o_win = _banded_attention(qkv, CB_AQ, CB_AKW, CB_AVW, biases['win'], no_sinks, False)

        sink_tab = jnp.broadcast_to(sinks[l][:, None] * LOG2E, (N_HEADS, LANES))
        o_b = _banded_attention(qkv, CB_BQ, CB_BK, CB_BV, biases['swa'], sink_tab, True)

        f_t = jnp.transpose(graw[:, N_GATE_COLS:N_GATE_COLS + N_HEADS].reshape(b, t, N_HEADS), (0, 2, 1))
        c_row = _fox_cumsum(f_t, forget_bias[l].reshape(N_HEADS, 1))
        o_c = _fox_attention(*_fox_operands(qkv, c_row), fox_tq)

        flat = lambda o: o.reshape(N_QBLK, bt, LANES)
        x2 = _merge(flat(o_cmp), flat(o_slc), flat(o_win), flat(o_b), flat(o_c), ga, mg, wb,
                    w_out[l].astype(BF16), x2)
        x2 = _ffn(x2, norm_ffn[l], w_gate[l].astype(BF16), w_up[l].astype(BF16), conv_w[l], conv_b[l],
                  w_down[l].astype(BF16), t)
    return x2.reshape(b, t, d)
```

```python
import functools
import math

import numpy as np
import jax
import jax.numpy as jnp
from jax import lax
from jax.experimental import pallas as pl
from jax.experimental.pallas import tpu as pltpu

F32 = jnp.float32
BF16 = jnp.bfloat16

HEAD_DIM = 64
LANES = 128
N_HEADS = 8
KV_GROUPS = 2
HEADS_PER_GROUP = N_HEADS // KV_GROUPS
N_QBLK = N_HEADS * HEAD_DIM // LANES
MIX_WIDTH = N_HEADS * HEAD_DIM
N_BRANCH = 3
Q_BLOCK = 128
CMP_LEN = 32
CMP_STRIDE = 16
CMP_HIDDEN = 256
SLC_LEN = 64
SLC_TOPK = 8
NSA_WINDOW = 256
SWA_WINDOW = 128
N_BUCKETS = 32
MAX_DISTANCE = 128
CONV_WIDTH = 3
RMS_EPS = 1e-6
LOG2E = math.log2(math.e)
NEG = -1e30
FORCE_SCORE = 1e9
SLC_TILE = 256
VMEM_LIMIT = 48 * 1024 * 1024

CB_AQ, CB_BQ, CB_CQ, CB_CK, CB_CV = 0, 4, 8, 12, 16
CB_AKC, CB_AVC, CB_AKS, CB_AVS, CB_AKW, CB_AVW, CB_BK, CB_BV = 20, 21, 22, 23, 24, 25, 26, 27
N_CB = 28
N_GATE_COLS = N_HEADS * N_BRANCH


def _cparams(*sem):
    return pltpu.CompilerParams(dimension_semantics=sem, vmem_limit_bytes=VMEM_LIMIT)


def _split_hi_lo(v):
    hi = v.astype(BF16)
    return hi, (v - hi.astype(F32)).astype(BF16)


def _dot(a, b):
    return jnp.dot(a, b, preferred_element_type=F32)


def _dot_nt(a, b):
    return lax.dot_general(a, b, (((1,), (1,)), ((), ())), preferred_element_type=F32)


def _dot2(v, m):
    hi, lo = _split_hi_lo(v)
    return _dot(hi, m) + _dot(lo, m)


def _gelu_tanh(x):
    return x * (0.5 * (1.0 + jnp.tanh(math.sqrt(2.0 / math.pi) * (x + 0.044715 * (x * x * x)))))


def _head_rms(y, bd, gain):
    ms = _dot2(y * y, bd) * (1.0 / HEAD_DIM)
    return (y * lax.rsqrt(ms + RMS_EPS)) * gain


def _lane_half_mask(g):
    return (lax.broadcasted_iota(jnp.int32, (1, LANES), 1) // HEAD_DIM) == g


def _stack_group_queries(q_ref, g):
    mask = _lane_half_mask(g)
    return jnp.concatenate([jnp.where(mask, q_ref[p, 0], jnp.zeros_like(q_ref[p, 0]))
                            for p in range(N_QBLK)], axis=0)


def _merge_group_outputs(o_ref, o_g0, o_g1):
    low = _lane_half_mask(0)
    tq = o_ref.shape[2]
    for p in range(N_QBLK):
        rows = slice(p * tq, (p + 1) * tq)
        o_ref[p, 0] = jnp.where(low, o_g0[rows], o_g1[rows]).astype(o_ref.dtype)


def _rmsnorm_kernel(x_ref, g_ref, o_ref):
    x = x_ref[...]
    ms = jnp.mean(x * x, axis=-1, keepdims=True)
    o_ref[...] = ((x * lax.rsqrt(ms + RMS_EPS)) * g_ref[...]).astype(o_ref.dtype)


def _rmsnorm(x2, g, tm=512):
    bt, d = x2.shape
    return pl.pallas_call(
        _rmsnorm_kernel,
        grid=(bt // tm,),
        in_specs=[pl.BlockSpec((tm, d), lambda i: (i, 0)), pl.BlockSpec((1, d), lambda i: (0, 0))],
        out_specs=pl.BlockSpec((tm, d), lambda i: (i, 0)),
        out_shape=jax.ShapeDtypeStruct((bt, d), BF16),
        compiler_params=_cparams("parallel"),
        name="rmsnorm",
    )(x2, g.reshape(1, d))


def _inproj_qkv_kernel(h_ref, w_ref, aux_ref, bd_ref, o_ref, *, n_sub):
    y = _dot(h_ref[...], w_ref[...])
    for c in range(n_sub):
        cols = slice(c * LANES, (c + 1) * LANES)
        yb = y[:, cols]
        normed = _head_rms(yb, bd_ref[...], aux_ref[0:1, cols])
        o_ref[c] = jnp.where(aux_ref[1:2, cols] > 0.5, normed, yb).astype(o_ref.dtype)


def _inproj_qkv(h, w, aux, bd, tm=1024, tn=512):
    bt, d = h.shape
    n = w.shape[1]
    n_sub = tn // LANES
    return pl.pallas_call(
        functools.partial(_inproj_qkv_kernel, n_sub=n_sub),
        grid=(bt // tm, n // tn),
        in_specs=[pl.BlockSpec((tm, d), lambda i, j: (i, 0)),
                  pl.BlockSpec((d, tn), lambda i, j: (0, j)),
                  pl.BlockSpec((8, tn), lambda i, j: (0, j)),
                  pl.BlockSpec((LANES, LANES), lambda i, j: (0, 0))],
        out_specs=pl.BlockSpec((n_sub, tm, LANES), lambda i, j: (j, i, 0)),
        out_shape=jax.ShapeDtypeStruct((n // LANES, bt, LANES), BF16),
        compiler_params=_cparams("parallel", "arbitrary"),
        name="inproj_qkv",
    )(h, w, aux, bd)


def _inproj_gates_kernel(h_ref, w_ref, eg_ref, raw_ref, ga_ref):
    y = _dot(h_ref[...], w_ref[...])
    raw_ref[...] = y
    ga = _dot2(jax.nn.sigmoid(y), eg_ref[...])
    for n in range(N_BRANCH):
        ga_ref[n] = ga[:, n * MIX_WIDTH:(n + 1) * MIX_WIDTH].astype(ga_ref.dtype)


def _inproj_gates(h, w, eg, tm=1024):
    bt, d = h.shape
    return pl.pallas_call(
        _inproj_gates_kernel,
        grid=(bt // tm,),
        in_specs=[pl.BlockSpec((tm, d), lambda i: (i, 0)),
                  pl.BlockSpec((d, LANES), lambda i: (0, 0)),
                  pl.BlockSpec((LANES, N_BRANCH * MIX_WIDTH), lambda i: (0, 0))],
        out_specs=[pl.BlockSpec((tm, LANES), lambda i: (i, 0)),
                   pl.BlockSpec((N_BRANCH, tm, MIX_WIDTH), lambda i: (0, i, 0))],
        out_shape=[jax.ShapeDtypeStruct((bt, LANES), F32),
                   jax.ShapeDtypeStruct((N_BRANCH, bt, MIX_WIDTH), BF16)],
        compiler_params=_cparams("parallel"),
        name="inproj_gates",
    )(h, w, eg)


def _inproj_merge_kernel(h_ref, w_ref, o_ref):
    o_ref[0] = jax.nn.sigmoid(_dot(h_ref[...], w_ref[...])).astype(o_ref.dtype)


def _inproj_merge(h, w, d_model, tm=1024, tn=512):
    bt, d = h.shape
    n = w.shape[1]
    per = d_model // tn
    return pl.pallas_call(
        _inproj_merge_kernel,
        grid=(bt // tm, n // tn),
        in_specs=[pl.BlockSpec((tm, d), lambda i, j: (i, 0)),
                  pl.BlockSpec((d, tn), lambda i, j: (0, j))],
        out_specs=pl.BlockSpec((1, tm, tn), lambda i, j: (j // per, i, j % per)),
        out_shape=jax.ShapeDtypeStruct((n // d_model, bt, d_model), BF16),
        compiler_params=_cparams("parallel", "arbitrary"),
        name="inproj_merge",
    )(h, w)


def _compress_kernel(c_ref, pos_ref, w1_ref, w2_ref, aux_ref, bd_ref, o_ref):
    n_chunks = c_ref.shape[3]
    y = jnp.zeros((n_chunks, LANES), F32)
    for g in range(KV_GROUPS):
        c = c_ref[0, 0, g].astype(F32)
        a = _dot((c + pos_ref[0, 0]).astype(BF16), w1_ref[0, 0])
        b = _dot((c + pos_ref[0, 1]).astype(BF16), w1_ref[0, 1])
        hid = _gelu_tanh(a + pltpu.roll(b, n_chunks - 1, 0))
        y = y + _dot(hid.astype(BF16), w2_ref[0, g])
    normed = _head_rms(y, bd_ref[...], aux_ref[0, 0:1, :])
    o_ref[0, 0] = jnp.where(aux_ref[0, 1:2, :] > 0.5, normed, y).astype(o_ref.dtype)


def _compress(chunks, pos, w1, w2p, aux, bd):
    _, b, g, n_chunks, cw = chunks.shape
    return pl.pallas_call(
        _compress_kernel,
        grid=(2, b),
        in_specs=[pl.BlockSpec((1, 1, g, n_chunks, cw), lambda w, i: (w, i, 0, 0, 0)),
                  pl.BlockSpec((1, 2, 1, cw), lambda w, i: (w, 0, 0, 0)),
                  pl.BlockSpec((1, 2, cw, CMP_HIDDEN), lambda w, i: (w, 0, 0, 0)),
                  pl.BlockSpec((1, g, CMP_HIDDEN, LANES), lambda w, i: (w, 0, 0, 0)),
                  pl.BlockSpec((1, 8, LANES), lambda w, i: (w, 0, 0)),
                  pl.BlockSpec((LANES, LANES), lambda w, i: (0, 0))],
        out_specs=pl.BlockSpec((1, 1, n_chunks, LANES), lambda w, i: (w, i, 0, 0)),
        out_shape=jax.ShapeDtypeStruct((2, b, n_chunks, LANES), BF16),
        compiler_params=_cparams("parallel", "parallel"),
        name="nsa_compress",
    )(chunks, pos, w1, w2p, aux, bd)


def _cmp_select_kernel(q_ref, kc_ref, vc_ref, bias_ref, ovt_ref, o_ref, qa_ref, *, n_slc, n_top):
    i = pl.program_id(1)
    ncp = kc_ref.shape[2]
    row_t = i * Q_BLOCK + lax.broadcasted_iota(jnp.int32, (Q_BLOCK, 1), 0)
    row_valid = jnp.concatenate([row_t >= CMP_LEN - 1] * HEADS_PER_GROUP, axis=0)
    jrow = lax.broadcasted_iota(jnp.int32, (n_slc, Q_BLOCK), 0)
    cur = (i * Q_BLOCK + lax.broadcasted_iota(jnp.int32, (n_slc, Q_BLOCK), 1)) // SLC_LEN
    block_valid = jrow <= cur

    def force(imp):
        forced = jnp.where(jrow == cur, FORCE_SCORE, jnp.where(jrow == cur - 1, FORCE_SCORE, imp))
        return jnp.where(jrow == 0, FORCE_SCORE, forced)
    outs = []
    for g in range(KV_GROUPS):
        qop = _stack_group_queries(q_ref, g)
        rows = HEADS_PER_GROUP * Q_BLOCK
        s = _dot_nt(qop, kc_ref[0, 0]) + bias_ref[g * HEADS_PER_GROUP:(g + 1) * HEADS_PER_GROUP].reshape(rows, ncp)
        e = jnp.exp2(s - jnp.max(s, axis=-1, keepdims=True))
        inv = jnp.where(row_valid, 1.0 / jnp.sum(e, axis=-1, keepdims=True), 0.0)
        p = e * inv
        outs.append(_dot(p.astype(BF16), vc_ref[0, 0]))
        psum = p[0:Q_BLOCK]
        for r in range(1, HEADS_PER_GROUP):
            psum = psum + p[r * Q_BLOCK:(r + 1) * Q_BLOCK]
        hi, lo = _split_hi_lo(psum)
        imp_t = (_dot_nt(ovt_ref[...], hi) + _dot_nt(ovt_ref[...], lo))[0:n_slc]
        score = jnp.where(block_valid, force(imp_t), NEG)
        groups = [score[a:a + 8] for a in range(0, n_slc, 8)]
        ranks = [jnp.zeros((8, Q_BLOCK), F32) for _ in groups]
        sub = lax.broadcasted_iota(jnp.int32, (8, Q_BLOCK), 0)
        for j in range(n_slc):
            other = score[j:j + 1, :]
            for a, blk in enumerate(groups):
                ge = jnp.where(other >= blk, 1.0, 0.0)
                if 8 * a > j:
                    ranks[a] = ranks[a] + ge
                elif 8 * a + 7 <= j:
                    ranks[a] = ranks[a] + jnp.where(other > blk, 1.0, 0.0)
                else:
                    ranks[a] = ranks[a] + jnp.where(sub > j - 8 * a, ge, jnp.where(other > blk, 1.0, 0.0))
        rank = jnp.concatenate(ranks, axis=0)
        sel_bias = jnp.where(rank < n_top, jnp.where(score > NEG / 2, 0.0, NEG), NEG)
        if n_slc < HEAD_DIM:
            sel_bias = jnp.concatenate([sel_bias, jnp.full((HEAD_DIM - n_slc, Q_BLOCK), NEG, F32)], axis=0)
        sel_t = jnp.concatenate([sel_bias, sel_bias], axis=0).T.astype(qa_ref.dtype)
        own_half = _lane_half_mask(g)
        for p in range(N_QBLK):
            qa_ref[g * HEADS_PER_GROUP + p, 0] = jnp.where(own_half, q_ref[p, 0], sel_t)
    _merge_group_outputs(o_ref, outs[0], outs[1])


def _cmp_select(qkv, cmp_kv, bias_c, ovt, n_slc):
    _, b, t, _ = qkv.shape
    nt = t // Q_BLOCK
    ncp = cmp_kv.shape[2]
    assert n_slc <= HEAD_DIM
    return pl.pallas_call(
        functools.partial(_cmp_select_kernel, n_slc=n_slc, n_top=min(SLC_TOPK, n_slc)),
        grid=(b, nt),
        in_specs=[pl.BlockSpec((N_QBLK, 1, Q_BLOCK, LANES), lambda bi, i: (CB_AQ // N_QBLK, bi, i, 0)),
                  pl.BlockSpec((1, 1, ncp, LANES), lambda bi, i: (0, bi, 0, 0)),
                  pl.BlockSpec((1, 1, ncp, LANES), lambda bi, i: (1, bi, 0, 0)),
                  pl.BlockSpec((N_HEADS, Q_BLOCK, ncp), lambda bi, i: (0, i, 0)),
                  pl.BlockSpec((LANES, ncp), lambda bi, i: (0, 0))],
        out_specs=[pl.BlockSpec((N_QBLK, 1, Q_BLOCK, LANES), lambda bi, i: (0, bi, i, 0)),
                   pl.BlockSpec((N_HEADS, 1, Q_BLOCK, LANES), lambda bi, i: (0, bi, i, 0))],
        out_shape=[jax.ShapeDtypeStruct((N_QBLK, b, t, LANES), BF16),
                   jax.ShapeDtypeStruct((N_HEADS, b, t, LANES), BF16)],
        compiler_params=_cparams("parallel", "parallel"),
        name="nsa_cmp_select",
    )(qkv, cmp_kv, cmp_kv, bias_c, ovt)


def _flash_update(s, m, acc, v):
    m_new = jnp.maximum(m, jnp.max(s, axis=-1, keepdims=True))
    p = jnp.exp2(s - m_new)
    return m_new, jnp.exp2(m - m_new) * acc + _dot(p.astype(BF16), v)


def _normalise(acc):
    return acc / pltpu.roll(acc, HEAD_DIM, 1)


def _augment_keys_values(ka_scr, va_scr, k2, xk, v2):
    low = _lane_half_mask(0)
    ones = jnp.ones_like(v2)
    ka_scr[0] = jnp.where(low, k2, xk)
    ka_scr[1] = jnp.where(low, xk, k2)
    va_scr[0] = jnp.where(low, v2, ones)
    va_scr[1] = jnp.where(low, ones, v2)


def _store_pair(scr, s):
    scr[0], scr[1] = s


def _load_pair(scr):
    return scr[0], scr[1]


def _pipelined_sweep(logits, update, n_tiles, sa_scr, sb_scr, carry):
    def single(j, c):
        _store_pair(sa_scr, logits(j))
        return update(j, _load_pair(sa_scr), c)

    odd = n_tiles % 2
    carry = lax.fori_loop(0, odd, single, carry)
    _store_pair(sa_scr, logits(odd))

    def pair(jp, c):
        j = odd + 2 * jp
        _store_pair(sb_scr, logits(j + 1))
        c = update(j, _load_pair(sa_scr), c)
        _store_pair(sa_scr, logits(j + 2))
        return update(j + 1, _load_pair(sb_scr), c)

    return lax.fori_loop(0, n_tiles // 2, pair, carry)


def _slc_kernel(qa_ref, k_ref, v_ref, hot_ref, nb_ref, o_ref, ka_scr, va_scr, sa_scr, sb_scr):
    i = pl.program_id(1)
    tq = qa_ref.shape[2]
    rows = HEADS_PER_GROUP * tq

    @pl.when(i == 0)
    def _():
        _augment_keys_values(ka_scr, va_scr, k_ref[0, 0], hot_ref[...], v_ref[0, 0])

    heads = [slice(g * HEADS_PER_GROUP, (g + 1) * HEADS_PER_GROUP) for g in range(KV_GROUPS)]
    qs = [qa_ref[hs, 0].reshape(rows, LANES) for hs in heads]

    def keys(j):
        return pl.ds(pl.multiple_of(j * tq, tq), tq)

    def logits(j):
        return tuple(_dot_nt(qs[g], ka_scr[g, keys(j), :]) for g in range(KV_GROUPS))

    def update(j, s, carry, which=None, pen=None):
        new = []
        for g in range(KV_GROUPS):
            sg = s[g]
            if which is not None:
                sg = (sg.reshape(HEADS_PER_GROUP, tq, tq) + nb_ref[which, heads[g]]).reshape(rows, tq)
            if pen is not None:
                sg = sg + pen
            new.append(_flash_update(sg, *carry[g], va_scr[g, keys(j), :]))
        return tuple(new)

    init = (jnp.full((rows, 1), NEG, F32), jnp.zeros((rows, LANES), F32))
    carry = _pipelined_sweep(logits, update, jnp.maximum(i - 1, 0), sa_scr, sb_scr, (init, init))
    _store_pair(sb_scr, logits(i))
    carry = update(jnp.maximum(i - 1, 0), _load_pair(sa_scr), carry, which=0, pen=jnp.where(i >= 1, 0.0, NEG))
    (_, acc0), (_, acc1) = update(i, _load_pair(sb_scr), carry, which=1)
    _merge_group_outputs(o_ref, _normalise(acc0), _normalise(acc1))


def _slc_attention(qa, qkv, hot2, near_bias):
    _, b, t, _ = qa.shape
    tq = near_bias.shape[2]
    return pl.pallas_call(
        _slc_kernel,
        grid=(b, t // tq),
        in_specs=[pl.BlockSpec((N_HEADS, 1, tq, LANES), lambda bi, i: (0, bi, i, 0)),
                  pl.BlockSpec((1, 1, t, LANES), lambda bi, i: (CB_AKS, bi, 0, 0)),
                  pl.BlockSpec((1, 1, t, LANES), lambda bi, i: (CB_AVS, bi, 0, 0)),
                  pl.BlockSpec(hot2.shape, lambda bi, i: (0, 0)),
                  pl.BlockSpec(near_bias.shape, lambda bi, i: (0, 0, 0, 0))],
        out_specs=pl.BlockSpec((N_QBLK, 1, tq, LANES), lambda bi, i: (0, bi, i, 0)),
        out_shape=jax.ShapeDtypeStruct((N_QBLK, b, t, LANES), BF16),
        scratch_shapes=[pltpu.VMEM((KV_GROUPS, t, LANES), BF16), pltpu.VMEM((KV_GROUPS, t, LANES), BF16),
                        pltpu.VMEM((KV_GROUPS, HEADS_PER_GROUP * tq, tq), F32),
                        pltpu.VMEM((KV_GROUPS, HEADS_PER_GROUP * tq, tq), F32)],
        compiler_params=_cparams("parallel", "arbitrary"),
        name="nsa_slc_attention",
    )(qa, qkv, qkv, hot2, near_bias)


def _banded_kernel(*refs, n_blk, use_sinks):
    q_ref = refs[0]
    k_refs = refs[1:1 + n_blk]
    v_refs = refs[1 + n_blk:1 + 2 * n_blk]
    bias_ref, sink_ref, o_ref = refs[1 + 2 * n_blk:]
    i = pl.program_id(1)
    n_keys = n_blk * Q_BLOCK
    rows = HEADS_PER_GROUP * Q_BLOCK
    key_pos = (i - (n_blk - 1)) * Q_BLOCK + lax.broadcasted_iota(jnp.int32, (1, n_keys), 1)
    pad_pen = jnp.where(key_pos >= 0, 0.0, NEG)
    kcat = jnp.concatenate([r[0, 0] for r in k_refs], axis=0)
    vcat = jnp.concatenate([r[0, 0] for r in v_refs], axis=0)
    outs = []
    for g in range(KV_GROUPS):
        qop = _stack_group_queries(q_ref, g)
        hs = slice(g * HEADS_PER_GROUP, (g + 1) * HEADS_PER_GROUP)
        s = _dot_nt(qop, kcat) + bias_ref[hs].reshape(rows, n_keys) + pad_pen
        m = jnp.max(s, axis=-1, keepdims=True)
        if use_sinks:
            sink = jnp.concatenate([jnp.broadcast_to(sink_ref[h:h + 1, 0:1], (Q_BLOCK, 1))
                                    for h in range(hs.start, hs.stop)], axis=0)
            m = jnp.maximum(m, sink)
        e = jnp.exp2(s - m)
        l = jnp.sum(e, axis=-1, keepdims=True)
        if use_sinks:
            l = l + jnp.exp2(sink - m)
        outs.append(_dot(e.astype(BF16), vcat) / l)
    _merge_group_outputs(o_ref, outs[0], outs[1])


def _banded_attention(qkv, cb_q, cb_k, cb_v, bias, sinks, use_sinks):
    _, b, t, _ = qkv.shape
    nt = t // Q_BLOCK
    n_blk = bias.shape[2] // Q_BLOCK

    def kv_spec(cb, back):
        return pl.BlockSpec((1, 1, Q_BLOCK, LANES), lambda bi, i: (cb, bi, jnp.maximum(i - back, 0), 0))

    backs = [n_blk - 1 - jb for jb in range(n_blk)]
    return pl.pallas_call(
        functools.partial(_banded_kernel, n_blk=n_blk, use_sinks=use_sinks),
        grid=(b, nt),
        in_specs=([pl.BlockSpec((N_QBLK, 1, Q_BLOCK, LANES), lambda bi, i: (cb_q // N_QBLK, bi, i, 0))]
                  + [kv_spec(cb_k, back) for back in backs] + [kv_spec(cb_v, back) for back in backs]
                  + [pl.BlockSpec(bias.shape, lambda bi, i: (0, 0, 0)),
                     pl.BlockSpec(sinks.shape, lambda bi, i: (0, 0))]),
        out_specs=pl.BlockSpec((N_QBLK, 1, Q_BLOCK, LANES), lambda bi, i: (0, bi, i, 0)),
        out_shape=jax.ShapeDtypeStruct((N_QBLK, b, t, LANES), BF16),
        compiler_params=_cparams("parallel", "parallel"),
        name="banded_attention",
    )(qkv, *([qkv] * (2 * n_blk)), bias, sinks)


FOX_EXTRA = 6


def _fox_prep_kernel(g_ref, fb_ref, pq_ref, pk_ref, oq_ref, ok_ref, xq_ref, xk_ref, carry_scr):
    @pl.when(pl.program_id(1) == 0)
    def _():
        carry_scr[...] = jnp.zeros_like(carry_scr)

    x = g_ref[...] + fb_ref[...]
    logf = jnp.minimum(x, 0.0) - jnp.log(1.0 + jnp.exp(-jnp.abs(x)))
    ch = x.shape[0]
    lower = jnp.where(lax.broadcasted_iota(jnp.int32, (ch, ch), 0)
                      >= lax.broadcasted_iota(jnp.int32, (ch, ch), 1), 1.0, 0.0).astype(BF16)

    def split3(v):
        hi, rest = v.astype(BF16), v - v.astype(BF16).astype(F32)
        return (hi,) + _split_hi_lo(rest)

    cs = sum(_dot(lower, part) for part in split3(logf)) + carry_scr[0:1, :]
    carry_scr[0:1, :] = cs[ch - 1:ch, :]
    parts = split3(cs * LOG2E)
    xq = sum(_dot(part, pq_ref[n]) for n, part in enumerate(parts)) + oq_ref[...]
    xk = sum(_dot(part, pk_ref[n]) for n, part in enumerate(parts)) + ok_ref[...]
    for p in range(N_QBLK):
        xq_ref[p, 0] = xq[:, p * LANES:(p + 1) * LANES].astype(xq_ref.dtype)
        xk_ref[p, 0] = xk[:, p * LANES:(p + 1) * LANES].astype(xk_ref.dtype)


def _fox_prep(graw, fb_row, b, t, chunk=512):
    chunk = min(chunk, t)
    nch = t // chunk
    pq = np.zeros((3, LANES, N_QBLK * LANES), np.float32)
    pk = np.zeros((3, LANES, N_QBLK * LANES), np.float32)
    oq = np.zeros((1, N_QBLK * LANES), np.float32)
    ok = np.zeros((1, N_QBLK * LANES), np.float32)
    for h in range(N_HEADS):
        col = (h // 2) * LANES + (HEAD_DIM if h % 2 == 0 else 0)
        for n in range(3):
            pq[n, N_GATE_COLS + h, col + n] = 1.0
            pk[n, N_GATE_COLS + h, col + 3 + n] = -1.0
        oq[0, col + 3:col + FOX_EXTRA] = 1.0
        ok[0, col:col + 3] = 1.0
    const = lambda a: pl.BlockSpec(a.shape, lambda bi, c: (0,) * a.ndim)
    out_spec = pl.BlockSpec((N_QBLK, 1, chunk, LANES), lambda bi, c: (0, bi, c, 0))
    return pl.pallas_call(
        _fox_prep_kernel,
        grid=(b, nch),
        in_specs=[pl.BlockSpec((chunk, LANES), lambda bi, c: (bi * nch + c, 0)),
                  const(fb_row), const(pq), const(pk), const(oq), const(ok)],
        out_specs=[out_spec, out_spec],
        out_shape=[jax.ShapeDtypeStruct((N_QBLK, b, t, LANES), BF16)] * 2,
        scratch_shapes=[pltpu.VMEM((8, LANES), F32)],
        compiler_params=_cparams("parallel", "arbitrary"),
        name="fox_prep",
    )(graw, fb_row, jnp.asarray(pq, BF16), jnp.asarray(pk, BF16), jnp.asarray(oq), jnp.asarray(ok))


def _fox_kernel(q_ref, xq_ref, k_ref, xk_ref, v_ref, o_ref, ka_scr, va_scr, sa_scr, sb_scr, *, tq):
    i = pl.program_id(2)

    @pl.when(i == 0)
    def _():
        _augment_keys_values(ka_scr, va_scr, k_ref[0, 0], xk_ref[0, 0], v_ref[0, 0])

    low = _lane_half_mask(0)
    q2, xq = q_ref[0, 0], xq_ref[0, 0]
    qs = [jnp.where(low, q2, xq), jnp.where(low, xq, q2)]
    causal_pen = jnp.where(lax.broadcasted_iota(jnp.int32, (tq, tq), 0)
                           >= lax.broadcasted_iota(jnp.int32, (tq, tq), 1), 0.0, NEG)

    def keys(j):
        return pl.ds(pl.multiple_of(j * tq, tq), tq)

    def logits(j):
        return tuple(_dot_nt(qs[hh], ka_scr[hh, keys(j), :]) for hh in range(2))

    def update(j, s, carry, pen=None):
        return tuple(_flash_update(s[hh] if pen is None else s[hh] + pen, *carry[hh], va_scr[hh, keys(j), :])
                     for hh in range(2))

    init = (jnp.full((tq, 1), NEG, F32), jnp.zeros((tq, LANES), F32))
    carry = _pipelined_sweep(logits, update, i, sa_scr, sb_scr, (init, init))
    (_, acc0), (_, acc1) = update(i, _load_pair(sa_scr), carry, pen=causal_pen)
    o_ref[0, 0] = jnp.where(low, _normalise(acc0), _normalise(acc1)).astype(o_ref.dtype)


def _fox_attention(qkv, xq, xk, tq):
    _, b, t, _ = qkv.shape
    q_spec = lambda cb: pl.BlockSpec((1, 1, tq, LANES), lambda bi, p, i: (cb + p, bi, i, 0))
    kv_spec = lambda cb: pl.BlockSpec((1, 1, t, LANES), lambda bi, p, i: (cb + p, bi, 0, 0))
    return pl.pallas_call(
        functools.partial(_fox_kernel, tq=tq),
        grid=(b, N_QBLK, t // tq),
        in_specs=[q_spec(CB_CQ), q_spec(0), kv_spec(CB_CK), kv_spec(0), kv_spec(CB_CV)],
        out_specs=pl.BlockSpec((1, 1, tq, LANES), lambda bi, p, i: (p, bi, i, 0)),
        out_shape=jax.ShapeDtypeStruct((N_QBLK, b, t, LANES), BF16),
        scratch_shapes=[pltpu.VMEM((2, t, LANES), BF16), pltpu.VMEM((2, t, LANES), BF16),
                        pltpu.VMEM((2, tq, tq), F32), pltpu.VMEM((2, tq, tq), F32)],
        compiler_params=_cparams("parallel", "parallel", "arbitrary"),
        name="fox_attention",
    )(qkv, xq, qkv, xk, qkv)


def _merge_kernel(ocmp_ref, oslc_ref, owin_ref, ob_ref, oc_ref, ga_ref, mg_ref, wb_ref, wo_ref, x_ref, o_ref):
    def cat(ref):
        return jnp.concatenate([ref[p] for p in range(N_QBLK)], axis=-1)

    oa = (ga_ref[0].astype(F32) * cat(ocmp_ref).astype(F32)
          + ga_ref[1].astype(F32) * cat(oslc_ref).astype(F32)
          + ga_ref[2].astype(F32) * cat(owin_ref).astype(F32)).astype(BF16)
    mix = (mg_ref[0].astype(F32) * _dot(oa, wb_ref[0])
           + mg_ref[1].astype(F32) * _dot(cat(ob_ref), wb_ref[1])
           + mg_ref[2].astype(F32) * _dot(cat(oc_ref), wb_ref[2]))
    o_ref[...] = x_ref[...] + _dot(mix.astype(BF16), wo_ref[...])


def _merge(ocmp, oslc, owin, ob, oc, ga, mg, wb, wo, x2, tm=512):
    bt, d = x2.shape
    o_spec = pl.BlockSpec((N_QBLK, tm, LANES), lambda i: (0, i, 0))
    return pl.pallas_call(
        _merge_kernel,
        grid=(bt // tm,),
        in_specs=[o_spec] * 5 + [pl.BlockSpec((N_BRANCH, tm, MIX_WIDTH), lambda i: (0, i, 0)),
                                 pl.BlockSpec((N_BRANCH, tm, d), lambda i: (0, i, 0)),
                                 pl.BlockSpec(wb.shape, lambda i: (0, 0, 0)),
                                 pl.BlockSpec(wo.shape, lambda i: (0, 0)),
                                 pl.BlockSpec((tm, d), lambda i: (i, 0))],
        out_specs=pl.BlockSpec((tm, d), lambda i: (i, 0)),
        out_shape=jax.ShapeDtypeStruct((bt, d), F32),
        compiler_params=_cparams("parallel"),
        name="branch_merge",
    )(ocmp, oslc, owin, ob, oc, ga, mg, wb, wo, x2)


def _ffn_kernel(x_ref, xh_ref, g_ref, wg_ref, wu_ref, cw_ref, cb_ref, wd_ref, o_ref,
                h_scr, hh_scr, a_scr, acc_scr, *, tm, tiles_per_seq):
    i = pl.program_id(0)
    f = pl.program_id(1)

    def norm(x):
        ms = jnp.mean(x * x, axis=-1, keepdims=True)
        return ((x * lax.rsqrt(ms + RMS_EPS)) * g_ref[...]).astype(BF16)

    @pl.when(f == 0)
    def _():
        h_scr[...] = norm(x_ref[...])
        hh_scr[...] = norm(xh_ref[...])
        acc_scr[...] = jnp.zeros_like(acc_scr)

    a = _dot(h_scr[...], wg_ref[...])
    halo = _dot(hh_scr[...], wg_ref[...]) * jnp.where(i % tiles_per_seq == 0, 0.0, 1.0)
    a_scr[0:8, :] = halo
    a_scr[8:8 + tm, :] = a
    conv = (cw_ref[0:1, :] * a_scr[6:6 + tm, :] + cw_ref[1:2, :] * a_scr[7:7 + tm, :]
            + cw_ref[2:3, :] * a + cb_ref[...])
    act = _gelu_tanh(conv) * _dot(h_scr[...], wu_ref[...])
    acc_scr[...] += _dot(act.astype(BF16), wd_ref[...])

    @pl.when(f == pl.num_programs(1) - 1)
    def _():
        o_ref[...] = x_ref[...] + acc_scr[...]


def _ffn(x2, g, wg, wu, cw, cb, wd, t, tm=1024, tf=256):
    bt, d = x2.shape
    ff = wg.shape[1]
    halo_blocks = tm // 8
    return pl.pallas_call(
        functools.partial(_ffn_kernel, tm=tm, tiles_per_seq=t // tm),
        grid=(bt // tm, ff // tf),
        in_specs=[pl.BlockSpec((tm, d), lambda i, f: (i, 0)),
                  pl.BlockSpec((8, d), lambda i, f: (jnp.maximum(i * halo_blocks - 1, 0), 0)),
                  pl.BlockSpec((1, d), lambda i, f: (0, 0)),
                  pl.BlockSpec((d, tf), lambda i, f: (0, f)),
                  pl.BlockSpec((d, tf), lambda i, f: (0, f)),
                  pl.BlockSpec((CONV_WIDTH, tf), lambda i, f: (0, f)),
                  pl.BlockSpec((1, tf), lambda i, f: (0, f)),
                  pl.BlockSpec((tf, d), lambda i, f: (f, 0))],
        out_specs=pl.BlockSpec((tm, d), lambda i, f: (i, 0)),
        out_shape=jax.ShapeDtypeStruct((bt, d), F32),
        scratch_shapes=[pltpu.VMEM((tm, d), BF16), pltpu.VMEM((8, d), BF16),
                        pltpu.VMEM((tm + 8, tf), F32), pltpu.VMEM((tm, d), F32)],
        compiler_params=_cparams("parallel", "arbitrary"),
        name="conv_ffn",
    )(x2, x2, g.reshape(1, d), wg, wu, cw, cb.reshape(1, ff), wd)


def _t5_bucket_np(dist):
    max_exact = N_BUCKETS // 2
    d = np.maximum(dist, 0)
    ratio = np.log(np.maximum(d, 1) / max_exact) / math.log(MAX_DISTANCE / max_exact)
    large = np.minimum(max_exact + (ratio * (N_BUCKETS - max_exact)).astype(np.int64), N_BUCKETS - 1)
    return np.where(d < max_exact, d, large)


def _expand_kernel(tab_ref, idx_ref, o_ref):
    tab = tab_ref[...]
    onehot = jnp.where(lax.broadcasted_iota(jnp.int32, (tab.shape[1], idx_ref.shape[1]), 0) == idx_ref[...],
                       1.0, 0.0).astype(BF16)
    hi, rest = tab.astype(BF16), tab - tab.astype(BF16).astype(F32)
    mid, lo = _split_hi_lo(rest)
    o_ref[...] = _dot(hi, onehot) + _dot(mid, onehot) + _dot(lo, onehot)


def _bias_from_dist(tab_t, dist, valid, chunk=4096):
    heads = tab_t.shape[0]
    tab = jnp.zeros((heads, LANES), F32).at[:, :N_BUCKETS].set(tab_t).at[:, N_BUCKETS].set(NEG)
    idx = np.where(valid, _t5_bucket_np(dist), N_BUCKETS).reshape(1, -1).astype(np.int32)
    n = idx.shape[1]
    chunk = min(chunk, n)
    out = pl.pallas_call(
        _expand_kernel,
        grid=(n // chunk,),
        in_specs=[pl.BlockSpec((heads, LANES), lambda i: (0, 0)), pl.BlockSpec((1, chunk), lambda i: (0, i))],
        out_specs=pl.BlockSpec((heads, chunk), lambda i: (0, i)),
        out_shape=jax.ShapeDtypeStruct((heads, n), F32),
        compiler_params=_cparams("parallel"),
        name="rel_bias_expand",
    )(tab, jnp.asarray(idx))
    return out.reshape((heads,) + dist.shape)


def _cmp_bias_kernel(g_ref, o_ref):
    i = pl.program_id(0)
    ncp = o_ref.shape[2]
    for h in range(o_ref.shape[0]):
        o_ref[h] = pltpu.roll(g_ref[h], i * (Q_BLOCK // CMP_STRIDE), 1)[:, :ncp]


def _cmp_bias(g_tab, t):
    heads, _, width = g_tab.shape
    ncp = t // CMP_STRIDE
    return pl.pallas_call(
        _cmp_bias_kernel,
        grid=(t // Q_BLOCK,),
        in_specs=[pl.BlockSpec(g_tab.shape, lambda i: (0, 0, 0))],
        out_specs=pl.BlockSpec((heads, Q_BLOCK, ncp), lambda i: (0, i, 0)),
        out_shape=jax.ShapeDtypeStruct((heads, t, ncp), F32),
        compiler_params=_cparams("parallel"),
        name="cmp_bias_build",
    )(g_tab)


def _static_tables(t):
    ncp = t // CMP_STRIDE
    n_cmp = (t - CMP_LEN) // CMP_STRIDE + 1
    n_slc = t // SLC_LEN
    lane = np.arange(LANES)
    bd = (lane[:, None] // HEAD_DIM == lane[None, :] // HEAD_DIM).astype(np.float32)
    eg = np.zeros((LANES, N_BRANCH * MIX_WIDTH), np.float32)
    for h in range(N_HEADS):
        base = (h % N_QBLK) * LANES + (h // N_QBLK) * HEAD_DIM
        for n in range(N_BRANCH):
            eg[h * N_BRANCH + n, n * MIX_WIDTH + base:n * MIX_WIDTH + base + HEAD_DIM] = 1.0
    c_start = np.arange(n_cmp) * CMP_STRIDE
    s_start = np.arange(n_slc) * SLC_LEN
    overlap = ((c_start[:, None] < s_start[None, :] + SLC_LEN) & (c_start[:, None] + CMP_LEN > s_start[None, :]))
    ovt = np.zeros((LANES, ncp), np.float32)
    ovt[:n_slc, :n_cmp] = overlap.T
    key_block = np.zeros((t, LANES), np.float32)
    key_block[np.arange(t), np.arange(t) // SLC_LEN] = 1.0
    key_block[np.arange(t), HEAD_DIM + np.arange(t) // SLC_LEN] = 1.0
    as_bf16 = lambda a: jnp.asarray(a, BF16)
    return dict(bd=as_bf16(bd), eg=as_bf16(eg), ovt=as_bf16(ovt), key_block=as_bf16(key_block), n_slc=n_slc)


def _bias_tables(rel_bias, t):
    n_cmp = (t - CMP_LEN) // CMP_STRIDE + 1
    ncp = t // CMP_STRIDE
    tab_a = rel_bias[:, :N_HEADS].T * LOG2E
    tab_b = rel_bias[:, N_HEADS:].T * LOG2E
    r = np.arange(Q_BLOCK)[:, None]

    def band(window):
        n_prev = -(-(window - 1) // Q_BLOCK)
        dist = n_prev * Q_BLOCK + r - np.arange((n_prev + 1) * Q_BLOCK)[None, :]
        return dist, (dist >= 0) & (dist < window)

    m = np.arange(2 * ncp)
    m = np.where(m < ncp, m, m - 2 * ncp)[None, :]
    dist_c = r - (m * CMP_STRIDE + CMP_LEN - 1)
    assert n_cmp == ncp - 1 and (dist_c[:, m[0] == Q_BLOCK // CMP_STRIDE - 1] < 0).all()
    tile = np.arange(SLC_TILE)
    dist_n = np.stack([SLC_TILE * (1 - which) + tile[:, None] - tile[None, :] for which in range(2)])
    near = _bias_from_dist(tab_a - tab_a[:, N_BUCKETS - 1:], dist_n, dist_n >= 0)
    return dict(win=_bias_from_dist(tab_a, *band(NSA_WINDOW)),
                swa=_bias_from_dist(tab_b, *band(SWA_WINDOW)),
                cmp=_cmp_bias(_bias_from_dist(tab_a, dist_c, dist_c >= 0), t),
                near=jnp.transpose(near, (1, 0, 2, 3)))


def _pair_cols(w):
    parts = []
    for p in range(N_QBLK):
        parts += [w[..., p * HEAD_DIM:(p + 1) * HEAD_DIM],
                  w[..., (N_QBLK + p) * HEAD_DIM:(N_QBLK + p + 1) * HEAD_DIM]]
    return jnp.concatenate(parts, axis=-1)


def _prep_layer(w_in, qk_gain, w_branch):
    kv = KV_GROUPS * HEAD_DIM
    widths = [('a_q', MIX_WIDTH), ('a_kc', kv), ('a_vc', kv), ('a_ks', kv), ('a_vs', kv), ('a_kw', kv), ('a_vw', kv),
              ('a_gate', N_GATE_COLS), ('b_q', MIX_WIDTH), ('b_k', kv), ('b_v', kv),
              ('c_q', MIX_WIDTH), ('c_k', MIX_WIDTH), ('c_v', MIX_WIDTH), ('c_f', N_HEADS),
              ('merge', N_BRANCH * w_in.shape[0])]
    cols, off = {}, 0
    for name, w in widths:
        cols[name] = w_in[:, off:off + w]
        off += w
    scale = HEAD_DIM ** -0.5 * LOG2E
    tile = lambda gvec, n: jnp.tile(gvec, n)
    zeros = lambda n: jnp.zeros((n,), F32)
    ones = lambda n: jnp.ones((n,), F32)
    pieces = [(_pair_cols(cols['a_q']), tile(qk_gain[0] * scale, N_HEADS), ones(MIX_WIDTH)),
              (_pair_cols(cols['b_q']), tile(qk_gain[2] * scale, N_HEADS), ones(MIX_WIDTH)),
              (cols['c_q'], tile(qk_gain[4] * scale, N_HEADS), ones(MIX_WIDTH)),
              (cols['c_k'], tile(qk_gain[5], N_HEADS), ones(MIX_WIDTH)),
              (cols['c_v'], zeros(MIX_WIDTH), zeros(MIX_WIDTH)),
              (cols['a_kc'], zeros(kv), zeros(kv)), (cols['a_vc'], zeros(kv), zeros(kv)),
              (cols['a_ks'], tile(qk_gain[1], KV_GROUPS), ones(kv)), (cols['a_vs'], zeros(kv), zeros(kv)),
              (cols['a_kw'], tile(qk_gain[1], KV_GROUPS), ones(kv)), (cols['a_vw'], zeros(kv), zeros(kv)),
              (cols['b_k'], tile(qk_gain[3], KV_GROUPS), ones(kv)), (cols['b_v'], zeros(kv), zeros(kv))]
    w_qkv = jnp.concatenate([p[0] for p in pieces], axis=1).astype(BF16)
    n_qkv = w_qkv.shape[1]
    aux = jnp.zeros((8, n_qkv), F32)
    aux = aux.at[0].set(jnp.concatenate([p[1] for p in pieces])).at[1].set(jnp.concatenate([p[2] for p in pieces]))
    d = w_in.shape[0]
    w_gates = jnp.concatenate([cols['a_gate'], cols['c_f'],
                               jnp.zeros((d, LANES - N_GATE_COLS - N_HEADS), F32)], axis=1).astype(BF16)
    wb = jnp.stack([_pair_cols(w_branch[0].T).T, _pair_cols(w_branch[1].T).T, w_branch[2]]).astype(BF16)
    return w_qkv, aux, w_gates, cols['merge'].astype(BF16), wb


def _prep_compress(cmp_pos, cmp_w1, cmp_w2, k_gain):
    half = CMP_STRIDE * HEAD_DIM
    pos = cmp_pos.reshape(2, 2, 1, half)
    w1 = cmp_w1.reshape(2, 2, half, CMP_HIDDEN).astype(BF16)
    z = jnp.zeros_like(cmp_w2)
    w2p = jnp.stack([jnp.concatenate([cmp_w2, z], axis=-1), jnp.concatenate([z, cmp_w2], axis=-1)], axis=1).astype(BF16)
    aux = jnp.zeros((2, 8, LANES), F32)
    aux = aux.at[0, 0].set(jnp.tile(k_gain, KV_GROUPS)).at[0, 1].set(1.0)
    return pos, w1, w2p, aux


def kernel(x, rel_bias, norm_mix, norm_ffn, w_in, forget_bias, qk_gain, cmp_pos, cmp_w1, cmp_w2, sinks, w_branch, w_out, w_gate, w_up, conv_w, conv_b, w_down):
    b, t, d = x.shape
    bt = b * t
    depth = w_in.shape[0]
    tabs = _static_tables(t)
    biases = _bias_tables(rel_bias, t)
    n_chunks = t // CMP_STRIDE
    fox_tq = min(512, t)
    x2 = x.reshape(bt, d)
    for l in range(depth):
        w_qkv, aux, w_gates, w_merge, wb = _prep_layer(w_in[l], qk_gain[l], w_branch[l])
        h = _rmsnorm(x2, norm_mix[l])
        qkv = _inproj_qkv(h, w_qkv, aux, tabs['bd']).reshape(N_CB, b, t, LANES)
        graw, ga = _inproj_gates(h, w_gates, tabs['eg'])
        mg = _inproj_merge(h, w_merge, d)

        chunks = jnp.stack([qkv[CB_AKC], qkv[CB_AVC]]).reshape(2, b, n_chunks, CMP_STRIDE, KV_GROUPS, HEAD_DIM)
        chunks = jnp.transpose(chunks, (0, 1, 4, 2, 3, 5)).reshape(2, b, KV_GROUPS, n_chunks, CMP_STRIDE * HEAD_DIM)
        cmp_kv = _compress(chunks, *_prep_compress(cmp_pos[l], cmp_w1[l], cmp_w2[l], qk_gain[l, 1]), tabs['bd'])
        o_cmp, qa = _cmp_select(qkv, cmp_kv, biases['cmp'], tabs['ovt'], tabs['n_slc'])
        o_slc = _slc_attention(qa, qkv, tabs['key_block'], biases['near'])
        no_sinks = jnp.zeros((N_HEADS, LANES), F32)
        o_win = _banded_attention(qkv, CB_AQ, CB_AKW, CB_AVW, biases['win'], no_sinks, False)

        sink_tab = jnp.broadcast_to(sinks[l][:, None] * LOG2E, (N_HEADS, LANES))
        o_b = _banded_attention(qkv, CB_BQ, CB_BK, CB_BV, biases['swa'], sink_tab, True)

        fb_row = jnp.zeros((1, LANES), F32).at[0, N_GATE_COLS:N_GATE_COLS + N_HEADS].set(forget_bias[l])
        o_c = _fox_attention(qkv, *_fox_prep(graw, fb_row, b, t), fox_tq)

        flat = lambda o: o.reshape(N_QBLK, bt, LANES)
        x2 = _merge(flat(o_cmp), flat(o_slc), flat(o_win), flat(o_b), flat(o_c), ga, mg, wb,
                    w_out[l].astype(BF16), x2)
        x2 = _ffn(x2, norm_ffn[l], w_gate[l].astype(BF16), w_up[l].astype(BF16), conv_w[l], conv_b[l],
                  w_down[l].astype(BF16), t)
    return x2.reshape(b, t, d)
```

```python
import functools
import math

import numpy as np
import jax
import jax.numpy as jnp
from jax import lax
from jax.experimental import pallas as pl
from jax.experimental.pallas import tpu as pltpu

F32 = jnp.float32
BF16 = jnp.bfloat16

HEAD_DIM = 64
LANES = 128
N_HEADS = 8
KV_GROUPS = 2
HEADS_PER_GROUP = N_HEADS // KV_GROUPS
N_QBLK = N_HEADS * HEAD_DIM // LANES
MIX_WIDTH = N_HEADS * HEAD_DIM
N_BRANCH = 3
Q_BLOCK = 128
CMP_LEN = 32
CMP_STRIDE = 16
CMP_HIDDEN = 256
SLC_LEN = 64
SLC_TOPK = 8
NSA_WINDOW = 256
SWA_WINDOW = 128
N_BUCKETS = 32
MAX_DISTANCE = 128
CONV_WIDTH = 3
RMS_EPS = 1e-6
LOG2E = math.log2(math.e)
NEG = -1e30
FORCE_SCORE = 1e9
SLC_TILE = 256
VMEM_LIMIT = 48 * 1024 * 1024

CB_AQ, CB_BQ, CB_CQ, CB_CK = 0, 4, 8, 12
CB_AKS, CB_AKW, CB_BK, CB_AVS = 16, 17, 18, 19
CB_CV, CB_AKC, CB_AVC, CB_AVW, CB_BV = 20, 24, 25, 26, 27
N_CB = 28
N_NORM_CB = 20
MM_CHUNK = 256
N_GATE_COLS = N_HEADS * N_BRANCH


def _cparams(*sem):
    return pltpu.CompilerParams(dimension_semantics=sem, vmem_limit_bytes=VMEM_LIMIT)


def _split_hi_lo(v):
    hi = v.astype(BF16)
    return hi, (v - hi.astype(F32)).astype(BF16)


def _dot(a, b):
    return jnp.dot(a, b, preferred_element_type=F32)


def _dot_nt(a, b):
    return lax.dot_general(a, b, (((1,), (1,)), ((), ())), preferred_element_type=F32)


def _gelu_tanh(x):
    return x * (0.5 * (1.0 + jnp.tanh(math.sqrt(2.0 / math.pi) * (x + 0.044715 * (x * x * x)))))


def _head_rms(y, bd, gain):
    ms = _dot((y * y).astype(BF16), bd) * (1.0 / HEAD_DIM)
    return (y * lax.rsqrt(ms + RMS_EPS)) * gain


def _lane_half_mask(g):
    return (lax.broadcasted_iota(jnp.int32, (1, LANES), 1) // HEAD_DIM) == g


def _stack_group_queries(q_ref, g):
    mask = _lane_half_mask(g)
    return jnp.concatenate([jnp.where(mask, q_ref[p, 0], jnp.zeros_like(q_ref[p, 0]))
                            for p in range(N_QBLK)], axis=0)


def _merge_group_outputs(o_ref, o_g0, o_g1):
    low = _lane_half_mask(0)
    tq = o_ref.shape[2]
    for p in range(N_QBLK):
        rows = slice(p * tq, (p + 1) * tq)
        o_ref[p, 0] = jnp.where(low, o_g0[rows], o_g1[rows]).astype(o_ref.dtype)


def _rmsnorm_kernel(x_ref, g_ref, o_ref):
    x = x_ref[...]
    ms = jnp.mean(x * x, axis=-1, keepdims=True)
    o_ref[...] = ((x * lax.rsqrt(ms + RMS_EPS)) * g_ref[...]).astype(o_ref.dtype)


def _rmsnorm(x2, g, tm=512):
    bt, d = x2.shape
    return pl.pallas_call(
        _rmsnorm_kernel,
        grid=(bt // tm,),
        in_specs=[pl.BlockSpec((tm, d), lambda i: (i, 0)), pl.BlockSpec((1, d), lambda i: (0, 0))],
        out_specs=pl.BlockSpec((tm, d), lambda i: (i, 0)),
        out_shape=jax.ShapeDtypeStruct((bt, d), BF16),
        compiler_params=_cparams("parallel"),
        name="rmsnorm",
    )(x2, g.reshape(1, d))


def _inproj_qkv_kernel(h_ref, w_ref, aux_ref, bd_ref, o_ref, *, n_norm_tiles):
    per_chunk = MM_CHUNK // LANES

    def emit(normalise):
        for c in range(w_ref.shape[1] // MM_CHUNK):
            cols = slice(c * MM_CHUNK, (c + 1) * MM_CHUNK)
            y = _dot(h_ref[...], w_ref[:, cols])
            if normalise:
                normed = _head_rms(y, bd_ref[...], aux_ref[0:1, cols])
                y = jnp.where(aux_ref[1:2, cols] > 0.5, normed, y)
            for s in range(per_chunk):
                o_ref[c * per_chunk + s] = y[:, s * LANES:(s + 1) * LANES].astype(o_ref.dtype)

    pl.when(pl.program_id(1) < n_norm_tiles)(lambda: emit(True))
    pl.when(pl.program_id(1) >= n_norm_tiles)(lambda: emit(False))


def _inproj_qkv(h, w, aux, bd, tm=1024, tn=512):
    bt, d = h.shape
    n = w.shape[1]
    n_sub = tn // LANES
    assert N_NORM_CB % n_sub == 0
    return pl.pallas_call(
        functools.partial(_inproj_qkv_kernel, n_norm_tiles=N_NORM_CB // n_sub),
        grid=(bt // tm, n // tn),
        in_specs=[pl.BlockSpec((tm, d), lambda i, j: (i, 0)),
                  pl.BlockSpec((d, tn), lambda i, j: (0, j)),
                  pl.BlockSpec((8, tn), lambda i, j: (0, j)),
                  pl.BlockSpec(bd.shape, lambda i, j: (0, 0))],
        out_specs=pl.BlockSpec((n_sub, tm, LANES), lambda i, j: (j, i, 0)),
        out_shape=jax.ShapeDtypeStruct((n // LANES, bt, LANES), BF16),
        compiler_params=_cparams("parallel", "arbitrary"),
        name="inproj_qkv",
    )(h, w, aux, bd)


def _inproj_gates_kernel(h_ref, w_ref, eg_ref, raw_ref, ga_ref):
    y = _dot(h_ref[...], w_ref[...])
    raw_ref[...] = y
    ga = _dot(jax.nn.sigmoid(y).astype(BF16), eg_ref[...])
    for n in range(N_BRANCH):
        ga_ref[n] = ga[:, n * MIX_WIDTH:(n + 1) * MIX_WIDTH].astype(ga_ref.dtype)


def _inproj_gates(h, w, eg, tm=1024):
    bt, d = h.shape
    return pl.pallas_call(
        _inproj_gates_kernel,
        grid=(bt // tm,),
        in_specs=[pl.BlockSpec((tm, d), lambda i: (i, 0)),
                  pl.BlockSpec((d, LANES), lambda i: (0, 0)),
                  pl.BlockSpec((LANES, N_BRANCH * MIX_WIDTH), lambda i: (0, 0))],
        out_specs=[pl.BlockSpec((tm, LANES), lambda i: (i, 0)),
                   pl.BlockSpec((N_BRANCH, tm, MIX_WIDTH), lambda i: (0, i, 0))],
        out_shape=[jax.ShapeDtypeStruct((bt, LANES), F32),
                   jax.ShapeDtypeStruct((N_BRANCH, bt, MIX_WIDTH), BF16)],
        compiler_params=_cparams("parallel"),
        name="inproj_gates",
    )(h, w, eg)


def _inproj_merge_kernel(h_ref, w_ref, o_ref):
    for c in range(w_ref.shape[1] // MM_CHUNK):
        cols = slice(c * MM_CHUNK, (c + 1) * MM_CHUNK)
        o_ref[0, :, cols] = jax.nn.sigmoid(_dot(h_ref[...], w_ref[:, cols])).astype(o_ref.dtype)


def _inproj_merge(h, w, d_model, tm=1024, tn=1024):
    bt, d = h.shape
    n = w.shape[1]
    per = d_model // tn
    return pl.pallas_call(
        _inproj_merge_kernel,
        grid=(bt // tm, n // tn),
        in_specs=[pl.BlockSpec((tm, d), lambda i, j: (i, 0)),
                  pl.BlockSpec((d, tn), lambda i, j: (0, j))],
        out_specs=pl.BlockSpec((1, tm, tn), lambda i, j: (j // per, i, j % per)),
        out_shape=jax.ShapeDtypeStruct((n // d_model, bt, d_model), BF16),
        compiler_params=_cparams("parallel", "arbitrary"),
        name="inproj_merge",
    )(h, w)


def _compress_kernel(c_ref, pos_ref, w1_ref, w2_ref, aux_ref, bd_ref, o_ref):
    n_chunks = c_ref.shape[3]
    y = jnp.zeros((n_chunks, LANES), F32)
    for g in range(KV_GROUPS):
        c = c_ref[0, 0, g].astype(F32)
        a = _dot((c + pos_ref[0, 0]).astype(BF16), w1_ref[0, 0])
        b = _dot((c + pos_ref[0, 1]).astype(BF16), w1_ref[0, 1])
        hid = _gelu_tanh(a + pltpu.roll(b, n_chunks - 1, 0))
        y = y + _dot(hid.astype(BF16), w2_ref[0, g])
    normed = _head_rms(y, bd_ref[0:LANES, 0:LANES], aux_ref[0, 0:1, :])
    o_ref[0, 0] = jnp.where(aux_ref[0, 1:2, :] > 0.5, normed, y).astype(o_ref.dtype)


def _compress(chunks, pos, w1, w2p, aux, bd):
    _, b, g, n_chunks, cw = chunks.shape
    return pl.pallas_call(
        _compress_kernel,
        grid=(2, b),
        in_specs=[pl.BlockSpec((1, 1, g, n_chunks, cw), lambda w, i: (w, i, 0, 0, 0)),
                  pl.BlockSpec((1, 2, 1, cw), lambda w, i: (w, 0, 0, 0)),
                  pl.BlockSpec((1, 2, cw, CMP_HIDDEN), lambda w, i: (w, 0, 0, 0)),
                  pl.BlockSpec((1, g, CMP_HIDDEN, LANES), lambda w, i: (w, 0, 0, 0)),
                  pl.BlockSpec((1, 8, LANES), lambda w, i: (w, 0, 0)),
                  pl.BlockSpec(bd.shape, lambda w, i: (0, 0))],
        out_specs=pl.BlockSpec((1, 1, n_chunks, LANES), lambda w, i: (w, i, 0, 0)),
        out_shape=jax.ShapeDtypeStruct((2, b, n_chunks, LANES), BF16),
        compiler_params=_cparams("parallel", "parallel"),
        name="nsa_compress",
    )(chunks, pos, w1, w2p, aux, bd)


def _cmp_select_kernel(q_ref, kc_ref, vc_ref, bias_ref, ovt_ref, o_ref, qa_ref, *, n_slc, n_top):
    i = pl.program_id(1)
    ncp = kc_ref.shape[2]
    row_t = i * Q_BLOCK + lax.broadcasted_iota(jnp.int32, (Q_BLOCK, 1), 0)
    row_valid = jnp.concatenate([row_t >= CMP_LEN - 1] * HEADS_PER_GROUP, axis=0)
    jrow = lax.broadcasted_iota(jnp.int32, (n_slc, Q_BLOCK), 0)
    cur = (i * Q_BLOCK + lax.broadcasted_iota(jnp.int32, (n_slc, Q_BLOCK), 1)) // SLC_LEN
    block_valid = jrow <= cur

    def force(imp):
        forced = jnp.where(jrow == cur, FORCE_SCORE, jnp.where(jrow == cur - 1, FORCE_SCORE, imp))
        return jnp.where(jrow == 0, FORCE_SCORE, forced)
    outs = []
    for g in range(KV_GROUPS):
        qop = _stack_group_queries(q_ref, g)
        rows = HEADS_PER_GROUP * Q_BLOCK
        s = _dot_nt(qop, kc_ref[0, 0]) + bias_ref[g * HEADS_PER_GROUP:(g + 1) * HEADS_PER_GROUP].reshape(rows, ncp)
        e = jnp.exp2(s - jnp.max(s, axis=-1, keepdims=True))
        inv = jnp.where(row_valid, 1.0 / jnp.sum(e, axis=-1, keepdims=True), 0.0)
        p = e * inv
        outs.append(_dot(p.astype(BF16), vc_ref[0, 0]))
        psum = p[0:Q_BLOCK]
        for r in range(1, HEADS_PER_GROUP):
            psum = psum + p[r * Q_BLOCK:(r + 1) * Q_BLOCK]
        hi, lo = _split_hi_lo(psum)
        imp_t = (_dot_nt(ovt_ref[...], hi) + _dot_nt(ovt_ref[...], lo))[0:n_slc]
        score = jnp.where(block_valid, force(imp_t), NEG)
        groups = [score[a:a + 8] for a in range(0, n_slc, 8)]
        ranks = [jnp.zeros((8, Q_BLOCK), F32) for _ in groups]
        sub = lax.broadcasted_iota(jnp.int32, (8, Q_BLOCK), 0)
        for j in range(n_slc):
            other = score[j:j + 1, :]
            for a, blk in enumerate(groups):
                ge = jnp.where(other >= blk, 1.0, 0.0)
                if 8 * a > j:
                    ranks[a] = ranks[a] + ge
                elif 8 * a + 7 <= j:
                    ranks[a] = ranks[a] + jnp.where(other > blk, 1.0, 0.0)
                else:
                    ranks[a] = ranks[a] + jnp.where(sub > j - 8 * a, ge, jnp.where(other > blk, 1.0, 0.0))
        rank = jnp.concatenate(ranks, axis=0)
        sel_bias = jnp.where(rank < n_top, jnp.where(score > NEG / 2, 0.0, NEG), NEG)
        if n_slc < HEAD_DIM:
            sel_bias = jnp.concatenate([sel_bias, jnp.full((HEAD_DIM - n_slc, Q_BLOCK), NEG, F32)], axis=0)
        sel_t = jnp.concatenate([sel_bias, sel_bias], axis=0).T.astype(qa_ref.dtype)
        own_half = _lane_half_mask(g)
        for p in range(N_QBLK):
            qa_ref[g * HEADS_PER_GROUP + p, 0] = jnp.where(own_half, q_ref[p, 0], sel_t)
    _merge_group_outputs(o_ref, outs[0], outs[1])


def _cmp_select(qkv, cmp_kv, bias_c, ovt, n_slc):
    _, b, t, _ = qkv.shape
    nt = t // Q_BLOCK
    ncp = cmp_kv.shape[2]
    assert n_slc <= HEAD_DIM
    return pl.pallas_call(
        functools.partial(_cmp_select_kernel, n_slc=n_slc, n_top=min(SLC_TOPK, n_slc)),
        grid=(b, nt),
        in_specs=[pl.BlockSpec((N_QBLK, 1, Q_BLOCK, LANES), lambda bi, i: (CB_AQ // N_QBLK, bi, i, 0)),
                  pl.BlockSpec((1, 1, ncp, LANES), lambda bi, i: (0, bi, 0, 0)),
                  pl.BlockSpec((1, 1, ncp, LANES), lambda bi, i: (1, bi, 0, 0)),
                  pl.BlockSpec((N_HEADS, Q_BLOCK, ncp), lambda bi, i: (0, i, 0)),
                  pl.BlockSpec((LANES, ncp), lambda bi, i: (0, 0))],
        out_specs=[pl.BlockSpec((N_QBLK, 1, Q_BLOCK, LANES), lambda bi, i: (0, bi, i, 0)),
                   pl.BlockSpec((N_HEADS, 1, Q_BLOCK, LANES), lambda bi, i: (0, bi, i, 0))],
        out_shape=[jax.ShapeDtypeStruct((N_QBLK, b, t, LANES), BF16),
                   jax.ShapeDtypeStruct((N_HEADS, b, t, LANES), BF16)],
        compiler_params=_cparams("parallel", "parallel"),
        name="nsa_cmp_select",
    )(qkv, cmp_kv, cmp_kv, bias_c, ovt)


def _flash_update(s, m, acc, v):
    m_new = jnp.maximum(m, jnp.max(s, axis=-1, keepdims=True))
    p = jnp.exp2(s - m_new)
    return m_new, jnp.exp2(m - m_new) * acc + _dot(p.astype(BF16), v)


def _normalise(acc):
    return acc / pltpu.roll(acc, HEAD_DIM, 1)


def _augment_keys_values(ka_scr, va_scr, k2, xk, v2):
    low = _lane_half_mask(0)
    ones = jnp.ones_like(v2)
    ka_scr[0] = jnp.where(low, k2, xk)
    ka_scr[1] = jnp.where(low, xk, k2)
    va_scr[0] = jnp.where(low, v2, ones)
    va_scr[1] = jnp.where(low, ones, v2)


def _store_pair(scr, s):
    scr[0], scr[1] = s


def _load_pair(scr):
    return scr[0], scr[1]


def _pipelined_sweep(logits, update, n_tiles, sa_scr, sb_scr, carry):
    def single(j, c):
        _store_pair(sa_scr, logits(j))
        return update(j, _load_pair(sa_scr), c)

    odd = n_tiles % 2
    carry = lax.fori_loop(0, odd, single, carry)
    _store_pair(sa_scr, logits(odd))

    def pair(jp, c):
        j = odd + 2 * jp
        _store_pair(sb_scr, logits(j + 1))
        c = update(j, _load_pair(sa_scr), c)
        _store_pair(sa_scr, logits(j + 2))
        return update(j + 1, _load_pair(sb_scr), c)

    return lax.fori_loop(0, n_tiles // 2, pair, carry)


def _slc_kernel(qa_ref, k_ref, v_ref, hot_ref, nb_ref, o_ref, ka_scr, va_scr, sa_scr, sb_scr):
    i = pl.program_id(1)
    tq = qa_ref.shape[2]
    rows = HEADS_PER_GROUP * tq

    @pl.when(i == 0)
    def _():
        _augment_keys_values(ka_scr, va_scr, k_ref[0, 0], hot_ref[...], v_ref[0, 0])

    heads = [slice(g * HEADS_PER_GROUP, (g + 1) * HEADS_PER_GROUP) for g in range(KV_GROUPS)]
    qs = [qa_ref[hs, 0].reshape(rows, LANES) for hs in heads]

    def keys(j):
        return pl.ds(pl.multiple_of(j * tq, tq), tq)

    def logits(j):
        return tuple(_dot_nt(qs[g], ka_scr[g, keys(j), :]) for g in range(KV_GROUPS))

    def update(j, s, carry, which=None, pen=None):
        new = []
        for g in range(KV_GROUPS):
            sg = s[g]
            if which is not None:
                sg = (sg.reshape(HEADS_PER_GROUP, tq, tq) + nb_ref[which, heads[g]]).reshape(rows, tq)
            if pen is not None:
                sg = sg + pen
            new.append(_flash_update(sg, *carry[g], va_scr[g, keys(j), :]))
        return tuple(new)

    init = (jnp.full((rows, 1), NEG, F32), jnp.zeros((rows, LANES), F32))
    carry = _pipelined_sweep(logits, update, jnp.maximum(i - 1, 0), sa_scr, sb_scr, (init, init))
    _store_pair(sb_scr, logits(i))
    carry = update(jnp.maximum(i - 1, 0), _load_pair(sa_scr), carry, which=0, pen=jnp.where(i >= 1, 0.0, NEG))
    (_, acc0), (_, acc1) = update(i, _load_pair(sb_scr), carry, which=1)
    _merge_group_outputs(o_ref, _normalise(acc0), _normalise(acc1))


def _slc_attention(qa, qkv, hot2, near_bias):
    _, b, t, _ = qa.shape
    tq = near_bias.shape[2]
    return pl.pallas_call(
        _slc_kernel,
        grid=(b, t // tq),
        in_specs=[pl.BlockSpec((N_HEADS, 1, tq, LANES), lambda bi, i: (0, bi, i, 0)),
                  pl.BlockSpec((1, 1, t, LANES), lambda bi, i: (CB_AKS, bi, 0, 0)),
                  pl.BlockSpec((1, 1, t, LANES), lambda bi, i: (CB_AVS, bi, 0, 0)),
                  pl.BlockSpec(hot2.shape, lambda bi, i: (0, 0)),
                  pl.BlockSpec(near_bias.shape, lambda bi, i: (0, 0, 0, 0))],
        out_specs=pl.BlockSpec((N_QBLK, 1, tq, LANES), lambda bi, i: (0, bi, i, 0)),
        out_shape=jax.ShapeDtypeStruct((N_QBLK, b, t, LANES), BF16),
        scratch_shapes=[pltpu.VMEM((KV_GROUPS, t, LANES), BF16), pltpu.VMEM((KV_GROUPS, t, LANES), BF16),
                        pltpu.VMEM((KV_GROUPS, HEADS_PER_GROUP * tq, tq), F32),
                        pltpu.VMEM((KV_GROUPS, HEADS_PER_GROUP * tq, tq), F32)],
        compiler_params=_cparams("parallel", "arbitrary"),
        name="nsa_slc_attention",
    )(qa, qkv, qkv, hot2, near_bias)


BAND_TQ = 256


def _banded_kernel(*refs, n_blk, use_sinks):
    q_ref = refs[0]
    k_refs = refs[1:1 + n_blk]
    v_refs = refs[1 + n_blk:1 + 2 * n_blk]
    bias_ref, o_ref = refs[1 + 2 * n_blk:]
    i = pl.program_id(1)
    tq = q_ref.shape[2]
    n_keys = n_blk * Q_BLOCK
    rows = HEADS_PER_GROUP * tq
    first_key = (i + 1) * tq - n_keys
    col = lax.broadcasted_iota(jnp.int32, (1, n_keys), 1)
    pad_pen = jnp.where(first_key + col >= 0, 0.0, NEG)
    kcat = jnp.concatenate([r[0, 0] for r in k_refs], axis=0)
    vcat = jnp.concatenate([r[0, 0] for r in v_refs], axis=0)
    if use_sinks:
        sink_row = lax.broadcasted_iota(jnp.int32, (n_keys, 1), 0) == 0
        kcat = jnp.where(sink_row, jnp.zeros_like(kcat), kcat)
        vcat = jnp.where(sink_row, jnp.zeros_like(vcat), vcat)
        pad_pen = jnp.where(col == 0, 0.0, pad_pen)
    outs = []
    for g in range(KV_GROUPS):
        own_half = _lane_half_mask(g)
        qop = jnp.concatenate([jnp.where(own_half, q_ref[p, 0], jnp.zeros_like(q_ref[p, 0]))
                               for p in range(N_QBLK)], axis=0)
        hs = slice(g * HEADS_PER_GROUP, (g + 1) * HEADS_PER_GROUP)
        s = _dot_nt(qop, kcat) + bias_ref[hs].reshape(rows, n_keys) + pad_pen
        e = jnp.exp2(s - jnp.max(s, axis=-1, keepdims=True))
        acc = _dot(e.astype(BF16), jnp.where(own_half, vcat, jnp.ones_like(vcat)))
        outs.append(_normalise(acc))
    _merge_group_outputs(o_ref, outs[0], outs[1])


def _banded_attention(qkv, cb_q, cb_k, cb_v, bias, use_sinks):
    _, b, t, _ = qkv.shape
    tq = bias.shape[1]
    n_blk = bias.shape[2] // Q_BLOCK
    per_tile = tq // Q_BLOCK

    def kv_spec(cb, back):
        return pl.BlockSpec((1, 1, Q_BLOCK, LANES),
                            lambda bi, i: (cb, bi, jnp.maximum((i + 1) * per_tile - 1 - back, 0), 0))

    backs = [n_blk - 1 - jb for jb in range(n_blk)]
    return pl.pallas_call(
        functools.partial(_banded_kernel, n_blk=n_blk, use_sinks=use_sinks),
        grid=(b, t // tq),
        in_specs=([pl.BlockSpec((N_QBLK, 1, tq, LANES), lambda bi, i: (cb_q // N_QBLK, bi, i, 0))]
                  + [kv_spec(cb_k, back) for back in backs] + [kv_spec(cb_v, back) for back in backs]
                  + [pl.BlockSpec(bias.shape, lambda bi, i: (0, 0, 0))]),
        out_specs=pl.BlockSpec((N_QBLK, 1, tq, LANES), lambda bi, i: (0, bi, i, 0)),
        out_shape=jax.ShapeDtypeStruct((N_QBLK, b, t, LANES), BF16),
        compiler_params=_cparams("parallel", "parallel"),
        name="banded_attention",
    )(qkv, *([qkv] * (2 * n_blk)), bias)


FOX_EXTRA = 6


def _fox_prep_kernel(g_ref, fb_ref, pq_ref, pk_ref, oq_ref, ok_ref, xq_ref, xk_ref, carry_scr):
    @pl.when(pl.program_id(1) == 0)
    def _():
        carry_scr[...] = jnp.zeros_like(carry_scr)

    x = g_ref[...] + fb_ref[...]
    logf = jnp.minimum(x, 0.0) - jnp.log(1.0 + jnp.exp(-jnp.abs(x)))
    ch = x.shape[0]
    lower = jnp.where(lax.broadcasted_iota(jnp.int32, (ch, ch), 0)
                      >= lax.broadcasted_iota(jnp.int32, (ch, ch), 1), 1.0, 0.0).astype(BF16)

    def split3(v):
        hi, rest = v.astype(BF16), v - v.astype(BF16).astype(F32)
        return (hi,) + _split_hi_lo(rest)

    cs = sum(_dot(lower, part) for part in split3(logf)) + carry_scr[0:1, :]
    carry_scr[0:1, :] = cs[ch - 1:ch, :]
    parts = split3(cs * LOG2E)
    xq = sum(_dot(part, pq_ref[n]) for n, part in enumerate(parts)) + oq_ref[...]
    xk = sum(_dot(part, pk_ref[n]) for n, part in enumerate(parts)) + ok_ref[...]
    for p in range(N_QBLK):
        xq_ref[p, 0] = xq[:, p * LANES:(p + 1) * LANES].astype(xq_ref.dtype)
        xk_ref[p, 0] = xk[:, p * LANES:(p + 1) * LANES].astype(xk_ref.dtype)


def _fox_prep(graw, fb_row, b, t, chunk=512):
    chunk = min(chunk, t)
    nch = t // chunk
    pq = np.zeros((3, LANES, N_QBLK * LANES), np.float32)
    pk = np.zeros((3, LANES, N_QBLK * LANES), np.float32)
    oq = np.zeros((1, N_QBLK * LANES), np.float32)
    ok = np.zeros((1, N_QBLK * LANES), np.float32)
    for h in range(N_HEADS):
        col = (h // 2) * LANES + (HEAD_DIM if h % 2 == 0 else 0)
        for n in range(3):
            pq[n, N_GATE_COLS + h, col + n] = 1.0
            pk[n, N_GATE_COLS + h, col + 3 + n] = -1.0
        oq[0, col + 3:col + FOX_EXTRA] = 1.0
        ok[0, col:col + 3] = 1.0
    const = lambda a: pl.BlockSpec(a.shape, lambda bi, c: (0,) * a.ndim)
    out_spec = pl.BlockSpec((N_QBLK, 1, chunk, LANES), lambda bi, c: (0, bi, c, 0))
    return pl.pallas_call(
        _fox_prep_kernel,
        grid=(b, nch),
        in_specs=[pl.BlockSpec((chunk, LANES), lambda bi, c: (bi * nch + c, 0)),
                  const(fb_row), const(pq), const(pk), const(oq), const(ok)],
        out_specs=[out_spec, out_spec],
        out_shape=[jax.ShapeDtypeStruct((N_QBLK, b, t, LANES), BF16)] * 2,
        scratch_shapes=[pltpu.VMEM((8, LANES), F32)],
        compiler_params=_cparams("parallel", "arbitrary"),
        name="fox_prep",
    )(graw, fb_row, jnp.asarray(pq, BF16), jnp.asarray(pk, BF16), jnp.asarray(oq), jnp.asarray(ok))


def _fox_kernel(q_ref, xq_ref, k_ref, xk_ref, v_ref, o_ref, ka_scr, va_scr, sa_scr, sb_scr, *, tq):
    i = pl.program_id(2)

    @pl.when(i == 0)
    def _():
        _augment_keys_values(ka_scr, va_scr, k_ref[0, 0], xk_ref[0, 0], v_ref[0, 0])

    low = _lane_half_mask(0)
    q2, xq = q_ref[0, 0], xq_ref[0, 0]
    qs = [jnp.where(low, q2, xq), jnp.where(low, xq, q2)]
    causal_pen = jnp.where(lax.broadcasted_iota(jnp.int32, (tq, tq), 0)
                           >= lax.broadcasted_iota(jnp.int32, (tq, tq), 1), 0.0, NEG)

    def keys(j):
        return pl.ds(pl.multiple_of(j * tq, tq), tq)

    def logits(j):
        return tuple(_dot_nt(qs[hh], ka_scr[hh, keys(j), :]) for hh in range(2))

    def update(j, s, carry, pen=None):
        return tuple(_flash_update(s[hh] if pen is None else s[hh] + pen, *carry[hh], va_scr[hh, keys(j), :])
                     for hh in range(2))

    init = (jnp.full((tq, 1), NEG, F32), jnp.zeros((tq, LANES), F32))
    carry = _pipelined_sweep(logits, update, i, sa_scr, sb_scr, (init, init))
    (_, acc0), (_, acc1) = update(i, _load_pair(sa_scr), carry, pen=causal_pen)
    o_ref[0, 0] = jnp.where(low, _normalise(acc0), _normalise(acc1)).astype(o_ref.dtype)


def _fox_attention(qkv, xq, xk, tq):
    _, b, t, _ = qkv.shape
    q_spec = lambda cb: pl.BlockSpec((1, 1, tq, LANES), lambda bi, p, i: (cb + p, bi, i, 0))
    kv_spec = lambda cb: pl.BlockSpec((1, 1, t, LANES), lambda bi, p, i: (cb + p, bi, 0, 0))
    return pl.pallas_call(
        functools.partial(_fox_kernel, tq=tq),
        grid=(b, N_QBLK, t // tq),
        in_specs=[q_spec(CB_CQ), q_spec(0), kv_spec(CB_CK), kv_spec(0), kv_spec(CB_CV)],
        out_specs=pl.BlockSpec((1, 1, tq, LANES), lambda bi, p, i: (p, bi, i, 0)),
        out_shape=jax.ShapeDtypeStruct((N_QBLK, b, t, LANES), BF16),
        scratch_shapes=[pltpu.VMEM((2, t, LANES), BF16), pltpu.VMEM((2, t, LANES), BF16),
                        pltpu.VMEM((2, tq, tq), F32), pltpu.VMEM((2, tq, tq), F32)],
        compiler_params=_cparams("parallel", "parallel", "arbitrary"),
        name="fox_attention",
    )(qkv, xq, qkv, xk, qkv)


def _merge_kernel(ocmp_ref, oslc_ref, owin_ref, ob_ref, oc_ref, ga_ref, mg_ref, wb_ref, wo_ref, x_ref, o_ref):
    def cat(ref):
        return jnp.concatenate([ref[p] for p in range(N_QBLK)], axis=-1)

    oa = (ga_ref[0].astype(F32) * cat(ocmp_ref).astype(F32)
          + ga_ref[1].astype(F32) * cat(oslc_ref).astype(F32)
          + ga_ref[2].astype(F32) * cat(owin_ref).astype(F32)).astype(BF16)
    branches = (oa, cat(ob_ref), cat(oc_ref))
    d = o_ref.shape[1]
    chunks = [slice(c, c + MM_CHUNK) for c in range(0, d, MM_CHUNK)]
    mix = jnp.concatenate(
        [sum(mg_ref[n, :, cols].astype(F32) * _dot(branches[n], wb_ref[n, :, cols]) for n in range(N_BRANCH))
         .astype(BF16) for cols in chunks], axis=-1)
    for cols in chunks:
        o_ref[:, cols] = x_ref[:, cols] + _dot(mix, wo_ref[:, cols])


def _merge(ocmp, oslc, owin, ob, oc, ga, mg, wb, wo, x2, tm=512):
    bt, d = x2.shape
    o_spec = pl.BlockSpec((N_QBLK, tm, LANES), lambda i: (0, i, 0))
    return pl.pallas_call(
        _merge_kernel,
        grid=(bt // tm,),
        in_specs=[o_spec] * 5 + [pl.BlockSpec((N_BRANCH, tm, MIX_WIDTH), lambda i: (0, i, 0)),
                                 pl.BlockSpec((N_BRANCH, tm, d), lambda i: (0, i, 0)),
                                 pl.BlockSpec(wb.shape, lambda i: (0, 0, 0)),
                                 pl.BlockSpec(wo.shape, lambda i: (0, 0)),
                                 pl.BlockSpec((tm, d), lambda i: (i, 0))],
        out_specs=pl.BlockSpec((tm, d), lambda i: (i, 0)),
        out_shape=jax.ShapeDtypeStruct((bt, d), F32),
        compiler_params=_cparams("parallel"),
        name="branch_merge",
    )(ocmp, oslc, owin, ob, oc, ga, mg, wb, wo, x2)


FFN_ROW_CHUNKS = 2


def _ffn_kernel(x_ref, xh_ref, g_ref, wg_ref, wu_ref, cw_ref, cb_ref, wd_ref, o_ref,
                h_scr, hh_scr, a_scr, acc_scr, *, tm, tiles_per_seq):
    i = pl.program_id(0)
    f = pl.program_id(1)

    def norm(x):
        ms = jnp.mean(x * x, axis=-1, keepdims=True)
        return ((x * lax.rsqrt(ms + RMS_EPS)) * g_ref[...]).astype(BF16)

    @pl.when(f == 0)
    def _():
        h_scr[...] = norm(x_ref[...])
        hh_scr[...] = norm(xh_ref[...])
        acc_scr[...] = jnp.zeros_like(acc_scr)

    tail = _dot(hh_scr[...], wg_ref[...]) * jnp.where(i % tiles_per_seq == 0, 0.0, 1.0)
    rc = tm // FFN_ROW_CHUNKS
    for r in range(FFN_ROW_CHUNKS):
        rows = slice(r * rc, (r + 1) * rc)
        a = _dot(h_scr[rows, :], wg_ref[...])
        a_scr[r, 0:8, :] = tail
        a_scr[r, 8:8 + rc, :] = a
        conv = (cw_ref[0:1, :] * a_scr[r, 6:6 + rc, :] + cw_ref[1:2, :] * a_scr[r, 7:7 + rc, :]
                + cw_ref[2:3, :] * a + cb_ref[...])
        act = _gelu_tanh(conv) * _dot(h_scr[rows, :], wu_ref[...])
        acc_scr[rows, :] += _dot(act.astype(BF16), wd_ref[...])
        tail = a[rc - 8:rc, :]

    @pl.when(f == pl.num_programs(1) - 1)
    def _():
        o_ref[...] = x_ref[...] + acc_scr[...]


def _ffn(x2, g, wg, wu, cw, cb, wd, t, tm=1024, tf=256):
    bt, d = x2.shape
    ff = wg.shape[1]
    halo_blocks = tm // 8
    return pl.pallas_call(
        functools.partial(_ffn_kernel, tm=tm, tiles_per_seq=t // tm),
        grid=(bt // tm, ff // tf),
        in_specs=[pl.BlockSpec((tm, d), lambda i, f: (i, 0)),
                  pl.BlockSpec((8, d), lambda i, f: (jnp.maximum(i * halo_blocks - 1, 0), 0)),
                  pl.BlockSpec((1, d), lambda i, f: (0, 0)),
                  pl.BlockSpec((d, tf), lambda i, f: (0, f)),
                  pl.BlockSpec((d, tf), lambda i, f: (0, f)),
                  pl.BlockSpec((CONV_WIDTH, tf), lambda i, f: (0, f)),
                  pl.BlockSpec((1, tf), lambda i, f: (0, f)),
                  pl.BlockSpec((tf, d), lambda i, f: (f, 0))],
        out_specs=pl.BlockSpec((tm, d), lambda i, f: (i, 0)),
        out_shape=jax.ShapeDtypeStruct((bt, d), F32),
        scratch_shapes=[pltpu.VMEM((tm, d), BF16), pltpu.VMEM((8, d), BF16),
                        pltpu.VMEM((FFN_ROW_CHUNKS, tm // FFN_ROW_CHUNKS + 8, tf), F32), pltpu.VMEM((tm, d), F32)],
        compiler_params=_cparams("parallel", "arbitrary"),
        name="conv_ffn",
    )(x2, x2, g.reshape(1, d), wg, wu, cw, cb.reshape(1, ff), wd)


def _t5_bucket_np(dist):
    max_exact = N_BUCKETS // 2
    d = np.maximum(dist, 0)
    ratio = np.log(np.maximum(d, 1) / max_exact) / math.log(MAX_DISTANCE / max_exact)
    large = np.minimum(max_exact + (ratio * (N_BUCKETS - max_exact)).astype(np.int64), N_BUCKETS - 1)
    return np.where(d < max_exact, d, large)


def _expand_kernel(tab_ref, idx_ref, o_ref):
    tab = tab_ref[...]
    onehot = jnp.where(lax.broadcasted_iota(jnp.int32, (tab.shape[1], idx_ref.shape[1]), 0) == idx_ref[...],
                       1.0, 0.0).astype(BF16)
    hi, rest = tab.astype(BF16), tab - tab.astype(BF16).astype(F32)
    mid, lo = _split_hi_lo(rest)
    o_ref[...] = _dot(hi, onehot) + _dot(mid, onehot) + _dot(lo, onehot)


def _bias_from_dist(tab_t, dist, valid, chunk=4096):
    heads = tab_t.shape[0]
    tab = jnp.zeros((heads, LANES), F32).at[:, :N_BUCKETS].set(tab_t).at[:, N_BUCKETS].set(NEG)
    idx = np.where(valid, _t5_bucket_np(dist), N_BUCKETS).reshape(1, -1).astype(np.int32)
    n = idx.shape[1]
    chunk = min(chunk, n)
    out = pl.pallas_call(
        _expand_kernel,
        grid=(n // chunk,),
        in_specs=[pl.BlockSpec((heads, LANES), lambda i: (0, 0)), pl.BlockSpec((1, chunk), lambda i: (0, i))],
        out_specs=pl.BlockSpec((heads, chunk), lambda i: (0, i)),
        out_shape=jax.ShapeDtypeStruct((heads, n), F32),
        compiler_params=_cparams("parallel"),
        name="rel_bias_expand",
    )(tab, jnp.asarray(idx))
    return out.reshape((heads,) + dist.shape)


def _cmp_bias_kernel(g_ref, o_ref):
    i = pl.program_id(0)
    ncp = o_ref.shape[2]
    for h in range(o_ref.shape[0]):
        o_ref[h] = pltpu.roll(g_ref[h], i * (Q_BLOCK // CMP_STRIDE), 1)[:, :ncp]


def _cmp_bias(g_tab, t):
    heads, _, width = g_tab.shape
    ncp = t // CMP_STRIDE
    return pl.pallas_call(
        _cmp_bias_kernel,
        grid=(t // Q_BLOCK,),
        in_specs=[pl.BlockSpec(g_tab.shape, lambda i: (0, 0, 0))],
        out_specs=pl.BlockSpec((heads, Q_BLOCK, ncp), lambda i: (0, i, 0)),
        out_shape=jax.ShapeDtypeStruct((heads, t, ncp), F32),
        compiler_params=_cparams("parallel"),
        name="cmp_bias_build",
    )(g_tab)


def _static_tables(t):
    ncp = t // CMP_STRIDE
    n_cmp = (t - CMP_LEN) // CMP_STRIDE + 1
    n_slc = t // SLC_LEN
    lane = np.arange(MM_CHUNK)
    bd = (lane[:, None] // HEAD_DIM == lane[None, :] // HEAD_DIM).astype(np.float32)
    eg = np.zeros((LANES, N_BRANCH * MIX_WIDTH), np.float32)
    for h in range(N_HEADS):
        base = (h % N_QBLK) * LANES + (h // N_QBLK) * HEAD_DIM
        for n in range(N_BRANCH):
            eg[h * N_BRANCH + n, n * MIX_WIDTH + base:n * MIX_WIDTH + base + HEAD_DIM] = 1.0
    c_start = np.arange(n_cmp) * CMP_STRIDE
    s_start = np.arange(n_slc) * SLC_LEN
    overlap = ((c_start[:, None] < s_start[None, :] + SLC_LEN) & (c_start[:, None] + CMP_LEN > s_start[None, :]))
    ovt = np.zeros((LANES, ncp), np.float32)
    ovt[:n_slc, :n_cmp] = overlap.T
    key_block = np.zeros((t, LANES), np.float32)
    key_block[np.arange(t), np.arange(t) // SLC_LEN] = 1.0
    key_block[np.arange(t), HEAD_DIM + np.arange(t) // SLC_LEN] = 1.0
    as_bf16 = lambda a: jnp.asarray(a, BF16)
    return dict(bd=as_bf16(bd), eg=as_bf16(eg), ovt=as_bf16(ovt), key_block=as_bf16(key_block), n_slc=n_slc)


def _bias_tables(rel_bias, t):
    n_cmp = (t - CMP_LEN) // CMP_STRIDE + 1
    ncp = t // CMP_STRIDE
    tab_a = rel_bias[:, :N_HEADS].T * LOG2E
    tab_b = rel_bias[:, N_HEADS:].T * LOG2E
    r = np.arange(Q_BLOCK)[:, None]

    def band(window):
        n_prev = -(-(window - 1) // Q_BLOCK)
        n_keys = n_prev * Q_BLOCK + BAND_TQ
        dist = (n_keys - BAND_TQ) + np.arange(BAND_TQ)[:, None] - np.arange(n_keys)[None, :]
        valid = (dist >= 0) & (dist < window)
        assert not valid[:, 0].any()
        return dist, valid

    m = np.arange(2 * ncp)
    m = np.where(m < ncp, m, m - 2 * ncp)[None, :]
    dist_c = r - (m * CMP_STRIDE + CMP_LEN - 1)
    assert n_cmp == ncp - 1 and (dist_c[:, m[0] == Q_BLOCK // CMP_STRIDE - 1] < 0).all()
    tile = np.arange(SLC_TILE)
    dist_n = np.stack([SLC_TILE * (1 - which) + tile[:, None] - tile[None, :] for which in range(2)])
    near = _bias_from_dist(tab_a - tab_a[:, N_BUCKETS - 1:], dist_n, dist_n >= 0)
    return dict(win=_bias_from_dist(tab_a, *band(NSA_WINDOW)),
                swa=_bias_from_dist(tab_b, *band(SWA_WINDOW)),
                cmp=_cmp_bias(_bias_from_dist(tab_a, dist_c, dist_c >= 0), t),
                near=jnp.transpose(near, (1, 0, 2, 3)))


def _pair_cols(w):
    parts = []
    for p in range(N_QBLK):
        parts += [w[..., p * HEAD_DIM:(p + 1) * HEAD_DIM],
                  w[..., (N_QBLK + p) * HEAD_DIM:(N_QBLK + p + 1) * HEAD_DIM]]
    return jnp.concatenate(parts, axis=-1)


def _prep_layer(w_in, qk_gain, w_branch):
    kv = KV_GROUPS * HEAD_DIM
    widths = [('a_q', MIX_WIDTH), ('a_kc', kv), ('a_vc', kv), ('a_ks', kv), ('a_vs', kv), ('a_kw', kv), ('a_vw', kv),
              ('a_gate', N_GATE_COLS), ('b_q', MIX_WIDTH), ('b_k', kv), ('b_v', kv),
              ('c_q', MIX_WIDTH), ('c_k', MIX_WIDTH), ('c_v', MIX_WIDTH), ('c_f', N_HEADS),
              ('merge', N_BRANCH * w_in.shape[0])]
    cols, off = {}, 0
    for name, w in widths:
        cols[name] = w_in[:, off:off + w]
        off += w
    scale = HEAD_DIM ** -0.5 * LOG2E
    tile = lambda gvec, n: jnp.tile(gvec, n)
    zeros = lambda n: jnp.zeros((n,), F32)
    ones = lambda n: jnp.ones((n,), F32)
    pieces = [(_pair_cols(cols['a_q']), tile(qk_gain[0] * scale, N_HEADS), ones(MIX_WIDTH)),
              (_pair_cols(cols['b_q']), tile(qk_gain[2] * scale, N_HEADS), ones(MIX_WIDTH)),
              (cols['c_q'], tile(qk_gain[4] * scale, N_HEADS), ones(MIX_WIDTH)),
              (cols['c_k'], tile(qk_gain[5], N_HEADS), ones(MIX_WIDTH)),
              (cols['a_ks'], tile(qk_gain[1], KV_GROUPS), ones(kv)),
              (cols['a_kw'], tile(qk_gain[1], KV_GROUPS), ones(kv)),
              (cols['b_k'], tile(qk_gain[3], KV_GROUPS), ones(kv)),
              (cols['a_vs'], zeros(kv), zeros(kv)),
              (cols['c_v'], zeros(MIX_WIDTH), zeros(MIX_WIDTH)),
              (cols['a_kc'], zeros(kv), zeros(kv)), (cols['a_vc'], zeros(kv), zeros(kv)),
              (cols['a_vw'], zeros(kv), zeros(kv)), (cols['b_v'], zeros(kv), zeros(kv))]
    w_qkv = jnp.concatenate([p[0] for p in pieces], axis=1).astype(BF16)
    n_qkv = w_qkv.shape[1]
    aux = jnp.zeros((8, n_qkv), F32)
    aux = aux.at[0].set(jnp.concatenate([p[1] for p in pieces])).at[1].set(jnp.concatenate([p[2] for p in pieces]))
    d = w_in.shape[0]
    w_gates = jnp.concatenate([cols['a_gate'], cols['c_f'],
                               jnp.zeros((d, LANES - N_GATE_COLS - N_HEADS), F32)], axis=1).astype(BF16)
    wb = jnp.stack([_pair_cols(w_branch[0].T).T, _pair_cols(w_branch[1].T).T, w_branch[2]]).astype(BF16)
    return w_qkv, aux, w_gates, cols['merge'].astype(BF16), wb


def _prep_compress(cmp_pos, cmp_w1, cmp_w2, k_gain):
    half = CMP_STRIDE * HEAD_DIM
    pos = cmp_pos.reshape(2, 2, 1, half)
    w1 = cmp_w1.reshape(2, 2, half, CMP_HIDDEN).astype(BF16)
    z = jnp.zeros_like(cmp_w2)
    w2p = jnp.stack([jnp.concatenate([cmp_w2, z], axis=-1), jnp.concatenate([z, cmp_w2], axis=-1)], axis=1).astype(BF16)
    aux = jnp.zeros((2, 8, LANES), F32)
    aux = aux.at[0, 0].set(jnp.tile(k_gain, KV_GROUPS)).at[0, 1].set(1.0)
    return pos, w1, w2p, aux


def kernel(x, rel_bias, norm_mix, norm_ffn, w_in, forget_bias, qk_gain, cmp_pos, cmp_w1, cmp_w2, sinks, w_branch, w_out, w_gate, w_up, conv_w, conv_b, w_down):
    b, t, d = x.shape
    bt = b * t
    depth = w_in.shape[0]
    tabs = _static_tables(t)
    biases = _bias_tables(rel_bias, t)
    n_chunks = t // CMP_STRIDE
    fox_tq = min(512, t)
    x2 = x.reshape(bt, d)
    for l in range(depth):
        w_qkv, aux, w_gates, w_merge, wb = _prep_layer(w_in[l], qk_gain[l], w_branch[l])
        h = _rmsnorm(x2, norm_mix[l])
        qkv = _inproj_qkv(h, w_qkv, aux, tabs['bd']).reshape(N_CB, b, t, LANES)
        graw, ga = _inproj_gates(h, w_gates, tabs['eg'])
        mg = _inproj_merge(h, w_merge, d)

        chunks = jnp.stack([qkv[CB_AKC], qkv[CB_AVC]]).reshape(2, b, n_chunks, CMP_STRIDE, KV_GROUPS, HEAD_DIM)
        chunks = jnp.transpose(chunks, (0, 1, 4, 2, 3, 5)).reshape(2, b, KV_GROUPS, n_chunks, CMP_STRIDE * HEAD_DIM)
        cmp_kv = _compress(chunks, *_prep_compress(cmp_pos[l], cmp_w1[l], cmp_w2[l], qk_gain[l, 1]), tabs['bd'])
        o_cmp, qa = _cmp_select(qkv, cmp_kv, biases['cmp'], tabs['ovt'], tabs['n_slc'])
        o_slc = _slc_attention(qa, qkv, tabs['key_block'], biases['near'])
        o_win = _banded_attention(qkv, CB_AQ, CB_AKW, CB_AVW, biases['win'], False)

        swa_bias = biases['swa'].at[:, :, 0].set(sinks[l][:, None] * LOG2E)
        o_b = _banded_attention(qkv, CB_BQ, CB_BK, CB_BV, swa_bias, True)

        fb_row = jnp.zeros((1, LANES), F32).at[0, N_GATE_COLS:N_GATE_COLS + N_HEADS].set(forget_bias[l])
        o_c = _fox_attention(qkv, *_fox_prep(graw, fb_row, b, t), fox_tq)

        flat = lambda o: o.reshape(N_QBLK, bt, LANES)
        x2 = _merge(flat(o_cmp), flat(o_slc), flat(o_win), flat(o_b), flat(o_c), ga, mg, wb,
                    w_out[l].astype(BF16), x2)
        x2 = _ffn(x2, norm_ffn[l], w_gate[l].astype(BF16), w_up[l].astype(BF16), conv_w[l], conv_b[l],
                  w_down[l].astype(BF16), t)
    return x2.reshape(b, t, d)
```

```python
import functools
import math

import numpy as np
import jax
import jax.numpy as jnp
from jax import lax
from jax.experimental import pallas as pl
from jax.experimental.pallas import tpu as pltpu

F32 = jnp.float32
BF16 = jnp.bfloat16

HEAD_DIM = 64
LANES = 128
N_HEADS = 8
KV_GROUPS = 2
HEADS_PER_GROUP = N_HEADS // KV_GROUPS
N_QBLK = N_HEADS * HEAD_DIM // LANES
MIX_WIDTH = N_HEADS * HEAD_DIM
N_BRANCH = 3
Q_BLOCK = 128
CMP_LEN = 32
CMP_STRIDE = 16
CMP_HIDDEN = 256
SLC_LEN = 64
SLC_TOPK = 8
NSA_WINDOW = 256
SWA_WINDOW = 128
N_BUCKETS = 32
MAX_DISTANCE = 128
CONV_WIDTH = 3
RMS_EPS = 1e-6
LOG2E = math.log2(math.e)
NEG = -1e30
FORCE_SCORE = 1e9
SLC_TILE = 256
VMEM_LIMIT = 48 * 1024 * 1024

CB_AQ, CB_BQ, CB_CQ, CB_CK = 0, 4, 8, 12
CB_AKS, CB_AKW, CB_BK, CB_AVS = 16, 17, 18, 19
CB_CV, CB_AKC, CB_AVC, CB_AVW, CB_BV = 20, 24, 25, 26, 27
N_CB = 28
N_NORM_CB = 20
MM_CHUNK = 256
N_GATE_COLS = N_HEADS * N_BRANCH


def _cparams(*sem):
    return pltpu.CompilerParams(dimension_semantics=sem, vmem_limit_bytes=VMEM_LIMIT)


def _split_hi_lo(v):
    hi = v.astype(BF16)
    return hi, (v - hi.astype(F32)).astype(BF16)


def _dot(a, b):
    return jnp.dot(a, b, preferred_element_type=F32)


def _dot_nt(a, b):
    return lax.dot_general(a, b, (((1,), (1,)), ((), ())), preferred_element_type=F32)


def _gelu_tanh(x):
    return x * (0.5 * (1.0 + jnp.tanh(math.sqrt(2.0 / math.pi) * (x + 0.044715 * (x * x * x)))))


def _head_rms(y, bd, gain):
    ms = _dot((y * y).astype(BF16), bd) * (1.0 / HEAD_DIM)
    return (y * lax.rsqrt(ms + RMS_EPS)) * gain


def _lane_half_mask(g):
    return (lax.broadcasted_iota(jnp.int32, (1, LANES), 1) // HEAD_DIM) == g


def _stack_group_queries(q_ref, g):
    mask = _lane_half_mask(g)
    return jnp.concatenate([jnp.where(mask, q_ref[p, 0], jnp.zeros_like(q_ref[p, 0]))
                            for p in range(N_QBLK)], axis=0)


def _merge_group_outputs(o_ref, o_g0, o_g1):
    low = _lane_half_mask(0)
    tq = o_ref.shape[2]
    for p in range(N_QBLK):
        rows = slice(p * tq, (p + 1) * tq)
        o_ref[p, 0] = jnp.where(low, o_g0[rows], o_g1[rows]).astype(o_ref.dtype)


def _rmsnorm_kernel(x_ref, g_ref, o_ref):
    x = x_ref[...]
    ms = jnp.mean(x * x, axis=-1, keepdims=True)
    o_ref[...] = ((x * lax.rsqrt(ms + RMS_EPS)) * g_ref[...]).astype(o_ref.dtype)


def _rmsnorm(x2, g, tm=512):
    bt, d = x2.shape
    return pl.pallas_call(
        _rmsnorm_kernel,
        grid=(bt // tm,),
        in_specs=[pl.BlockSpec((tm, d), lambda i: (i, 0)), pl.BlockSpec((1, d), lambda i: (0, 0))],
        out_specs=pl.BlockSpec((tm, d), lambda i: (i, 0)),
        out_shape=jax.ShapeDtypeStruct((bt, d), BF16),
        compiler_params=_cparams("parallel"),
        name="rmsnorm",
    )(x2, g.reshape(1, d))


def _inproj_qkv_kernel(h_ref, w_ref, aux_ref, bd_ref, o_ref, *, n_norm_tiles):
    per_chunk = MM_CHUNK // LANES

    half = h_ref.shape[0] // 2

    def emit(normalise):
        for rows in (slice(0, half), slice(half, 2 * half)):
            y_all = _dot(h_ref[rows, :], w_ref[...])
            for c in range(w_ref.shape[1] // MM_CHUNK):
                cols = slice(c * MM_CHUNK, (c + 1) * MM_CHUNK)
                y = y_all[:, cols]
                if normalise:
                    normed = _head_rms(y, bd_ref[...], aux_ref[0:1, cols])
                    y = jnp.where(aux_ref[1:2, cols] > 0.5, normed, y)
                for s in range(per_chunk):
                    o_ref[c * per_chunk + s, rows, :] = y[:, s * LANES:(s + 1) * LANES].astype(o_ref.dtype)

    pl.when(pl.program_id(1) < n_norm_tiles)(lambda: emit(True))
    pl.when(pl.program_id(1) >= n_norm_tiles)(lambda: emit(False))


def _inproj_qkv(h, w, aux, bd, tm=1024, tn=512):
    bt, d = h.shape
    n = w.shape[1]
    n_sub = tn // LANES
    assert N_NORM_CB % n_sub == 0
    return pl.pallas_call(
        functools.partial(_inproj_qkv_kernel, n_norm_tiles=N_NORM_CB // n_sub),
        grid=(bt // tm, n // tn),
        in_specs=[pl.BlockSpec((tm, d), lambda i, j: (i, 0)),
                  pl.BlockSpec((d, tn), lambda i, j: (0, j)),
                  pl.BlockSpec((8, tn), lambda i, j: (0, j)),
                  pl.BlockSpec(bd.shape, lambda i, j: (0, 0))],
        out_specs=pl.BlockSpec((n_sub, tm, LANES), lambda i, j: (j, i, 0)),
        out_shape=jax.ShapeDtypeStruct((n // LANES, bt, LANES), BF16),
        compiler_params=_cparams("parallel", "arbitrary"),
        name="inproj_qkv",
    )(h, w, aux, bd)


def _inproj_gates_kernel(h_ref, w_ref, eg_ref, raw_ref, ga_ref):
    y = _dot(h_ref[...], w_ref[...])
    raw_ref[...] = y
    ga = _dot(jax.nn.sigmoid(y).astype(BF16), eg_ref[...])
    for n in range(N_BRANCH):
        ga_ref[n] = ga[:, n * MIX_WIDTH:(n + 1) * MIX_WIDTH].astype(ga_ref.dtype)


def _inproj_gates(h, w, eg, tm=1024):
    bt, d = h.shape
    return pl.pallas_call(
        _inproj_gates_kernel,
        grid=(bt // tm,),
        in_specs=[pl.BlockSpec((tm, d), lambda i: (i, 0)),
                  pl.BlockSpec((d, LANES), lambda i: (0, 0)),
                  pl.BlockSpec((LANES, N_BRANCH * MIX_WIDTH), lambda i: (0, 0))],
        out_specs=[pl.BlockSpec((tm, LANES), lambda i: (i, 0)),
                   pl.BlockSpec((N_BRANCH, tm, MIX_WIDTH), lambda i: (0, i, 0))],
        out_shape=[jax.ShapeDtypeStruct((bt, LANES), F32),
                   jax.ShapeDtypeStruct((N_BRANCH, bt, MIX_WIDTH), BF16)],
        compiler_params=_cparams("parallel"),
        name="inproj_gates",
    )(h, w, eg)


def _inproj_merge_kernel(h_ref, w_ref, o_ref):
    width = 2 * MM_CHUNK
    for c in range(w_ref.shape[1] // width):
        cols = slice(c * width, (c + 1) * width)
        o_ref[0, :, cols] = jax.nn.sigmoid(_dot(h_ref[...], w_ref[:, cols])).astype(o_ref.dtype)


def _inproj_merge(h, w, d_model, tm=1024, tn=1024):
    bt, d = h.shape
    n = w.shape[1]
    per = d_model // tn
    return pl.pallas_call(
        _inproj_merge_kernel,
        grid=(bt // tm, n // tn),
        in_specs=[pl.BlockSpec((tm, d), lambda i, j: (i, 0)),
                  pl.BlockSpec((d, tn), lambda i, j: (0, j))],
        out_specs=pl.BlockSpec((1, tm, tn), lambda i, j: (j // per, i, j % per)),
        out_shape=jax.ShapeDtypeStruct((n // d_model, bt, d_model), BF16),
        compiler_params=_cparams("parallel", "arbitrary"),
        name="inproj_merge",
    )(h, w)


def _compress_kernel(c_ref, pos_ref, w1_ref, w2_ref, aux_ref, bd_ref, o_ref):
    n_chunks = c_ref.shape[3]
    y = jnp.zeros((n_chunks, LANES), F32)
    for g in range(KV_GROUPS):
        c = c_ref[0, 0, g].astype(F32)
        a = _dot((c + pos_ref[0, 0]).astype(BF16), w1_ref[0, 0])
        b = _dot((c + pos_ref[0, 1]).astype(BF16), w1_ref[0, 1])
        hid = _gelu_tanh(a + pltpu.roll(b, n_chunks - 1, 0))
        y = y + _dot(hid.astype(BF16), w2_ref[0, g])
    normed = _head_rms(y, bd_ref[0:LANES, 0:LANES], aux_ref[0, 0:1, :])
    o_ref[0, 0] = jnp.where(aux_ref[0, 1:2, :] > 0.5, normed, y).astype(o_ref.dtype)


def _compress(chunks, pos, w1, w2p, aux, bd):
    _, b, g, n_chunks, cw = chunks.shape
    return pl.pallas_call(
        _compress_kernel,
        grid=(2, b),
        in_specs=[pl.BlockSpec((1, 1, g, n_chunks, cw), lambda w, i: (w, i, 0, 0, 0)),
                  pl.BlockSpec((1, 2, 1, cw), lambda w, i: (w, 0, 0, 0)),
                  pl.BlockSpec((1, 2, cw, CMP_HIDDEN), lambda w, i: (w, 0, 0, 0)),
                  pl.BlockSpec((1, g, CMP_HIDDEN, LANES), lambda w, i: (w, 0, 0, 0)),
                  pl.BlockSpec((1, 8, LANES), lambda w, i: (w, 0, 0)),
                  pl.BlockSpec(bd.shape, lambda w, i: (0, 0))],
        out_specs=pl.BlockSpec((1, 1, n_chunks, LANES), lambda w, i: (w, i, 0, 0)),
        out_shape=jax.ShapeDtypeStruct((2, b, n_chunks, LANES), BF16),
        compiler_params=_cparams("parallel", "parallel"),
        name="nsa_compress",
    )(chunks, pos, w1, w2p, aux, bd)


def _cmp_select_kernel(q_ref, kc_ref, vc_ref, bias_ref, ovt_ref, o_ref, qa_ref, *, n_slc, n_top):
    i = pl.program_id(1)
    ncp = kc_ref.shape[2]
    row_t = i * Q_BLOCK + lax.broadcasted_iota(jnp.int32, (Q_BLOCK, 1), 0)
    row_valid = jnp.concatenate([row_t >= CMP_LEN - 1] * HEADS_PER_GROUP, axis=0)
    jrow = lax.broadcasted_iota(jnp.int32, (n_slc, Q_BLOCK), 0)
    cur = (i * Q_BLOCK + lax.broadcasted_iota(jnp.int32, (n_slc, Q_BLOCK), 1)) // SLC_LEN
    block_valid = jrow <= cur

    def force(imp):
        forced = jnp.where(jrow == cur, FORCE_SCORE, jnp.where(jrow == cur - 1, FORCE_SCORE, imp))
        return jnp.where(jrow == 0, FORCE_SCORE, forced)
    outs = []
    for g in range(KV_GROUPS):
        qop = _stack_group_queries(q_ref, g)
        rows = HEADS_PER_GROUP * Q_BLOCK
        s = _dot_nt(qop, kc_ref[0, 0]) + bias_ref[g * HEADS_PER_GROUP:(g + 1) * HEADS_PER_GROUP].reshape(rows, ncp)
        e = jnp.exp2(s - jnp.max(s, axis=-1, keepdims=True))
        inv = jnp.where(row_valid, 1.0 / jnp.sum(e, axis=-1, keepdims=True), 0.0)
        p = e * inv
        outs.append(_dot(p.astype(BF16), vc_ref[0, 0]))
        psum = p[0:Q_BLOCK]
        for r in range(1, HEADS_PER_GROUP):
            psum = psum + p[r * Q_BLOCK:(r + 1) * Q_BLOCK]
        hi, lo = _split_hi_lo(psum)
        imp_t = (_dot_nt(ovt_ref[...], hi) + _dot_nt(ovt_ref[...], lo))[0:n_slc]
        score = jnp.where(block_valid, force(imp_t), NEG)
        groups = [score[a:a + 8] for a in range(0, n_slc, 8)]
        ranks = [jnp.zeros((8, Q_BLOCK), F32) for _ in groups]
        sub = lax.broadcasted_iota(jnp.int32, (8, Q_BLOCK), 0)
        for j in range(n_slc):
            other = score[j:j + 1, :]
            for a, blk in enumerate(groups):
                ge = jnp.where(other >= blk, 1.0, 0.0)
                if 8 * a > j:
                    ranks[a] = ranks[a] + ge
                elif 8 * a + 7 <= j:
                    ranks[a] = ranks[a] + jnp.where(other > blk, 1.0, 0.0)
                else:
                    ranks[a] = ranks[a] + jnp.where(sub > j - 8 * a, ge, jnp.where(other > blk, 1.0, 0.0))
        rank = jnp.concatenate(ranks, axis=0)
        sel_bias = jnp.where(rank < n_top, jnp.where(score > NEG / 2, 0.0, NEG), NEG)
        if n_slc < HEAD_DIM:
            sel_bias = jnp.concatenate([sel_bias, jnp.full((HEAD_DIM - n_slc, Q_BLOCK), NEG, F32)], axis=0)
        sel_t = jnp.concatenate([sel_bias, sel_bias], axis=0).T.astype(qa_ref.dtype)
        own_half = _lane_half_mask(g)
        for p in range(N_QBLK):
            qa_ref[g * HEADS_PER_GROUP + p, 0] = jnp.where(own_half, q_ref[p, 0], sel_t)
    _merge_group_outputs(o_ref, outs[0], outs[1])


def _cmp_select(qkv, cmp_kv, bias_c, ovt, n_slc):
    _, b, t, _ = qkv.shape
    nt = t // Q_BLOCK
    ncp = cmp_kv.shape[2]
    assert n_slc <= HEAD_DIM
    return pl.pallas_call(
        functools.partial(_cmp_select_kernel, n_slc=n_slc, n_top=min(SLC_TOPK, n_slc)),
        grid=(b, nt),
        in_specs=[pl.BlockSpec((N_QBLK, 1, Q_BLOCK, LANES), lambda bi, i: (CB_AQ // N_QBLK, bi, i, 0)),
                  pl.BlockSpec((1, 1, ncp, LANES), lambda bi, i: (0, bi, 0, 0)),
                  pl.BlockSpec((1, 1, ncp, LANES), lambda bi, i: (1, bi, 0, 0)),
                  pl.BlockSpec((N_HEADS, Q_BLOCK, ncp), lambda bi, i: (0, i, 0)),
                  pl.BlockSpec((LANES, ncp), lambda bi, i: (0, 0))],
        out_specs=[pl.BlockSpec((N_QBLK, 1, Q_BLOCK, LANES), lambda bi, i: (0, bi, i, 0)),
                   pl.BlockSpec((N_HEADS, 1, Q_BLOCK, LANES), lambda bi, i: (0, bi, i, 0))],
        out_shape=[jax.ShapeDtypeStruct((N_QBLK, b, t, LANES), BF16),
                   jax.ShapeDtypeStruct((N_HEADS, b, t, LANES), BF16)],
        compiler_params=_cparams("parallel", "parallel"),
        name="nsa_cmp_select",
    )(qkv, cmp_kv, cmp_kv, bias_c, ovt)


def _flash_update(s, m, acc, v):
    m_new = jnp.maximum(m, jnp.max(s, axis=-1, keepdims=True))
    p = jnp.exp2(s - m_new)
    return m_new, jnp.exp2(m - m_new) * acc + _dot(p.astype(BF16), v)


def _normalise(acc):
    return acc / pltpu.roll(acc, HEAD_DIM, 1)


def _augment_keys_values(ka_scr, va_scr, k2, xk, v2):
    low = _lane_half_mask(0)
    ones = jnp.ones_like(v2)
    ka_scr[0] = jnp.where(low, k2, xk)
    ka_scr[1] = jnp.where(low, xk, k2)
    va_scr[0] = jnp.where(low, v2, ones)
    va_scr[1] = jnp.where(low, ones, v2)


def _store_pair(scr, s):
    scr[0], scr[1] = s


def _load_pair(scr):
    return scr[0], scr[1]


def _pipelined_sweep(logits, update, n_tiles, sa_scr, sb_scr, carry):
    def single(j, c):
        _store_pair(sa_scr, logits(j))
        return update(j, _load_pair(sa_scr), c)

    odd = n_tiles % 2
    carry = lax.fori_loop(0, odd, single, carry)
    _store_pair(sa_scr, logits(odd))

    def pair(jp, c):
        j = odd + 2 * jp
        _store_pair(sb_scr, logits(j + 1))
        c = update(j, _load_pair(sa_scr), c)
        _store_pair(sa_scr, logits(j + 2))
        return update(j + 1, _load_pair(sb_scr), c)

    return lax.fori_loop(0, n_tiles // 2, pair, carry)


def _slc_kernel(qa_ref, k_ref, v_ref, hot_ref, nb_ref, o_ref, ka_scr, va_scr, sa_scr, sb_scr):
    i = pl.program_id(1)
    tq = qa_ref.shape[2]
    rows = HEADS_PER_GROUP * tq

    @pl.when(i == 0)
    def _():
        _augment_keys_values(ka_scr, va_scr, k_ref[0, 0], hot_ref[...], v_ref[0, 0])

    heads = [slice(g * HEADS_PER_GROUP, (g + 1) * HEADS_PER_GROUP) for g in range(KV_GROUPS)]
    qs = [qa_ref[hs, 0].reshape(rows, LANES) for hs in heads]

    def keys(j):
        return pl.ds(pl.multiple_of(j * tq, tq), tq)

    def logits(j):
        return tuple(_dot_nt(qs[g], ka_scr[g, keys(j), :]) for g in range(KV_GROUPS))

    def update(j, s, carry, which=None, pen=None):
        new = []
        for g in range(KV_GROUPS):
            sg = s[g]
            if which is not None:
                sg = (sg.reshape(HEADS_PER_GROUP, tq, tq) + nb_ref[which, heads[g]]).reshape(rows, tq)
            if pen is not None:
                sg = sg + pen
            new.append(_flash_update(sg, *carry[g], va_scr[g, keys(j), :]))
        return tuple(new)

    init = (jnp.full((rows, 1), NEG, F32), jnp.zeros((rows, LANES), F32))
    carry = _pipelined_sweep(logits, update, jnp.maximum(i - 1, 0), sa_scr, sb_scr, (init, init))
    _store_pair(sb_scr, logits(i))
    carry = update(jnp.maximum(i - 1, 0), _load_pair(sa_scr), carry, which=0, pen=jnp.where(i >= 1, 0.0, NEG))
    (_, acc0), (_, acc1) = update(i, _load_pair(sb_scr), carry, which=1)
    _merge_group_outputs(o_ref, _normalise(acc0), _normalise(acc1))


def _slc_attention(qa, qkv, hot2, near_bias):
    _, b, t, _ = qa.shape
    tq = near_bias.shape[2]
    return pl.pallas_call(
        _slc_kernel,
        grid=(b, t // tq),
        in_specs=[pl.BlockSpec((N_HEADS, 1, tq, LANES), lambda bi, i: (0, bi, i, 0)),
                  pl.BlockSpec((1, 1, t, LANES), lambda bi, i: (CB_AKS, bi, 0, 0)),
                  pl.BlockSpec((1, 1, t, LANES), lambda bi, i: (CB_AVS, bi, 0, 0)),
                  pl.BlockSpec(hot2.shape, lambda bi, i: (0, 0)),
                  pl.BlockSpec(near_bias.shape, lambda bi, i: (0, 0, 0, 0))],
        out_specs=pl.BlockSpec((N_QBLK, 1, tq, LANES), lambda bi, i: (0, bi, i, 0)),
        out_shape=jax.ShapeDtypeStruct((N_QBLK, b, t, LANES), BF16),
        scratch_shapes=[pltpu.VMEM((KV_GROUPS, t, LANES), BF16), pltpu.VMEM((KV_GROUPS, t, LANES), BF16),
                        pltpu.VMEM((KV_GROUPS, HEADS_PER_GROUP * tq, tq), F32),
                        pltpu.VMEM((KV_GROUPS, HEADS_PER_GROUP * tq, tq), F32)],
        compiler_params=_cparams("parallel", "arbitrary"),
        name="nsa_slc_attention",
    )(qa, qkv, qkv, hot2, near_bias)


BAND_TQ = 256


def _banded_kernel(*refs, n_blk, use_sinks):
    q_ref = refs[0]
    k_refs = refs[1:1 + n_blk]
    v_refs = refs[1 + n_blk:1 + 2 * n_blk]
    bias_ref, o_ref = refs[1 + 2 * n_blk:]
    i = pl.program_id(1)
    tq = q_ref.shape[2]
    n_keys = n_blk * Q_BLOCK
    rows = HEADS_PER_GROUP * tq
    first_key = (i + 1) * tq - n_keys
    col = lax.broadcasted_iota(jnp.int32, (1, n_keys), 1)
    pad_pen = jnp.where(first_key + col >= 0, 0.0, NEG)
    kcat = jnp.concatenate([r[0, 0] for r in k_refs], axis=0)
    vcat = jnp.concatenate([r[0, 0] for r in v_refs], axis=0)
    if use_sinks:
        sink_row = lax.broadcasted_iota(jnp.int32, (n_keys, 1), 0) == 0
        kcat = jnp.where(sink_row, jnp.zeros_like(kcat), kcat)
        vcat = jnp.where(sink_row, jnp.zeros_like(vcat), vcat)
        pad_pen = jnp.where(col == 0, 0.0, pad_pen)
    outs = []
    for g in range(KV_GROUPS):
        own_half = _lane_half_mask(g)
        qop = jnp.concatenate([jnp.where(own_half, q_ref[p, 0], jnp.zeros_like(q_ref[p, 0]))
                               for p in range(N_QBLK)], axis=0)
        hs = slice(g * HEADS_PER_GROUP, (g + 1) * HEADS_PER_GROUP)
        s = _dot_nt(qop, kcat) + bias_ref[hs].reshape(rows, n_keys) + pad_pen
        e = jnp.exp2(s - jnp.max(s, axis=-1, keepdims=True))
        acc = _dot(e.astype(BF16), jnp.where(own_half, vcat, jnp.ones_like(vcat)))
        outs.append(_normalise(acc))
    _merge_group_outputs(o_ref, outs[0], outs[1])


def _banded_attention(qkv, cb_q, cb_k, cb_v, bias, use_sinks):
    _, b, t, _ = qkv.shape
    tq = bias.shape[1]
    n_blk = bias.shape[2] // Q_BLOCK
    per_tile = tq // Q_BLOCK

    def kv_spec(cb, back):
        return pl.BlockSpec((1, 1, Q_BLOCK, LANES),
                            lambda bi, i: (cb, bi, jnp.maximum((i + 1) * per_tile - 1 - back, 0), 0))

    backs = [n_blk - 1 - jb for jb in range(n_blk)]
    return pl.pallas_call(
        functools.partial(_banded_kernel, n_blk=n_blk, use_sinks=use_sinks),
        grid=(b, t // tq),
        in_specs=([pl.BlockSpec((N_QBLK, 1, tq, LANES), lambda bi, i: (cb_q // N_QBLK, bi, i, 0))]
                  + [kv_spec(cb_k, back) for back in backs] + [kv_spec(cb_v, back) for back in backs]
                  + [pl.BlockSpec(bias.shape, lambda bi, i: (0, 0, 0))]),
        out_specs=pl.BlockSpec((N_QBLK, 1, tq, LANES), lambda bi, i: (0, bi, i, 0)),
        out_shape=jax.ShapeDtypeStruct((N_QBLK, b, t, LANES), BF16),
        compiler_params=_cparams("parallel", "parallel"),
        name="banded_attention",
    )(qkv, *([qkv] * (2 * n_blk)), bias)


FOX_EXTRA = 6


def _fox_prep_kernel(g_ref, fb_ref, pq_ref, pk_ref, oq_ref, ok_ref, xq_ref, xk_ref, carry_scr):
    @pl.when(pl.program_id(1) == 0)
    def _():
        carry_scr[...] = jnp.zeros_like(carry_scr)

    x = g_ref[...] + fb_ref[...]
    logf = jnp.minimum(x, 0.0) - jnp.log(1.0 + jnp.exp(-jnp.abs(x)))
    ch = x.shape[0]
    lower = jnp.where(lax.broadcasted_iota(jnp.int32, (ch, ch), 0)
                      >= lax.broadcasted_iota(jnp.int32, (ch, ch), 1), 1.0, 0.0).astype(BF16)

    def split3(v):
        hi, rest = v.astype(BF16), v - v.astype(BF16).astype(F32)
        return (hi,) + _split_hi_lo(rest)

    cs = sum(_dot(lower, part) for part in split3(logf)) + carry_scr[0:1, :]
    carry_scr[0:1, :] = cs[ch - 1:ch, :]
    parts = split3(cs * LOG2E)
    xq = sum(_dot(part, pq_ref[n]) for n, part in enumerate(parts)) + oq_ref[...]
    xk = sum(_dot(part, pk_ref[n]) for n, part in enumerate(parts)) + ok_ref[...]
    for p in range(N_QBLK):
        xq_ref[p, 0] = xq[:, p * LANES:(p + 1) * LANES].astype(xq_ref.dtype)
        xk_ref[p, 0] = xk[:, p * LANES:(p + 1) * LANES].astype(xk_ref.dtype)


def _fox_prep(graw, fb_row, b, t, chunk=512):
    chunk = min(chunk, t)
    nch = t // chunk
    pq = np.zeros((3, LANES, N_QBLK * LANES), np.float32)
    pk = np.zeros((3, LANES, N_QBLK * LANES), np.float32)
    oq = np.zeros((1, N_QBLK * LANES), np.float32)
    ok = np.zeros((1, N_QBLK * LANES), np.float32)
    for h in range(N_HEADS):
        col = (h // 2) * LANES + (HEAD_DIM if h % 2 == 0 else 0)
        for n in range(3):
            pq[n, N_GATE_COLS + h, col + n] = 1.0
            pk[n, N_GATE_COLS + h, col + 3 + n] = -1.0
        oq[0, col + 3:col + FOX_EXTRA] = 1.0
        ok[0, col:col + 3] = 1.0
    const = lambda a: pl.BlockSpec(a.shape, lambda bi, c: (0,) * a.ndim)
    out_spec = pl.BlockSpec((N_QBLK, 1, chunk, LANES), lambda bi, c: (0, bi, c, 0))
    return pl.pallas_call(
        _fox_prep_kernel,
        grid=(b, nch),
        in_specs=[pl.BlockSpec((chunk, LANES), lambda bi, c: (bi * nch + c, 0)),
                  const(fb_row), const(pq), const(pk), const(oq), const(ok)],
        out_specs=[out_spec, out_spec],
        out_shape=[jax.ShapeDtypeStruct((N_QBLK, b, t, LANES), BF16)] * 2,
        scratch_shapes=[pltpu.VMEM((8, LANES), F32)],
        compiler_params=_cparams("parallel", "arbitrary"),
        name="fox_prep",
    )(graw, fb_row, jnp.asarray(pq, BF16), jnp.asarray(pk, BF16), jnp.asarray(oq), jnp.asarray(ok))


def _fox_kernel(q_ref, xq_ref, k_ref, xk_ref, v_ref, o_ref, ka_scr, va_scr, sa_scr, sb_scr, *, tq):
    i = pl.program_id(2)

    @pl.when(i == 0)
    def _():
        _augment_keys_values(ka_scr, va_scr, k_ref[0, 0], xk_ref[0, 0], v_ref[0, 0])

    low = _lane_half_mask(0)
    q2, xq = q_ref[0, 0], xq_ref[0, 0]
    qs = [jnp.where(low, q2, xq), jnp.where(low, xq, q2)]
    causal_pen = jnp.where(lax.broadcasted_iota(jnp.int32, (tq, tq), 0)
                           >= lax.broadcasted_iota(jnp.int32, (tq, tq), 1), 0.0, NEG)

    def keys(j):
        return pl.ds(pl.multiple_of(j * tq, tq), tq)

    def logits(j):
        return tuple(_dot_nt(qs[hh], ka_scr[hh, keys(j), :]) for hh in range(2))

    def update(j, s, carry, pen=None):
        return tuple(_flash_update(s[hh] if pen is None else s[hh] + pen, *carry[hh], va_scr[hh, keys(j), :])
                     for hh in range(2))

    init = (jnp.full((tq, 1), NEG, F32), jnp.zeros((tq, LANES), F32))
    carry = _pipelined_sweep(logits, update, i, sa_scr, sb_scr, (init, init))
    (_, acc0), (_, acc1) = update(i, _load_pair(sa_scr), carry, pen=causal_pen)
    o_ref[0, 0] = jnp.where(low, _normalise(acc0), _normalise(acc1)).astype(o_ref.dtype)


def _fox_attention(qkv, xq, xk, tq):
    _, b, t, _ = qkv.shape
    q_spec = lambda cb: pl.BlockSpec((1, 1, tq, LANES), lambda bi, p, i: (cb + p, bi, i, 0))
    kv_spec = lambda cb: pl.BlockSpec((1, 1, t, LANES), lambda bi, p, i: (cb + p, bi, 0, 0))
    return pl.pallas_call(
        functools.partial(_fox_kernel, tq=tq),
        grid=(b, N_QBLK, t // tq),
        in_specs=[q_spec(CB_CQ), q_spec(0), kv_spec(CB_CK), kv_spec(0), kv_spec(CB_CV)],
        out_specs=pl.BlockSpec((1, 1, tq, LANES), lambda bi, p, i: (p, bi, i, 0)),
        out_shape=jax.ShapeDtypeStruct((N_QBLK, b, t, LANES), BF16),
        scratch_shapes=[pltpu.VMEM((2, t, LANES), BF16), pltpu.VMEM((2, t, LANES), BF16),
                        pltpu.VMEM((2, tq, tq), F32), pltpu.VMEM((2, tq, tq), F32)],
        compiler_params=_cparams("parallel", "parallel", "arbitrary"),
        name="fox_attention",
    )(qkv, xq, qkv, xk, qkv)


def _merge_kernel(ocmp_ref, oslc_ref, owin_ref, ob_ref, oc_ref, ga_ref, mg_ref, wb_ref, wo_ref, x_ref, o_ref):
    def cat(ref):
        return jnp.concatenate([ref[p] for p in range(N_QBLK)], axis=-1)

    oa = (ga_ref[0].astype(F32) * cat(ocmp_ref).astype(F32)
          + ga_ref[1].astype(F32) * cat(oslc_ref).astype(F32)
          + ga_ref[2].astype(F32) * cat(owin_ref).astype(F32)).astype(BF16)
    branches = (oa, cat(ob_ref), cat(oc_ref))
    d = o_ref.shape[1]
    chunks = [slice(c, c + MM_CHUNK) for c in range(0, d, MM_CHUNK)]
    mix = jnp.concatenate(
        [sum(mg_ref[n, :, cols].astype(F32) * _dot(branches[n], wb_ref[n, :, cols]) for n in range(N_BRANCH))
         .astype(BF16) for cols in chunks], axis=-1)
    for cols in chunks:
        o_ref[:, cols] = x_ref[:, cols] + _dot(mix, wo_ref[:, cols])


def _merge(ocmp, oslc, owin, ob, oc, ga, mg, wb, wo, x2, tm=512):
    bt, d = x2.shape
    o_spec = pl.BlockSpec((N_QBLK, tm, LANES), lambda i: (0, i, 0))
    return pl.pallas_call(
        _merge_kernel,
        grid=(bt // tm,),
        in_specs=[o_spec] * 5 + [pl.BlockSpec((N_BRANCH, tm, MIX_WIDTH), lambda i: (0, i, 0)),
                                 pl.BlockSpec((N_BRANCH, tm, d), lambda i: (0, i, 0)),
                                 pl.BlockSpec(wb.shape, lambda i: (0, 0, 0)),
                                 pl.BlockSpec(wo.shape, lambda i: (0, 0)),
                                 pl.BlockSpec((tm, d), lambda i: (i, 0))],
        out_specs=pl.BlockSpec((tm, d), lambda i: (i, 0)),
        out_shape=jax.ShapeDtypeStruct((bt, d), F32),
        compiler_params=_cparams("parallel"),
        name="branch_merge",
    )(ocmp, oslc, owin, ob, oc, ga, mg, wb, wo, x2)


FFN_TF = 512


def _ffn_kernel(x_ref, xh_ref, g_ref, wg_ref, wu_ref, cw_ref, cb_ref, wd_ref, o_ref,
                h_scr, hh_scr, a_scr, acc_scr, *, tm, tiles_per_seq):
    i = pl.program_id(0)
    f = pl.program_id(1)

    def norm(x):
        ms = jnp.mean(x * x, axis=-1, keepdims=True)
        return ((x * lax.rsqrt(ms + RMS_EPS)) * g_ref[...]).astype(BF16)

    @pl.when(f == 0)
    def _():
        h_scr[...] = norm(x_ref[...])
        hh_scr[...] = norm(xh_ref[...])
        acc_scr[...] = jnp.zeros_like(acc_scr)

    a_scr[0:8, :] = _dot(hh_scr[...], wg_ref[...]) * jnp.where(i % tiles_per_seq == 0, 0.0, 1.0)
    acts = []
    for c in range(wg_ref.shape[1] // MM_CHUNK):
        cols = slice(c * MM_CHUNK, (c + 1) * MM_CHUNK)
        a = _dot(h_scr[...], wg_ref[:, cols])
        a_scr[8:8 + tm, cols] = a
        conv = (cw_ref[0:1, cols] * a_scr[6:6 + tm, cols] + cw_ref[1:2, cols] * a_scr[7:7 + tm, cols]
                + cw_ref[2:3, cols] * a + cb_ref[:, cols])
        acts.append((_gelu_tanh(conv) * _dot(h_scr[...], wu_ref[:, cols])).astype(BF16))
    acc_scr[...] += _dot(jnp.concatenate(acts, axis=-1), wd_ref[...])

    @pl.when(f == pl.num_programs(1) - 1)
    def _():
        o_ref[...] = x_ref[...] + acc_scr[...]


def _ffn(x2, g, wg, wu, cw, cb, wd, t, tm=1024, tf=FFN_TF):
    bt, d = x2.shape
    pad = -wg.shape[1] % tf
    wg, wu, cw = [jnp.pad(a, ((0, 0), (0, pad))) for a in (wg, wu, cw)]
    cb, wd = jnp.pad(cb, (0, pad)), jnp.pad(wd, ((0, pad), (0, 0)))
    ff = wg.shape[1]
    halo_blocks = tm // 8
    return pl.pallas_call(
        functools.partial(_ffn_kernel, tm=tm, tiles_per_seq=t // tm),
        grid=(bt // tm, ff // tf),
        in_specs=[pl.BlockSpec((tm, d), lambda i, f: (i, 0)),
                  pl.BlockSpec((8, d), lambda i, f: (jnp.maximum(i * halo_blocks - 1, 0), 0)),
                  pl.BlockSpec((1, d), lambda i, f: (0, 0)),
                  pl.BlockSpec((d, tf), lambda i, f: (0, f)),
                  pl.BlockSpec((d, tf), lambda i, f: (0, f)),
                  pl.BlockSpec((CONV_WIDTH, tf), lambda i, f: (0, f)),
                  pl.BlockSpec((1, tf), lambda i, f: (0, f)),
                  pl.BlockSpec((tf, d), lambda i, f: (f, 0))],
        out_specs=pl.BlockSpec((tm, d), lambda i, f: (i, 0)),
        out_shape=jax.ShapeDtypeStruct((bt, d), F32),
        scratch_shapes=[pltpu.VMEM((tm, d), BF16), pltpu.VMEM((8, d), BF16),
                        pltpu.VMEM((tm + 8, tf), F32), pltpu.VMEM((tm, d), F32)],
        compiler_params=_cparams("parallel", "arbitrary"),
        name="conv_ffn",
    )(x2, x2, g.reshape(1, d), wg, wu, cw, cb.reshape(1, ff), wd)


def _t5_bucket_np(dist):
    max_exact = N_BUCKETS // 2
    d = np.maximum(dist, 0)
    ratio = np.log(np.maximum(d, 1) / max_exact) / math.log(MAX_DISTANCE / max_exact)
    large = np.minimum(max_exact + (ratio * (N_BUCKETS - max_exact)).astype(np.int64), N_BUCKETS - 1)
    return np.where(d < max_exact, d, large)


def _expand_kernel(tab_ref, idx_ref, o_ref):
    tab = tab_ref[...]
    onehot = jnp.where(lax.broadcasted_iota(jnp.int32, (tab.shape[1], idx_ref.shape[1]), 0) == idx_ref[...],
                       1.0, 0.0).astype(BF16)
    hi, rest = tab.astype(BF16), tab - tab.astype(BF16).astype(F32)
    mid, lo = _split_hi_lo(rest)
    o_ref[...] = _dot(hi, onehot) + _dot(mid, onehot) + _dot(lo, onehot)


def _bias_from_dist(tab_t, dist, valid, chunk=4096):
    heads = tab_t.shape[0]
    tab = jnp.zeros((heads, LANES), F32).at[:, :N_BUCKETS].set(tab_t).at[:, N_BUCKETS].set(NEG)
    idx = np.where(valid, _t5_bucket_np(dist), N_BUCKETS).reshape(1, -1).astype(np.int32)
    n = idx.shape[1]
    chunk = min(chunk, n)
    out = pl.pallas_call(
        _expand_kernel,
        grid=(n // chunk,),
        in_specs=[pl.BlockSpec((heads, LANES), lambda i: (0, 0)), pl.BlockSpec((1, chunk), lambda i: (0, i))],
        out_specs=pl.BlockSpec((heads, chunk), lambda i: (0, i)),
        out_shape=jax.ShapeDtypeStruct((heads, n), F32),
        compiler_params=_cparams("parallel"),
        name="rel_bias_expand",
    )(tab, jnp.asarray(idx))
    return out.reshape((heads,) + dist.shape)


def _cmp_bias_kernel(g_ref, o_ref):
    i = pl.program_id(0)
    ncp = o_ref.shape[2]
    for h in range(o_ref.shape[0]):
        o_ref[h] = pltpu.roll(g_ref[h], i * (Q_BLOCK // CMP_STRIDE), 1)[:, :ncp]


def _cmp_bias(g_tab, t):
    heads, _, width = g_tab.shape
    ncp = t // CMP_STRIDE
    return pl.pallas_call(
        _cmp_bias_kernel,
        grid=(t // Q_BLOCK,),
        in_specs=[pl.BlockSpec(g_tab.shape, lambda i: (0, 0, 0))],
        out_specs=pl.BlockSpec((heads, Q_BLOCK, ncp), lambda i: (0, i, 0)),
        out_shape=jax.ShapeDtypeStruct((heads, t, ncp), F32),
        compiler_params=_cparams("parallel"),
        name="cmp_bias_build",
    )(g_tab)


def _static_tables(t):
    ncp = t // CMP_STRIDE
    n_cmp = (t - CMP_LEN) // CMP_STRIDE + 1
    n_slc = t // SLC_LEN
    lane = np.arange(MM_CHUNK)
    bd = (lane[:, None] // HEAD_DIM == lane[None, :] // HEAD_DIM).astype(np.float32)
    eg = np.zeros((LANES, N_BRANCH * MIX_WIDTH), np.float32)
    for h in range(N_HEADS):
        base = (h % N_QBLK) * LANES + (h // N_QBLK) * HEAD_DIM
        for n in range(N_BRANCH):
            eg[h * N_BRANCH + n, n * MIX_WIDTH + base:n * MIX_WIDTH + base + HEAD_DIM] = 1.0
    c_start = np.arange(n_cmp) * CMP_STRIDE
    s_start = np.arange(n_slc) * SLC_LEN
    overlap = ((c_start[:, None] < s_start[None, :] + SLC_LEN) & (c_start[:, None] + CMP_LEN > s_start[None, :]))
    ovt = np.zeros((LANES, ncp), np.float32)
    ovt[:n_slc, :n_cmp] = overlap.T
    key_block = np.zeros((t, LANES), np.float32)
    key_block[np.arange(t), np.arange(t) // SLC_LEN] = 1.0
    key_block[np.arange(t), HEAD_DIM + np.arange(t) // SLC_LEN] = 1.0
    as_bf16 = lambda a: jnp.asarray(a, BF16)
    return dict(bd=as_bf16(bd), eg=as_bf16(eg), ovt=as_bf16(ovt), key_block=as_bf16(key_block), n_slc=n_slc)


def _bias_tables(rel_bias, t):
    n_cmp = (t - CMP_LEN) // CMP_STRIDE + 1
    ncp = t // CMP_STRIDE
    tab_a = rel_bias[:, :N_HEADS].T * LOG2E
    tab_b = rel_bias[:, N_HEADS:].T * LOG2E
    r = np.arange(Q_BLOCK)[:, None]

    def band(window):
        n_prev = -(-(window - 1) // Q_BLOCK)
        n_keys = n_prev * Q_BLOCK + BAND_TQ
        dist = (n_keys - BAND_TQ) + np.arange(BAND_TQ)[:, None] - np.arange(n_keys)[None, :]
        valid = (dist >= 0) & (dist < window)
        assert not valid[:, 0].any()
        return dist, valid

    m = np.arange(2 * ncp)
    m = np.where(m < ncp, m, m - 2 * ncp)[None, :]
    dist_c = r - (m * CMP_STRIDE + CMP_LEN - 1)
    assert n_cmp == ncp - 1 and (dist_c[:, m[0] == Q_BLOCK // CMP_STRIDE - 1] < 0).all()
    tile = np.arange(SLC_TILE)
    dist_n = np.stack([SLC_TILE * (1 - which) + tile[:, None] - tile[None, :] for which in range(2)])
    near = _bias_from_dist(tab_a - tab_a[:, N_BUCKETS - 1:], dist_n, dist_n >= 0)
    return dict(win=_bias_from_dist(tab_a, *band(NSA_WINDOW)),
                swa=_bias_from_dist(tab_b, *band(SWA_WINDOW)),
                cmp=_cmp_bias(_bias_from_dist(tab_a, dist_c, dist_c >= 0), t),
                near=jnp.transpose(near, (1, 0, 2, 3)))


def _pair_cols(w):
    parts = []
    for p in range(N_QBLK):
        parts += [w[..., p * HEAD_DIM:(p + 1) * HEAD_DIM],
                  w[..., (N_QBLK + p) * HEAD_DIM:(N_QBLK + p + 1) * HEAD_DIM]]
    return jnp.concatenate(parts, axis=-1)


def _prep_layer(w_in, qk_gain, w_branch):
    kv = KV_GROUPS * HEAD_DIM
    widths = [('a_q', MIX_WIDTH), ('a_kc', kv), ('a_vc', kv), ('a_ks', kv), ('a_vs', kv), ('a_kw', kv), ('a_vw', kv),
              ('a_gate', N_GATE_COLS), ('b_q', MIX_WIDTH), ('b_k', kv), ('b_v', kv),
              ('c_q', MIX_WIDTH), ('c_k', MIX_WIDTH), ('c_v', MIX_WIDTH), ('c_f', N_HEADS),
              ('merge', N_BRANCH * w_in.shape[0])]
    cols, off = {}, 0
    for name, w in widths:
        cols[name] = w_in[:, off:off + w]
        off += w
    scale = HEAD_DIM ** -0.5 * LOG2E
    tile = lambda gvec, n: jnp.tile(gvec, n)
    zeros = lambda n: jnp.zeros((n,), F32)
    ones = lambda n: jnp.ones((n,), F32)
    pieces = [(_pair_cols(cols['a_q']), tile(qk_gain[0] * scale, N_HEADS), ones(MIX_WIDTH)),
              (_pair_cols(cols['b_q']), tile(qk_gain[2] * scale, N_HEADS), ones(MIX_WIDTH)),
              (cols['c_q'], tile(qk_gain[4] * scale, N_HEADS), ones(MIX_WIDTH)),
              (cols['c_k'], tile(qk_gain[5], N_HEADS), ones(MIX_WIDTH)),
              (cols['a_ks'], tile(qk_gain[1], KV_GROUPS), ones(kv)),
              (cols['a_kw'], tile(qk_gain[1], KV_GROUPS), ones(kv)),
              (cols['b_k'], tile(qk_gain[3], KV_GROUPS), ones(kv)),
              (cols['a_vs'], zeros(kv), zeros(kv)),
              (cols['c_v'], zeros(MIX_WIDTH), zeros(MIX_WIDTH)),
              (cols['a_kc'], zeros(kv), zeros(kv)), (cols['a_vc'], zeros(kv), zeros(kv)),
              (cols['a_vw'], zeros(kv), zeros(kv)), (cols['b_v'], zeros(kv), zeros(kv))]
    w_qkv = jnp.concatenate([p[0] for p in pieces], axis=1).astype(BF16)
    n_qkv = w_qkv.shape[1]
    aux = jnp.zeros((8, n_qkv), F32)
    aux = aux.at[0].set(jnp.concatenate([p[1] for p in pieces])).at[1].set(jnp.concatenate([p[2] for p in pieces]))
    d = w_in.shape[0]
    w_gates = jnp.concatenate([cols['a_gate'], cols['c_f'],
                               jnp.zeros((d, LANES - N_GATE_COLS - N_HEADS), F32)], axis=1).astype(BF16)
    wb = jnp.stack([_pair_cols(w_branch[0].T).T, _pair_cols(w_branch[1].T).T, w_branch[2]]).astype(BF16)
    return w_qkv, aux, w_gates, cols['merge'].astype(BF16), wb


def _prep_compress(cmp_pos, cmp_w1, cmp_w2, k_gain):
    half = CMP_STRIDE * HEAD_DIM
    pos = cmp_pos.reshape(2, 2, 1, half)
    w1 = cmp_w1.reshape(2, 2, half, CMP_HIDDEN).astype(BF16)
    z = jnp.zeros_like(cmp_w2)
    w2p = jnp.stack([jnp.concatenate([cmp_w2, z], axis=-1), jnp.concatenate([z, cmp_w2], axis=-1)], axis=1).astype(BF16)
    aux = jnp.zeros((2, 8, LANES), F32)
    aux = aux.at[0, 0].set(jnp.tile(k_gain, KV_GROUPS)).at[0, 1].set(1.0)
    return pos, w1, w2p, aux


def kernel(x, rel_bias, norm_mix, norm_ffn, w_in, forget_bias, qk_gain, cmp_pos, cmp_w1, cmp_w2, sinks, w_branch, w_out, w_gate, w_up, conv_w, conv_b, w_down):
    b, t, d = x.shape
    bt = b * t
    depth = w_in.shape[0]
    tabs = _static_tables(t)
    biases = _bias_tables(rel_bias, t)
    n_chunks = t // CMP_STRIDE
    fox_tq = min(512, t)
    x2 = x.reshape(bt, d)
    for l in range(depth):
        w_qkv, aux, w_gates, w_merge, wb = _prep_layer(w_in[l], qk_gain[l], w_branch[l])
        h = _rmsnorm(x2, norm_mix[l])
        qkv = _inproj_qkv(h, w_qkv, aux, tabs['bd']).reshape(N_CB, b, t, LANES)
        graw, ga = _inproj_gates(h, w_gates, tabs['eg'])
        mg = _inproj_merge(h, w_merge, d)

        chunks = jnp.stack([qkv[CB_AKC], qkv[CB_AVC]]).reshape(2, b, n_chunks, CMP_STRIDE, KV_GROUPS, HEAD_DIM)
        chunks = jnp.transpose(chunks, (0, 1, 4, 2, 3, 5)).reshape(2, b, KV_GROUPS, n_chunks, CMP_STRIDE * HEAD_DIM)
        cmp_kv = _compress(chunks, *_prep_compress(cmp_pos[l], cmp_w1[l], cmp_w2[l], qk_gain[l, 1]), tabs['bd'])
        o_cmp, qa = _cmp_select(qkv, cmp_kv, biases['cmp'], tabs['ovt'], tabs['n_slc'])
        o_slc = _slc_attention(qa, qkv, tabs['key_block'], biases['near'])
        o_win = _banded_attention(qkv, CB_AQ, CB_AKW, CB_AVW, biases['win'], False)

        swa_bias = biases['swa'].at[:, :, 0].set(sinks[l][:, None] * LOG2E)
        o_b = _banded_attention(qkv, CB_BQ, CB_BK, CB_BV, swa_bias, True)

        fb_row = jnp.zeros((1, LANES), F32).at[0, N_GATE_COLS:N_GATE_COLS + N_HEADS].set(forget_bias[l])
        o_c = _fox_attention(qkv, *_fox_prep(graw, fb_row, b, t), fox_tq)

        flat = lambda o: o.reshape(N_QBLK, bt, LANES)
        x2 = _merge(flat(o_cmp), flat(o_slc), flat(o_win), flat(o_b), flat(o_c), ga, mg, wb,
                    w_out[l].astype(BF16), x2)
        x2 = _ffn(x2, norm_ffn[l], w_gate[l].astype(BF16), w_up[l].astype(BF16), conv_w[l], conv_b[l],
                  w_down[l].astype(BF16), t)
    return x2.reshape(b, t, d)
```

```python
import functools
import math

import numpy as np
import jax
import jax.numpy as jnp
from jax import lax
from jax.experimental import pallas as pl
from jax.experimental.pallas import tpu as pltpu

F32 = jnp.float32
BF16 = jnp.bfloat16

HEAD_DIM = 64
LANES = 128
N_HEADS = 8
KV_GROUPS = 2
HEADS_PER_GROUP = N_HEADS // KV_GROUPS
N_QBLK = N_HEADS * HEAD_DIM // LANES
MIX_WIDTH = N_HEADS * HEAD_DIM
N_BRANCH = 3
Q_BLOCK = 128
CMP_LEN = 32
CMP_STRIDE = 16
CMP_HIDDEN = 256
SLC_LEN = 64
SLC_TOPK = 8
NSA_WINDOW = 256
SWA_WINDOW = 128
N_BUCKETS = 32
MAX_DISTANCE = 128
CONV_WIDTH = 3
RMS_EPS = 1e-6
LOG2E = math.log2(math.e)
NEG = -1e30
FORCE_SCORE = 1e9
TAKEN = -3e38
SLC_TILE = 256
VMEM_LIMIT = 48 * 1024 * 1024

CB_AQ, CB_BQ, CB_CQ, CB_CK = 0, 4, 8, 12
CB_AKS, CB_AKW, CB_BK, CB_AVS = 16, 17, 18, 19
CB_CV, CB_AKC, CB_AVC, CB_AVW, CB_BV = 20, 24, 25, 26, 27
N_CB = 28
N_NORM_CB = 20
MM_CHUNK = 256
MM_ROWS = 512
N_GATE_COLS = N_HEADS * N_BRANCH


def _cparams(*sem):
    return pltpu.CompilerParams(dimension_semantics=sem, vmem_limit_bytes=VMEM_LIMIT)


def _split_hi_lo(v):
    hi = v.astype(BF16)
    return hi, (v - hi.astype(F32)).astype(BF16)


def _dot(a, b):
    return jnp.dot(a, b, preferred_element_type=F32)


def _dot_nt(a, b):
    return lax.dot_general(a, b, (((1,), (1,)), ((), ())), preferred_element_type=F32)


def _gelu_tanh(x):
    return x * (0.5 * (1.0 + jnp.tanh(math.sqrt(2.0 / math.pi) * (x + 0.044715 * (x * x * x)))))


def _head_rms(y, bd, gain):
    ms = _dot((y * y).astype(BF16), bd) * (1.0 / HEAD_DIM)
    return (y * lax.rsqrt(ms + RMS_EPS)) * gain


def _lane_half_mask(g):
    return (lax.broadcasted_iota(jnp.int32, (1, LANES), 1) // HEAD_DIM) == g


def _merge_group_outputs(o_ref, o_g0, o_g1):
    low = _lane_half_mask(0)
    tq = o_ref.shape[2]
    for p in range(N_QBLK):
        rows = slice(p * tq, (p + 1) * tq)
        o_ref[p, 0] = jnp.where(low, o_g0[rows], o_g1[rows]).astype(o_ref.dtype)


def _rmsnorm_kernel(x_ref, g_ref, o_ref):
    x = x_ref[...]
    ms = jnp.mean(x * x, axis=-1, keepdims=True)
    o_ref[...] = ((x * lax.rsqrt(ms + RMS_EPS)) * g_ref[...]).astype(o_ref.dtype)


def _rmsnorm(x2, g, tm=512):
    bt, d = x2.shape
    return pl.pallas_call(
        _rmsnorm_kernel,
        grid=(bt // tm,),
        in_specs=[pl.BlockSpec((tm, d), lambda i: (i, 0)), pl.BlockSpec((1, d), lambda i: (0, 0))],
        out_specs=pl.BlockSpec((tm, d), lambda i: (i, 0)),
        out_shape=jax.ShapeDtypeStruct((bt, d), BF16),
        compiler_params=_cparams("parallel"),
        name="rmsnorm",
    )(x2, g.reshape(1, d))


def _inproj_qkv_kernel(h_ref, w_ref, aux_ref, bd_ref, o_ref, *, n_norm_tiles):
    per_chunk = MM_CHUNK // LANES

    def emit(normalise):
        for r in range(0, h_ref.shape[0], MM_ROWS):
            rows = slice(r, r + MM_ROWS)
            y_all = _dot(h_ref[rows, :], w_ref[...])
            for c in range(w_ref.shape[1] // MM_CHUNK):
                cols = slice(c * MM_CHUNK, (c + 1) * MM_CHUNK)
                y = y_all[:, cols]
                if normalise:
                    normed = _head_rms(y, bd_ref[...], aux_ref[0:1, cols])
                    y = jnp.where(aux_ref[1:2, cols] > 0.5, normed, y)
                for s in range(per_chunk):
                    o_ref[c * per_chunk + s, rows, :] = y[:, s * LANES:(s + 1) * LANES].astype(o_ref.dtype)

    pl.when(pl.program_id(1) < n_norm_tiles)(lambda: emit(True))
    pl.when(pl.program_id(1) >= n_norm_tiles)(lambda: emit(False))


def _inproj_qkv(h, w, aux, bd, tm=2048, tn=512):
    bt, d = h.shape
    n = w.shape[1]
    n_sub = tn // LANES
    tm = min(tm, bt)
    assert N_NORM_CB % n_sub == 0 and tm % MM_ROWS == 0
    return pl.pallas_call(
        functools.partial(_inproj_qkv_kernel, n_norm_tiles=N_NORM_CB // n_sub),
        grid=(bt // tm, n // tn),
        in_specs=[pl.BlockSpec((tm, d), lambda i, j: (i, 0)),
                  pl.BlockSpec((d, tn), lambda i, j: (0, j)),
                  pl.BlockSpec((8, tn), lambda i, j: (0, j)),
                  pl.BlockSpec(bd.shape, lambda i, j: (0, 0))],
        out_specs=pl.BlockSpec((n_sub, tm, LANES), lambda i, j: (j, i, 0)),
        out_shape=jax.ShapeDtypeStruct((n // LANES, bt, LANES), BF16),
        compiler_params=_cparams("parallel", "arbitrary"),
        name="inproj_qkv",
    )(h, w, aux, bd)


def _inproj_gates_kernel(h_ref, w_ref, eg_ref, raw_ref, ga_ref):
    y = _dot(h_ref[...], w_ref[...])
    raw_ref[...] = y
    ga = _dot(jax.nn.sigmoid(y).astype(BF16), eg_ref[...])
    for n in range(N_BRANCH):
        ga_ref[n] = ga[:, n * MIX_WIDTH:(n + 1) * MIX_WIDTH].astype(ga_ref.dtype)


def _inproj_gates(h, w, eg, tm=1024):
    bt, d = h.shape
    return pl.pallas_call(
        _inproj_gates_kernel,
        grid=(bt // tm,),
        in_specs=[pl.BlockSpec((tm, d), lambda i: (i, 0)),
                  pl.BlockSpec((d, LANES), lambda i: (0, 0)),
                  pl.BlockSpec((LANES, N_BRANCH * MIX_WIDTH), lambda i: (0, 0))],
        out_specs=[pl.BlockSpec((tm, LANES), lambda i: (i, 0)),
                   pl.BlockSpec((N_BRANCH, tm, MIX_WIDTH), lambda i: (0, i, 0))],
        out_shape=[jax.ShapeDtypeStruct((bt, LANES), F32),
                   jax.ShapeDtypeStruct((N_BRANCH, bt, MIX_WIDTH), BF16)],
        compiler_params=_cparams("parallel"),
        name="inproj_gates",
    )(h, w, eg)


def _inproj_merge_kernel(h_ref, w_ref, o_ref):
    width = 2 * MM_CHUNK
    for c in range(w_ref.shape[1] // width):
        cols = slice(c * width, (c + 1) * width)
        o_ref[0, :, cols] = jax.nn.sigmoid(_dot(h_ref[...], w_ref[:, cols])).astype(o_ref.dtype)


def _inproj_merge(h, w, d_model, tm=1024, tn=1024):
    bt, d = h.shape
    n = w.shape[1]
    per = d_model // tn
    return pl.pallas_call(
        _inproj_merge_kernel,
        grid=(bt // tm, n // tn),
        in_specs=[pl.BlockSpec((tm, d), lambda i, j: (i, 0)),
                  pl.BlockSpec((d, tn), lambda i, j: (0, j))],
        out_specs=pl.BlockSpec((1, tm, tn), lambda i, j: (j // per, i, j % per)),
        out_shape=jax.ShapeDtypeStruct((n // d_model, bt, d_model), BF16),
        compiler_params=_cparams("parallel", "arbitrary"),
        name="inproj_merge",
    )(h, w)


def _compress_kernel(c_ref, pos_ref, w1_ref, w2_ref, aux_ref, bd_ref, o_ref):
    n_chunks = c_ref.shape[3]
    y = jnp.zeros((n_chunks, LANES), F32)
    for g in range(KV_GROUPS):
        c = c_ref[0, 0, g].astype(F32)
        a = _dot((c + pos_ref[0, 0]).astype(BF16), w1_ref[0, 0])
        b = _dot((c + pos_ref[0, 1]).astype(BF16), w1_ref[0, 1])
        hid = _gelu_tanh(a + pltpu.roll(b, n_chunks - 1, 0))
        y = y + _dot(hid.astype(BF16), w2_ref[0, g])
    normed = _head_rms(y, bd_ref[0:LANES, 0:LANES], aux_ref[0, 0:1, :])
    o_ref[0, 0] = jnp.where(aux_ref[0, 1:2, :] > 0.5, normed, y).astype(o_ref.dtype)


def _compress(chunks, pos, w1, w2p, aux, bd):
    _, b, g, n_chunks, cw = chunks.shape
    return pl.pallas_call(
        _compress_kernel,
        grid=(2, b),
        in_specs=[pl.BlockSpec((1, 1, g, n_chunks, cw), lambda w, i: (w, i, 0, 0, 0)),
                  pl.BlockSpec((1, 2, 1, cw), lambda w, i: (w, 0, 0, 0)),
                  pl.BlockSpec((1, 2, cw, CMP_HIDDEN), lambda w, i: (w, 0, 0, 0)),
                  pl.BlockSpec((1, g, CMP_HIDDEN, LANES), lambda w, i: (w, 0, 0, 0)),
                  pl.BlockSpec((1, 8, LANES), lambda w, i: (w, 0, 0)),
                  pl.BlockSpec(bd.shape, lambda w, i: (0, 0))],
        out_specs=pl.BlockSpec((1, 1, n_chunks, LANES), lambda w, i: (w, i, 0, 0)),
        out_shape=jax.ShapeDtypeStruct((2, b, n_chunks, LANES), BF16),
        compiler_params=_cparams("parallel", "parallel"),
        name="nsa_compress",
    )(chunks, pos, w1, w2p, aux, bd)


CMP_TQ = 256


def _cmp_select_kernel(q_ref, kc_ref, vc_ref, bias_ref, ovt_ref, o_ref, qa_ref, *, n_slc, n_top):
    ncp = kc_ref.shape[2]
    rows = HEADS_PER_GROUP * Q_BLOCK
    jrow = lax.broadcasted_iota(jnp.int32, (n_slc, Q_BLOCK), 0)
    jrow_f = jrow.astype(F32)
    low = _lane_half_mask(0)
    for sub in range(q_ref.shape[2] // Q_BLOCK):
        rs = slice(sub * Q_BLOCK, (sub + 1) * Q_BLOCK)
        t0 = pl.program_id(1) * q_ref.shape[2] + sub * Q_BLOCK
        row_t = t0 + lax.broadcasted_iota(jnp.int32, (Q_BLOCK, 1), 0)
        row_valid = jnp.concatenate([row_t >= CMP_LEN - 1] * HEADS_PER_GROUP, axis=0)
        cur = (t0 + lax.broadcasted_iota(jnp.int32, (n_slc, Q_BLOCK), 1)) // SLC_LEN

        def force(imp):
            forced = jnp.where(jrow == cur, FORCE_SCORE, jnp.where(jrow == cur - 1, FORCE_SCORE, imp))
            return jnp.where(jrow == 0, FORCE_SCORE, forced)

        outs = []
        for g in range(KV_GROUPS):
            own_half = _lane_half_mask(g)
            hs = slice(g * HEADS_PER_GROUP, (g + 1) * HEADS_PER_GROUP)
            qop = jnp.concatenate([jnp.where(own_half, q_ref[p, 0, rs, :], jnp.zeros((Q_BLOCK, LANES), q_ref.dtype))
                                   for p in range(N_QBLK)], axis=0)
            s = _dot_nt(qop, kc_ref[0, 0]) + bias_ref[hs, rs, :].reshape(rows, ncp)
            e = jnp.exp2(s - jnp.max(s, axis=-1, keepdims=True))
            inv = jnp.where(row_valid, 1.0 / jnp.sum(e, axis=-1, keepdims=True), 0.0)
            p = e * inv
            outs.append(_dot(p.astype(BF16), vc_ref[0, 0]))
            psum = p[0:Q_BLOCK]
            for r in range(1, HEADS_PER_GROUP):
                psum = psum + p[r * Q_BLOCK:(r + 1) * Q_BLOCK]
            hi, lo = _split_hi_lo(psum)
            imp_t = (_dot_nt(ovt_ref[...], hi) + _dot_nt(ovt_ref[...], lo))[0:n_slc]
            score = jnp.where(jrow <= cur, force(imp_t), NEG)
            sel_bias = jnp.full((n_slc, Q_BLOCK), NEG, F32)
            for _ in range(n_top):
                best = jnp.max(score, axis=0, keepdims=True)
                first = jnp.min(jnp.where(score == best, jrow_f, float(n_slc)), axis=0, keepdims=True)
                hit = jrow_f == first
                sel_bias = jnp.where(hit, jnp.where(best > NEG / 2, 0.0, NEG), sel_bias)
                score = jnp.where(hit, TAKEN, score)
            if n_slc < HEAD_DIM:
                sel_bias = jnp.concatenate([sel_bias, jnp.full((HEAD_DIM - n_slc, Q_BLOCK), NEG, F32)], axis=0)
            sel_t = jnp.concatenate([sel_bias, sel_bias], axis=0).T.astype(qa_ref.dtype)
            for p in range(N_QBLK):
                qa_ref[g * HEADS_PER_GROUP + p, 0, rs, :] = jnp.where(own_half, q_ref[p, 0, rs, :], sel_t)
        for p in range(N_QBLK):
            prs = slice(p * Q_BLOCK, (p + 1) * Q_BLOCK)
            o_ref[p, 0, rs, :] = jnp.where(low, outs[0][prs], outs[1][prs]).astype(o_ref.dtype)


def _cmp_select(qkv, cmp_kv, bias_c, ovt, n_slc, tq=CMP_TQ):
    _, b, t, _ = qkv.shape
    ncp = cmp_kv.shape[2]
    assert n_slc <= HEAD_DIM
    return pl.pallas_call(
        functools.partial(_cmp_select_kernel, n_slc=n_slc, n_top=min(SLC_TOPK, n_slc)),
        grid=(b, t // tq),
        in_specs=[pl.BlockSpec((N_QBLK, 1, tq, LANES), lambda bi, i: (CB_AQ // N_QBLK, bi, i, 0)),
                  pl.BlockSpec((1, 1, ncp, LANES), lambda bi, i: (0, bi, 0, 0)),
                  pl.BlockSpec((1, 1, ncp, LANES), lambda bi, i: (1, bi, 0, 0)),
                  pl.BlockSpec((N_HEADS, tq, ncp), lambda bi, i: (0, i, 0)),
                  pl.BlockSpec((LANES, ncp), lambda bi, i: (0, 0))],
        out_specs=[pl.BlockSpec((N_QBLK, 1, tq, LANES), lambda bi, i: (0, bi, i, 0)),
                   pl.BlockSpec((N_HEADS, 1, tq, LANES), lambda bi, i: (0, bi, i, 0))],
        out_shape=[jax.ShapeDtypeStruct((N_QBLK, b, t, LANES), BF16),
                   jax.ShapeDtypeStruct((N_HEADS, b, t, LANES), BF16)],
        compiler_params=_cparams("parallel", "parallel"),
        name="nsa_cmp_select",
    )(qkv, cmp_kv, cmp_kv, bias_c, ovt)


def _flash_update(s, m, acc, v):
    m_new = jnp.maximum(m, jnp.max(s, axis=-1, keepdims=True))
    p = jnp.exp2(s - m_new)
    return m_new, jnp.exp2(m - m_new) * acc + _dot(p.astype(BF16), v)


def _normalise(acc):
    return acc / pltpu.roll(acc, HEAD_DIM, 1)


def _augment_keys_values(ka_scr, va_scr, k2, xk, v2):
    low = _lane_half_mask(0)
    ones = jnp.ones_like(v2)
    ka_scr[0] = jnp.where(low, k2, xk)
    ka_scr[1] = jnp.where(low, xk, k2)
    va_scr[0] = jnp.where(low, v2, ones)
    va_scr[1] = jnp.where(low, ones, v2)


def _store_pair(scr, s):
    scr[0], scr[1] = s


def _load_pair(scr):
    return scr[0], scr[1]


def _pipelined_sweep(logits, update, n_tiles, sa_scr, sb_scr, carry):
    def single(j, c):
        _store_pair(sa_scr, logits(j))
        return update(j, _load_pair(sa_scr), c)

    odd = n_tiles % 2
    carry = lax.fori_loop(0, odd, single, carry)
    _store_pair(sa_scr, logits(odd))

    def pair(jp, c):
        j = odd + 2 * jp
        _store_pair(sb_scr, logits(j + 1))
        c = update(j, _load_pair(sa_scr), c)
        _store_pair(sa_scr, logits(j + 2))
        return update(j + 1, _load_pair(sb_scr), c)

    return lax.fori_loop(0, n_tiles // 2, pair, carry)


def _slc_kernel(qa_ref, k_ref, v_ref, hot_ref, nb_ref, o_ref, ka_scr, va_scr, sa_scr, sb_scr):
    i = pl.program_id(1)
    tq = qa_ref.shape[2]
    rows = HEADS_PER_GROUP * tq

    @pl.when(i == 0)
    def _():
        _augment_keys_values(ka_scr, va_scr, k_ref[0, 0], hot_ref[...], v_ref[0, 0])

    heads = [slice(g * HEADS_PER_GROUP, (g + 1) * HEADS_PER_GROUP) for g in range(KV_GROUPS)]
    qs = [qa_ref[hs, 0].reshape(rows, LANES) for hs in heads]

    def keys(j):
        return pl.ds(pl.multiple_of(j * tq, tq), tq)

    def logits(j):
        return tuple(_dot_nt(qs[g], ka_scr[g, keys(j), :]) for g in range(KV_GROUPS))

    def update(j, s, carry, which=None, pen=None):
        new = []
        for g in range(KV_GROUPS):
            sg = s[g]
            if which is not None:
                sg = (sg.reshape(HEADS_PER_GROUP, tq, tq) + nb_ref[which, heads[g]]).reshape(rows, tq)
            if pen is not None:
                sg = sg + pen
            new.append(_flash_update(sg, *carry[g], va_scr[g, keys(j), :]))
        return tuple(new)

    init = (jnp.full((rows, 1), NEG, F32), jnp.zeros((rows, LANES), F32))
    carry = _pipelined_sweep(logits, update, jnp.maximum(i - 1, 0), sa_scr, sb_scr, (init, init))
    _store_pair(sb_scr, logits(i))
    carry = update(jnp.maximum(i - 1, 0), _load_pair(sa_scr), carry, which=0, pen=jnp.where(i >= 1, 0.0, NEG))
    (_, acc0), (_, acc1) = update(i, _load_pair(sb_scr), carry, which=1)
    _merge_group_outputs(o_ref, _normalise(acc0), _normalise(acc1))


def _slc_attention(qa, qkv, hot2, near_bias):
    _, b, t, _ = qa.shape
    tq = near_bias.shape[2]
    return pl.pallas_call(
        _slc_kernel,
        grid=(b, t // tq),
        in_specs=[pl.BlockSpec((N_HEADS, 1, tq, LANES), lambda bi, i: (0, bi, i, 0)),
                  pl.BlockSpec((1, 1, t, LANES), lambda bi, i: (CB_AKS, bi, 0, 0)),
                  pl.BlockSpec((1, 1, t, LANES), lambda bi, i: (CB_AVS, bi, 0, 0)),
                  pl.BlockSpec(hot2.shape, lambda bi, i: (0, 0)),
                  pl.BlockSpec(near_bias.shape, lambda bi, i: (0, 0, 0, 0))],
        out_specs=pl.BlockSpec((N_QBLK, 1, tq, LANES), lambda bi, i: (0, bi, i, 0)),
        out_shape=jax.ShapeDtypeStruct((N_QBLK, b, t, LANES), BF16),
        scratch_shapes=[pltpu.VMEM((KV_GROUPS, t, LANES), BF16), pltpu.VMEM((KV_GROUPS, t, LANES), BF16),
                        pltpu.VMEM((KV_GROUPS, HEADS_PER_GROUP * tq, tq), F32),
                        pltpu.VMEM((KV_GROUPS, HEADS_PER_GROUP * tq, tq), F32)],
        compiler_params=_cparams("parallel", "arbitrary"),
        name="nsa_slc_attention",
    )(qa, qkv, qkv, hot2, near_bias)


BAND_TQ = 256


def _banded_kernel(*refs, n_blk, use_sinks):
    q_ref = refs[0]
    k_refs = refs[1:1 + n_blk]
    v_refs = refs[1 + n_blk:1 + 2 * n_blk]
    bias_ref, o_ref = refs[1 + 2 * n_blk:]
    i = pl.program_id(1)
    tq = q_ref.shape[2]
    n_keys = n_blk * Q_BLOCK
    rows = HEADS_PER_GROUP * tq
    first_key = (i + 1) * tq - n_keys
    col = lax.broadcasted_iota(jnp.int32, (1, n_keys), 1)
    pad_pen = jnp.where(first_key + col >= 0, 0.0, NEG)
    kcat = jnp.concatenate([r[0, 0] for r in k_refs], axis=0)
    vcat = jnp.concatenate([r[0, 0] for r in v_refs], axis=0)
    if use_sinks:
        sink_row = lax.broadcasted_iota(jnp.int32, (n_keys, 1), 0) == 0
        kcat = jnp.where(sink_row, jnp.zeros_like(kcat), kcat)
        vcat = jnp.where(sink_row, jnp.zeros_like(vcat), vcat)
        pad_pen = jnp.where(col == 0, 0.0, pad_pen)
    outs = []
    for g in range(KV_GROUPS):
        own_half = _lane_half_mask(g)
        qop = jnp.concatenate([jnp.where(own_half, q_ref[p, 0], jnp.zeros_like(q_ref[p, 0]))
                               for p in range(N_QBLK)], axis=0)
        hs = slice(g * HEADS_PER_GROUP, (g + 1) * HEADS_PER_GROUP)
        s = _dot_nt(qop, kcat) + bias_ref[hs].reshape(rows, n_keys) + pad_pen
        e = jnp.exp2(s - jnp.max(s, axis=-1, keepdims=True))
        acc = _dot(e.astype(BF16), jnp.where(own_half, vcat, jnp.ones_like(vcat)))
        outs.append(_normalise(acc))
    _merge_group_outputs(o_ref, outs[0], outs[1])


def _banded_attention(qkv, cb_q, cb_k, cb_v, bias, use_sinks):
    _, b, t, _ = qkv.shape
    tq = bias.shape[1]
    n_blk = bias.shape[2] // Q_BLOCK
    per_tile = tq // Q_BLOCK

    def kv_spec(cb, back):
        return pl.BlockSpec((1, 1, Q_BLOCK, LANES),
                            lambda bi, i: (cb, bi, jnp.maximum((i + 1) * per_tile - 1 - back, 0), 0))

    backs = [n_blk - 1 - jb for jb in range(n_blk)]
    return pl.pallas_call(
        functools.partial(_banded_kernel, n_blk=n_blk, use_sinks=use_sinks),
        grid=(b, t // tq),
        in_specs=([pl.BlockSpec((N_QBLK, 1, tq, LANES), lambda bi, i: (cb_q // N_QBLK, bi, i, 0))]
                  + [kv_spec(cb_k, back) for back in backs] + [kv_spec(cb_v, back) for back in backs]
                  + [pl.BlockSpec(bias.shape, lambda bi, i: (0, 0, 0))]),
        out_specs=pl.BlockSpec((N_QBLK, 1, tq, LANES), lambda bi, i: (0, bi, i, 0)),
        out_shape=jax.ShapeDtypeStruct((N_QBLK, b, t, LANES), BF16),
        compiler_params=_cparams("parallel", "parallel"),
        name="banded_attention",
    )(qkv, *([qkv] * (2 * n_blk)), bias)


FOX_EXTRA = 6


def _fox_prep_kernel(g_ref, fb_ref, pq_ref, pk_ref, oq_ref, ok_ref, xq_ref, xk_ref, carry_scr):
    @pl.when(pl.program_id(1) == 0)
    def _():
        carry_scr[...] = jnp.zeros_like(carry_scr)

    x = g_ref[...] + fb_ref[...]
    logf = jnp.minimum(x, 0.0) - jnp.log(1.0 + jnp.exp(-jnp.abs(x)))
    ch = x.shape[0]
    lower = jnp.where(lax.broadcasted_iota(jnp.int32, (ch, ch), 0)
                      >= lax.broadcasted_iota(jnp.int32, (ch, ch), 1), 1.0, 0.0).astype(BF16)

    def split3(v):
        hi, rest = v.astype(BF16), v - v.astype(BF16).astype(F32)
        return (hi,) + _split_hi_lo(rest)

    cs = sum(_dot(lower, part) for part in split3(logf)) + carry_scr[0:1, :]
    carry_scr[0:1, :] = cs[ch - 1:ch, :]
    parts = split3(cs * LOG2E)
    xq = sum(_dot(part, pq_ref[n]) for n, part in enumerate(parts)) + oq_ref[...]
    xk = sum(_dot(part, pk_ref[n]) for n, part in enumerate(parts)) + ok_ref[...]
    for p in range(N_QBLK):
        xq_ref[p, 0] = xq[:, p * LANES:(p + 1) * LANES].astype(xq_ref.dtype)
        xk_ref[p, 0] = xk[:, p * LANES:(p + 1) * LANES].astype(xk_ref.dtype)


def _fox_prep(graw, fb_row, b, t, chunk=512):
    chunk = min(chunk, t)
    nch = t // chunk
    pq = np.zeros((3, LANES, N_QBLK * LANES), np.float32)
    pk = np.zeros((3, LANES, N_QBLK * LANES), np.float32)
    oq = np.zeros((1, N_QBLK * LANES), np.float32)
    ok = np.zeros((1, N_QBLK * LANES), np.float32)
    for h in range(N_HEADS):
        col = (h // 2) * LANES + (HEAD_DIM if h % 2 == 0 else 0)
        for n in range(3):
            pq[n, N_GATE_COLS + h, col + n] = 1.0
            pk[n, N_GATE_COLS + h, col + 3 + n] = -1.0
        oq[0, col + 3:col + FOX_EXTRA] = 1.0
        ok[0, col:col + 3] = 1.0
    const = lambda a: pl.BlockSpec(a.shape, lambda bi, c: (0,) * a.ndim)
    out_spec = pl.BlockSpec((N_QBLK, 1, chunk, LANES), lambda bi, c: (0, bi, c, 0))
    return pl.pallas_call(
        _fox_prep_kernel,
        grid=(b, nch),
        in_specs=[pl.BlockSpec((chunk, LANES), lambda bi, c: (bi * nch + c, 0)),
                  const(fb_row), const(pq), const(pk), const(oq), const(ok)],
        out_specs=[out_spec, out_spec],
        out_shape=[jax.ShapeDtypeStruct((N_QBLK, b, t, LANES), BF16)] * 2,
        scratch_shapes=[pltpu.VMEM((8, LANES), F32)],
        compiler_params=_cparams("parallel", "arbitrary"),
        name="fox_prep",
    )(graw, fb_row, jnp.asarray(pq, BF16), jnp.asarray(pk, BF16), jnp.asarray(oq), jnp.asarray(ok))


def _fox_kernel(q_ref, xq_ref, k_ref, xk_ref, v_ref, o_ref, ka_scr, va_scr, sa_scr, sb_scr, *, tq):
    i = pl.program_id(2)

    @pl.when(i == 0)
    def _():
        _augment_keys_values(ka_scr, va_scr, k_ref[0, 0], xk_ref[0, 0], v_ref[0, 0])

    low = _lane_half_mask(0)
    q2, xq = q_ref[0, 0], xq_ref[0, 0]
    qs = [jnp.where(low, q2, xq), jnp.where(low, xq, q2)]
    causal_pen = jnp.where(lax.broadcasted_iota(jnp.int32, (tq, tq), 0)
                           >= lax.broadcasted_iota(jnp.int32, (tq, tq), 1), 0.0, NEG)

    def keys(j):
        return pl.ds(pl.multiple_of(j * tq, tq), tq)

    def logits(j):
        return tuple(_dot_nt(qs[hh], ka_scr[hh, keys(j), :]) for hh in range(2))

    def update(j, s, carry, pen=None):
        return tuple(_flash_update(s[hh] if pen is None else s[hh] + pen, *carry[hh], va_scr[hh, keys(j), :])
                     for hh in range(2))

    init = (jnp.full((tq, 1), NEG, F32), jnp.zeros((tq, LANES), F32))
    carry = _pipelined_sweep(logits, update, i, sa_scr, sb_scr, (init, init))
    (_, acc0), (_, acc1) = update(i, _load_pair(sa_scr), carry, pen=causal_pen)
    o_ref[0, 0] = jnp.where(low, _normalise(acc0), _normalise(acc1)).astype(o_ref.dtype)


def _fox_attention(qkv, xq, xk, tq):
    _, b, t, _ = qkv.shape
    q_spec = lambda cb: pl.BlockSpec((1, 1, tq, LANES), lambda bi, p, i: (cb + p, bi, i, 0))
    kv_spec = lambda cb: pl.BlockSpec((1, 1, t, LANES), lambda bi, p, i: (cb + p, bi, 0, 0))
    return pl.pallas_call(
        functools.partial(_fox_kernel, tq=tq),
        grid=(b, N_QBLK, t // tq),
        in_specs=[q_spec(CB_CQ), q_spec(0), kv_spec(CB_CK), kv_spec(0), kv_spec(CB_CV)],
        out_specs=pl.BlockSpec((1, 1, tq, LANES), lambda bi, p, i: (p, bi, i, 0)),
        out_shape=jax.ShapeDtypeStruct((N_QBLK, b, t, LANES), BF16),
        scratch_shapes=[pltpu.VMEM((2, t, LANES), BF16), pltpu.VMEM((2, t, LANES), BF16),
                        pltpu.VMEM((2, tq, tq), F32), pltpu.VMEM((2, tq, tq), F32)],
        compiler_params=_cparams("parallel", "parallel", "arbitrary"),
        name="fox_attention",
    )(qkv, xq, qkv, xk, qkv)


def _merge_kernel(ocmp_ref, oslc_ref, owin_ref, ob_ref, oc_ref, ga_ref, mg_ref, wb_ref, wo_ref, x_ref, o_ref):
    def cat(ref):
        return jnp.concatenate([ref[p] for p in range(N_QBLK)], axis=-1)

    oa = (ga_ref[0].astype(F32) * cat(ocmp_ref).astype(F32)
          + ga_ref[1].astype(F32) * cat(oslc_ref).astype(F32)
          + ga_ref[2].astype(F32) * cat(owin_ref).astype(F32)).astype(BF16)
    branches = (oa, cat(ob_ref), cat(oc_ref))
    d = o_ref.shape[1]
    chunks = [slice(c, c + MM_CHUNK) for c in range(0, d, MM_CHUNK)]
    mix = jnp.concatenate(
        [sum(mg_ref[n, :, cols].astype(F32) * _dot(branches[n], wb_ref[n, :, cols]) for n in range(N_BRANCH))
         .astype(BF16) for cols in chunks], axis=-1)
    for cols in chunks:
        o_ref[:, cols] = x_ref[:, cols] + _dot(mix, wo_ref[:, cols])


def _merge(ocmp, oslc, owin, ob, oc, ga, mg, wb, wo, x2, tm=512):
    bt, d = x2.shape
    o_spec = pl.BlockSpec((N_QBLK, tm, LANES), lambda i: (0, i, 0))
    return pl.pallas_call(
        _merge_kernel,
        grid=(bt // tm,),
        in_specs=[o_spec] * 5 + [pl.BlockSpec((N_BRANCH, tm, MIX_WIDTH), lambda i: (0, i, 0)),
                                 pl.BlockSpec((N_BRANCH, tm, d), lambda i: (0, i, 0)),
                                 pl.BlockSpec(wb.shape, lambda i: (0, 0, 0)),
                                 pl.BlockSpec(wo.shape, lambda i: (0, 0)),
                                 pl.BlockSpec((tm, d), lambda i: (i, 0))],
        out_specs=pl.BlockSpec((tm, d), lambda i: (i, 0)),
        out_shape=jax.ShapeDtypeStruct((bt, d), F32),
        compiler_params=_cparams("parallel"),
        name="branch_merge",
    )(ocmp, oslc, owin, ob, oc, ga, mg, wb, wo, x2)


FFN_TF = 512


def _ffn_kernel(x_ref, xh_ref, g_ref, wg_ref, wu_ref, cw_ref, cb_ref, wd_ref, o_ref,
                h_scr, hh_scr, a_scr, acc_scr, *, tm, tiles_per_seq):
    i = pl.program_id(0)
    f = pl.program_id(1)

    def norm(x):
        ms = jnp.mean(x * x, axis=-1, keepdims=True)
        return ((x * lax.rsqrt(ms + RMS_EPS)) * g_ref[...]).astype(BF16)

    @pl.when(f == 0)
    def _():
        h_scr[...] = norm(x_ref[...])
        hh_scr[...] = norm(xh_ref[...])
        acc_scr[...] = jnp.zeros_like(acc_scr)

    a_scr[0:8, :] = _dot(hh_scr[...], wg_ref[...]) * jnp.where(i % tiles_per_seq == 0, 0.0, 1.0)
    acts = []
    for c in range(wg_ref.shape[1] // MM_CHUNK):
        cols = slice(c * MM_CHUNK, (c + 1) * MM_CHUNK)
        a = _dot(h_scr[...], wg_ref[:, cols])
        a_scr[8:8 + tm, cols] = a
        conv = (cw_ref[0:1, cols] * a_scr[6:6 + tm, cols] + cw_ref[1:2, cols] * a_scr[7:7 + tm, cols]
                + cw_ref[2:3, cols] * a + cb_ref[:, cols])
        acts.append((_gelu_tanh(conv) * _dot(h_scr[...], wu_ref[:, cols])).astype(BF16))
    acc_scr[...] += _dot(jnp.concatenate(acts, axis=-1), wd_ref[...])

    @pl.when(f == pl.num_programs(1) - 1)
    def _():
        o_ref[...] = x_ref[...] + acc_scr[...]


def _ffn(x2, g, wg, wu, cw, cb, wd, t, tm=1024, tf=FFN_TF):
    bt, d = x2.shape
    pad = -wg.shape[1] % tf
    wg, wu, cw = [jnp.pad(a, ((0, 0), (0, pad))) for a in (wg, wu, cw)]
    cb, wd = jnp.pad(cb, (0, pad)), jnp.pad(wd, ((0, pad), (0, 0)))
    ff = wg.shape[1]
    halo_blocks = tm // 8
    return pl.pallas_call(
        functools.partial(_ffn_kernel, tm=tm, tiles_per_seq=t // tm),
        grid=(bt // tm, ff // tf),
        in_specs=[pl.BlockSpec((tm, d), lambda i, f: (i, 0)),
                  pl.BlockSpec((8, d), lambda i, f: (jnp.maximum(i * halo_blocks - 1, 0), 0)),
                  pl.BlockSpec((1, d), lambda i, f: (0, 0)),
                  pl.BlockSpec((d, tf), lambda i, f: (0, f)),
                  pl.BlockSpec((d, tf), lambda i, f: (0, f)),
                  pl.BlockSpec((CONV_WIDTH, tf), lambda i, f: (0, f)),
                  pl.BlockSpec((1, tf), lambda i, f: (0, f)),
                  pl.BlockSpec((tf, d), lambda i, f: (f, 0))],
        out_specs=pl.BlockSpec((tm, d), lambda i, f: (i, 0)),
        out_shape=jax.ShapeDtypeStruct((bt, d), F32),
        scratch_shapes=[pltpu.VMEM((tm, d), BF16), pltpu.VMEM((8, d), BF16),
                        pltpu.VMEM((tm + 8, tf), F32), pltpu.VMEM((tm, d), F32)],
        compiler_params=_cparams("parallel", "arbitrary"),
        name="conv_ffn",
    )(x2, x2, g.reshape(1, d), wg, wu, cw, cb.reshape(1, ff), wd)


def _t5_bucket_np(dist):
    max_exact = N_BUCKETS // 2
    d = np.maximum(dist, 0)
    ratio = np.log(np.maximum(d, 1) / max_exact) / math.log(MAX_DISTANCE / max_exact)
    large = np.minimum(max_exact + (ratio * (N_BUCKETS - max_exact)).astype(np.int64), N_BUCKETS - 1)
    return np.where(d < max_exact, d, large)


def _expand_kernel(tab_ref, idx_ref, o_ref):
    tab = tab_ref[...]
    onehot = jnp.where(lax.broadcasted_iota(jnp.int32, (tab.shape[1], idx_ref.shape[1]), 0) == idx_ref[...],
                       1.0, 0.0).astype(BF16)
    hi, rest = tab.astype(BF16), tab - tab.astype(BF16).astype(F32)
    mid, lo = _split_hi_lo(rest)
    o_ref[...] = _dot(hi, onehot) + _dot(mid, onehot) + _dot(lo, onehot)


def _bias_from_dist(tab_t, dist, valid, chunk=4096):
    heads = tab_t.shape[0]
    tab = jnp.zeros((heads, LANES), F32).at[:, :N_BUCKETS].set(tab_t).at[:, N_BUCKETS].set(NEG)
    idx = np.where(valid, _t5_bucket_np(dist), N_BUCKETS).reshape(1, -1).astype(np.int32)
    n = idx.shape[1]
    chunk = min(chunk, n)
    out = pl.pallas_call(
        _expand_kernel,
        grid=(n // chunk,),
        in_specs=[pl.BlockSpec((heads, LANES), lambda i: (0, 0)), pl.BlockSpec((1, chunk), lambda i: (0, i))],
        out_specs=pl.BlockSpec((heads, chunk), lambda i: (0, i)),
        out_shape=jax.ShapeDtypeStruct((heads, n), F32),
        compiler_params=_cparams("parallel"),
        name="rel_bias_expand",
    )(tab, jnp.asarray(idx))
    return out.reshape((heads,) + dist.shape)


def _cmp_bias_kernel(g_ref, o_ref):
    i = pl.program_id(0)
    ncp = o_ref.shape[2]
    for h in range(o_ref.shape[0]):
        o_ref[h] = pltpu.roll(g_ref[h], i * (Q_BLOCK // CMP_STRIDE), 1)[:, :ncp]


def _cmp_bias(g_tab, t):
    heads, _, width = g_tab.shape
    ncp = t // CMP_STRIDE
    return pl.pallas_call(
        _cmp_bias_kernel,
        grid=(t // Q_BLOCK,),
        in_specs=[pl.BlockSpec(g_tab.shape, lambda i: (0, 0, 0))],
        out_specs=pl.BlockSpec((heads, Q_BLOCK, ncp), lambda i: (0, i, 0)),
        out_shape=jax.ShapeDtypeStruct((heads, t, ncp), F32),
        compiler_params=_cparams("parallel"),
        name="cmp_bias_build",
    )(g_tab)


def _static_tables(t):
    ncp = t // CMP_STRIDE
    n_cmp = (t - CMP_LEN) // CMP_STRIDE + 1
    n_slc = t // SLC_LEN
    lane = np.arange(MM_CHUNK)
    bd = (lane[:, None] // HEAD_DIM == lane[None, :] // HEAD_DIM).astype(np.float32)
    eg = np.zeros((LANES, N_BRANCH * MIX_WIDTH), np.float32)
    for h in range(N_HEADS):
        base = (h % N_QBLK) * LANES + (h // N_QBLK) * HEAD_DIM
        for n in range(N_BRANCH):
            eg[h * N_BRANCH + n, n * MIX_WIDTH + base:n * MIX_WIDTH + base + HEAD_DIM] = 1.0
    c_start = np.arange(n_cmp) * CMP_STRIDE
    s_start = np.arange(n_slc) * SLC_LEN
    overlap = ((c_start[:, None] < s_start[None, :] + SLC_LEN) & (c_start[:, None] + CMP_LEN > s_start[None, :]))
    ovt = np.zeros((LANES, ncp), np.float32)
    ovt[:n_slc, :n_cmp] = overlap.T
    key_block = np.zeros((t, LANES), np.float32)
    key_block[np.arange(t), np.arange(t) // SLC_LEN] = 1.0
    key_block[np.arange(t), HEAD_DIM + np.arange(t) // SLC_LEN] = 1.0
    as_bf16 = lambda a: jnp.asarray(a, BF16)
    return dict(bd=as_bf16(bd), eg=as_bf16(eg), ovt=as_bf16(ovt), key_block=as_bf16(key_block), n_slc=n_slc)


def _bias_tables(rel_bias, t):
    n_cmp = (t - CMP_LEN) // CMP_STRIDE + 1
    ncp = t // CMP_STRIDE
    tab_a = rel_bias[:, :N_HEADS].T * LOG2E
    tab_b = rel_bias[:, N_HEADS:].T * LOG2E
    r = np.arange(Q_BLOCK)[:, None]

    def band(window):
        n_prev = -(-(window - 1) // Q_BLOCK)
        n_keys = n_prev * Q_BLOCK + BAND_TQ
        dist = (n_keys - BAND_TQ) + np.arange(BAND_TQ)[:, None] - np.arange(n_keys)[None, :]
        valid = (dist >= 0) & (dist < window)
        assert not valid[:, 0].any()
        return dist, valid

    m = np.arange(2 * ncp)
    m = np.where(m < ncp, m, m - 2 * ncp)[None, :]
    dist_c = r - (m * CMP_STRIDE + CMP_LEN - 1)
    assert n_cmp == ncp - 1 and (dist_c[:, m[0] == Q_BLOCK // CMP_STRIDE - 1] < 0).all()
    tile = np.arange(SLC_TILE)
    dist_n = np.stack([SLC_TILE * (1 - which) + tile[:, None] - tile[None, :] for which in range(2)])
    near = _bias_from_dist(tab_a - tab_a[:, N_BUCKETS - 1:], dist_n, dist_n >= 0)
    return dict(win=_bias_from_dist(tab_a, *band(NSA_WINDOW)),
                swa=_bias_from_dist(tab_b, *band(SWA_WINDOW)),
                cmp=_cmp_bias(_bias_from_dist(tab_a, dist_c, dist_c >= 0), t),
                near=jnp.transpose(near, (1, 0, 2, 3)))


def _pair_cols(w):
    parts = []
    for p in range(N_QBLK):
        parts += [w[..., p * HEAD_DIM:(p + 1) * HEAD_DIM],
                  w[..., (N_QBLK + p) * HEAD_DIM:(N_QBLK + p + 1) * HEAD_DIM]]
    return jnp.concatenate(parts, axis=-1)


def _prep_layer(w_in, qk_gain, w_branch):
    kv = KV_GROUPS * HEAD_DIM
    widths = [('a_q', MIX_WIDTH), ('a_kc', kv), ('a_vc', kv), ('a_ks', kv), ('a_vs', kv), ('a_kw', kv), ('a_vw', kv),
              ('a_gate', N_GATE_COLS), ('b_q', MIX_WIDTH), ('b_k', kv), ('b_v', kv),
              ('c_q', MIX_WIDTH), ('c_k', MIX_WIDTH), ('c_v', MIX_WIDTH), ('c_f', N_HEADS),
              ('merge', N_BRANCH * w_in.shape[0])]
    cols, off = {}, 0
    for name, w in widths:
        cols[name] = w_in[:, off:off + w]
        off += w
    scale = HEAD_DIM ** -0.5 * LOG2E
    tile = lambda gvec, n: jnp.tile(gvec, n)
    zeros = lambda n: jnp.zeros((n,), F32)
    ones = lambda n: jnp.ones((n,), F32)
    pieces = [(_pair_cols(cols['a_q']), tile(qk_gain[0] * scale, N_HEADS), ones(MIX_WIDTH)),
              (_pair_cols(cols['b_q']), tile(qk_gain[2] * scale, N_HEADS), ones(MIX_WIDTH)),
              (cols['c_q'], tile(qk_gain[4] * scale, N_HEADS), ones(MIX_WIDTH)),
              (cols['c_k'], tile(qk_gain[5], N_HEADS), ones(MIX_WIDTH)),
              (cols['a_ks'], tile(qk_gain[1], KV_GROUPS), ones(kv)),
              (cols['a_kw'], tile(qk_gain[1], KV_GROUPS), ones(kv)),
              (cols['b_k'], tile(qk_gain[3], KV_GROUPS), ones(kv)),
              (cols['a_vs'], zeros(kv), zeros(kv)),
              (cols['c_v'], zeros(MIX_WIDTH), zeros(MIX_WIDTH)),
              (cols['a_kc'], zeros(kv), zeros(kv)), (cols['a_vc'], zeros(kv), zeros(kv)),
              (cols['a_vw'], zeros(kv), zeros(kv)), (cols['b_v'], zeros(kv), zeros(kv))]
    w_qkv = jnp.concatenate([p[0] for p in pieces], axis=1).astype(BF16)
    n_qkv = w_qkv.shape[1]
    aux = jnp.zeros((8, n_qkv), F32)
    aux = aux.at[0].set(jnp.concatenate([p[1] for p in pieces])).at[1].set(jnp.concatenate([p[2] for p in pieces]))
    d = w_in.shape[0]
    w_gates = jnp.concatenate([cols['a_gate'], cols['c_f'],
                               jnp.zeros((d, LANES - N_GATE_COLS - N_HEADS), F32)], axis=1).astype(BF16)
    wb = jnp.stack([_pair_cols(w_branch[0].T).T, _pair_cols(w_branch[1].T).T, w_branch[2]]).astype(BF16)
    return w_qkv, aux, w_gates, cols['merge'].astype(BF16), wb


def _prep_compress(cmp_pos, cmp_w1, cmp_w2, k_gain):
    half = CMP_STRIDE * HEAD_DIM
    pos = cmp_pos.reshape(2, 2, 1, half)
    w1 = cmp_w1.reshape(2, 2, half, CMP_HIDDEN).astype(BF16)
    z = jnp.zeros_like(cmp_w2)
    w2p = jnp.stack([jnp.concatenate([cmp_w2, z], axis=-1), jnp.concatenate([z, cmp_w2], axis=-1)], axis=1).astype(BF16)
    aux = jnp.zeros((2, 8, LANES), F32)
    aux = aux.at[0, 0].set(jnp.tile(k_gain, KV_GROUPS)).at[0, 1].set(1.0)
    return pos, w1, w2p, aux


def kernel(x, rel_bias, norm_mix, norm_ffn, w_in, forget_bias, qk_gain, cmp_pos, cmp_w1, cmp_w2, sinks, w_branch, w_out, w_gate, w_up, conv_w, conv_b, w_down):
    b, t, d = x.shape
    bt = b * t
    depth = w_in.shape[0]
    tabs = _static_tables(t)
    biases = _bias_tables(rel_bias, t)
    n_chunks = t // CMP_STRIDE
    fox_tq = min(512, t)
    x2 = x.reshape(bt, d)
    for l in range(depth):
        w_qkv, aux, w_gates, w_merge, wb = _prep_layer(w_in[l], qk_gain[l], w_branch[l])
        h = _rmsnorm(x2, norm_mix[l])
        qkv = _inproj_qkv(h, w_qkv, aux, tabs['bd']).reshape(N_CB, b, t, LANES)
        graw, ga = _inproj_gates(h, w_gates, tabs['eg'])
        mg = _inproj_merge(h, w_merge, d)

        chunks = jnp.stack([qkv[CB_AKC], qkv[CB_AVC]]).reshape(2, b, n_chunks, CMP_STRIDE, KV_GROUPS, HEAD_DIM)
        chunks = jnp.transpose(chunks, (0, 1, 4, 2, 3, 5)).reshape(2, b, KV_GROUPS, n_chunks, CMP_STRIDE * HEAD_DIM)
        cmp_kv = _compress(chunks, *_prep_compress(cmp_pos[l], cmp_w1[l], cmp_w2[l], qk_gain[l, 1]), tabs['bd'])
        o_cmp, qa = _cmp_select(qkv, cmp_kv, biases['cmp'], tabs['ovt'], tabs['n_slc'])
        o_slc = _slc_attention(qa, qkv, tabs['key_block'], biases['near'])
        o_win = _banded_attention(qkv, CB_AQ, CB_AKW, CB_AVW, biases['win'], False)

        swa_bias = biases['swa'].at[:, :, 0].set(sinks[l][:, None] * LOG2E)
        o_b = _banded_attention(qkv, CB_BQ, CB_BK, CB_BV, swa_bias, True)

        fb_row = jnp.zeros((1, LANES), F32).at[0, N_GATE_COLS:N_GATE_COLS + N_HEADS].set(forget_bias[l])
        o_c = _fox_attention(qkv, *_fox_prep(graw, fb_row, b, t), fox_tq)

        flat = lambda o: o.reshape(N_QBLK, bt, LANES)
        x2 = _merge(flat(o_cmp), flat(o_slc), flat(o_win), flat(o_b), flat(o_c), ga, mg, wb,
                    w_out[l].astype(BF16), x2)
        x2 = _ffn(x2, norm_ffn[l], w_gate[l].astype(BF16), w_up[l].astype(BF16), conv_w[l], conv_b[l],
                  w_down[l].astype(BF16), t)
    return x2.reshape(b, t, d)
```

```python
import functools
import math

import numpy as np
import jax
import jax.numpy as jnp
from jax import lax
from jax.experimental import pallas as pl
from jax.experimental.pallas import tpu as pltpu

F32 = jnp.float32
BF16 = jnp.bfloat16

HEAD_DIM = 64
LANES = 128
N_HEADS = 8
KV_GROUPS = 2
HEADS_PER_GROUP = N_HEADS // KV_GROUPS
N_QBLK = N_HEADS * HEAD_DIM // LANES
MIX_WIDTH = N_HEADS * HEAD_DIM
N_BRANCH = 3
Q_BLOCK = 128
CMP_LEN = 32
CMP_STRIDE = 16
CMP_HIDDEN = 256
SLC_LEN = 64
SLC_TOPK = 8
NSA_WINDOW = 256
SWA_WINDOW = 128
N_BUCKETS = 32
MAX_DISTANCE = 128
CONV_WIDTH = 3
RMS_EPS = 1e-6
LOG2E = math.log2(math.e)
NEG = -1e30
FORCE_SCORE = 1e9
TAKEN = -3e38
SLC_TILE = 256
VMEM_LIMIT = 48 * 1024 * 1024

CB_AQ, CB_BQ, CB_CQ, CB_CK = 0, 4, 8, 12
CB_AKS, CB_AKW, CB_BK, CB_AVS = 16, 17, 18, 19
CB_CV, CB_AKC, CB_AVC, CB_AVW, CB_BV = 20, 24, 25, 26, 27
N_CB = 28
N_NORM_CB = 20
MM_CHUNK = 256
MM_ROWS = 512
N_GATE_COLS = N_HEADS * N_BRANCH


def _cparams(*sem):
    return pltpu.CompilerParams(dimension_semantics=sem, vmem_limit_bytes=VMEM_LIMIT)


def _split_hi_lo(v):
    hi = v.astype(BF16)
    return hi, (v - hi.astype(F32)).astype(BF16)


def _dot(a, b):
    return jnp.dot(a, b, preferred_element_type=F32)


def _dot_nt(a, b):
    return lax.dot_general(a, b, (((1,), (1,)), ((), ())), preferred_element_type=F32)


def _gelu_tanh(x):
    return x * (0.5 * (1.0 + jnp.tanh(math.sqrt(2.0 / math.pi) * (x + 0.044715 * (x * x * x)))))


def _head_rms(y, bd, gain):
    ms = _dot((y * y).astype(BF16), bd) * (1.0 / HEAD_DIM)
    return (y * lax.rsqrt(ms + RMS_EPS)) * gain


def _lane_half_mask(g):
    return (lax.broadcasted_iota(jnp.int32, (1, LANES), 1) // HEAD_DIM) == g


def _merge_group_outputs(o_ref, o_g0, o_g1):
    low = _lane_half_mask(0)
    tq = o_ref.shape[2]
    for p in range(N_QBLK):
        rows = slice(p * tq, (p + 1) * tq)
        o_ref[p, 0] = jnp.where(low, o_g0[rows], o_g1[rows]).astype(o_ref.dtype)


def _rmsnorm_kernel(x_ref, g_ref, o_ref):
    x = x_ref[...]
    ms = jnp.mean(x * x, axis=-1, keepdims=True)
    o_ref[...] = ((x * lax.rsqrt(ms + RMS_EPS)) * g_ref[...]).astype(o_ref.dtype)


def _rmsnorm(x2, g, tm=2048):
    bt, d = x2.shape
    tm = min(tm, bt)
    return pl.pallas_call(
        _rmsnorm_kernel,
        grid=(bt // tm,),
        in_specs=[pl.BlockSpec((tm, d), lambda i: (i, 0)), pl.BlockSpec((1, d), lambda i: (0, 0))],
        out_specs=pl.BlockSpec((tm, d), lambda i: (i, 0)),
        out_shape=jax.ShapeDtypeStruct((bt, d), BF16),
        compiler_params=_cparams("parallel"),
        name="rmsnorm",
    )(x2, g.reshape(1, d))


def _inproj_qkv_kernel(h_ref, w_ref, aux_ref, bd_ref, o_ref, *, n_norm_tiles):
    per_chunk = MM_CHUNK // LANES

    def emit(normalise):
        for r in range(0, h_ref.shape[0], MM_ROWS):
            rows = slice(r, r + MM_ROWS)
            y_all = _dot(h_ref[rows, :], w_ref[...])
            for c in range(w_ref.shape[1] // MM_CHUNK):
                cols = slice(c * MM_CHUNK, (c + 1) * MM_CHUNK)
                y = y_all[:, cols]
                if normalise:
                    normed = _head_rms(y, bd_ref[...], aux_ref[0:1, cols])
                    y = jnp.where(aux_ref[1:2, cols] > 0.5, normed, y)
                for s in range(per_chunk):
                    o_ref[c * per_chunk + s, rows, :] = y[:, s * LANES:(s + 1) * LANES].astype(o_ref.dtype)

    pl.when(pl.program_id(1) < n_norm_tiles)(lambda: emit(True))
    pl.when(pl.program_id(1) >= n_norm_tiles)(lambda: emit(False))


def _inproj_qkv(h, w, aux, bd, tm=2048, tn=512):
    bt, d = h.shape
    n = w.shape[1]
    n_sub = tn // LANES
    tm = min(tm, bt)
    assert N_NORM_CB % n_sub == 0 and tm % MM_ROWS == 0
    return pl.pallas_call(
        functools.partial(_inproj_qkv_kernel, n_norm_tiles=N_NORM_CB // n_sub),
        grid=(bt // tm, n // tn),
        in_specs=[pl.BlockSpec((tm, d), lambda i, j: (i, 0)),
                  pl.BlockSpec((d, tn), lambda i, j: (0, j)),
                  pl.BlockSpec((8, tn), lambda i, j: (0, j)),
                  pl.BlockSpec(bd.shape, lambda i, j: (0, 0))],
        out_specs=pl.BlockSpec((n_sub, tm, LANES), lambda i, j: (j, i, 0)),
        out_shape=jax.ShapeDtypeStruct((n // LANES, bt, LANES), BF16),
        compiler_params=_cparams("parallel", "arbitrary"),
        name="inproj_qkv",
    )(h, w, aux, bd)


def _inproj_gates_kernel(h_ref, w_ref, eg_ref, raw_ref, ga_ref):
    y = _dot(h_ref[...], w_ref[...])
    raw_ref[...] = y
    ga = _dot(jax.nn.sigmoid(y).astype(BF16), eg_ref[...])
    for n in range(N_BRANCH):
        ga_ref[n] = ga[:, n * MIX_WIDTH:(n + 1) * MIX_WIDTH].astype(ga_ref.dtype)


def _inproj_gates(h, w, eg, tm=1024):
    bt, d = h.shape
    return pl.pallas_call(
        _inproj_gates_kernel,
        grid=(bt // tm,),
        in_specs=[pl.BlockSpec((tm, d), lambda i: (i, 0)),
                  pl.BlockSpec((d, LANES), lambda i: (0, 0)),
                  pl.BlockSpec((LANES, N_BRANCH * MIX_WIDTH), lambda i: (0, 0))],
        out_specs=[pl.BlockSpec((tm, LANES), lambda i: (i, 0)),
                   pl.BlockSpec((N_BRANCH, tm, MIX_WIDTH), lambda i: (0, i, 0))],
        out_shape=[jax.ShapeDtypeStruct((bt, LANES), F32),
                   jax.ShapeDtypeStruct((N_BRANCH, bt, MIX_WIDTH), BF16)],
        compiler_params=_cparams("parallel"),
        name="inproj_gates",
    )(h, w, eg)


def _inproj_merge_kernel(h_ref, w_ref, o_ref):
    width = 2 * MM_CHUNK
    for c in range(w_ref.shape[1] // width):
        cols = slice(c * width, (c + 1) * width)
        o_ref[0, :, cols] = jax.nn.sigmoid(_dot(h_ref[...], w_ref[:, cols])).astype(o_ref.dtype)


def _inproj_merge(h, w, d_model, tm=1024, tn=1024):
    bt, d = h.shape
    n = w.shape[1]
    per = d_model // tn
    return pl.pallas_call(
        _inproj_merge_kernel,
        grid=(bt // tm, n // tn),
        in_specs=[pl.BlockSpec((tm, d), lambda i, j: (i, 0)),
                  pl.BlockSpec((d, tn), lambda i, j: (0, j))],
        out_specs=pl.BlockSpec((1, tm, tn), lambda i, j: (j // per, i, j % per)),
        out_shape=jax.ShapeDtypeStruct((n // d_model, bt, d_model), BF16),
        compiler_params=_cparams("parallel", "arbitrary"),
        name="inproj_merge",
    )(h, w)


def _compress_kernel(z_ref, pos_ref, w1_ref, w2_ref, aux_ref, bd_ref, o_ref, zf_scr):
    n_chunks = o_ref.shape[2]
    zf_scr[...] = z_ref[0, 0].astype(F32)
    halves = []
    for half in range(CMP_LEN // CMP_STRIDE):
        acc = jnp.zeros((n_chunks, KV_GROUPS * CMP_HIDDEN), F32)
        for l in range(CMP_STRIDE):
            rows = zf_scr[pl.ds(l, n_chunks, stride=CMP_STRIDE), :]
            pos = pos_ref[0, half * CMP_STRIDE + l:half * CMP_STRIDE + l + 1, :]
            acc = acc + _dot((rows + pos).astype(BF16), w1_ref[0, half * CMP_STRIDE + l])
        halves.append(acc)
    hid = _gelu_tanh(halves[0] + pltpu.roll(halves[1], n_chunks - 1, 0))
    y = _dot(hid.astype(BF16), w2_ref[0])
    normed = _head_rms(y, bd_ref[0:LANES, 0:LANES], aux_ref[0, 0:1, :])
    o_ref[0, 0] = jnp.where(aux_ref[0, 1:2, :] > 0.5, normed, y).astype(o_ref.dtype)


def _compress(qkv, pos, w1, w2, aux, bd):
    _, b, t, _ = qkv.shape
    n_chunks = t // CMP_STRIDE
    return pl.pallas_call(
        _compress_kernel,
        grid=(2, b),
        in_specs=[pl.BlockSpec((1, 1, t, LANES), lambda w, i: (CB_AKC + w, i, 0, 0)),
                  pl.BlockSpec((1,) + pos.shape[1:], lambda w, i: (w, 0, 0)),
                  pl.BlockSpec((1,) + w1.shape[1:], lambda w, i: (w, 0, 0, 0)),
                  pl.BlockSpec((1,) + w2.shape[1:], lambda w, i: (w, 0, 0)),
                  pl.BlockSpec((1, 8, LANES), lambda w, i: (w, 0, 0)),
                  pl.BlockSpec(bd.shape, lambda w, i: (0, 0))],
        out_specs=pl.BlockSpec((1, 1, n_chunks, LANES), lambda w, i: (w, i, 0, 0)),
        out_shape=jax.ShapeDtypeStruct((2, b, n_chunks, LANES), BF16),
        scratch_shapes=[pltpu.VMEM((t, LANES), F32)],
        compiler_params=_cparams("parallel", "parallel"),
        name="nsa_compress",
    )(qkv, pos, w1, w2, aux, bd)


CMP_TQ = 256


def _cmp_select_kernel(q_ref, kc_ref, vc_ref, bias_ref, ovt_ref, o_ref, qa_ref, *, n_slc, n_top):
    ncp = kc_ref.shape[2]
    rows = HEADS_PER_GROUP * Q_BLOCK
    jrow = lax.broadcasted_iota(jnp.int32, (n_slc, Q_BLOCK), 0)
    jrow_f = jrow.astype(F32)
    low = _lane_half_mask(0)
    for sub in range(q_ref.shape[2] // Q_BLOCK):
        rs = slice(sub * Q_BLOCK, (sub + 1) * Q_BLOCK)
        t0 = pl.program_id(1) * q_ref.shape[2] + sub * Q_BLOCK
        row_t = t0 + lax.broadcasted_iota(jnp.int32, (Q_BLOCK, 1), 0)
        row_valid = jnp.concatenate([row_t >= CMP_LEN - 1] * HEADS_PER_GROUP, axis=0)
        cur = (t0 + lax.broadcasted_iota(jnp.int32, (n_slc, Q_BLOCK), 1)) // SLC_LEN

        def force(imp):
            forced = jnp.where(jrow == cur, FORCE_SCORE, jnp.where(jrow == cur - 1, FORCE_SCORE, imp))
            return jnp.where(jrow == 0, FORCE_SCORE, forced)

        outs = []
        for g in range(KV_GROUPS):
            own_half = _lane_half_mask(g)
            hs = slice(g * HEADS_PER_GROUP, (g + 1) * HEADS_PER_GROUP)
            qop = jnp.concatenate([jnp.where(own_half, q_ref[p, 0, rs, :], jnp.zeros((Q_BLOCK, LANES), q_ref.dtype))
                                   for p in range(N_QBLK)], axis=0)
            s = _dot_nt(qop, kc_ref[0, 0]) + bias_ref[hs, rs, :].reshape(rows, ncp)
            e = jnp.exp2(s - jnp.max(s, axis=-1, keepdims=True))
            inv = jnp.where(row_valid, 1.0 / jnp.sum(e, axis=-1, keepdims=True), 0.0)
            p = e * inv
            outs.append(_dot(p.astype(BF16), vc_ref[0, 0]))
            psum = p[0:Q_BLOCK]
            for r in range(1, HEADS_PER_GROUP):
                psum = psum + p[r * Q_BLOCK:(r + 1) * Q_BLOCK]
            hi, lo = _split_hi_lo(psum)
            imp_t = (_dot_nt(ovt_ref[...], hi) + _dot_nt(ovt_ref[...], lo))[0:n_slc]
            score = jnp.where(jrow <= cur, force(imp_t), NEG)
            sel_bias = jnp.full((n_slc, Q_BLOCK), NEG, F32)
            for _ in range(n_top):
                best = jnp.max(score, axis=0, keepdims=True)
                first = jnp.min(jnp.where(score == best, jrow_f, float(n_slc)), axis=0, keepdims=True)
                hit = jrow_f == first
                sel_bias = jnp.where(hit, jnp.where(best > NEG / 2, 0.0, NEG), sel_bias)
                score = jnp.where(hit, TAKEN, score)
            if n_slc < HEAD_DIM:
                sel_bias = jnp.concatenate([sel_bias, jnp.full((HEAD_DIM - n_slc, Q_BLOCK), NEG, F32)], axis=0)
            sel_t = jnp.concatenate([sel_bias, sel_bias], axis=0).T.astype(qa_ref.dtype)
            for p in range(N_QBLK):
                qa_ref[g * HEADS_PER_GROUP + p, 0, rs, :] = jnp.where(own_half, q_ref[p, 0, rs, :], sel_t)
        for p in range(N_QBLK):
            prs = slice(p * Q_BLOCK, (p + 1) * Q_BLOCK)
            o_ref[p, 0, rs, :] = jnp.where(low, outs[0][prs], outs[1][prs]).astype(o_ref.dtype)


def _cmp_select(qkv, cmp_kv, bias_c, ovt, n_slc, tq=CMP_TQ):
    _, b, t, _ = qkv.shape
    ncp = cmp_kv.shape[2]
    assert n_slc <= HEAD_DIM
    return pl.pallas_call(
        functools.partial(_cmp_select_kernel, n_slc=n_slc, n_top=min(SLC_TOPK, n_slc)),
        grid=(b, t // tq),
        in_specs=[pl.BlockSpec((N_QBLK, 1, tq, LANES), lambda bi, i: (CB_AQ // N_QBLK, bi, i, 0)),
                  pl.BlockSpec((1, 1, ncp, LANES), lambda bi, i: (0, bi, 0, 0)),
                  pl.BlockSpec((1, 1, ncp, LANES), lambda bi, i: (1, bi, 0, 0)),
                  pl.BlockSpec((N_HEADS, tq, ncp), lambda bi, i: (0, i, 0)),
                  pl.BlockSpec((LANES, ncp), lambda bi, i: (0, 0))],
        out_specs=[pl.BlockSpec((N_QBLK, 1, tq, LANES), lambda bi, i: (0, bi, i, 0)),
                   pl.BlockSpec((N_HEADS, 1, tq, LANES), lambda bi, i: (0, bi, i, 0))],
        out_shape=[jax.ShapeDtypeStruct((N_QBLK, b, t, LANES), BF16),
                   jax.ShapeDtypeStruct((N_HEADS, b, t, LANES), BF16)],
        compiler_params=_cparams("parallel", "parallel"),
        name="nsa_cmp_select",
    )(qkv, cmp_kv, cmp_kv, bias_c, ovt)


def _flash_update(s, m, acc, v):
    m_new = jnp.maximum(m, jnp.max(s, axis=-1, keepdims=True))
    p = jnp.exp2(s - m_new)
    return m_new, jnp.exp2(m - m_new) * acc + _dot(p.astype(BF16), v)


def _normalise(acc):
    return acc / pltpu.roll(acc, HEAD_DIM, 1)


def _augment_keys_values(ka_scr, va_scr, k2, xk, v2, base=0):
    low = _lane_half_mask(0)
    ones = jnp.ones_like(v2)
    ka_scr[base] = jnp.where(low, k2, xk)
    ka_scr[base + 1] = jnp.where(low, xk, k2)
    va_scr[base] = jnp.where(low, v2, ones)
    va_scr[base + 1] = jnp.where(low, ones, v2)


def _store_all(scr, s):
    for n, sn in enumerate(s):
        scr[n] = sn


def _load_all(scr):
    return tuple(scr[n] for n in range(scr.shape[0]))


def _pipelined_sweep(logits, update, n_tiles, sa_scr, sb_scr, carry):
    def single(j, c):
        _store_all(sa_scr, logits(j))
        return update(j, _load_all(sa_scr), c)

    odd = n_tiles % 2
    carry = lax.fori_loop(0, odd, single, carry)
    _store_all(sa_scr, logits(odd))

    def pair(jp, c):
        j = odd + 2 * jp
        _store_all(sb_scr, logits(j + 1))
        c = update(j, _load_all(sa_scr), c)
        _store_all(sa_scr, logits(j + 2))
        return update(j + 1, _load_all(sb_scr), c)

    return lax.fori_loop(0, n_tiles // 2, pair, carry)


def _slc_kernel(qa_ref, k_ref, v_ref, hot_ref, nb_ref, o_ref, ka_scr, va_scr, sa_scr, sb_scr):
    i = pl.program_id(1)
    tq = qa_ref.shape[2]
    rows = HEADS_PER_GROUP * tq

    @pl.when(i == 0)
    def _():
        _augment_keys_values(ka_scr, va_scr, k_ref[0, 0], hot_ref[...], v_ref[0, 0])

    heads = [slice(g * HEADS_PER_GROUP, (g + 1) * HEADS_PER_GROUP) for g in range(KV_GROUPS)]
    qs = [qa_ref[hs, 0].reshape(rows, LANES) for hs in heads]

    def keys(j):
        return pl.ds(pl.multiple_of(j * tq, tq), tq)

    def logits(j):
        return tuple(_dot_nt(qs[g], ka_scr[g, keys(j), :]) for g in range(KV_GROUPS))

    def update(j, s, carry, which=None, pen=None):
        new = []
        for g in range(KV_GROUPS):
            sg = s[g]
            if which is not None:
                sg = (sg.reshape(HEADS_PER_GROUP, tq, tq) + nb_ref[which, heads[g]]).reshape(rows, tq)
            if pen is not None:
                sg = sg + pen
            new.append(_flash_update(sg, *carry[g], va_scr[g, keys(j), :]))
        return tuple(new)

    init = (jnp.full((rows, 1), NEG, F32), jnp.zeros((rows, LANES), F32))
    carry = _pipelined_sweep(logits, update, jnp.maximum(i - 1, 0), sa_scr, sb_scr, (init, init))
    _store_all(sb_scr, logits(i))
    carry = update(jnp.maximum(i - 1, 0), _load_all(sa_scr), carry, which=0, pen=jnp.where(i >= 1, 0.0, NEG))
    (_, acc0), (_, acc1) = update(i, _load_all(sb_scr), carry, which=1)
    _merge_group_outputs(o_ref, _normalise(acc0), _normalise(acc1))


def _slc_attention(qa, qkv, hot2, near_bias):
    _, b, t, _ = qa.shape
    tq = near_bias.shape[2]
    return pl.pallas_call(
        _slc_kernel,
        grid=(b, t // tq),
        in_specs=[pl.BlockSpec((N_HEADS, 1, tq, LANES), lambda bi, i: (0, bi, i, 0)),
                  pl.BlockSpec((1, 1, t, LANES), lambda bi, i: (CB_AKS, bi, 0, 0)),
                  pl.BlockSpec((1, 1, t, LANES), lambda bi, i: (CB_AVS, bi, 0, 0)),
                  pl.BlockSpec(hot2.shape, lambda bi, i: (0, 0)),
                  pl.BlockSpec(near_bias.shape, lambda bi, i: (0, 0, 0, 0))],
        out_specs=pl.BlockSpec((N_QBLK, 1, tq, LANES), lambda bi, i: (0, bi, i, 0)),
        out_shape=jax.ShapeDtypeStruct((N_QBLK, b, t, LANES), BF16),
        scratch_shapes=[pltpu.VMEM((KV_GROUPS, t, LANES), BF16), pltpu.VMEM((KV_GROUPS, t, LANES), BF16),
                        pltpu.VMEM((KV_GROUPS, HEADS_PER_GROUP * tq, tq), F32),
                        pltpu.VMEM((KV_GROUPS, HEADS_PER_GROUP * tq, tq), F32)],
        compiler_params=_cparams("parallel", "arbitrary"),
        name="nsa_slc_attention",
    )(qa, qkv, qkv, hot2, near_bias)


BAND_TQ = 256


def _banded_kernel(*refs, n_blk, use_sinks):
    q_ref = refs[0]
    k_refs = refs[1:1 + n_blk]
    v_refs = refs[1 + n_blk:1 + 2 * n_blk]
    bias_ref, o_ref = refs[1 + 2 * n_blk:]
    i = pl.program_id(1)
    tq = q_ref.shape[2]
    n_keys = n_blk * Q_BLOCK
    rows = HEADS_PER_GROUP * tq
    first_key = (i + 1) * tq - n_keys
    col = lax.broadcasted_iota(jnp.int32, (1, n_keys), 1)
    pad_pen = jnp.where(first_key + col >= 0, 0.0, NEG)
    kcat = jnp.concatenate([r[0, 0] for r in k_refs], axis=0)
    vcat = jnp.concatenate([r[0, 0] for r in v_refs], axis=0)
    if use_sinks:
        sink_row = lax.broadcasted_iota(jnp.int32, (n_keys, 1), 0) == 0
        kcat = jnp.where(sink_row, jnp.zeros_like(kcat), kcat)
        vcat = jnp.where(sink_row, jnp.zeros_like(vcat), vcat)
        pad_pen = jnp.where(col == 0, 0.0, pad_pen)
    outs = []
    for g in range(KV_GROUPS):
        own_half = _lane_half_mask(g)
        qop = jnp.concatenate([jnp.where(own_half, q_ref[p, 0], jnp.zeros_like(q_ref[p, 0]))
                               for p in range(N_QBLK)], axis=0)
        hs = slice(g * HEADS_PER_GROUP, (g + 1) * HEADS_PER_GROUP)
        s = _dot_nt(qop, kcat) + bias_ref[hs].reshape(rows, n_keys) + pad_pen
        e = jnp.exp2(s - jnp.max(s, axis=-1, keepdims=True))
        acc = _dot(e.astype(BF16), jnp.where(own_half, vcat, jnp.ones_like(vcat)))
        outs.append(_normalise(acc))
    _merge_group_outputs(o_ref, outs[0], outs[1])


def _banded_attention(qkv, cb_q, cb_k, cb_v, bias, use_sinks):
    _, b, t, _ = qkv.shape
    tq = bias.shape[1]
    n_blk = bias.shape[2] // Q_BLOCK
    per_tile = tq // Q_BLOCK

    def kv_spec(cb, back):
        return pl.BlockSpec((1, 1, Q_BLOCK, LANES),
                            lambda bi, i: (cb, bi, jnp.maximum((i + 1) * per_tile - 1 - back, 0), 0))

    backs = [n_blk - 1 - jb for jb in range(n_blk)]
    return pl.pallas_call(
        functools.partial(_banded_kernel, n_blk=n_blk, use_sinks=use_sinks),
        grid=(b, t // tq),
        in_specs=([pl.BlockSpec((N_QBLK, 1, tq, LANES), lambda bi, i: (cb_q // N_QBLK, bi, i, 0))]
                  + [kv_spec(cb_k, back) for back in backs] + [kv_spec(cb_v, back) for back in backs]
                  + [pl.BlockSpec(bias.shape, lambda bi, i: (0, 0, 0))]),
        out_specs=pl.BlockSpec((N_QBLK, 1, tq, LANES), lambda bi, i: (0, bi, i, 0)),
        out_shape=jax.ShapeDtypeStruct((N_QBLK, b, t, LANES), BF16),
        compiler_params=_cparams("parallel", "parallel"),
        name="banded_attention",
    )(qkv, *([qkv] * (2 * n_blk)), bias)


FOX_EXTRA = 6


def _fox_prep_kernel(g_ref, fb_ref, pq_ref, pk_ref, oq_ref, ok_ref, xq_ref, xk_ref, carry_scr):
    @pl.when(pl.program_id(1) == 0)
    def _():
        carry_scr[...] = jnp.zeros_like(carry_scr)

    x = g_ref[...] + fb_ref[...]
    logf = jnp.minimum(x, 0.0) - jnp.log(1.0 + jnp.exp(-jnp.abs(x)))
    ch = x.shape[0]
    lower = jnp.where(lax.broadcasted_iota(jnp.int32, (ch, ch), 0)
                      >= lax.broadcasted_iota(jnp.int32, (ch, ch), 1), 1.0, 0.0).astype(BF16)

    def split3(v):
        hi, rest = v.astype(BF16), v - v.astype(BF16).astype(F32)
        return (hi,) + _split_hi_lo(rest)

    cs = sum(_dot(lower, part) for part in split3(logf)) + carry_scr[0:1, :]
    carry_scr[0:1, :] = cs[ch - 1:ch, :]
    parts = split3(cs * LOG2E)
    xq = sum(_dot(part, pq_ref[n]) for n, part in enumerate(parts)) + oq_ref[...]
    xk = sum(_dot(part, pk_ref[n]) for n, part in enumerate(parts)) + ok_ref[...]
    for p in range(N_QBLK):
        xq_ref[p, 0] = xq[:, p * LANES:(p + 1) * LANES].astype(xq_ref.dtype)
        xk_ref[p, 0] = xk[:, p * LANES:(p + 1) * LANES].astype(xk_ref.dtype)


def _fox_prep(graw, fb_row, b, t, chunk=512):
    chunk = min(chunk, t)
    nch = t // chunk
    pq = np.zeros((3, LANES, N_QBLK * LANES), np.float32)
    pk = np.zeros((3, LANES, N_QBLK * LANES), np.float32)
    oq = np.zeros((1, N_QBLK * LANES), np.float32)
    ok = np.zeros((1, N_QBLK * LANES), np.float32)
    for h in range(N_HEADS):
        col = (h // 2) * LANES + (HEAD_DIM if h % 2 == 0 else 0)
        for n in range(3):
            pq[n, N_GATE_COLS + h, col + n] = 1.0
            pk[n, N_GATE_COLS + h, col + 3 + n] = -1.0
        oq[0, col + 3:col + FOX_EXTRA] = 1.0
        ok[0, col:col + 3] = 1.0
    const = lambda a: pl.BlockSpec(a.shape, lambda bi, c: (0,) * a.ndim)
    out_spec = pl.BlockSpec((N_QBLK, 1, chunk, LANES), lambda bi, c: (0, bi, c, 0))
    return pl.pallas_call(
        _fox_prep_kernel,
        grid=(b, nch),
        in_specs=[pl.BlockSpec((chunk, LANES), lambda bi, c: (bi * nch + c, 0)),
                  const(fb_row), const(pq), const(pk), const(oq), const(ok)],
        out_specs=[out_spec, out_spec],
        out_shape=[jax.ShapeDtypeStruct((N_QBLK, b, t, LANES), BF16)] * 2,
        scratch_shapes=[pltpu.VMEM((8, LANES), F32)],
        compiler_params=_cparams("parallel", "arbitrary"),
        name="fox_prep",
    )(graw, fb_row, jnp.asarray(pq, BF16), jnp.asarray(pk, BF16), jnp.asarray(oq), jnp.asarray(ok))


def _worklist_sweep(n_steps, logits, update, sa_scr, sb_scr):
    odd = n_steps % 2
    if odd:
        _store_all(sa_scr, logits(0))
        update(0, _load_all(sa_scr))
    if n_steps < 2:
        return
    _store_all(sa_scr, logits(odd))

    def pair(jp, _):
        n = odd + 2 * jp
        _store_all(sb_scr, logits(n + 1))
        update(n, _load_all(sa_scr))
        _store_all(sa_scr, logits(jnp.minimum(n + 2, n_steps - 1)))
        update(n + 1, _load_all(sb_scr))
        return 0

    lax.fori_loop(0, n_steps // 2, pair, 0)


def _fox_kernel(jt_ref, it_ref, q_ref, xq_ref, k_ref, xk_ref, v_ref, o_ref,
                qa_scr, ka_scr, va_scr, m_scr, acc_scr, sa_scr, sb_scr, *, tq):
    nq = q_ref.shape[2] // tq
    low = _lane_half_mask(0)
    _augment_keys_values(ka_scr, va_scr, k_ref[0, 0], xk_ref[0, 0], v_ref[0, 0])
    q2, xq = q_ref[0, 0], xq_ref[0, 0]
    qa_scr[0] = jnp.where(low, q2, xq)
    qa_scr[1] = jnp.where(low, xq, q2)
    causal_pen = jnp.where(lax.broadcasted_iota(jnp.int32, (tq, tq), 0)
                           >= lax.broadcasted_iota(jnp.int32, (tq, tq), 1), 0.0, NEG)

    def tile(n):
        return pl.ds(pl.multiple_of(n * tq, tq), tq)

    def logits(j, i):
        return tuple(_dot_nt(qa_scr[h, tile(i), :], ka_scr[h, tile(j), :]) for h in range(2))

    def diag_update(i, s):
        for h in range(2):
            sh = s[h] + causal_pen
            m = jnp.max(sh, axis=-1, keepdims=True)
            m_scr[i, h] = m
            acc_scr[i, h] = _dot(jnp.exp2(sh - m).astype(BF16), va_scr[h, tile(i), :])

    _worklist_sweep(nq, lambda n: logits(n, n), diag_update, sa_scr, sb_scr)

    def off_update(n, s):
        j, i = jt_ref[n], it_ref[n]
        for h in range(2):
            m_scr[i, h], acc_scr[i, h] = _flash_update(s[h], m_scr[i, h], acc_scr[i, h], va_scr[h, tile(j), :])

    _worklist_sweep(jt_ref.shape[0], lambda n: logits(jt_ref[n], it_ref[n]), off_update, sa_scr, sb_scr)

    def finish(i, _):
        o_ref[0, 0, tile(i), :] = jnp.where(low, _normalise(acc_scr[i, 0]), _normalise(acc_scr[i, 1])).astype(o_ref.dtype)
        return 0

    lax.fori_loop(0, nq, finish, 0)


def _fox_attention(qkv, xq, xk, tq):
    _, b, t, _ = qkv.shape
    nq = t // tq
    pairs = [(j, i) for j in range(nq) for i in range(j + 1, nq)]
    jt = jnp.asarray([p[0] for p in pairs], jnp.int32)
    it = jnp.asarray([p[1] for p in pairs], jnp.int32)
    smem = pl.BlockSpec(memory_space=pltpu.SMEM)
    seq = lambda cb: pl.BlockSpec((1, 1, t, LANES), lambda bi, p: (cb + p, bi, 0, 0))
    return pl.pallas_call(
        functools.partial(_fox_kernel, tq=tq),
        grid=(b, N_QBLK),
        in_specs=[smem, smem, seq(CB_CQ), seq(0), seq(CB_CK), seq(0), seq(CB_CV)],
        out_specs=pl.BlockSpec((1, 1, t, LANES), lambda bi, p: (p, bi, 0, 0)),
        out_shape=jax.ShapeDtypeStruct((N_QBLK, b, t, LANES), BF16),
        scratch_shapes=[pltpu.VMEM((2, t, LANES), BF16), pltpu.VMEM((2, t, LANES), BF16),
                        pltpu.VMEM((2, t, LANES), BF16),
                        pltpu.VMEM((nq, 2, tq, 1), F32), pltpu.VMEM((nq, 2, tq, LANES), F32),
                        pltpu.VMEM((2, tq, tq), F32), pltpu.VMEM((2, tq, tq), F32)],
        compiler_params=_cparams("parallel", "parallel"),
        name="fox_attention",
    )(jt, it, qkv, xq, qkv, xk, qkv)


def _merge_kernel(ocmp_ref, oslc_ref, owin_ref, ob_ref, oc_ref, ga_ref, mg_ref, wb_ref, wo_ref, x_ref, o_ref):
    def cat(ref):
        return jnp.concatenate([ref[p] for p in range(N_QBLK)], axis=-1)

    oa = (ga_ref[0].astype(F32) * cat(ocmp_ref).astype(F32)
          + ga_ref[1].astype(F32) * cat(oslc_ref).astype(F32)
          + ga_ref[2].astype(F32) * cat(owin_ref).astype(F32)).astype(BF16)
    branches = (oa, cat(ob_ref), cat(oc_ref))
    d = o_ref.shape[1]
    chunks = [slice(c, c + MM_CHUNK) for c in range(0, d, MM_CHUNK)]
    mix = jnp.concatenate(
        [sum(mg_ref[n, :, cols].astype(F32) * _dot(branches[n], wb_ref[n, :, cols]) for n in range(N_BRANCH))
         .astype(BF16) for cols in chunks], axis=-1)
    for cols in chunks:
        o_ref[:, cols] = x_ref[:, cols] + _dot(mix, wo_ref[:, cols])


def _merge(ocmp, oslc, owin, ob, oc, ga, mg, wb, wo, x2, tm=512):
    bt, d = x2.shape
    o_spec = pl.BlockSpec((N_QBLK, tm, LANES), lambda i: (0, i, 0))
    return pl.pallas_call(
        _merge_kernel,
        grid=(bt // tm,),
        in_specs=[o_spec] * 5 + [pl.BlockSpec((N_BRANCH, tm, MIX_WIDTH), lambda i: (0, i, 0)),
                                 pl.BlockSpec((N_BRANCH, tm, d), lambda i: (0, i, 0)),
                                 pl.BlockSpec(wb.shape, lambda i: (0, 0, 0)),
                                 pl.BlockSpec(wo.shape, lambda i: (0, 0)),
                                 pl.BlockSpec((tm, d), lambda i: (i, 0))],
        out_specs=pl.BlockSpec((tm, d), lambda i: (i, 0)),
        out_shape=jax.ShapeDtypeStruct((bt, d), F32),
        compiler_params=_cparams("parallel"),
        name="branch_merge",
    )(ocmp, oslc, owin, ob, oc, ga, mg, wb, wo, x2)


FFN_TF = 512


def _ffn_kernel(x_ref, xh_ref, g_ref, wg_ref, wu_ref, cw_ref, cb_ref, wd_ref, o_ref,
                h_scr, hh_scr, a_scr, acc_scr, *, tm, tiles_per_seq):
    i = pl.program_id(0)
    f = pl.program_id(1)

    def norm(x):
        ms = jnp.mean(x * x, axis=-1, keepdims=True)
        return ((x * lax.rsqrt(ms + RMS_EPS)) * g_ref[...]).astype(BF16)

    @pl.when(f == 0)
    def _():
        h_scr[...] = norm(x_ref[...])
        hh_scr[...] = norm(xh_ref[...])
        acc_scr[...] = jnp.zeros_like(acc_scr)

    a_scr[0:8, :] = _dot(hh_scr[...], wg_ref[...]) * jnp.where(i % tiles_per_seq == 0, 0.0, 1.0)
    acts = []
    for c in range(wg_ref.shape[1] // MM_CHUNK):
        cols = slice(c * MM_CHUNK, (c + 1) * MM_CHUNK)
        a = _dot(h_scr[...], wg_ref[:, cols])
        a_scr[8:8 + tm, cols] = a
        conv = (cw_ref[0:1, cols] * a_scr[6:6 + tm, cols] + cw_ref[1:2, cols] * a_scr[7:7 + tm, cols]
                + cw_ref[2:3, cols] * a + cb_ref[:, cols])
        acts.append((_gelu_tanh(conv) * _dot(h_scr[...], wu_ref[:, cols])).astype(BF16))
    acc_scr[...] += _dot(jnp.concatenate(acts, axis=-1), wd_ref[...])

    @pl.when(f == pl.num_programs(1) - 1)
    def _():
        o_ref[...] = x_ref[...] + acc_scr[...]


def _ffn(x2, g, wg, wu, cw, cb, wd, t, tm=1024, tf=FFN_TF):
    bt, d = x2.shape
    pad = -wg.shape[1] % tf
    wg, wu, cw = [jnp.pad(a, ((0, 0), (0, pad))) for a in (wg, wu, cw)]
    cb, wd = jnp.pad(cb, (0, pad)), jnp.pad(wd, ((0, pad), (0, 0)))
    ff = wg.shape[1]
    halo_blocks = tm // 8
    return pl.pallas_call(
        functools.partial(_ffn_kernel, tm=tm, tiles_per_seq=t // tm),
        grid=(bt // tm, ff // tf),
        in_specs=[pl.BlockSpec((tm, d), lambda i, f: (i, 0)),
                  pl.BlockSpec((8, d), lambda i, f: (jnp.maximum(i * halo_blocks - 1, 0), 0)),
                  pl.BlockSpec((1, d), lambda i, f: (0, 0)),
                  pl.BlockSpec((d, tf), lambda i, f: (0, f)),
                  pl.BlockSpec((d, tf), lambda i, f: (0, f)),
                  pl.BlockSpec((CONV_WIDTH, tf), lambda i, f: (0, f)),
                  pl.BlockSpec((1, tf), lambda i, f: (0, f)),
                  pl.BlockSpec((tf, d), lambda i, f: (f, 0))],
        out_specs=pl.BlockSpec((tm, d), lambda i, f: (i, 0)),
        out_shape=jax.ShapeDtypeStruct((bt, d), F32),
        scratch_shapes=[pltpu.VMEM((tm, d), BF16), pltpu.VMEM((8, d), BF16),
                        pltpu.VMEM((tm + 8, tf), F32), pltpu.VMEM((tm, d), F32)],
        compiler_params=_cparams("parallel", "arbitrary"),
        name="conv_ffn",
    )(x2, x2, g.reshape(1, d), wg, wu, cw, cb.reshape(1, ff), wd)


def _t5_bucket_np(dist):
    max_exact = N_BUCKETS // 2
    d = np.maximum(dist, 0)
    ratio = np.log(np.maximum(d, 1) / max_exact) / math.log(MAX_DISTANCE / max_exact)
    large = np.minimum(max_exact + (ratio * (N_BUCKETS - max_exact)).astype(np.int64), N_BUCKETS - 1)
    return np.where(d < max_exact, d, large)


def _expand_kernel(tab_ref, idx_ref, o_ref):
    tab = tab_ref[...]
    onehot = jnp.where(lax.broadcasted_iota(jnp.int32, (tab.shape[1], idx_ref.shape[1]), 0) == idx_ref[...],
                       1.0, 0.0).astype(BF16)
    hi, rest = tab.astype(BF16), tab - tab.astype(BF16).astype(F32)
    mid, lo = _split_hi_lo(rest)
    o_ref[...] = _dot(hi, onehot) + _dot(mid, onehot) + _dot(lo, onehot)


def _bias_from_dist(tab_t, dist, valid, chunk=4096):
    heads = tab_t.shape[0]
    tab = jnp.zeros((heads, LANES), F32).at[:, :N_BUCKETS].set(tab_t).at[:, N_BUCKETS].set(NEG)
    idx = np.where(valid, _t5_bucket_np(dist), N_BUCKETS).reshape(1, -1).astype(np.int32)
    n = idx.shape[1]
    chunk = min(chunk, n)
    out = pl.pallas_call(
        _expand_kernel,
        grid=(n // chunk,),
        in_specs=[pl.BlockSpec((heads, LANES), lambda i: (0, 0)), pl.BlockSpec((1, chunk), lambda i: (0, i))],
        out_specs=pl.BlockSpec((heads, chunk), lambda i: (0, i)),
        out_shape=jax.ShapeDtypeStruct((heads, n), F32),
        compiler_params=_cparams("parallel"),
        name="rel_bias_expand",
    )(tab, jnp.asarray(idx))
    return out.reshape((heads,) + dist.shape)


def _cmp_bias_kernel(g_ref, o_ref):
    i = pl.program_id(0)
    ncp = o_ref.shape[2]
    for h in range(o_ref.shape[0]):
        o_ref[h] = pltpu.roll(g_ref[h], i * (Q_BLOCK // CMP_STRIDE), 1)[:, :ncp]


def _cmp_bias(g_tab, t):
    heads, _, width = g_tab.shape
    ncp = t // CMP_STRIDE
    return pl.pallas_call(
        _cmp_bias_kernel,
        grid=(t // Q_BLOCK,),
        in_specs=[pl.BlockSpec(g_tab.shape, lambda i: (0, 0, 0))],
        out_specs=pl.BlockSpec((heads, Q_BLOCK, ncp), lambda i: (0, i, 0)),
        out_shape=jax.ShapeDtypeStruct((heads, t, ncp), F32),
        compiler_params=_cparams("parallel"),
        name="cmp_bias_build",
    )(g_tab)


def _static_tables(t):
    ncp = t // CMP_STRIDE
    n_cmp = (t - CMP_LEN) // CMP_STRIDE + 1
    n_slc = t // SLC_LEN
    lane = np.arange(MM_CHUNK)
    bd = (lane[:, None] // HEAD_DIM == lane[None, :] // HEAD_DIM).astype(np.float32)
    eg = np.zeros((LANES, N_BRANCH * MIX_WIDTH), np.float32)
    for h in range(N_HEADS):
        base = (h % N_QBLK) * LANES + (h // N_QBLK) * HEAD_DIM
        for n in range(N_BRANCH):
            eg[h * N_BRANCH + n, n * MIX_WIDTH + base:n * MIX_WIDTH + base + HEAD_DIM] = 1.0
    c_start = np.arange(n_cmp) * CMP_STRIDE
    s_start = np.arange(n_slc) * SLC_LEN
    overlap = ((c_start[:, None] < s_start[None, :] + SLC_LEN) & (c_start[:, None] + CMP_LEN > s_start[None, :]))
    ovt = np.zeros((LANES, ncp), np.float32)
    ovt[:n_slc, :n_cmp] = overlap.T
    key_block = np.zeros((t, LANES), np.float32)
    key_block[np.arange(t), np.arange(t) // SLC_LEN] = 1.0
    key_block[np.arange(t), HEAD_DIM + np.arange(t) // SLC_LEN] = 1.0
    as_bf16 = lambda a: jnp.asarray(a, BF16)
    return dict(bd=as_bf16(bd), eg=as_bf16(eg), ovt=as_bf16(ovt), key_block=as_bf16(key_block), n_slc=n_slc)


def _bias_tables(rel_bias, t):
    n_cmp = (t - CMP_LEN) // CMP_STRIDE + 1
    ncp = t // CMP_STRIDE
    tab_a = rel_bias[:, :N_HEADS].T * LOG2E
    tab_b = rel_bias[:, N_HEADS:].T * LOG2E
    r = np.arange(Q_BLOCK)[:, None]

    def band(window):
        n_prev = -(-(window - 1) // Q_BLOCK)
        n_keys = n_prev * Q_BLOCK + BAND_TQ
        dist = (n_keys - BAND_TQ) + np.arange(BAND_TQ)[:, None] - np.arange(n_keys)[None, :]
        valid = (dist >= 0) & (dist < window)
        assert not valid[:, 0].any()
        return dist, valid

    m = np.arange(2 * ncp)
    m = np.where(m < ncp, m, m - 2 * ncp)[None, :]
    dist_c = r - (m * CMP_STRIDE + CMP_LEN - 1)
    assert n_cmp == ncp - 1 and (dist_c[:, m[0] == Q_BLOCK // CMP_STRIDE - 1] < 0).all()
    tile = np.arange(SLC_TILE)
    dist_n = np.stack([SLC_TILE * (1 - which) + tile[:, None] - tile[None, :] for which in range(2)])
    near = _bias_from_dist(tab_a - tab_a[:, N_BUCKETS - 1:], dist_n, dist_n >= 0)
    return dict(win=_bias_from_dist(tab_a, *band(NSA_WINDOW)),
                swa=_bias_from_dist(tab_b, *band(SWA_WINDOW)),
                cmp=_cmp_bias(_bias_from_dist(tab_a, dist_c, dist_c >= 0), t),
                near=jnp.transpose(near, (1, 0, 2, 3)))


def _pair_cols(w):
    parts = []
    for p in range(N_QBLK):
        parts += [w[..., p * HEAD_DIM:(p + 1) * HEAD_DIM],
                  w[..., (N_QBLK + p) * HEAD_DIM:(N_QBLK + p + 1) * HEAD_DIM]]
    return jnp.concatenate(parts, axis=-1)


def _prep_layer(w_in, qk_gain, w_branch):
    kv = KV_GROUPS * HEAD_DIM
    widths = [('a_q', MIX_WIDTH), ('a_kc', kv), ('a_vc', kv), ('a_ks', kv), ('a_vs', kv), ('a_kw', kv), ('a_vw', kv),
              ('a_gate', N_GATE_COLS), ('b_q', MIX_WIDTH), ('b_k', kv), ('b_v', kv),
              ('c_q', MIX_WIDTH), ('c_k', MIX_WIDTH), ('c_v', MIX_WIDTH), ('c_f', N_HEADS),
              ('merge', N_BRANCH * w_in.shape[0])]
    cols, off = {}, 0
    for name, w in widths:
        cols[name] = w_in[:, off:off + w]
        off += w
    scale = HEAD_DIM ** -0.5 * LOG2E
    tile = lambda gvec, n: jnp.tile(gvec, n)
    zeros = lambda n: jnp.zeros((n,), F32)
    ones = lambda n: jnp.ones((n,), F32)
    pieces = [(_pair_cols(cols['a_q']), tile(qk_gain[0] * scale, N_HEADS), ones(MIX_WIDTH)),
              (_pair_cols(cols['b_q']), tile(qk_gain[2] * scale, N_HEADS), ones(MIX_WIDTH)),
              (cols['c_q'], tile(qk_gain[4] * scale, N_HEADS), ones(MIX_WIDTH)),
              (cols['c_k'], tile(qk_gain[5], N_HEADS), ones(MIX_WIDTH)),
              (cols['a_ks'], tile(qk_gain[1], KV_GROUPS), ones(kv)),
              (cols['a_kw'], tile(qk_gain[1], KV_GROUPS), ones(kv)),
              (cols['b_k'], tile(qk_gain[3], KV_GROUPS), ones(kv)),
              (cols['a_vs'], zeros(kv), zeros(kv)),
              (cols['c_v'], zeros(MIX_WIDTH), zeros(MIX_WIDTH)),
              (cols['a_kc'], zeros(kv), zeros(kv)), (cols['a_vc'], zeros(kv), zeros(kv)),
              (cols['a_vw'], zeros(kv), zeros(kv)), (cols['b_v'], zeros(kv), zeros(kv))]
    w_qkv = jnp.concatenate([p[0] for p in pieces], axis=1).astype(BF16)
    n_qkv = w_qkv.shape[1]
    aux = jnp.zeros((8, n_qkv), F32)
    aux = aux.at[0].set(jnp.concatenate([p[1] for p in pieces])).at[1].set(jnp.concatenate([p[2] for p in pieces]))
    d = w_in.shape[0]
    w_gates = jnp.concatenate([cols['a_gate'], cols['c_f'],
                               jnp.zeros((d, LANES - N_GATE_COLS - N_HEADS), F32)], axis=1).astype(BF16)
    wb = jnp.stack([_pair_cols(w_branch[0].T).T, _pair_cols(w_branch[1].T).T, w_branch[2]]).astype(BF16)
    return w_qkv, aux, w_gates, cols['merge'].astype(BF16), wb


def _prep_compress(cmp_pos, cmp_w1, cmp_w2, k_gain):
    def both_groups(w):
        z = jnp.zeros_like(w)
        return jnp.concatenate([jnp.concatenate([w, z], -1), jnp.concatenate([z, w], -1)], -2)

    pos = jnp.concatenate([cmp_pos, cmp_pos], -1)
    w1 = both_groups(cmp_w1.reshape(2, CMP_LEN, HEAD_DIM, CMP_HIDDEN)).astype(BF16)
    w2 = both_groups(cmp_w2).astype(BF16)
    aux = jnp.zeros((2, 8, LANES), F32)
    aux = aux.at[0, 0].set(jnp.tile(k_gain, KV_GROUPS)).at[0, 1].set(1.0)
    return pos, w1, w2, aux


def kernel(x, rel_bias, norm_mix, norm_ffn, w_in, forget_bias, qk_gain, cmp_pos, cmp_w1, cmp_w2, sinks, w_branch, w_out, w_gate, w_up, conv_w, conv_b, w_down):
    b, t, d = x.shape
    bt = b * t
    depth = w_in.shape[0]
    tabs = _static_tables(t)
    biases = _bias_tables(rel_bias, t)
    fox_tq = min(512, t)
    x2 = x.reshape(bt, d)
    for l in range(depth):
        w_qkv, aux, w_gates, w_merge, wb = _prep_layer(w_in[l], qk_gain[l], w_branch[l])
        h = _rmsnorm(x2, norm_mix[l])
        qkv = _inproj_qkv(h, w_qkv, aux, tabs['bd']).reshape(N_CB, b, t, LANES)
        graw, ga = _inproj_gates(h, w_gates, tabs['eg'])
        mg = _inproj_merge(h, w_merge, d)

        cmp_kv = _compress(qkv, *_prep_compress(cmp_pos[l], cmp_w1[l], cmp_w2[l], qk_gain[l, 1]), tabs['bd'])
        o_cmp, qa = _cmp_select(qkv, cmp_kv, biases['cmp'], tabs['ovt'], tabs['n_slc'])
        o_slc = _slc_attention(qa, qkv, tabs['key_block'], biases['near'])
        o_win = _banded_attention(qkv, CB_AQ, CB_AKW, CB_AVW, biases['win'], False)

        swa_bias = biases['swa'].at[:, :, 0].set(sinks[l][:, None] * LOG2E)
        o_b = _banded_attention(qkv, CB_BQ, CB_BK, CB_BV, swa_bias, True)

        fb_row = jnp.zeros((1, LANES), F32).at[0, N_GATE_COLS:N_GATE_COLS + N_HEADS].set(forget_bias[l])
        o_c = _fox_attention(qkv, *_fox_prep(graw, fb_row, b, t), fox_tq)

        flat = lambda o: o.reshape(N_QBLK, bt, LANES)
        x2 = _merge(flat(o_cmp), flat(o_slc), flat(o_win), flat(o_b), flat(o_c), ga, mg, wb,
                    w_out[l].astype(BF16), x2)
        x2 = _ffn(x2, norm_ffn[l], w_gate[l].astype(BF16), w_up[l].astype(BF16), conv_w[l], conv_b[l],
                  w_down[l].astype(BF16), t)
    return x2.reshape(b, t, d)
```

```python
import functools
import math

import numpy as np
import jax
import jax.numpy as jnp
from jax import lax
from jax.experimental import pallas as pl
from jax.experimental.pallas import tpu as pltpu

F32 = jnp.float32
BF16 = jnp.bfloat16

HEAD_DIM = 64
LANES = 128
N_HEADS = 8
KV_GROUPS = 2
HEADS_PER_GROUP = N_HEADS // KV_GROUPS
N_QBLK = N_HEADS * HEAD_DIM // LANES
MIX_WIDTH = N_HEADS * HEAD_DIM
N_BRANCH = 3
Q_BLOCK = 128
CMP_LEN = 32
CMP_STRIDE = 16
CMP_HIDDEN = 256
SLC_LEN = 64
SLC_TOPK = 8
NSA_WINDOW = 256
SWA_WINDOW = 128
N_BUCKETS = 32
MAX_DISTANCE = 128
CONV_WIDTH = 3
RMS_EPS = 1e-6
LOG2E = math.log2(math.e)
NEG = -1e30
FORCE_SCORE = 1e9
TAKEN = -3e38
SLC_TILE = 256
VMEM_LIMIT = 48 * 1024 * 1024

CB_AQ, CB_BQ, CB_CQ, CB_CK = 0, 4, 8, 12
CB_AKS, CB_AKW, CB_BK, CB_AVS = 16, 17, 18, 19
CB_CV, CB_AKC, CB_AVC, CB_AVW, CB_BV = 20, 24, 25, 26, 27
N_CB = 28
N_NORM_CB = 20
MM_CHUNK = 256
MM_ROWS = 512
N_GATE_COLS = N_HEADS * N_BRANCH


def _cparams(*sem):
    return pltpu.CompilerParams(dimension_semantics=sem, vmem_limit_bytes=VMEM_LIMIT)


def _split_hi_lo(v):
    hi = v.astype(BF16)
    return hi, (v - hi.astype(F32)).astype(BF16)


def _dot(a, b):
    return jnp.dot(a, b, preferred_element_type=F32)


def _dot_nt(a, b):
    return lax.dot_general(a, b, (((1,), (1,)), ((), ())), preferred_element_type=F32)


def _gelu_tanh(x):
    return x * (0.5 * (1.0 + jnp.tanh(math.sqrt(2.0 / math.pi) * (x + 0.044715 * (x * x * x)))))


def _head_rms(y, bd, gain):
    ms = _dot((y * y).astype(BF16), bd) * (1.0 / HEAD_DIM)
    return (y * lax.rsqrt(ms + RMS_EPS)) * gain


def _lane_half_mask(g):
    return (lax.broadcasted_iota(jnp.int32, (1, LANES), 1) // HEAD_DIM) == g


def _merge_group_outputs(o_ref, o_g0, o_g1):
    low = _lane_half_mask(0)
    tq = o_ref.shape[2]
    for p in range(N_QBLK):
        rows = slice(p * tq, (p + 1) * tq)
        o_ref[p, 0] = jnp.where(low, o_g0[rows], o_g1[rows]).astype(o_ref.dtype)


def _rmsnorm_kernel(x_ref, g_ref, o_ref):
    x = x_ref[...]
    ms = jnp.mean(x * x, axis=-1, keepdims=True)
    o_ref[...] = ((x * lax.rsqrt(ms + RMS_EPS)) * g_ref[...]).astype(o_ref.dtype)


def _rmsnorm(x2, g, tm=2048):
    bt, d = x2.shape
    tm = min(tm, bt)
    return pl.pallas_call(
        _rmsnorm_kernel,
        grid=(bt // tm,),
        in_specs=[pl.BlockSpec((tm, d), lambda i: (i, 0)), pl.BlockSpec((1, d), lambda i: (0, 0))],
        out_specs=pl.BlockSpec((tm, d), lambda i: (i, 0)),
        out_shape=jax.ShapeDtypeStruct((bt, d), BF16),
        compiler_params=_cparams("parallel"),
        name="rmsnorm",
    )(x2, g.reshape(1, d))


def _inproj_qkv_kernel(h_ref, w_ref, aux_ref, bd_ref, o_ref, *, n_norm_tiles):
    per_chunk = MM_CHUNK // LANES

    def emit(normalise):
        for r in range(0, h_ref.shape[0], MM_ROWS):
            rows = slice(r, r + MM_ROWS)
            y_all = _dot(h_ref[rows, :], w_ref[...])
            for c in range(w_ref.shape[1] // MM_CHUNK):
                cols = slice(c * MM_CHUNK, (c + 1) * MM_CHUNK)
                y = y_all[:, cols]
                if normalise:
                    normed = _head_rms(y, bd_ref[...], aux_ref[0:1, cols])
                    y = jnp.where(aux_ref[1:2, cols] > 0.5, normed, y)
                for s in range(per_chunk):
                    o_ref[c * per_chunk + s, rows, :] = y[:, s * LANES:(s + 1) * LANES].astype(o_ref.dtype)

    pl.when(pl.program_id(1) < n_norm_tiles)(lambda: emit(True))
    pl.when(pl.program_id(1) >= n_norm_tiles)(lambda: emit(False))


def _inproj_qkv(h, w, aux, bd, tm=2048, tn=512):
    bt, d = h.shape
    n = w.shape[1]
    n_sub = tn // LANES
    tm = min(tm, bt)
    assert N_NORM_CB % n_sub == 0 and tm % MM_ROWS == 0
    return pl.pallas_call(
        functools.partial(_inproj_qkv_kernel, n_norm_tiles=N_NORM_CB // n_sub),
        grid=(bt // tm, n // tn),
        in_specs=[pl.BlockSpec((tm, d), lambda i, j: (i, 0)),
                  pl.BlockSpec((d, tn), lambda i, j: (0, j)),
                  pl.BlockSpec((8, tn), lambda i, j: (0, j)),
                  pl.BlockSpec(bd.shape, lambda i, j: (0, 0))],
        out_specs=pl.BlockSpec((n_sub, tm, LANES), lambda i, j: (j, i, 0)),
        out_shape=jax.ShapeDtypeStruct((n // LANES, bt, LANES), BF16),
        compiler_params=_cparams("parallel", "arbitrary"),
        name="inproj_qkv",
    )(h, w, aux, bd)


def _inproj_gates_kernel(h_ref, w_ref, eg_ref, raw_ref, ga_ref):
    y = _dot(h_ref[...], w_ref[...])
    raw_ref[...] = y
    ga = _dot(jax.nn.sigmoid(y).astype(BF16), eg_ref[...])
    for n in range(N_BRANCH):
        ga_ref[n] = ga[:, n * MIX_WIDTH:(n + 1) * MIX_WIDTH].astype(ga_ref.dtype)


def _inproj_gates(h, w, eg, tm=1024):
    bt, d = h.shape
    return pl.pallas_call(
        _inproj_gates_kernel,
        grid=(bt // tm,),
        in_specs=[pl.BlockSpec((tm, d), lambda i: (i, 0)),
                  pl.BlockSpec((d, LANES), lambda i: (0, 0)),
                  pl.BlockSpec((LANES, N_BRANCH * MIX_WIDTH), lambda i: (0, 0))],
        out_specs=[pl.BlockSpec((tm, LANES), lambda i: (i, 0)),
                   pl.BlockSpec((N_BRANCH, tm, MIX_WIDTH), lambda i: (0, i, 0))],
        out_shape=[jax.ShapeDtypeStruct((bt, LANES), F32),
                   jax.ShapeDtypeStruct((N_BRANCH, bt, MIX_WIDTH), BF16)],
        compiler_params=_cparams("parallel"),
        name="inproj_gates",
    )(h, w, eg)


def _inproj_merge_kernel(h_ref, w_ref, o_ref):
    width = 2 * MM_CHUNK
    for c in range(w_ref.shape[1] // width):
        cols = slice(c * width, (c + 1) * width)
        o_ref[0, :, cols] = jax.nn.sigmoid(_dot(h_ref[...], w_ref[:, cols])).astype(o_ref.dtype)


def _inproj_merge(h, w, d_model, tm=1024, tn=1024):
    bt, d = h.shape
    n = w.shape[1]
    per = d_model // tn
    return pl.pallas_call(
        _inproj_merge_kernel,
        grid=(bt // tm, n // tn),
        in_specs=[pl.BlockSpec((tm, d), lambda i, j: (i, 0)),
                  pl.BlockSpec((d, tn), lambda i, j: (0, j))],
        out_specs=pl.BlockSpec((1, tm, tn), lambda i, j: (j // per, i, j % per)),
        out_shape=jax.ShapeDtypeStruct((n // d_model, bt, d_model), BF16),
        compiler_params=_cparams("parallel", "arbitrary"),
        name="inproj_merge",
    )(h, w)


def _compress_kernel(z_ref, pos_ref, w1_ref, w2_ref, aux_ref, bd_ref, o_ref, zf_scr):
    n_chunks = o_ref.shape[2]
    zf_scr[...] = z_ref[0, 0].astype(F32)
    halves = []
    for half in range(CMP_LEN // CMP_STRIDE):
        acc = jnp.zeros((n_chunks, KV_GROUPS * CMP_HIDDEN), F32)
        for l in range(CMP_STRIDE):
            rows = zf_scr[pl.ds(l, n_chunks, stride=CMP_STRIDE), :]
            pos = pos_ref[0, half * CMP_STRIDE + l:half * CMP_STRIDE + l + 1, :]
            acc = acc + _dot((rows + pos).astype(BF16), w1_ref[0, half * CMP_STRIDE + l])
        halves.append(acc)
    hid = _gelu_tanh(halves[0] + pltpu.roll(halves[1], n_chunks - 1, 0))
    y = _dot(hid.astype(BF16), w2_ref[0])
    normed = _head_rms(y, bd_ref[0:LANES, 0:LANES], aux_ref[0, 0:1, :])
    o_ref[0, 0] = jnp.where(aux_ref[0, 1:2, :] > 0.5, normed, y).astype(o_ref.dtype)


def _compress(qkv, pos, w1, w2, aux, bd):
    _, b, t, _ = qkv.shape
    n_chunks = t // CMP_STRIDE
    return pl.pallas_call(
        _compress_kernel,
        grid=(2, b),
        in_specs=[pl.BlockSpec((1, 1, t, LANES), lambda w, i: (CB_AKC + w, i, 0, 0)),
                  pl.BlockSpec((1,) + pos.shape[1:], lambda w, i: (w, 0, 0)),
                  pl.BlockSpec((1,) + w1.shape[1:], lambda w, i: (w, 0, 0, 0)),
                  pl.BlockSpec((1,) + w2.shape[1:], lambda w, i: (w, 0, 0)),
                  pl.BlockSpec((1, 8, LANES), lambda w, i: (w, 0, 0)),
                  pl.BlockSpec(bd.shape, lambda w, i: (0, 0))],
        out_specs=pl.BlockSpec((1, 1, n_chunks, LANES), lambda w, i: (w, i, 0, 0)),
        out_shape=jax.ShapeDtypeStruct((2, b, n_chunks, LANES), BF16),
        scratch_shapes=[pltpu.VMEM((t, LANES), F32)],
        compiler_params=_cparams("parallel", "parallel"),
        name="nsa_compress",
    )(qkv, pos, w1, w2, aux, bd)


CMP_TQ = 256


def _cmp_select_kernel(q_ref, kc_ref, vc_ref, bias_ref, ovt_ref, o_ref, qa_ref, *, n_slc, n_top):
    ncp = kc_ref.shape[2]
    rows = HEADS_PER_GROUP * Q_BLOCK
    jrow = lax.broadcasted_iota(jnp.int32, (n_slc, Q_BLOCK), 0)
    jrow_f = jrow.astype(F32)
    low = _lane_half_mask(0)
    for sub in range(q_ref.shape[2] // Q_BLOCK):
        rs = slice(sub * Q_BLOCK, (sub + 1) * Q_BLOCK)
        t0 = pl.program_id(1) * q_ref.shape[2] + sub * Q_BLOCK
        row_t = t0 + lax.broadcasted_iota(jnp.int32, (Q_BLOCK, 1), 0)
        row_valid = jnp.concatenate([row_t >= CMP_LEN - 1] * HEADS_PER_GROUP, axis=0)
        cur = (t0 + lax.broadcasted_iota(jnp.int32, (n_slc, Q_BLOCK), 1)) // SLC_LEN

        def force(imp):
            forced = jnp.where(jrow == cur, FORCE_SCORE, jnp.where(jrow == cur - 1, FORCE_SCORE, imp))
            return jnp.where(jrow == 0, FORCE_SCORE, forced)

        outs = []
        for g in range(KV_GROUPS):
            own_half = _lane_half_mask(g)
            hs = slice(g * HEADS_PER_GROUP, (g + 1) * HEADS_PER_GROUP)
            qop = jnp.concatenate([jnp.where(own_half, q_ref[p, 0, rs, :], jnp.zeros((Q_BLOCK, LANES), q_ref.dtype))
                                   for p in range(N_QBLK)], axis=0)
            s = _dot_nt(qop, kc_ref[0, 0]) + bias_ref[hs, rs, :].reshape(rows, ncp)
            e = jnp.exp2(s - jnp.max(s, axis=-1, keepdims=True))
            inv = jnp.where(row_valid, 1.0 / jnp.sum(e, axis=-1, keepdims=True), 0.0)
            p = e * inv
            outs.append(_dot(p.astype(BF16), vc_ref[0, 0]))
            psum = p[0:Q_BLOCK]
            for r in range(1, HEADS_PER_GROUP):
                psum = psum + p[r * Q_BLOCK:(r + 1) * Q_BLOCK]
            hi, lo = _split_hi_lo(psum)
            imp_t = (_dot_nt(ovt_ref[...], hi) + _dot_nt(ovt_ref[...], lo))[0:n_slc]
            score = jnp.where(jrow <= cur, force(imp_t), NEG)
            sel_bias = jnp.full((n_slc, Q_BLOCK), NEG, F32)
            for _ in range(n_top):
                best = jnp.max(score, axis=0, keepdims=True)
                first = jnp.min(jnp.where(score == best, jrow_f, float(n_slc)), axis=0, keepdims=True)
                hit = jrow_f == first
                sel_bias = jnp.where(hit, jnp.where(best > NEG / 2, 0.0, NEG), sel_bias)
                score = jnp.where(hit, TAKEN, score)
            if n_slc < HEAD_DIM:
                sel_bias = jnp.concatenate([sel_bias, jnp.full((HEAD_DIM - n_slc, Q_BLOCK), NEG, F32)], axis=0)
            sel_t = jnp.concatenate([sel_bias, sel_bias], axis=0).T.astype(qa_ref.dtype)
            for p in range(N_QBLK):
                qa_ref[g * HEADS_PER_GROUP + p, 0, rs, :] = jnp.where(own_half, q_ref[p, 0, rs, :], sel_t)
        for p in range(N_QBLK):
            prs = slice(p * Q_BLOCK, (p + 1) * Q_BLOCK)
            o_ref[p, 0, rs, :] = jnp.where(low, outs[0][prs], outs[1][prs]).astype(o_ref.dtype)


def _cmp_select(qkv, cmp_kv, bias_c, ovt, n_slc, tq=CMP_TQ):
    _, b, t, _ = qkv.shape
    ncp = cmp_kv.shape[2]
    assert n_slc <= HEAD_DIM
    return pl.pallas_call(
        functools.partial(_cmp_select_kernel, n_slc=n_slc, n_top=min(SLC_TOPK, n_slc)),
        grid=(b, t // tq),
        in_specs=[pl.BlockSpec((N_QBLK, 1, tq, LANES), lambda bi, i: (CB_AQ // N_QBLK, bi, i, 0)),
                  pl.BlockSpec((1, 1, ncp, LANES), lambda bi, i: (0, bi, 0, 0)),
                  pl.BlockSpec((1, 1, ncp, LANES), lambda bi, i: (1, bi, 0, 0)),
                  pl.BlockSpec((N_HEADS, tq, ncp), lambda bi, i: (0, i, 0)),
                  pl.BlockSpec((LANES, ncp), lambda bi, i: (0, 0))],
        out_specs=[pl.BlockSpec((N_QBLK, 1, tq, LANES), lambda bi, i: (0, bi, i, 0)),
                   pl.BlockSpec((N_HEADS, 1, tq, LANES), lambda bi, i: (0, bi, i, 0))],
        out_shape=[jax.ShapeDtypeStruct((N_QBLK, b, t, LANES), BF16),
                   jax.ShapeDtypeStruct((N_HEADS, b, t, LANES), BF16)],
        compiler_params=_cparams("parallel", "parallel"),
        name="nsa_cmp_select",
    )(qkv, cmp_kv, cmp_kv, bias_c, ovt)


def _lane_chunks(s):
    return [s[:, c:c + LANES] for c in range(0, s.shape[1], LANES)]


def _row_max(s):
    chunks = _lane_chunks(s)
    m = chunks[0]
    for c in chunks[1:]:
        m = jnp.maximum(m, c)
    return jnp.broadcast_to(jnp.max(m, axis=-1, keepdims=True), m.shape)


def _exp2_shifted(s, m):
    return jnp.concatenate([jnp.exp2(c - m) for c in _lane_chunks(s)], axis=-1)


def _flash_start(s, v):
    m = _row_max(s)
    return m, _dot(_exp2_shifted(s, m).astype(BF16), v)


def _flash_update(s, m, acc, v):
    m_new = jnp.maximum(m, _row_max(s))
    return m_new, jnp.exp2(m - m_new) * acc + _dot(_exp2_shifted(s, m_new).astype(BF16), v)


def _normalise(acc):
    return acc / pltpu.roll(acc, HEAD_DIM, 1)


def _augment_keys_values(ka_scr, va_scr, k2, xk, v2, base=0):
    low = _lane_half_mask(0)
    ones = jnp.ones_like(v2)
    ka_scr[base] = jnp.where(low, k2, xk)
    ka_scr[base + 1] = jnp.where(low, xk, k2)
    va_scr[base] = jnp.where(low, v2, ones)
    va_scr[base + 1] = jnp.where(low, ones, v2)


def _store_all(scr, s):
    for n, sn in enumerate(s):
        scr[n] = sn


def _load_all(scr):
    return tuple(scr[n] for n in range(scr.shape[0]))


def _worklist_sweep(n_steps, logits, update, sa_scr, sb_scr):
    def single(n, _):
        _store_all(sa_scr, logits(n))
        update(n, _load_all(sa_scr))
        return 0

    odd = n_steps % 2
    lax.fori_loop(0, odd, single, 0)
    _store_all(sa_scr, logits(odd))

    def pair(jp, _):
        n = odd + 2 * jp
        _store_all(sb_scr, logits(n + 1))
        update(n, _load_all(sa_scr))
        _store_all(sa_scr, logits(jnp.minimum(n + 2, n_steps - 1)))
        update(n + 1, _load_all(sb_scr))
        return 0

    lax.fori_loop(0, n_steps // 2, pair, 0)


SLC_QGROUP = 4


def _slc_kernel(nfar_ref, jt_ref, it_ref, qa_ref, k_ref, v_ref, hot_ref, nb_ref, o_ref,
                ka_scr, va_scr, m_scr, acc_scr, sa_scr, sb_scr, *, tq):
    gi = pl.program_id(1)
    qg = qa_ref.shape[2] // tq
    rows = HEADS_PER_GROUP * tq
    _augment_keys_values(ka_scr, va_scr, k_ref[0, 0], hot_ref[...], v_ref[0, 0])
    heads = [slice(g * HEADS_PER_GROUP, (g + 1) * HEADS_PER_GROUP) for g in range(KV_GROUPS)]

    def tile(n):
        return pl.ds(pl.multiple_of(n * tq, tq), tq)

    def logits(j, il):
        return tuple(_dot_nt(qa_ref[heads[g], 0, tile(il), :].reshape(rows, LANES), ka_scr[g, tile(j), :])
                     for g in range(KV_GROUPS))

    def biased(s, which, g):
        return (s.reshape(HEADS_PER_GROUP, tq, tq) + nb_ref[which, heads[g]]).reshape(rows, tq)

    def diag_update(il, s):
        for g in range(KV_GROUPS):
            m_scr[il, g], acc_scr[il, g] = _flash_start(biased(s[g], 1, g), va_scr[g, tile(gi * qg + il), :])

    _worklist_sweep(qg, lambda il: logits(gi * qg + il, il), diag_update, sa_scr, sb_scr)

    def fold(il, g, sg, j):
        m_scr[il, g], acc_scr[il, g] = _flash_update(sg, m_scr[il, g], acc_scr[il, g], va_scr[g, tile(j), :])

    def prev_update(il, s):
        i = gi * qg + il
        pen = jnp.where(i >= 1, 0.0, NEG)
        for g in range(KV_GROUPS):
            fold(il, g, biased(s[g], 0, g) + pen, jnp.maximum(i - 1, 0))

    _worklist_sweep(qg, lambda il: logits(jnp.maximum(gi * qg + il - 1, 0), il), prev_update, sa_scr, sb_scr)

    def far_update(n, s):
        for g in range(KV_GROUPS):
            fold(it_ref[gi, n], g, s[g], jt_ref[gi, n])

    _worklist_sweep(nfar_ref[gi], lambda n: logits(jt_ref[gi, n], it_ref[gi, n]), far_update, sa_scr, sb_scr)

    low = _lane_half_mask(0)
    for il in range(qg):
        o_g0, o_g1 = _normalise(acc_scr[il, 0]), _normalise(acc_scr[il, 1])
        for p in range(N_QBLK):
            head_rows = slice(p * tq, (p + 1) * tq)
            o_ref[p, 0, il * tq:(il + 1) * tq, :] = jnp.where(low, o_g0[head_rows], o_g1[head_rows]).astype(o_ref.dtype)


def _slc_attention(qa, qkv, hot2, near_bias, qg=SLC_QGROUP):
    _, b, t, _ = qa.shape
    tq = near_bias.shape[2]
    nq = t // tq
    qg = min(qg, nq)
    n_groups = nq // qg
    lists = [[(j, il) for j in range(nq) for il in range(qg) if j <= g * qg + il - 2] for g in range(n_groups)]
    width = max(2, max(len(steps) for steps in lists))
    table = lambda k: jnp.asarray([[s[k] for s in steps] + [0] * (width - len(steps)) for steps in lists], jnp.int32)
    nfar = jnp.asarray([len(steps) for steps in lists], jnp.int32)
    smem = pl.BlockSpec(memory_space=pltpu.SMEM)
    rows = HEADS_PER_GROUP * tq
    return pl.pallas_call(
        functools.partial(_slc_kernel, tq=tq),
        grid=(b, n_groups),
        in_specs=[smem, smem, smem,
                  pl.BlockSpec((N_HEADS, 1, qg * tq, LANES), lambda bi, gi: (0, bi, gi, 0)),
                  pl.BlockSpec((1, 1, t, LANES), lambda bi, gi: (CB_AKS, bi, 0, 0)),
                  pl.BlockSpec((1, 1, t, LANES), lambda bi, gi: (CB_AVS, bi, 0, 0)),
                  pl.BlockSpec(hot2.shape, lambda bi, gi: (0, 0)),
                  pl.BlockSpec(near_bias.shape, lambda bi, gi: (0, 0, 0, 0))],
        out_specs=pl.BlockSpec((N_QBLK, 1, qg * tq, LANES), lambda bi, gi: (0, bi, gi, 0)),
        out_shape=jax.ShapeDtypeStruct((N_QBLK, b, t, LANES), BF16),
        scratch_shapes=[pltpu.VMEM((KV_GROUPS, t, LANES), BF16), pltpu.VMEM((KV_GROUPS, t, LANES), BF16),
                        pltpu.VMEM((qg, KV_GROUPS, rows, LANES), F32), pltpu.VMEM((qg, KV_GROUPS, rows, LANES), F32),
                        pltpu.VMEM((KV_GROUPS, rows, tq), F32), pltpu.VMEM((KV_GROUPS, rows, tq), F32)],
        compiler_params=_cparams("parallel", "parallel"),
        name="nsa_slc_attention",
    )(nfar, table(0), table(1), qa, qkv, qkv, hot2, near_bias)


BAND_TQ = 256


def _banded_kernel(*refs, n_blk, use_sinks):
    q_ref = refs[0]
    k_refs = refs[1:1 + n_blk]
    v_refs = refs[1 + n_blk:1 + 2 * n_blk]
    bias_ref, o_ref = refs[1 + 2 * n_blk:]
    i = pl.program_id(1)
    tq = q_ref.shape[2]
    n_keys = n_blk * Q_BLOCK
    rows = HEADS_PER_GROUP * tq
    first_key = (i + 1) * tq - n_keys
    col = lax.broadcasted_iota(jnp.int32, (1, n_keys), 1)
    pad_pen = jnp.where(first_key + col >= 0, 0.0, NEG)
    kcat = jnp.concatenate([r[0, 0] for r in k_refs], axis=0)
    vcat = jnp.concatenate([r[0, 0] for r in v_refs], axis=0)
    if use_sinks:
        sink_row = lax.broadcasted_iota(jnp.int32, (n_keys, 1), 0) == 0
        kcat = jnp.where(sink_row, jnp.zeros_like(kcat), kcat)
        vcat = jnp.where(sink_row, jnp.zeros_like(vcat), vcat)
        pad_pen = jnp.where(col == 0, 0.0, pad_pen)
    outs = []
    for g in range(KV_GROUPS):
        own_half = _lane_half_mask(g)
        qop = jnp.concatenate([jnp.where(own_half, q_ref[p, 0], jnp.zeros_like(q_ref[p, 0]))
                               for p in range(N_QBLK)], axis=0)
        hs = slice(g * HEADS_PER_GROUP, (g + 1) * HEADS_PER_GROUP)
        s = _dot_nt(qop, kcat) + bias_ref[hs].reshape(rows, n_keys) + pad_pen
        e = jnp.exp2(s - jnp.max(s, axis=-1, keepdims=True))
        acc = _dot(e.astype(BF16), jnp.where(own_half, vcat, jnp.ones_like(vcat)))
        outs.append(_normalise(acc))
    _merge_group_outputs(o_ref, outs[0], outs[1])


def _banded_attention(qkv, cb_q, cb_k, cb_v, bias, use_sinks):
    _, b, t, _ = qkv.shape
    tq = bias.shape[1]
    n_blk = bias.shape[2] // Q_BLOCK
    per_tile = tq // Q_BLOCK

    def kv_spec(cb, back):
        return pl.BlockSpec((1, 1, Q_BLOCK, LANES),
                            lambda bi, i: (cb, bi, jnp.maximum((i + 1) * per_tile - 1 - back, 0), 0))

    backs = [n_blk - 1 - jb for jb in range(n_blk)]
    return pl.pallas_call(
        functools.partial(_banded_kernel, n_blk=n_blk, use_sinks=use_sinks),
        grid=(b, t // tq),
        in_specs=([pl.BlockSpec((N_QBLK, 1, tq, LANES), lambda bi, i: (cb_q // N_QBLK, bi, i, 0))]
                  + [kv_spec(cb_k, back) for back in backs] + [kv_spec(cb_v, back) for back in backs]
                  + [pl.BlockSpec(bias.shape, lambda bi, i: (0, 0, 0))]),
        out_specs=pl.BlockSpec((N_QBLK, 1, tq, LANES), lambda bi, i: (0, bi, i, 0)),
        out_shape=jax.ShapeDtypeStruct((N_QBLK, b, t, LANES), BF16),
        compiler_params=_cparams("parallel", "parallel"),
        name="banded_attention",
    )(qkv, *([qkv] * (2 * n_blk)), bias)


FOX_EXTRA = 6


def _fox_prep_kernel(g_ref, fb_ref, pq_ref, pk_ref, oq_ref, ok_ref, xq_ref, xk_ref, carry_scr):
    @pl.when(pl.program_id(1) == 0)
    def _():
        carry_scr[...] = jnp.zeros_like(carry_scr)

    x = g_ref[...] + fb_ref[...]
    logf = jnp.minimum(x, 0.0) - jnp.log(1.0 + jnp.exp(-jnp.abs(x)))
    ch = x.shape[0]
    lower = jnp.where(lax.broadcasted_iota(jnp.int32, (ch, ch), 0)
                      >= lax.broadcasted_iota(jnp.int32, (ch, ch), 1), 1.0, 0.0).astype(BF16)

    def split3(v):
        hi, rest = v.astype(BF16), v - v.astype(BF16).astype(F32)
        return (hi,) + _split_hi_lo(rest)

    cs = sum(_dot(lower, part) for part in split3(logf)) + carry_scr[0:1, :]
    carry_scr[0:1, :] = cs[ch - 1:ch, :]
    parts = split3(cs * LOG2E)
    xq = sum(_dot(part, pq_ref[n]) for n, part in enumerate(parts)) + oq_ref[...]
    xk = sum(_dot(part, pk_ref[n]) for n, part in enumerate(parts)) + ok_ref[...]
    for p in range(N_QBLK):
        xq_ref[p, 0] = xq[:, p * LANES:(p + 1) * LANES].astype(xq_ref.dtype)
        xk_ref[p, 0] = xk[:, p * LANES:(p + 1) * LANES].astype(xk_ref.dtype)


def _fox_prep(graw, fb_row, b, t, chunk=512):
    chunk = min(chunk, t)
    nch = t // chunk
    pq = np.zeros((3, LANES, N_QBLK * LANES), np.float32)
    pk = np.zeros((3, LANES, N_QBLK * LANES), np.float32)
    oq = np.zeros((1, N_QBLK * LANES), np.float32)
    ok = np.zeros((1, N_QBLK * LANES), np.float32)
    for h in range(N_HEADS):
        col = (h // 2) * LANES + (HEAD_DIM if h % 2 == 0 else 0)
        for n in range(3):
            pq[n, N_GATE_COLS + h, col + n] = 1.0
            pk[n, N_GATE_COLS + h, col + 3 + n] = -1.0
        oq[0, col + 3:col + FOX_EXTRA] = 1.0
        ok[0, col:col + 3] = 1.0
    const = lambda a: pl.BlockSpec(a.shape, lambda bi, c: (0,) * a.ndim)
    out_spec = pl.BlockSpec((N_QBLK, 1, chunk, LANES), lambda bi, c: (0, bi, c, 0))
    return pl.pallas_call(
        _fox_prep_kernel,
        grid=(b, nch),
        in_specs=[pl.BlockSpec((chunk, LANES), lambda bi, c: (bi * nch + c, 0)),
                  const(fb_row), const(pq), const(pk), const(oq), const(ok)],
        out_specs=[out_spec, out_spec],
        out_shape=[jax.ShapeDtypeStruct((N_QBLK, b, t, LANES), BF16)] * 2,
        scratch_shapes=[pltpu.VMEM((8, LANES), F32)],
        compiler_params=_cparams("parallel", "arbitrary"),
        name="fox_prep",
    )(graw, fb_row, jnp.asarray(pq, BF16), jnp.asarray(pk, BF16), jnp.asarray(oq), jnp.asarray(ok))


def _fox_kernel(jt_ref, it_ref, q_ref, xq_ref, k_ref, xk_ref, v_ref, o_ref,
                qa_scr, ka_scr, va_scr, m_scr, acc_scr, sa_scr, sb_scr, *, tq, n_off):
    nq = q_ref.shape[2] // tq
    low = _lane_half_mask(0)
    _augment_keys_values(ka_scr, va_scr, k_ref[0, 0], xk_ref[0, 0], v_ref[0, 0])
    q2, xq = q_ref[0, 0], xq_ref[0, 0]
    qa_scr[0] = jnp.where(low, q2, xq)
    qa_scr[1] = jnp.where(low, xq, q2)
    causal_pen = jnp.where(lax.broadcasted_iota(jnp.int32, (tq, tq), 0)
                           >= lax.broadcasted_iota(jnp.int32, (tq, tq), 1), 0.0, NEG)

    def tile(n):
        return pl.ds(pl.multiple_of(n * tq, tq), tq)

    def logits(j, i):
        return tuple(_dot_nt(qa_scr[h, tile(i), :], ka_scr[h, tile(j), :]) for h in range(2))

    def diag_update(i, s):
        for h in range(2):
            m_scr[i, h], acc_scr[i, h] = _flash_start(s[h] + causal_pen, va_scr[h, tile(i), :])

    _worklist_sweep(nq, lambda n: logits(n, n), diag_update, sa_scr, sb_scr)

    def off_update(n, s):
        j, i = jt_ref[n], it_ref[n]
        for h in range(2):
            m_scr[i, h], acc_scr[i, h] = _flash_update(s[h], m_scr[i, h], acc_scr[i, h], va_scr[h, tile(j), :])

    _worklist_sweep(n_off, lambda n: logits(jt_ref[n], it_ref[n]), off_update, sa_scr, sb_scr)

    def finish(i, _):
        o_ref[0, 0, tile(i), :] = jnp.where(low, _normalise(acc_scr[i, 0]), _normalise(acc_scr[i, 1])).astype(o_ref.dtype)
        return 0

    lax.fori_loop(0, nq, finish, 0)


def _fox_attention(qkv, xq, xk, tq):
    _, b, t, _ = qkv.shape
    nq = t // tq
    pairs = [(j, i) for j in range(nq) for i in range(j + 1, nq)]
    n_off = len(pairs)
    pairs += [(0, 0)] * max(0, 2 - n_off)
    jt = jnp.asarray([p[0] for p in pairs], jnp.int32)
    it = jnp.asarray([p[1] for p in pairs], jnp.int32)
    smem = pl.BlockSpec(memory_space=pltpu.SMEM)
    seq = lambda cb: pl.BlockSpec((1, 1, t, LANES), lambda bi, p: (cb + p, bi, 0, 0))
    return pl.pallas_call(
        functools.partial(_fox_kernel, tq=tq, n_off=n_off),
        grid=(b, N_QBLK),
        in_specs=[smem, smem, seq(CB_CQ), seq(0), seq(CB_CK), seq(0), seq(CB_CV)],
        out_specs=pl.BlockSpec((1, 1, t, LANES), lambda bi, p: (p, bi, 0, 0)),
        out_shape=jax.ShapeDtypeStruct((N_QBLK, b, t, LANES), BF16),
        scratch_shapes=[pltpu.VMEM((2, t, LANES), BF16), pltpu.VMEM((2, t, LANES), BF16),
                        pltpu.VMEM((2, t, LANES), BF16),
                        pltpu.VMEM((nq, 2, tq, LANES), F32), pltpu.VMEM((nq, 2, tq, LANES), F32),
                        pltpu.VMEM((2, tq, tq), F32), pltpu.VMEM((2, tq, tq), F32)],
        compiler_params=_cparams("parallel", "parallel"),
        name="fox_attention",
    )(jt, it, qkv, xq, qkv, xk, qkv)


def _merge_kernel(ocmp_ref, oslc_ref, owin_ref, ob_ref, oc_ref, ga_ref, mg_ref, wb_ref, wo_ref, x_ref, o_ref):
    def cat(ref):
        return jnp.concatenate([ref[p] for p in range(N_QBLK)], axis=-1)

    oa = (ga_ref[0].astype(F32) * cat(ocmp_ref).astype(F32)
          + ga_ref[1].astype(F32) * cat(oslc_ref).astype(F32)
          + ga_ref[2].astype(F32) * cat(owin_ref).astype(F32)).astype(BF16)
    branches = (oa, cat(ob_ref), cat(oc_ref))
    d = o_ref.shape[1]
    chunks = [slice(c, c + MM_CHUNK) for c in range(0, d, MM_CHUNK)]
    mix = jnp.concatenate(
        [sum(mg_ref[n, :, cols].astype(F32) * _dot(branches[n], wb_ref[n, :, cols]) for n in range(N_BRANCH))
         .astype(BF16) for cols in chunks], axis=-1)
    for cols in chunks:
        o_ref[:, cols] = x_ref[:, cols] + _dot(mix, wo_ref[:, cols])


def _merge(ocmp, oslc, owin, ob, oc, ga, mg, wb, wo, x2, tm=512):
    bt, d = x2.shape
    o_spec = pl.BlockSpec((N_QBLK, tm, LANES), lambda i: (0, i, 0))
    return pl.pallas_call(
        _merge_kernel,
        grid=(bt // tm,),
        in_specs=[o_spec] * 5 + [pl.BlockSpec((N_BRANCH, tm, MIX_WIDTH), lambda i: (0, i, 0)),
                                 pl.BlockSpec((N_BRANCH, tm, d), lambda i: (0, i, 0)),
                                 pl.BlockSpec(wb.shape, lambda i: (0, 0, 0)),
                                 pl.BlockSpec(wo.shape, lambda i: (0, 0)),
                                 pl.BlockSpec((tm, d), lambda i: (i, 0))],
        out_specs=pl.BlockSpec((tm, d), lambda i: (i, 0)),
        out_shape=jax.ShapeDtypeStruct((bt, d), F32),
        compiler_params=_cparams("parallel"),
        name="branch_merge",
    )(ocmp, oslc, owin, ob, oc, ga, mg, wb, wo, x2)


FFN_TF = 512


def _ffn_kernel(x_ref, xh_ref, g_ref, wg_ref, wu_ref, cw_ref, cb_ref, wd_ref, o_ref,
                h_scr, hh_scr, a_scr, acc_scr, *, tm, tiles_per_seq):
    i = pl.program_id(0)
    f = pl.program_id(1)

    def norm(x):
        ms = jnp.mean(x * x, axis=-1, keepdims=True)
        return ((x * lax.rsqrt(ms + RMS_EPS)) * g_ref[...]).astype(BF16)

    @pl.when(f == 0)
    def _():
        h_scr[...] = norm(x_ref[...])
        hh_scr[...] = norm(xh_ref[...])
        acc_scr[...] = jnp.zeros_like(acc_scr)

    a_scr[0:8, :] = _dot(hh_scr[...], wg_ref[...]) * jnp.where(i % tiles_per_seq == 0, 0.0, 1.0)
    acts = []
    for c in range(wg_ref.shape[1] // MM_CHUNK):
        cols = slice(c * MM_CHUNK, (c + 1) * MM_CHUNK)
        a = _dot(h_scr[...], wg_ref[:, cols])
        a_scr[8:8 + tm, cols] = a
        conv = (cw_ref[0:1, cols] * a_scr[6:6 + tm, cols] + cw_ref[1:2, cols] * a_scr[7:7 + tm, cols]
                + cw_ref[2:3, cols] * a + cb_ref[:, cols])
        acts.append((_gelu_tanh(conv) * _dot(h_scr[...], wu_ref[:, cols])).astype(BF16))
    acc_scr[...] += _dot(jnp.concatenate(acts, axis=-1), wd_ref[...])

    @pl.when(f == pl.num_programs(1) - 1)
    def _():
        o_ref[...] = x_ref[...] + acc_scr[...]


def _ffn(x2, g, wg, wu, cw, cb, wd, t, tm=1024, tf=FFN_TF):
    bt, d = x2.shape
    pad = -wg.shape[1] % tf
    wg, wu, cw = [jnp.pad(a, ((0, 0), (0, pad))) for a in (wg, wu, cw)]
    cb, wd = jnp.pad(cb, (0, pad)), jnp.pad(wd, ((0, pad), (0, 0)))
    ff = wg.shape[1]
    halo_blocks = tm // 8
    return pl.pallas_call(
        functools.partial(_ffn_kernel, tm=tm, tiles_per_seq=t // tm),
        grid=(bt // tm, ff // tf),
        in_specs=[pl.BlockSpec((tm, d), lambda i, f: (i, 0)),
                  pl.BlockSpec((8, d), lambda i, f: (jnp.maximum(i * halo_blocks - 1, 0), 0)),
                  pl.BlockSpec((1, d), lambda i, f: (0, 0)),
                  pl.BlockSpec((d, tf), lambda i, f: (0, f)),
                  pl.BlockSpec((d, tf), lambda i, f: (0, f)),
                  pl.BlockSpec((CONV_WIDTH, tf), lambda i, f: (0, f)),
                  pl.BlockSpec((1, tf), lambda i, f: (0, f)),
                  pl.BlockSpec((tf, d), lambda i, f: (f, 0))],
        out_specs=pl.BlockSpec((tm, d), lambda i, f: (i, 0)),
        out_shape=jax.ShapeDtypeStruct((bt, d), F32),
        scratch_shapes=[pltpu.VMEM((tm, d), BF16), pltpu.VMEM((8, d), BF16),
                        pltpu.VMEM((tm + 8, tf), F32), pltpu.VMEM((tm, d), F32)],
        compiler_params=_cparams("parallel", "arbitrary"),
        name="conv_ffn",
    )(x2, x2, g.reshape(1, d), wg, wu, cw, cb.reshape(1, ff), wd)


def _t5_bucket_np(dist):
    max_exact = N_BUCKETS // 2
    d = np.maximum(dist, 0)
    ratio = np.log(np.maximum(d, 1) / max_exact) / math.log(MAX_DISTANCE / max_exact)
    large = np.minimum(max_exact + (ratio * (N_BUCKETS - max_exact)).astype(np.int64), N_BUCKETS - 1)
    return np.where(d < max_exact, d, large)


def _expand_kernel(tab_ref, idx_ref, o_ref):
    tab = tab_ref[...]
    onehot = jnp.where(lax.broadcasted_iota(jnp.int32, (tab.shape[1], idx_ref.shape[1]), 0) == idx_ref[...],
                       1.0, 0.0).astype(BF16)
    hi, rest = tab.astype(BF16), tab - tab.astype(BF16).astype(F32)
    mid, lo = _split_hi_lo(rest)
    o_ref[...] = _dot(hi, onehot) + _dot(mid, onehot) + _dot(lo, onehot)


def _bias_from_dist(tab_t, dist, valid, chunk=4096):
    heads = tab_t.shape[0]
    tab = jnp.zeros((heads, LANES), F32).at[:, :N_BUCKETS].set(tab_t).at[:, N_BUCKETS].set(NEG)
    idx = np.where(valid, _t5_bucket_np(dist), N_BUCKETS).reshape(1, -1).astype(np.int32)
    n = idx.shape[1]
    chunk = min(chunk, n)
    out = pl.pallas_call(
        _expand_kernel,
        grid=(n // chunk,),
        in_specs=[pl.BlockSpec((heads, LANES), lambda i: (0, 0)), pl.BlockSpec((1, chunk), lambda i: (0, i))],
        out_specs=pl.BlockSpec((heads, chunk), lambda i: (0, i)),
        out_shape=jax.ShapeDtypeStruct((heads, n), F32),
        compiler_params=_cparams("parallel"),
        name="rel_bias_expand",
    )(tab, jnp.asarray(idx))
    return out.reshape((heads,) + dist.shape)


def _cmp_bias_kernel(g_ref, o_ref):
    i = pl.program_id(0)
    ncp = o_ref.shape[2]
    for h in range(o_ref.shape[0]):
        o_ref[h] = pltpu.roll(g_ref[h], i * (Q_BLOCK // CMP_STRIDE), 1)[:, :ncp]


def _cmp_bias(g_tab, t):
    heads, _, width = g_tab.shape
    ncp = t // CMP_STRIDE
    return pl.pallas_call(
        _cmp_bias_kernel,
        grid=(t // Q_BLOCK,),
        in_specs=[pl.BlockSpec(g_tab.shape, lambda i: (0, 0, 0))],
        out_specs=pl.BlockSpec((heads, Q_BLOCK, ncp), lambda i: (0, i, 0)),
        out_shape=jax.ShapeDtypeStruct((heads, t, ncp), F32),
        compiler_params=_cparams("parallel"),
        name="cmp_bias_build",
    )(g_tab)


def _static_tables(t):
    ncp = t // CMP_STRIDE
    n_cmp = (t - CMP_LEN) // CMP_STRIDE + 1
    n_slc = t // SLC_LEN
    lane = np.arange(MM_CHUNK)
    bd = (lane[:, None] // HEAD_DIM == lane[None, :] // HEAD_DIM).astype(np.float32)
    eg = np.zeros((LANES, N_BRANCH * MIX_WIDTH), np.float32)
    for h in range(N_HEADS):
        base = (h % N_QBLK) * LANES + (h // N_QBLK) * HEAD_DIM
        for n in range(N_BRANCH):
            eg[h * N_BRANCH + n, n * MIX_WIDTH + base:n * MIX_WIDTH + base + HEAD_DIM] = 1.0
    c_start = np.arange(n_cmp) * CMP_STRIDE
    s_start = np.arange(n_slc) * SLC_LEN
    overlap = ((c_start[:, None] < s_start[None, :] + SLC_LEN) & (c_start[:, None] + CMP_LEN > s_start[None, :]))
    ovt = np.zeros((LANES, ncp), np.float32)
    ovt[:n_slc, :n_cmp] = overlap.T
    key_block = np.zeros((t, LANES), np.float32)
    key_block[np.arange(t), np.arange(t) // SLC_LEN] = 1.0
    key_block[np.arange(t), HEAD_DIM + np.arange(t) // SLC_LEN] = 1.0
    as_bf16 = lambda a: jnp.asarray(a, BF16)
    return dict(bd=as_bf16(bd), eg=as_bf16(eg), ovt=as_bf16(ovt), key_block=as_bf16(key_block), n_slc=n_slc)


def _bias_tables(rel_bias, t):
    n_cmp = (t - CMP_LEN) // CMP_STRIDE + 1
    ncp = t // CMP_STRIDE
    tab_a = rel_bias[:, :N_HEADS].T * LOG2E
    tab_b = rel_bias[:, N_HEADS:].T * LOG2E
    r = np.arange(Q_BLOCK)[:, None]

    def band(window):
        n_prev = -(-(window - 1) // Q_BLOCK)
        n_keys = n_prev * Q_BLOCK + BAND_TQ
        dist = (n_keys - BAND_TQ) + np.arange(BAND_TQ)[:, None] - np.arange(n_keys)[None, :]
        valid = (dist >= 0) & (dist < window)
        assert not valid[:, 0].any()
        return dist, valid

    m = np.arange(2 * ncp)
    m = np.where(m < ncp, m, m - 2 * ncp)[None, :]
    dist_c = r - (m * CMP_STRIDE + CMP_LEN - 1)
    assert n_cmp == ncp - 1 and (dist_c[:, m[0] == Q_BLOCK // CMP_STRIDE - 1] < 0).all()
    tile = np.arange(SLC_TILE)
    dist_n = np.stack([SLC_TILE * (1 - which) + tile[:, None] - tile[None, :] for which in range(2)])
    near = _bias_from_dist(tab_a - tab_a[:, N_BUCKETS - 1:], dist_n, dist_n >= 0)
    return dict(win=_bias_from_dist(tab_a, *band(NSA_WINDOW)),
                swa=_bias_from_dist(tab_b, *band(SWA_WINDOW)),
                cmp=_cmp_bias(_bias_from_dist(tab_a, dist_c, dist_c >= 0), t),
                near=jnp.transpose(near, (1, 0, 2, 3)))


def _pair_cols(w):
    parts = []
    for p in range(N_QBLK):
        parts += [w[..., p * HEAD_DIM:(p + 1) * HEAD_DIM],
                  w[..., (N_QBLK + p) * HEAD_DIM:(N_QBLK + p + 1) * HEAD_DIM]]
    return jnp.concatenate(parts, axis=-1)


def _prep_layer(w_in, qk_gain, w_branch):
    kv = KV_GROUPS * HEAD_DIM
    widths = [('a_q', MIX_WIDTH), ('a_kc', kv), ('a_vc', kv), ('a_ks', kv), ('a_vs', kv), ('a_kw', kv), ('a_vw', kv),
              ('a_gate', N_GATE_COLS), ('b_q', MIX_WIDTH), ('b_k', kv), ('b_v', kv),
              ('c_q', MIX_WIDTH), ('c_k', MIX_WIDTH), ('c_v', MIX_WIDTH), ('c_f', N_HEADS),
              ('merge', N_BRANCH * w_in.shape[0])]
    cols, off = {}, 0
    for name, w in widths:
        cols[name] = w_in[:, off:off + w]
        off += w
    scale = HEAD_DIM ** -0.5 * LOG2E
    tile = lambda gvec, n: jnp.tile(gvec, n)
    zeros = lambda n: jnp.zeros((n,), F32)
    ones = lambda n: jnp.ones((n,), F32)
    pieces = [(_pair_cols(cols['a_q']), tile(qk_gain[0] * scale, N_HEADS), ones(MIX_WIDTH)),
              (_pair_cols(cols['b_q']), tile(qk_gain[2] * scale, N_HEADS), ones(MIX_WIDTH)),
              (cols['c_q'], tile(qk_gain[4] * scale, N_HEADS), ones(MIX_WIDTH)),
              (cols['c_k'], tile(qk_gain[5], N_HEADS), ones(MIX_WIDTH)),
              (cols['a_ks'], tile(qk_gain[1], KV_GROUPS), ones(kv)),
              (cols['a_kw'], tile(qk_gain[1], KV_GROUPS), ones(kv)),
              (cols['b_k'], tile(qk_gain[3], KV_GROUPS), ones(kv)),
              (cols['a_vs'], zeros(kv), zeros(kv)),
              (cols['c_v'], zeros(MIX_WIDTH), zeros(MIX_WIDTH)),
              (cols['a_kc'], zeros(kv), zeros(kv)), (cols['a_vc'], zeros(kv), zeros(kv)),
              (cols['a_vw'], zeros(kv), zeros(kv)), (cols['b_v'], zeros(kv), zeros(kv))]
    w_qkv = jnp.concatenate([p[0] for p in pieces], axis=1).astype(BF16)
    n_qkv = w_qkv.shape[1]
    aux = jnp.zeros((8, n_qkv), F32)
    aux = aux.at[0].set(jnp.concatenate([p[1] for p in pieces])).at[1].set(jnp.concatenate([p[2] for p in pieces]))
    d = w_in.shape[0]
    w_gates = jnp.concatenate([cols['a_gate'], cols['c_f'],
                               jnp.zeros((d, LANES - N_GATE_COLS - N_HEADS), F32)], axis=1).astype(BF16)
    wb = jnp.stack([_pair_cols(w_branch[0].T).T, _pair_cols(w_branch[1].T).T, w_branch[2]]).astype(BF16)
    return w_qkv, aux, w_gates, cols['merge'].astype(BF16), wb


def _prep_compress(cmp_pos, cmp_w1, cmp_w2, k_gain):
    def both_groups(w):
        z = jnp.zeros_like(w)
        return jnp.concatenate([jnp.concatenate([w, z], -1), jnp.concatenate([z, w], -1)], -2)

    pos = jnp.concatenate([cmp_pos, cmp_pos], -1)
    w1 = both_groups(cmp_w1.reshape(2, CMP_LEN, HEAD_DIM, CMP_HIDDEN)).astype(BF16)
    w2 = both_groups(cmp_w2).astype(BF16)
    aux = jnp.zeros((2, 8, LANES), F32)
    aux = aux.at[0, 0].set(jnp.tile(k_gain, KV_GROUPS)).at[0, 1].set(1.0)
    return pos, w1, w2, aux


def kernel(x, rel_bias, norm_mix, norm_ffn, w_in, forget_bias, qk_gain, cmp_pos, cmp_w1, cmp_w2, sinks, w_branch, w_out, w_gate, w_up, conv_w, conv_b, w_down):
    b, t, d = x.shape
    bt = b * t
    depth = w_in.shape[0]
    tabs = _static_tables(t)
    biases = _bias_tables(rel_bias, t)
    fox_tq = min(512, t)
    x2 = x.reshape(bt, d)
    for l in range(depth):
        w_qkv, aux, w_gates, w_merge, wb = _prep_layer(w_in[l], qk_gain[l], w_branch[l])
        h = _rmsnorm(x2, norm_mix[l])
        qkv = _inproj_qkv(h, w_qkv, aux, tabs['bd']).reshape(N_CB, b, t, LANES)
        graw, ga = _inproj_gates(h, w_gates, tabs['eg'])
        mg = _inproj_merge(h, w_merge, d)

        cmp_kv = _compress(qkv, *_prep_compress(cmp_pos[l], cmp_w1[l], cmp_w2[l], qk_gain[l, 1]), tabs['bd'])
        o_cmp, qa = _cmp_select(qkv, cmp_kv, biases['cmp'], tabs['ovt'], tabs['n_slc'])
        o_slc = _slc_attention(qa, qkv, tabs['key_block'], biases['near'])
        o_win = _banded_attention(qkv, CB_AQ, CB_AKW, CB_AVW, biases['win'], False)

        swa_bias = biases['swa'].at[:, :, 0].set(sinks[l][:, None] * LOG2E)
        o_b = _banded_attention(qkv, CB_BQ, CB_BK, CB_BV, swa_bias, True)

        fb_row = jnp.zeros((1, LANES), F32).at[0, N_GATE_COLS:N_GATE_COLS + N_HEADS].set(forget_bias[l])
        o_c = _fox_attention(qkv, *_fox_prep(graw, fb_row, b, t), fox_tq)

        flat = lambda o: o.reshape(N_QBLK, bt, LANES)
        x2 = _merge(flat(o_cmp), flat(o_slc), flat(o_win), flat(o_b), flat(o_c), ga, mg, wb,
                    w_out[l].astype(BF16), x2)
        x2 = _ffn(x2, norm_ffn[l], w_gate[l].astype(BF16), w_up[l].astype(BF16), conv_w[l], conv_b[l],
                  w_down[l].astype(BF16), t)
    return x2.reshape(b, t, d)
```

```python
import functools
import math

import numpy as np
import jax
import jax.numpy as jnp
from jax import lax
from jax.experimental import pallas as pl
from jax.experimental.pallas import tpu as pltpu

F32 = jnp.float32
BF16 = jnp.bfloat16

HEAD_DIM = 64
LANES = 128
N_HEADS = 8
KV_GROUPS = 2
HEADS_PER_GROUP = N_HEADS // KV_GROUPS
N_QBLK = N_HEADS * HEAD_DIM // LANES
MIX_WIDTH = N_HEADS * HEAD_DIM
N_BRANCH = 3
Q_BLOCK = 128
CMP_LEN = 32
CMP_STRIDE = 16
CMP_HIDDEN = 256
SLC_LEN = 64
SLC_TOPK = 8
NSA_WINDOW = 256
SWA_WINDOW = 128
N_BUCKETS = 32
MAX_DISTANCE = 128
CONV_WIDTH = 3
RMS_EPS = 1e-6
LOG2E = math.log2(math.e)
NEG = -1e30
FORCE_SCORE = 1e9
TAKEN = -3e38
SLC_TILE = 256
VMEM_LIMIT = 48 * 1024 * 1024

CB_AQ, CB_BQ, CB_CQ, CB_CK = 0, 4, 8, 12
CB_AKS, CB_AKW, CB_BK, CB_AVS = 16, 17, 18, 19
CB_CV, CB_AKC, CB_AVC, CB_AVW, CB_BV = 20, 24, 25, 26, 27
N_CB = 28
N_NORM_CB = 20
MM_CHUNK = 256
MM_ROWS = 512
N_GATE_COLS = N_HEADS * N_BRANCH


def _cparams(*sem):
    return pltpu.CompilerParams(dimension_semantics=sem, vmem_limit_bytes=VMEM_LIMIT)


def _split_hi_lo(v):
    hi = v.astype(BF16)
    return hi, (v - hi.astype(F32)).astype(BF16)


def _dot(a, b):
    return jnp.dot(a, b, preferred_element_type=F32)


def _dot_nt(a, b):
    return lax.dot_general(a, b, (((1,), (1,)), ((), ())), preferred_element_type=F32)


def _gelu_tanh(x):
    return x * (0.5 * (1.0 + jnp.tanh(math.sqrt(2.0 / math.pi) * (x + 0.044715 * (x * x * x)))))


def _head_rms(y, bd, gain):
    ms = _dot((y * y).astype(BF16), bd) * (1.0 / HEAD_DIM)
    return (y * lax.rsqrt(ms + RMS_EPS)) * gain


def _lane_half_mask(g):
    return (lax.broadcasted_iota(jnp.int32, (1, LANES), 1) // HEAD_DIM) == g


def _merge_group_outputs(o_ref, o_g0, o_g1):
    low = _lane_half_mask(0)
    tq = o_ref.shape[2]
    for p in range(N_QBLK):
        rows = slice(p * tq, (p + 1) * tq)
        o_ref[p, 0] = jnp.where(low, o_g0[rows], o_g1[rows]).astype(o_ref.dtype)


def _rmsnorm_kernel(x_ref, g_ref, o_ref):
    x = x_ref[...]
    ms = jnp.mean(x * x, axis=-1, keepdims=True)
    o_ref[...] = ((x * lax.rsqrt(ms + RMS_EPS)) * g_ref[...]).astype(o_ref.dtype)


def _rmsnorm(x2, g, tm=2048):
    bt, d = x2.shape
    tm = min(tm, bt)
    return pl.pallas_call(
        _rmsnorm_kernel,
        grid=(bt // tm,),
        in_specs=[pl.BlockSpec((tm, d), lambda i: (i, 0)), pl.BlockSpec((1, d), lambda i: (0, 0))],
        out_specs=pl.BlockSpec((tm, d), lambda i: (i, 0)),
        out_shape=jax.ShapeDtypeStruct((bt, d), BF16),
        compiler_params=_cparams("parallel"),
        name="rmsnorm",
    )(x2, g.reshape(1, d))


def _inproj_qkv_kernel(h_ref, w_ref, aux_ref, bd_ref, o_ref, *, n_norm_tiles):
    per_chunk = MM_CHUNK // LANES

    def emit(normalise):
        for r in range(0, h_ref.shape[0], MM_ROWS):
            rows = slice(r, r + MM_ROWS)
            y_all = _dot(h_ref[rows, :], w_ref[...])
            for c in range(w_ref.shape[1] // MM_CHUNK):
                cols = slice(c * MM_CHUNK, (c + 1) * MM_CHUNK)
                y = y_all[:, cols]
                if normalise:
                    normed = _head_rms(y, bd_ref[...], aux_ref[0:1, cols])
                    y = jnp.where(aux_ref[1:2, cols] > 0.5, normed, y)
                for s in range(per_chunk):
                    o_ref[c * per_chunk + s, rows, :] = y[:, s * LANES:(s + 1) * LANES].astype(o_ref.dtype)

    pl.when(pl.program_id(1) < n_norm_tiles)(lambda: emit(True))
    pl.when(pl.program_id(1) >= n_norm_tiles)(lambda: emit(False))


def _inproj_qkv(h, w, aux, bd, tm=2048, tn=512):
    bt, d = h.shape
    n = w.shape[1]
    n_sub = tn // LANES
    tm = min(tm, bt)
    assert N_NORM_CB % n_sub == 0 and tm % MM_ROWS == 0
    return pl.pallas_call(
        functools.partial(_inproj_qkv_kernel, n_norm_tiles=N_NORM_CB // n_sub),
        grid=(bt // tm, n // tn),
        in_specs=[pl.BlockSpec((tm, d), lambda i, j: (i, 0)),
                  pl.BlockSpec((d, tn), lambda i, j: (0, j)),
                  pl.BlockSpec((8, tn), lambda i, j: (0, j)),
                  pl.BlockSpec(bd.shape, lambda i, j: (0, 0))],
        out_specs=pl.BlockSpec((n_sub, tm, LANES), lambda i, j: (j, i, 0)),
        out_shape=jax.ShapeDtypeStruct((n // LANES, bt, LANES), BF16),
        compiler_params=_cparams("parallel", "arbitrary"),
        name="inproj_qkv",
    )(h, w, aux, bd)


def _inproj_gates_kernel(h_ref, w_ref, raw_ref, sig_ref):
    y = _dot(h_ref[...], w_ref[...])
    raw_ref[...] = y
    sig_ref[...] = jax.nn.sigmoid(y).astype(sig_ref.dtype)


def _inproj_gates(h, w, tm=1024):
    bt, d = h.shape
    return pl.pallas_call(
        _inproj_gates_kernel,
        grid=(bt // tm,),
        in_specs=[pl.BlockSpec((tm, d), lambda i: (i, 0)),
                  pl.BlockSpec((d, LANES), lambda i: (0, 0))],
        out_specs=[pl.BlockSpec((tm, LANES), lambda i: (i, 0))] * 2,
        out_shape=[jax.ShapeDtypeStruct((bt, LANES), F32), jax.ShapeDtypeStruct((bt, LANES), BF16)],
        compiler_params=_cparams("parallel"),
        name="inproj_gates",
    )(h, w)


def _inproj_merge_kernel(h_ref, w_ref, o_ref):
    width = 2 * MM_CHUNK
    for c in range(w_ref.shape[1] // width):
        cols = slice(c * width, (c + 1) * width)
        o_ref[0, :, cols] = jax.nn.sigmoid(_dot(h_ref[...], w_ref[:, cols])).astype(o_ref.dtype)


def _inproj_merge(h, w, d_model, tm=1024, tn=1024):
    bt, d = h.shape
    n = w.shape[1]
    per = d_model // tn
    return pl.pallas_call(
        _inproj_merge_kernel,
        grid=(bt // tm, n // tn),
        in_specs=[pl.BlockSpec((tm, d), lambda i, j: (i, 0)),
                  pl.BlockSpec((d, tn), lambda i, j: (0, j))],
        out_specs=pl.BlockSpec((1, tm, tn), lambda i, j: (j // per, i, j % per)),
        out_shape=jax.ShapeDtypeStruct((n // d_model, bt, d_model), BF16),
        compiler_params=_cparams("parallel", "arbitrary"),
        name="inproj_merge",
    )(h, w)


def _compress_kernel(z_ref, pos_ref, w1_ref, w2_ref, aux_ref, bd_ref, o_ref, zf_scr):
    n_chunks = o_ref.shape[2]
    zf_scr[...] = z_ref[0, 0].astype(F32)
    halves = []
    for half in range(CMP_LEN // CMP_STRIDE):
        acc = jnp.zeros((n_chunks, KV_GROUPS * CMP_HIDDEN), F32)
        for l in range(CMP_STRIDE):
            rows = zf_scr[pl.ds(l, n_chunks, stride=CMP_STRIDE), :]
            pos = pos_ref[0, half * CMP_STRIDE + l:half * CMP_STRIDE + l + 1, :]
            acc = acc + _dot((rows + pos).astype(BF16), w1_ref[0, half * CMP_STRIDE + l])
        halves.append(acc)
    hid = _gelu_tanh(halves[0] + pltpu.roll(halves[1], n_chunks - 1, 0))
    y = _dot(hid.astype(BF16), w2_ref[0])
    normed = _head_rms(y, bd_ref[0:LANES, 0:LANES], aux_ref[0, 0:1, :])
    o_ref[0, 0] = jnp.where(aux_ref[0, 1:2, :] > 0.5, normed, y).astype(o_ref.dtype)


def _compress(qkv, pos, w1, w2, aux, bd):
    _, b, t, _ = qkv.shape
    n_chunks = t // CMP_STRIDE
    return pl.pallas_call(
        _compress_kernel,
        grid=(2, b),
        in_specs=[pl.BlockSpec((1, 1, t, LANES), lambda w, i: (CB_AKC + w, i, 0, 0)),
                  pl.BlockSpec((1,) + pos.shape[1:], lambda w, i: (w, 0, 0)),
                  pl.BlockSpec((1,) + w1.shape[1:], lambda w, i: (w, 0, 0, 0)),
                  pl.BlockSpec((1,) + w2.shape[1:], lambda w, i: (w, 0, 0)),
                  pl.BlockSpec((1, 8, LANES), lambda w, i: (w, 0, 0)),
                  pl.BlockSpec(bd.shape, lambda w, i: (0, 0))],
        out_specs=pl.BlockSpec((1, 1, n_chunks, LANES), lambda w, i: (w, i, 0, 0)),
        out_shape=jax.ShapeDtypeStruct((2, b, n_chunks, LANES), BF16),
        scratch_shapes=[pltpu.VMEM((t, LANES), F32)],
        compiler_params=_cparams("parallel", "parallel"),
        name="nsa_compress",
    )(qkv, pos, w1, w2, aux, bd)


CMP_TQ = 256


def _cmp_select_kernel(q_ref, kc_ref, vc_ref, bias_ref, ovt_ref, o_ref, qa_ref, *, n_slc, n_top):
    ncp = kc_ref.shape[2]
    rows = HEADS_PER_GROUP * Q_BLOCK
    jrow = lax.broadcasted_iota(jnp.int32, (n_slc, Q_BLOCK), 0)
    jrow_f = jrow.astype(F32)
    low = _lane_half_mask(0)
    for sub in range(q_ref.shape[2] // Q_BLOCK):
        rs = slice(sub * Q_BLOCK, (sub + 1) * Q_BLOCK)
        t0 = pl.program_id(1) * q_ref.shape[2] + sub * Q_BLOCK
        row_t = t0 + lax.broadcasted_iota(jnp.int32, (Q_BLOCK, 1), 0)
        row_valid = jnp.concatenate([row_t >= CMP_LEN - 1] * HEADS_PER_GROUP, axis=0)
        cur = (t0 + lax.broadcasted_iota(jnp.int32, (n_slc, Q_BLOCK), 1)) // SLC_LEN

        def force(imp):
            forced = jnp.where(jrow == cur, FORCE_SCORE, jnp.where(jrow == cur - 1, FORCE_SCORE, imp))
            return jnp.where(jrow == 0, FORCE_SCORE, forced)

        outs = []
        for g in range(KV_GROUPS):
            own_half = _lane_half_mask(g)
            hs = slice(g * HEADS_PER_GROUP, (g + 1) * HEADS_PER_GROUP)
            qop = jnp.concatenate([jnp.where(own_half, q_ref[p, 0, rs, :], jnp.zeros((Q_BLOCK, LANES), q_ref.dtype))
                                   for p in range(N_QBLK)], axis=0)
            s = _dot_nt(qop, kc_ref[0, 0]) + bias_ref[hs, rs, :].reshape(rows, ncp)
            e = jnp.exp2(s - jnp.max(s, axis=-1, keepdims=True))
            inv = jnp.where(row_valid, 1.0 / jnp.sum(e, axis=-1, keepdims=True), 0.0)
            p = e * inv
            outs.append(_dot(p.astype(BF16), vc_ref[0, 0]))
            psum = p[0:Q_BLOCK]
            for r in range(1, HEADS_PER_GROUP):
                psum = psum + p[r * Q_BLOCK:(r + 1) * Q_BLOCK]
            hi, lo = _split_hi_lo(psum)
            imp_t = (_dot_nt(ovt_ref[...], hi) + _dot_nt(ovt_ref[...], lo))[0:n_slc]
            score = jnp.where(jrow <= cur, force(imp_t), NEG)
            sel_bias = jnp.full((n_slc, Q_BLOCK), NEG, F32)
            for _ in range(n_top):
                best = jnp.max(score, axis=0, keepdims=True)
                first = jnp.min(jnp.where(score == best, jrow_f, float(n_slc)), axis=0, keepdims=True)
                hit = jrow_f == first
                sel_bias = jnp.where(hit, jnp.where(best > NEG / 2, 0.0, NEG), sel_bias)
                score = jnp.where(hit, TAKEN, score)
            if n_slc < HEAD_DIM:
                sel_bias = jnp.concatenate([sel_bias, jnp.full((HEAD_DIM - n_slc, Q_BLOCK), NEG, F32)], axis=0)
            sel_t = jnp.concatenate([sel_bias, sel_bias], axis=0).T.astype(qa_ref.dtype)
            for p in range(N_QBLK):
                qa_ref[g * HEADS_PER_GROUP + p, 0, rs, :] = jnp.where(own_half, q_ref[p, 0, rs, :], sel_t)
        for p in range(N_QBLK):
            prs = slice(p * Q_BLOCK, (p + 1) * Q_BLOCK)
            o_ref[p, 0, rs, :] = jnp.where(low, outs[0][prs], outs[1][prs]).astype(o_ref.dtype)


def _cmp_select(qkv, cmp_kv, bias_c, ovt, n_slc, tq=CMP_TQ):
    _, b, t, _ = qkv.shape
    ncp = cmp_kv.shape[2]
    assert n_slc <= HEAD_DIM
    return pl.pallas_call(
        functools.partial(_cmp_select_kernel, n_slc=n_slc, n_top=min(SLC_TOPK, n_slc)),
        grid=(b, t // tq),
        in_specs=[pl.BlockSpec((N_QBLK, 1, tq, LANES), lambda bi, i: (CB_AQ // N_QBLK, bi, i, 0)),
                  pl.BlockSpec((1, 1, ncp, LANES), lambda bi, i: (0, bi, 0, 0)),
                  pl.BlockSpec((1, 1, ncp, LANES), lambda bi, i: (1, bi, 0, 0)),
                  pl.BlockSpec((N_HEADS, tq, ncp), lambda bi, i: (0, i, 0)),
                  pl.BlockSpec((LANES, ncp), lambda bi, i: (0, 0))],
        out_specs=[pl.BlockSpec((N_QBLK, 1, tq, LANES), lambda bi, i: (0, bi, i, 0)),
                   pl.BlockSpec((N_HEADS, 1, tq, LANES), lambda bi, i: (0, bi, i, 0))],
        out_shape=[jax.ShapeDtypeStruct((N_QBLK, b, t, LANES), BF16),
                   jax.ShapeDtypeStruct((N_HEADS, b, t, LANES), BF16)],
        compiler_params=_cparams("parallel", "parallel"),
        name="nsa_cmp_select",
    )(qkv, cmp_kv, cmp_kv, bias_c, ovt)


def _lane_chunks(s):
    return [s[:, c:c + LANES] for c in range(0, s.shape[1], LANES)]


def _row_max(s):
    chunks = _lane_chunks(s)
    m = chunks[0]
    for c in chunks[1:]:
        m = jnp.maximum(m, c)
    return jnp.broadcast_to(jnp.max(m, axis=-1, keepdims=True), m.shape)


def _exp2_shifted(s, m):
    return jnp.concatenate([jnp.exp2(c - m) for c in _lane_chunks(s)], axis=-1)


def _flash_start(s, v):
    m = _row_max(s)
    return m, _dot(_exp2_shifted(s, m).astype(BF16), v)


def _flash_update(s, m, acc, v):
    m_new = jnp.maximum(m, _row_max(s))
    return m_new, jnp.exp2(m - m_new) * acc + _dot(_exp2_shifted(s, m_new).astype(BF16), v)


def _normalise(acc):
    return acc / pltpu.roll(acc, HEAD_DIM, 1)


def _augment_keys_values(ka_scr, va_scr, k2, xk, v2, base=0):
    low = _lane_half_mask(0)
    ones = jnp.ones_like(v2)
    ka_scr[base] = jnp.where(low, k2, xk)
    ka_scr[base + 1] = jnp.where(low, xk, k2)
    va_scr[base] = jnp.where(low, v2, ones)
    va_scr[base + 1] = jnp.where(low, ones, v2)


def _store_all(scr, s):
    for n, sn in enumerate(s):
        scr[n] = sn


def _load_all(scr):
    return tuple(scr[n] for n in range(scr.shape[0]))


def _worklist_sweep(n_steps, logits, update, sa_scr, sb_scr):
    def single(n, _):
        _store_all(sa_scr, logits(n))
        update(n, _load_all(sa_scr))
        return 0

    odd = n_steps % 2
    lax.fori_loop(0, odd, single, 0)
    _store_all(sa_scr, logits(odd))

    def pair(jp, _):
        n = odd + 2 * jp
        _store_all(sb_scr, logits(n + 1))
        update(n, _load_all(sa_scr))
        _store_all(sa_scr, logits(jnp.minimum(n + 2, n_steps - 1)))
        update(n + 1, _load_all(sb_scr))
        return 0

    lax.fori_loop(0, n_steps // 2, pair, 0)


SLC_QGROUP = 4


def _slc_kernel(nfar_ref, jt_ref, it_ref, qa_ref, k_ref, v_ref, hot_ref, nb_ref, o_ref,
                ka_scr, va_scr, m_scr, acc_scr, sa_scr, sb_scr, *, tq):
    gi = pl.program_id(1)
    qg = qa_ref.shape[2] // tq
    rows = HEADS_PER_GROUP * tq
    _augment_keys_values(ka_scr, va_scr, k_ref[0, 0], hot_ref[...], v_ref[0, 0])
    heads = [slice(g * HEADS_PER_GROUP, (g + 1) * HEADS_PER_GROUP) for g in range(KV_GROUPS)]

    def tile(n):
        return pl.ds(pl.multiple_of(n * tq, tq), tq)

    def logits(j, il):
        return tuple(_dot_nt(qa_ref[heads[g], 0, tile(il), :].reshape(rows, LANES), ka_scr[g, tile(j), :])
                     for g in range(KV_GROUPS))

    def biased(s, which, g):
        return (s.reshape(HEADS_PER_GROUP, tq, tq) + nb_ref[which, heads[g]]).reshape(rows, tq)

    def diag_update(il, s):
        for g in range(KV_GROUPS):
            m_scr[il, g], acc_scr[il, g] = _flash_start(biased(s[g], 1, g), va_scr[g, tile(gi * qg + il), :])

    _worklist_sweep(qg, lambda il: logits(gi * qg + il, il), diag_update, sa_scr, sb_scr)

    def fold(il, g, sg, j):
        m_scr[il, g], acc_scr[il, g] = _flash_update(sg, m_scr[il, g], acc_scr[il, g], va_scr[g, tile(j), :])

    def prev_update(il, s):
        i = gi * qg + il
        pen = jnp.where(i >= 1, 0.0, NEG)
        for g in range(KV_GROUPS):
            fold(il, g, biased(s[g], 0, g) + pen, jnp.maximum(i - 1, 0))

    _worklist_sweep(qg, lambda il: logits(jnp.maximum(gi * qg + il - 1, 0), il), prev_update, sa_scr, sb_scr)

    def far_update(n, s):
        for g in range(KV_GROUPS):
            fold(it_ref[gi, n], g, s[g], jt_ref[gi, n])

    _worklist_sweep(nfar_ref[gi], lambda n: logits(jt_ref[gi, n], it_ref[gi, n]), far_update, sa_scr, sb_scr)

    low = _lane_half_mask(0)
    for il in range(qg):
        o_g0, o_g1 = _normalise(acc_scr[il, 0]), _normalise(acc_scr[il, 1])
        for p in range(N_QBLK):
            head_rows = slice(p * tq, (p + 1) * tq)
            o_ref[p, 0, il * tq:(il + 1) * tq, :] = jnp.where(low, o_g0[head_rows], o_g1[head_rows]).astype(o_ref.dtype)


def _slc_attention(qa, qkv, hot2, near_bias, qg=SLC_QGROUP):
    _, b, t, _ = qa.shape
    tq = near_bias.shape[2]
    nq = t // tq
    qg = min(qg, nq)
    n_groups = nq // qg
    lists = [[(j, il) for j in range(nq) for il in range(qg) if j <= g * qg + il - 2] for g in range(n_groups)]
    width = max(2, max(len(steps) for steps in lists))
    table = lambda k: jnp.asarray([[s[k] for s in steps] + [0] * (width - len(steps)) for steps in lists], jnp.int32)
    nfar = jnp.asarray([len(steps) for steps in lists], jnp.int32)
    smem = pl.BlockSpec(memory_space=pltpu.SMEM)
    rows = HEADS_PER_GROUP * tq
    return pl.pallas_call(
        functools.partial(_slc_kernel, tq=tq),
        grid=(b, n_groups),
        in_specs=[smem, smem, smem,
                  pl.BlockSpec((N_HEADS, 1, qg * tq, LANES), lambda bi, gi: (0, bi, gi, 0)),
                  pl.BlockSpec((1, 1, t, LANES), lambda bi, gi: (CB_AKS, bi, 0, 0)),
                  pl.BlockSpec((1, 1, t, LANES), lambda bi, gi: (CB_AVS, bi, 0, 0)),
                  pl.BlockSpec(hot2.shape, lambda bi, gi: (0, 0)),
                  pl.BlockSpec(near_bias.shape, lambda bi, gi: (0, 0, 0, 0))],
        out_specs=pl.BlockSpec((N_QBLK, 1, qg * tq, LANES), lambda bi, gi: (0, bi, gi, 0)),
        out_shape=jax.ShapeDtypeStruct((N_QBLK, b, t, LANES), BF16),
        scratch_shapes=[pltpu.VMEM((KV_GROUPS, t, LANES), BF16), pltpu.VMEM((KV_GROUPS, t, LANES), BF16),
                        pltpu.VMEM((qg, KV_GROUPS, rows, LANES), F32), pltpu.VMEM((qg, KV_GROUPS, rows, LANES), F32),
                        pltpu.VMEM((KV_GROUPS, rows, tq), F32), pltpu.VMEM((KV_GROUPS, rows, tq), F32)],
        compiler_params=_cparams("parallel", "parallel"),
        name="nsa_slc_attention",
    )(nfar, table(0), table(1), qa, qkv, qkv, hot2, near_bias)


BAND_TQ = 256


def _banded_kernel(*refs, n_blk, use_sinks):
    q_ref = refs[0]
    k_refs = refs[1:1 + n_blk]
    v_refs = refs[1 + n_blk:1 + 2 * n_blk]
    bias_ref, o_ref = refs[1 + 2 * n_blk:]
    i = pl.program_id(1)
    tq = q_ref.shape[2]
    n_keys = n_blk * Q_BLOCK
    rows = HEADS_PER_GROUP * tq
    first_key = (i + 1) * tq - n_keys
    col = lax.broadcasted_iota(jnp.int32, (1, n_keys), 1)
    pad_pen = jnp.where(first_key + col >= 0, 0.0, NEG)
    kcat = jnp.concatenate([r[0, 0] for r in k_refs], axis=0)
    vcat = jnp.concatenate([r[0, 0] for r in v_refs], axis=0)
    if use_sinks:
        sink_row = lax.broadcasted_iota(jnp.int32, (n_keys, 1), 0) == 0
        kcat = jnp.where(sink_row, jnp.zeros_like(kcat), kcat)
        vcat = jnp.where(sink_row, jnp.zeros_like(vcat), vcat)
        pad_pen = jnp.where(col == 0, 0.0, pad_pen)
    outs = []
    for g in range(KV_GROUPS):
        own_half = _lane_half_mask(g)
        qop = jnp.concatenate([jnp.where(own_half, q_ref[p, 0], jnp.zeros_like(q_ref[p, 0]))
                               for p in range(N_QBLK)], axis=0)
        hs = slice(g * HEADS_PER_GROUP, (g + 1) * HEADS_PER_GROUP)
        s = _dot_nt(qop, kcat) + bias_ref[hs].reshape(rows, n_keys) + pad_pen
        e = jnp.exp2(s - jnp.max(s, axis=-1, keepdims=True))
        acc = _dot(e.astype(BF16), jnp.where(own_half, vcat, jnp.ones_like(vcat)))
        outs.append(_normalise(acc))
    _merge_group_outputs(o_ref, outs[0], outs[1])


def _banded_attention(qkv, cb_q, cb_k, cb_v, bias, use_sinks):
    _, b, t, _ = qkv.shape
    tq = bias.shape[1]
    n_blk = bias.shape[2] // Q_BLOCK
    per_tile = tq // Q_BLOCK

    def kv_spec(cb, back):
        return pl.BlockSpec((1, 1, Q_BLOCK, LANES),
                            lambda bi, i: (cb, bi, jnp.maximum((i + 1) * per_tile - 1 - back, 0), 0))

    backs = [n_blk - 1 - jb for jb in range(n_blk)]
    return pl.pallas_call(
        functools.partial(_banded_kernel, n_blk=n_blk, use_sinks=use_sinks),
        grid=(b, t // tq),
        in_specs=([pl.BlockSpec((N_QBLK, 1, tq, LANES), lambda bi, i: (cb_q // N_QBLK, bi, i, 0))]
                  + [kv_spec(cb_k, back) for back in backs] + [kv_spec(cb_v, back) for back in backs]
                  + [pl.BlockSpec(bias.shape, lambda bi, i: (0, 0, 0))]),
        out_specs=pl.BlockSpec((N_QBLK, 1, tq, LANES), lambda bi, i: (0, bi, i, 0)),
        out_shape=jax.ShapeDtypeStruct((N_QBLK, b, t, LANES), BF16),
        compiler_params=_cparams("parallel", "parallel"),
        name="banded_attention",
    )(qkv, *([qkv] * (2 * n_blk)), bias)


FOX_EXTRA = 6


def _fox_prep_kernel(g_ref, fb_ref, pq_ref, pk_ref, oq_ref, ok_ref, xq_ref, xk_ref, carry_scr):
    @pl.when(pl.program_id(1) == 0)
    def _():
        carry_scr[...] = jnp.zeros_like(carry_scr)

    x = g_ref[...] + fb_ref[...]
    logf = jnp.minimum(x, 0.0) - jnp.log(1.0 + jnp.exp(-jnp.abs(x)))
    ch = x.shape[0]
    lower = jnp.where(lax.broadcasted_iota(jnp.int32, (ch, ch), 0)
                      >= lax.broadcasted_iota(jnp.int32, (ch, ch), 1), 1.0, 0.0).astype(BF16)

    def split3(v):
        hi, rest = v.astype(BF16), v - v.astype(BF16).astype(F32)
        return (hi,) + _split_hi_lo(rest)

    cs = sum(_dot(lower, part) for part in split3(logf)) + carry_scr[0:1, :]
    carry_scr[0:1, :] = cs[ch - 1:ch, :]
    parts = split3(cs * LOG2E)
    xq = sum(_dot(part, pq_ref[n]) for n, part in enumerate(parts)) + oq_ref[...]
    xk = sum(_dot(part, pk_ref[n]) for n, part in enumerate(parts)) + ok_ref[...]
    for p in range(N_QBLK):
        xq_ref[p, 0] = xq[:, p * LANES:(p + 1) * LANES].astype(xq_ref.dtype)
        xk_ref[p, 0] = xk[:, p * LANES:(p + 1) * LANES].astype(xk_ref.dtype)


def _fox_prep(graw, fb_row, b, t, chunk=512):
    chunk = min(chunk, t)
    nch = t // chunk
    pq = np.zeros((3, LANES, N_QBLK * LANES), np.float32)
    pk = np.zeros((3, LANES, N_QBLK * LANES), np.float32)
    oq = np.zeros((1, N_QBLK * LANES), np.float32)
    ok = np.zeros((1, N_QBLK * LANES), np.float32)
    for h in range(N_HEADS):
        col = (h // 2) * LANES + (HEAD_DIM if h % 2 == 0 else 0)
        for n in range(3):
            pq[n, N_GATE_COLS + h, col + n] = 1.0
            pk[n, N_GATE_COLS + h, col + 3 + n] = -1.0
        oq[0, col + 3:col + FOX_EXTRA] = 1.0
        ok[0, col:col + 3] = 1.0
    const = lambda a: pl.BlockSpec(a.shape, lambda bi, c: (0,) * a.ndim)
    out_spec = pl.BlockSpec((N_QBLK, 1, chunk, LANES), lambda bi, c: (0, bi, c, 0))
    return pl.pallas_call(
        _fox_prep_kernel,
        grid=(b, nch),
        in_specs=[pl.BlockSpec((chunk, LANES), lambda bi, c: (bi * nch + c, 0)),
                  const(fb_row), const(pq), const(pk), const(oq), const(ok)],
        out_specs=[out_spec, out_spec],
        out_shape=[jax.ShapeDtypeStruct((N_QBLK, b, t, LANES), BF16)] * 2,
        scratch_shapes=[pltpu.VMEM((8, LANES), F32)],
        compiler_params=_cparams("parallel", "arbitrary"),
        name="fox_prep",
    )(graw, fb_row, jnp.asarray(pq, BF16), jnp.asarray(pk, BF16), jnp.asarray(oq), jnp.asarray(ok))


def _fox_kernel(jt_ref, it_ref, q_ref, xq_ref, k_ref, xk_ref, v_ref, o_ref,
                qa_scr, ka_scr, va_scr, m_scr, acc_scr, sa_scr, sb_scr, *, tq, n_off):
    nq = q_ref.shape[2] // tq
    low = _lane_half_mask(0)
    _augment_keys_values(ka_scr, va_scr, k_ref[0, 0], xk_ref[0, 0], v_ref[0, 0])
    q2, xq = q_ref[0, 0], xq_ref[0, 0]
    qa_scr[0] = jnp.where(low, q2, xq)
    qa_scr[1] = jnp.where(low, xq, q2)
    causal_pen = jnp.where(lax.broadcasted_iota(jnp.int32, (tq, tq), 0)
                           >= lax.broadcasted_iota(jnp.int32, (tq, tq), 1), 0.0, NEG)

    def tile(n):
        return pl.ds(pl.multiple_of(n * tq, tq), tq)

    def logits(j, i):
        return tuple(_dot_nt(qa_scr[h, tile(i), :], ka_scr[h, tile(j), :]) for h in range(2))

    def diag_update(i, s):
        for h in range(2):
            m_scr[i, h], acc_scr[i, h] = _flash_start(s[h] + causal_pen, va_scr[h, tile(i), :])

    _worklist_sweep(nq, lambda n: logits(n, n), diag_update, sa_scr, sb_scr)

    def off_update(n, s):
        j, i = jt_ref[n], it_ref[n]
        for h in range(2):
            m_scr[i, h], acc_scr[i, h] = _flash_update(s[h], m_scr[i, h], acc_scr[i, h], va_scr[h, tile(j), :])

    _worklist_sweep(n_off, lambda n: logits(jt_ref[n], it_ref[n]), off_update, sa_scr, sb_scr)

    def finish(i, _):
        o_ref[0, 0, tile(i), :] = jnp.where(low, _normalise(acc_scr[i, 0]), _normalise(acc_scr[i, 1])).astype(o_ref.dtype)
        return 0

    lax.fori_loop(0, nq, finish, 0)


def _fox_attention(qkv, xq, xk, tq):
    _, b, t, _ = qkv.shape
    nq = t // tq
    pairs = [(j, i) for j in range(nq) for i in range(j + 1, nq)]
    n_off = len(pairs)
    pairs += [(0, 0)] * max(0, 2 - n_off)
    jt = jnp.asarray([p[0] for p in pairs], jnp.int32)
    it = jnp.asarray([p[1] for p in pairs], jnp.int32)
    smem = pl.BlockSpec(memory_space=pltpu.SMEM)
    seq = lambda cb: pl.BlockSpec((1, 1, t, LANES), lambda bi, p: (cb + p, bi, 0, 0))
    return pl.pallas_call(
        functools.partial(_fox_kernel, tq=tq, n_off=n_off),
        grid=(b, N_QBLK),
        in_specs=[smem, smem, seq(CB_CQ), seq(0), seq(CB_CK), seq(0), seq(CB_CV)],
        out_specs=pl.BlockSpec((1, 1, t, LANES), lambda bi, p: (p, bi, 0, 0)),
        out_shape=jax.ShapeDtypeStruct((N_QBLK, b, t, LANES), BF16),
        scratch_shapes=[pltpu.VMEM((2, t, LANES), BF16), pltpu.VMEM((2, t, LANES), BF16),
                        pltpu.VMEM((2, t, LANES), BF16),
                        pltpu.VMEM((nq, 2, tq, LANES), F32), pltpu.VMEM((nq, 2, tq, LANES), F32),
                        pltpu.VMEM((2, tq, tq), F32), pltpu.VMEM((2, tq, tq), F32)],
        compiler_params=_cparams("parallel", "parallel"),
        name="fox_attention",
    )(jt, it, qkv, xq, qkv, xk, qkv)


def _merge_kernel(ocmp_ref, oslc_ref, owin_ref, ob_ref, oc_ref, gs_ref, eg_ref, mg_ref, wb_ref, wo_ref, x_ref, o_ref):
    def cat(ref):
        return jnp.concatenate([ref[p] for p in range(N_QBLK)], axis=-1)

    ga = _dot(gs_ref[...], eg_ref[...])
    oa = sum(ga[:, n * MIX_WIDTH:(n + 1) * MIX_WIDTH] * cat(ref).astype(F32)
             for n, ref in enumerate((ocmp_ref, oslc_ref, owin_ref))).astype(BF16)
    branches = (oa, cat(ob_ref), cat(oc_ref))
    d = o_ref.shape[1]
    chunks = [slice(c, c + MM_CHUNK) for c in range(0, d, MM_CHUNK)]
    mix = jnp.concatenate(
        [sum(mg_ref[n, :, cols].astype(F32) * _dot(branches[n], wb_ref[n, :, cols]) for n in range(N_BRANCH))
         .astype(BF16) for cols in chunks], axis=-1)
    for cols in chunks:
        o_ref[:, cols] = x_ref[:, cols] + _dot(mix, wo_ref[:, cols])


def _merge(ocmp, oslc, owin, ob, oc, gsig, eg, mg, wb, wo, x2, tm=512):
    bt, d = x2.shape
    o_spec = pl.BlockSpec((N_QBLK, tm, LANES), lambda i: (0, i, 0))
    return pl.pallas_call(
        _merge_kernel,
        grid=(bt // tm,),
        in_specs=[o_spec] * 5 + [pl.BlockSpec((tm, LANES), lambda i: (i, 0)),
                                 pl.BlockSpec(eg.shape, lambda i: (0, 0)),
                                 pl.BlockSpec((N_BRANCH, tm, d), lambda i: (0, i, 0)),
                                 pl.BlockSpec(wb.shape, lambda i: (0, 0, 0)),
                                 pl.BlockSpec(wo.shape, lambda i: (0, 0)),
                                 pl.BlockSpec((tm, d), lambda i: (i, 0))],
        out_specs=pl.BlockSpec((tm, d), lambda i: (i, 0)),
        out_shape=jax.ShapeDtypeStruct((bt, d), F32),
        compiler_params=_cparams("parallel"),
        name="branch_merge",
    )(ocmp, oslc, owin, ob, oc, gsig, eg, mg, wb, wo, x2)


FFN_TF = 1024


def _ffn_kernel(x_ref, xh_ref, g_ref, gn_ref, wg_ref, wu_ref, cw_ref, cb_ref, wd_ref, o_ref, *rest,
                tm, tiles_per_seq, emit_next):
    hn_ref = rest[0] if emit_next else None
    h_scr, hh_scr, a_scr, acc_scr = rest[-4:]
    i = pl.program_id(0)
    f = pl.program_id(1)

    def norm(x, gain_ref):
        ms = jnp.mean(x * x, axis=-1, keepdims=True)
        return ((x * lax.rsqrt(ms + RMS_EPS)) * gain_ref[...]).astype(BF16)

    @pl.when(f == 0)
    def _():
        h_scr[...] = norm(x_ref[...], g_ref)
        hh_scr[...] = norm(xh_ref[...], g_ref)
        acc_scr[...] = jnp.zeros_like(acc_scr)

    a_scr[0:8, :] = _dot(hh_scr[...], wg_ref[...]) * jnp.where(i % tiles_per_seq == 0, 0.0, 1.0)
    acts = []
    for c in range(wg_ref.shape[1] // MM_CHUNK):
        cols = slice(c * MM_CHUNK, (c + 1) * MM_CHUNK)
        a = _dot(h_scr[...], wg_ref[:, cols])
        a_scr[8:8 + tm, cols] = a
        conv = (cw_ref[0:1, cols] * a_scr[6:6 + tm, cols] + cw_ref[1:2, cols] * a_scr[7:7 + tm, cols]
                + cw_ref[2:3, cols] * a + cb_ref[:, cols])
        acts.append((_gelu_tanh(conv) * _dot(h_scr[...], wu_ref[:, cols])).astype(BF16))
    acc_scr[...] += _dot(jnp.concatenate(acts, axis=-1), wd_ref[...])

    @pl.when(f == pl.num_programs(1) - 1)
    def _():
        out = x_ref[...] + acc_scr[...]
        o_ref[...] = out
        if emit_next:
            hn_ref[...] = norm(out, gn_ref)


def _ffn(x2, g, g_next, wg, wu, cw, cb, wd, t, tm=1024, tf=FFN_TF):
    bt, d = x2.shape
    emit_next = g_next is not None
    g_next = g_next if emit_next else g
    pad = -wg.shape[1] % tf
    wg, wu, cw = [jnp.pad(a, ((0, 0), (0, pad))) for a in (wg, wu, cw)]
    cb, wd = jnp.pad(cb, (0, pad)), jnp.pad(wd, ((0, pad), (0, 0)))
    ff = wg.shape[1]
    halo_blocks = tm // 8
    return pl.pallas_call(
        functools.partial(_ffn_kernel, tm=tm, tiles_per_seq=t // tm, emit_next=emit_next),
        grid=(bt // tm, ff // tf),
        in_specs=[pl.BlockSpec((tm, d), lambda i, f: (i, 0)),
                  pl.BlockSpec((8, d), lambda i, f: (jnp.maximum(i * halo_blocks - 1, 0), 0)),
                  pl.BlockSpec((1, d), lambda i, f: (0, 0)),
                  pl.BlockSpec((1, d), lambda i, f: (0, 0)),
                  pl.BlockSpec((d, tf), lambda i, f: (0, f)),
                  pl.BlockSpec((d, tf), lambda i, f: (0, f)),
                  pl.BlockSpec((CONV_WIDTH, tf), lambda i, f: (0, f)),
                  pl.BlockSpec((1, tf), lambda i, f: (0, f)),
                  pl.BlockSpec((tf, d), lambda i, f: (f, 0))],
        out_specs=[pl.BlockSpec((tm, d), lambda i, f: (i, 0))] * (1 + emit_next),
        out_shape=[jax.ShapeDtypeStruct((bt, d), F32), jax.ShapeDtypeStruct((bt, d), BF16)][:1 + emit_next],
        scratch_shapes=[pltpu.VMEM((tm, d), BF16), pltpu.VMEM((8, d), BF16),
                        pltpu.VMEM((tm + 8, tf), F32), pltpu.VMEM((tm, d), F32)],
        compiler_params=_cparams("parallel", "arbitrary"),
        name="conv_ffn",
    )(x2, x2, g.reshape(1, d), g_next.reshape(1, d), wg, wu, cw, cb.reshape(1, ff), wd)


def _t5_bucket_np(dist):
    max_exact = N_BUCKETS // 2
    d = np.maximum(dist, 0)
    ratio = np.log(np.maximum(d, 1) / max_exact) / math.log(MAX_DISTANCE / max_exact)
    large = np.minimum(max_exact + (ratio * (N_BUCKETS - max_exact)).astype(np.int64), N_BUCKETS - 1)
    return np.where(d < max_exact, d, large)


def _expand_kernel(tab_ref, idx_ref, o_ref):
    tab = tab_ref[...]
    onehot = jnp.where(lax.broadcasted_iota(jnp.int32, (tab.shape[1], idx_ref.shape[1]), 0) == idx_ref[...],
                       1.0, 0.0).astype(BF16)
    hi, rest = tab.astype(BF16), tab - tab.astype(BF16).astype(F32)
    mid, lo = _split_hi_lo(rest)
    o_ref[...] = _dot(hi, onehot) + _dot(mid, onehot) + _dot(lo, onehot)


def _bias_from_dist(tab_t, dist, valid, chunk=4096):
    heads = tab_t.shape[0]
    tab = jnp.zeros((heads, LANES), F32).at[:, :N_BUCKETS].set(tab_t).at[:, N_BUCKETS].set(NEG)
    idx = np.where(valid, _t5_bucket_np(dist), N_BUCKETS).reshape(1, -1).astype(np.int32)
    n = idx.shape[1]
    chunk = min(chunk, n)
    out = pl.pallas_call(
        _expand_kernel,
        grid=(n // chunk,),
        in_specs=[pl.BlockSpec((heads, LANES), lambda i: (0, 0)), pl.BlockSpec((1, chunk), lambda i: (0, i))],
        out_specs=pl.BlockSpec((heads, chunk), lambda i: (0, i)),
        out_shape=jax.ShapeDtypeStruct((heads, n), F32),
        compiler_params=_cparams("parallel"),
        name="rel_bias_expand",
    )(tab, jnp.asarray(idx))
    return out.reshape((heads,) + dist.shape)


def _cmp_bias_kernel(g_ref, o_ref):
    i = pl.program_id(0)
    ncp = o_ref.shape[2]
    for h in range(o_ref.shape[0]):
        o_ref[h] = pltpu.roll(g_ref[h], i * (Q_BLOCK // CMP_STRIDE), 1)[:, :ncp]


def _cmp_bias(g_tab, t):
    heads, _, width = g_tab.shape
    ncp = t // CMP_STRIDE
    return pl.pallas_call(
        _cmp_bias_kernel,
        grid=(t // Q_BLOCK,),
        in_specs=[pl.BlockSpec(g_tab.shape, lambda i: (0, 0, 0))],
        out_specs=pl.BlockSpec((heads, Q_BLOCK, ncp), lambda i: (0, i, 0)),
        out_shape=jax.ShapeDtypeStruct((heads, t, ncp), F32),
        compiler_params=_cparams("parallel"),
        name="cmp_bias_build",
    )(g_tab)


def _static_tables(t):
    ncp = t // CMP_STRIDE
    n_cmp = (t - CMP_LEN) // CMP_STRIDE + 1
    n_slc = t // SLC_LEN
    lane = np.arange(MM_CHUNK)
    bd = (lane[:, None] // HEAD_DIM == lane[None, :] // HEAD_DIM).astype(np.float32)
    eg = np.zeros((LANES, N_BRANCH * MIX_WIDTH), np.float32)
    for h in range(N_HEADS):
        base = (h % N_QBLK) * LANES + (h // N_QBLK) * HEAD_DIM
        for n in range(N_BRANCH):
            eg[h * N_BRANCH + n, n * MIX_WIDTH + base:n * MIX_WIDTH + base + HEAD_DIM] = 1.0
    c_start = np.arange(n_cmp) * CMP_STRIDE
    s_start = np.arange(n_slc) * SLC_LEN
    overlap = ((c_start[:, None] < s_start[None, :] + SLC_LEN) & (c_start[:, None] + CMP_LEN > s_start[None, :]))
    ovt = np.zeros((LANES, ncp), np.float32)
    ovt[:n_slc, :n_cmp] = overlap.T
    key_block = np.zeros((t, LANES), np.float32)
    key_block[np.arange(t), np.arange(t) // SLC_LEN] = 1.0
    key_block[np.arange(t), HEAD_DIM + np.arange(t) // SLC_LEN] = 1.0
    as_bf16 = lambda a: jnp.asarray(a, BF16)
    return dict(bd=as_bf16(bd), eg=as_bf16(eg), ovt=as_bf16(ovt), key_block=as_bf16(key_block), n_slc=n_slc)


def _bias_tables(rel_bias, t):
    n_cmp = (t - CMP_LEN) // CMP_STRIDE + 1
    ncp = t // CMP_STRIDE
    tab_a = rel_bias[:, :N_HEADS].T * LOG2E
    tab_b = rel_bias[:, N_HEADS:].T * LOG2E
    r = np.arange(Q_BLOCK)[:, None]

    def band(window):
        n_prev = -(-(window - 1) // Q_BLOCK)
        n_keys = n_prev * Q_BLOCK + BAND_TQ
        dist = (n_keys - BAND_TQ) + np.arange(BAND_TQ)[:, None] - np.arange(n_keys)[None, :]
        valid = (dist >= 0) & (dist < window)
        assert not valid[:, 0].any()
        return dist, valid

    m = np.arange(2 * ncp)
    m = np.where(m < ncp, m, m - 2 * ncp)[None, :]
    dist_c = r - (m * CMP_STRIDE + CMP_LEN - 1)
    assert n_cmp == ncp - 1 and (dist_c[:, m[0] == Q_BLOCK // CMP_STRIDE - 1] < 0).all()
    tile = np.arange(SLC_TILE)
    dist_n = np.stack([SLC_TILE * (1 - which) + tile[:, None] - tile[None, :] for which in range(2)])
    near = _bias_from_dist(tab_a - tab_a[:, N_BUCKETS - 1:], dist_n, dist_n >= 0)
    return dict(win=_bias_from_dist(tab_a, *band(NSA_WINDOW)),
                swa=_bias_from_dist(tab_b, *band(SWA_WINDOW)),
                cmp=_cmp_bias(_bias_from_dist(tab_a, dist_c, dist_c >= 0), t),
                near=jnp.transpose(near, (1, 0, 2, 3)))


def _pair_cols(w):
    parts = []
    for p in range(N_QBLK):
        parts += [w[..., p * HEAD_DIM:(p + 1) * HEAD_DIM],
                  w[..., (N_QBLK + p) * HEAD_DIM:(N_QBLK + p + 1) * HEAD_DIM]]
    return jnp.concatenate(parts, axis=-1)


def _prep_layer(w_in, qk_gain, w_branch):
    kv = KV_GROUPS * HEAD_DIM
    widths = [('a_q', MIX_WIDTH), ('a_kc', kv), ('a_vc', kv), ('a_ks', kv), ('a_vs', kv), ('a_kw', kv), ('a_vw', kv),
              ('a_gate', N_GATE_COLS), ('b_q', MIX_WIDTH), ('b_k', kv), ('b_v', kv),
              ('c_q', MIX_WIDTH), ('c_k', MIX_WIDTH), ('c_v', MIX_WIDTH), ('c_f', N_HEADS),
              ('merge', N_BRANCH * w_in.shape[0])]
    cols, off = {}, 0
    for name, w in widths:
        cols[name] = w_in[:, off:off + w]
        off += w
    scale = HEAD_DIM ** -0.5 * LOG2E
    tile = lambda gvec, n: jnp.tile(gvec, n)
    zeros = lambda n: jnp.zeros((n,), F32)
    ones = lambda n: jnp.ones((n,), F32)
    pieces = [(_pair_cols(cols['a_q']), tile(qk_gain[0] * scale, N_HEADS), ones(MIX_WIDTH)),
              (_pair_cols(cols['b_q']), tile(qk_gain[2] * scale, N_HEADS), ones(MIX_WIDTH)),
              (cols['c_q'], tile(qk_gain[4] * scale, N_HEADS), ones(MIX_WIDTH)),
              (cols['c_k'], tile(qk_gain[5], N_HEADS), ones(MIX_WIDTH)),
              (cols['a_ks'], tile(qk_gain[1], KV_GROUPS), ones(kv)),
              (cols['a_kw'], tile(qk_gain[1], KV_GROUPS), ones(kv)),
              (cols['b_k'], tile(qk_gain[3], KV_GROUPS), ones(kv)),
              (cols['a_vs'], zeros(kv), zeros(kv)),
              (cols['c_v'], zeros(MIX_WIDTH), zeros(MIX_WIDTH)),
              (cols['a_kc'], zeros(kv), zeros(kv)), (cols['a_vc'], zeros(kv), zeros(kv)),
              (cols['a_vw'], zeros(kv), zeros(kv)), (cols['b_v'], zeros(kv), zeros(kv))]
    w_qkv = jnp.concatenate([p[0] for p in pieces], axis=1).astype(BF16)
    n_qkv = w_qkv.shape[1]
    aux = jnp.zeros((8, n_qkv), F32)
    aux = aux.at[0].set(jnp.concatenate([p[1] for p in pieces])).at[1].set(jnp.concatenate([p[2] for p in pieces]))
    d = w_in.shape[0]
    w_gates = jnp.concatenate([cols['a_gate'], cols['c_f'],
                               jnp.zeros((d, LANES - N_GATE_COLS - N_HEADS), F32)], axis=1).astype(BF16)
    wb = jnp.stack([_pair_cols(w_branch[0].T).T, _pair_cols(w_branch[1].T).T, w_branch[2]]).astype(BF16)
    return w_qkv, aux, w_gates, cols['merge'].astype(BF16), wb


def _prep_compress(cmp_pos, cmp_w1, cmp_w2, k_gain):
    def both_groups(w):
        z = jnp.zeros_like(w)
        return jnp.concatenate([jnp.concatenate([w, z], -1), jnp.concatenate([z, w], -1)], -2)

    pos = jnp.concatenate([cmp_pos, cmp_pos], -1)
    w1 = both_groups(cmp_w1.reshape(2, CMP_LEN, HEAD_DIM, CMP_HIDDEN)).astype(BF16)
    w2 = both_groups(cmp_w2).astype(BF16)
    aux = jnp.zeros((2, 8, LANES), F32)
    aux = aux.at[0, 0].set(jnp.tile(k_gain, KV_GROUPS)).at[0, 1].set(1.0)
    return pos, w1, w2, aux


def kernel(x, rel_bias, norm_mix, norm_ffn, w_in, forget_bias, qk_gain, cmp_pos, cmp_w1, cmp_w2, sinks, w_branch, w_out, w_gate, w_up, conv_w, conv_b, w_down):
    b, t, d = x.shape
    bt = b * t
    depth = w_in.shape[0]
    tabs = _static_tables(t)
    biases = _bias_tables(rel_bias, t)
    fox_tq = min(512, t)
    x2 = x.reshape(bt, d)
    h = _rmsnorm(x2, norm_mix[0])
    for l in range(depth):
        w_qkv, aux, w_gates, w_merge, wb = _prep_layer(w_in[l], qk_gain[l], w_branch[l])
        qkv = _inproj_qkv(h, w_qkv, aux, tabs['bd']).reshape(N_CB, b, t, LANES)
        graw, gsig = _inproj_gates(h, w_gates)
        mg = _inproj_merge(h, w_merge, d)

        cmp_kv = _compress(qkv, *_prep_compress(cmp_pos[l], cmp_w1[l], cmp_w2[l], qk_gain[l, 1]), tabs['bd'])
        o_cmp, qa = _cmp_select(qkv, cmp_kv, biases['cmp'], tabs['ovt'], tabs['n_slc'])
        o_slc = _slc_attention(qa, qkv, tabs['key_block'], biases['near'])
        o_win = _banded_attention(qkv, CB_AQ, CB_AKW, CB_AVW, biases['win'], False)

        swa_bias = biases['swa'].at[:, :, 0].set(sinks[l][:, None] * LOG2E)
        o_b = _banded_attention(qkv, CB_BQ, CB_BK, CB_BV, swa_bias, True)

        fb_row = jnp.zeros((1, LANES), F32).at[0, N_GATE_COLS:N_GATE_COLS + N_HEADS].set(forget_bias[l])
        o_c = _fox_attention(qkv, *_fox_prep(graw, fb_row, b, t), fox_tq)

        flat = lambda o: o.reshape(N_QBLK, bt, LANES)
        x2 = _merge(flat(o_cmp), flat(o_slc), flat(o_win), flat(o_b), flat(o_c), gsig, tabs['eg'], mg, wb,
                    w_out[l].astype(BF16), x2)
        x2, *h = _ffn(x2, norm_ffn[l], norm_mix[l + 1] if l + 1 < depth else None, w_gate[l].astype(BF16),
                      w_up[l].astype(BF16), conv_w[l], conv_b[l], w_down[l].astype(BF16), t)
        h = h[0] if h else None
    return x2.reshape(b, t, d)
```

```python
import functools
import math

import numpy as np
import jax
import jax.numpy as jnp
from jax import lax
from jax.experimental import pallas as pl
from jax.experimental.pallas import tpu as pltpu

F32 = jnp.float32
BF16 = jnp.bfloat16

HEAD_DIM = 64
LANES = 128
N_HEADS = 8
KV_GROUPS = 2
HEADS_PER_GROUP = N_HEADS // KV_GROUPS
N_QBLK = N_HEADS * HEAD_DIM // LANES
MIX_WIDTH = N_HEADS * HEAD_DIM
N_BRANCH = 3
Q_BLOCK = 128
CMP_LEN = 32
CMP_STRIDE = 16
CMP_HIDDEN = 256
SLC_LEN = 64
SLC_TOPK = 8
NSA_WINDOW = 256
SWA_WINDOW = 128
N_BUCKETS = 32
MAX_DISTANCE = 128
CONV_WIDTH = 3
RMS_EPS = 1e-6
LOG2E = math.log2(math.e)
NEG = -1e30
FORCE_SCORE = 1e9
TAKEN = -3e38
SLC_TILE = 256
VMEM_LIMIT = 48 * 1024 * 1024

CB_AQ, CB_BQ, CB_CQ, CB_CK = 0, 4, 8, 12
CB_AKS, CB_AKW, CB_BK, CB_AVS = 16, 17, 18, 19
CB_CV, CB_AKC, CB_AVC, CB_AVW, CB_BV = 20, 24, 25, 26, 27
N_CB = 28
N_NORM_CB = 20
MM_CHUNK = 256
MM_ROWS = 512
N_GATE_COLS = N_HEADS * N_BRANCH


def _cparams(*sem):
    return pltpu.CompilerParams(dimension_semantics=sem, vmem_limit_bytes=VMEM_LIMIT)


def _split_hi_lo(v):
    hi = v.astype(BF16)
    return hi, (v - hi.astype(F32)).astype(BF16)


def _dot(a, b):
    return jnp.dot(a, b, preferred_element_type=F32)


def _dot_nt(a, b):
    return lax.dot_general(a, b, (((1,), (1,)), ((), ())), preferred_element_type=F32)


def _gelu_tanh(x):
    return x * (0.5 * (1.0 + jnp.tanh(math.sqrt(2.0 / math.pi) * (x + 0.044715 * (x * x * x)))))


def _head_rms(y, bd, gain):
    ms = _dot((y * y).astype(BF16), bd) * (1.0 / HEAD_DIM)
    return (y * lax.rsqrt(ms + RMS_EPS)) * gain


def _lane_half_mask(g):
    return (lax.broadcasted_iota(jnp.int32, (1, LANES), 1) // HEAD_DIM) == g


def _merge_group_outputs(o_ref, o_g0, o_g1):
    low = _lane_half_mask(0)
    tq = o_ref.shape[2]
    for p in range(N_QBLK):
        rows = slice(p * tq, (p + 1) * tq)
        o_ref[p, 0] = jnp.where(low, o_g0[rows], o_g1[rows]).astype(o_ref.dtype)


def _rmsnorm_kernel(x_ref, g_ref, o_ref):
    x = x_ref[...]
    ms = jnp.mean(x * x, axis=-1, keepdims=True)
    o_ref[...] = ((x * lax.rsqrt(ms + RMS_EPS)) * g_ref[...]).astype(o_ref.dtype)


def _rmsnorm(x2, g, tm=2048):
    bt, d = x2.shape
    tm = min(tm, bt)
    return pl.pallas_call(
        _rmsnorm_kernel,
        grid=(bt // tm,),
        in_specs=[pl.BlockSpec((tm, d), lambda i: (i, 0)), pl.BlockSpec((1, d), lambda i: (0, 0))],
        out_specs=pl.BlockSpec((tm, d), lambda i: (i, 0)),
        out_shape=jax.ShapeDtypeStruct((bt, d), BF16),
        compiler_params=_cparams("parallel"),
        name="rmsnorm",
    )(x2, g.reshape(1, d))


def _inproj_qkv_kernel(h_ref, w_ref, aux_ref, bd_ref, o_ref, *, n_norm_tiles):
    per_chunk = MM_CHUNK // LANES

    def emit(normalise):
        for r in range(0, h_ref.shape[0], MM_ROWS):
            rows = slice(r, r + MM_ROWS)
            y_all = _dot(h_ref[rows, :], w_ref[...])
            for c in range(w_ref.shape[1] // MM_CHUNK):
                cols = slice(c * MM_CHUNK, (c + 1) * MM_CHUNK)
                y = y_all[:, cols]
                if normalise:
                    normed = _head_rms(y, bd_ref[...], aux_ref[0:1, cols])
                    y = jnp.where(aux_ref[1:2, cols] > 0.5, normed, y)
                for s in range(per_chunk):
                    o_ref[c * per_chunk + s, rows, :] = y[:, s * LANES:(s + 1) * LANES].astype(o_ref.dtype)

    pl.when(pl.program_id(1) < n_norm_tiles)(lambda: emit(True))
    pl.when(pl.program_id(1) >= n_norm_tiles)(lambda: emit(False))


def _inproj_qkv(h, w, aux, bd, tm=2048, tn=512):
    bt, d = h.shape
    n = w.shape[1]
    n_sub = tn // LANES
    tm = min(tm, bt)
    assert N_NORM_CB % n_sub == 0 and tm % MM_ROWS == 0
    return pl.pallas_call(
        functools.partial(_inproj_qkv_kernel, n_norm_tiles=N_NORM_CB // n_sub),
        grid=(bt // tm, n // tn),
        in_specs=[pl.BlockSpec((tm, d), lambda i, j: (i, 0)),
                  pl.BlockSpec((d, tn), lambda i, j: (0, j)),
                  pl.BlockSpec((8, tn), lambda i, j: (0, j)),
                  pl.BlockSpec(bd.shape, lambda i, j: (0, 0))],
        out_specs=pl.BlockSpec((n_sub, tm, LANES), lambda i, j: (j, i, 0)),
        out_shape=jax.ShapeDtypeStruct((n // LANES, bt, LANES), BF16),
        compiler_params=_cparams("parallel", "arbitrary"),
        name="inproj_qkv",
    )(h, w, aux, bd)


def _inproj_gates_kernel(h_ref, w_ref, raw_ref, sig_ref):
    y = _dot(h_ref[...], w_ref[...])
    raw_ref[...] = y
    sig_ref[...] = jax.nn.sigmoid(y).astype(sig_ref.dtype)


def _inproj_gates(h, w, tm=1024):
    bt, d = h.shape
    return pl.pallas_call(
        _inproj_gates_kernel,
        grid=(bt // tm,),
        in_specs=[pl.BlockSpec((tm, d), lambda i: (i, 0)),
                  pl.BlockSpec((d, LANES), lambda i: (0, 0))],
        out_specs=[pl.BlockSpec((tm, LANES), lambda i: (i, 0))] * 2,
        out_shape=[jax.ShapeDtypeStruct((bt, LANES), F32), jax.ShapeDtypeStruct((bt, LANES), BF16)],
        compiler_params=_cparams("parallel"),
        name="inproj_gates",
    )(h, w)


def _inproj_merge_kernel(h_ref, w_ref, o_ref):
    width = 2 * MM_CHUNK
    for c in range(w_ref.shape[1] // width):
        cols = slice(c * width, (c + 1) * width)
        o_ref[0, :, cols] = jax.nn.sigmoid(_dot(h_ref[...], w_ref[:, cols])).astype(o_ref.dtype)


def _inproj_merge(h, w, d_model, tm=1024, tn=1024):
    bt, d = h.shape
    n = w.shape[1]
    per = d_model // tn
    return pl.pallas_call(
        _inproj_merge_kernel,
        grid=(bt // tm, n // tn),
        in_specs=[pl.BlockSpec((tm, d), lambda i, j: (i, 0)),
                  pl.BlockSpec((d, tn), lambda i, j: (0, j))],
        out_specs=pl.BlockSpec((1, tm, tn), lambda i, j: (j // per, i, j % per)),
        out_shape=jax.ShapeDtypeStruct((n // d_model, bt, d_model), BF16),
        compiler_params=_cparams("parallel", "arbitrary"),
        name="inproj_merge",
    )(h, w)


def _compress_kernel(z_ref, pos_ref, w1_ref, w2_ref, aux_ref, bd_ref, o_ref, zf_scr):
    n_chunks = o_ref.shape[2]
    zf_scr[...] = z_ref[0, 0].astype(F32)
    halves = []
    for half in range(CMP_LEN // CMP_STRIDE):
        acc = jnp.zeros((n_chunks, KV_GROUPS * CMP_HIDDEN), F32)
        for l in range(CMP_STRIDE):
            rows = zf_scr[pl.ds(l, n_chunks, stride=CMP_STRIDE), :]
            pos = pos_ref[0, half * CMP_STRIDE + l:half * CMP_STRIDE + l + 1, :]
            acc = acc + _dot((rows + pos).astype(BF16), w1_ref[0, half * CMP_STRIDE + l])
        halves.append(acc)
    hid = _gelu_tanh(halves[0] + pltpu.roll(halves[1], n_chunks - 1, 0))
    y = _dot(hid.astype(BF16), w2_ref[0])
    normed = _head_rms(y, bd_ref[0:LANES, 0:LANES], aux_ref[0, 0:1, :])
    o_ref[0, 0] = jnp.where(aux_ref[0, 1:2, :] > 0.5, normed, y).astype(o_ref.dtype)


def _compress(qkv, pos, w1, w2, aux, bd):
    _, b, t, _ = qkv.shape
    n_chunks = t // CMP_STRIDE
    return pl.pallas_call(
        _compress_kernel,
        grid=(2, b),
        in_specs=[pl.BlockSpec((1, 1, t, LANES), lambda w, i: (CB_AKC + w, i, 0, 0)),
                  pl.BlockSpec((1,) + pos.shape[1:], lambda w, i: (w, 0, 0)),
                  pl.BlockSpec((1,) + w1.shape[1:], lambda w, i: (w, 0, 0, 0)),
                  pl.BlockSpec((1,) + w2.shape[1:], lambda w, i: (w, 0, 0)),
                  pl.BlockSpec((1, 8, LANES), lambda w, i: (w, 0, 0)),
                  pl.BlockSpec(bd.shape, lambda w, i: (0, 0))],
        out_specs=pl.BlockSpec((1, 1, n_chunks, LANES), lambda w, i: (w, i, 0, 0)),
        out_shape=jax.ShapeDtypeStruct((2, b, n_chunks, LANES), BF16),
        scratch_shapes=[pltpu.VMEM((t, LANES), F32)],
        compiler_params=_cparams("parallel", "parallel"),
        name="nsa_compress",
    )(qkv, pos, w1, w2, aux, bd)


CMP_TQ = 256


def _cmp_select_kernel(q_ref, kc_ref, vc_ref, bias_ref, ovt_ref, o_ref, qa_ref, *, n_slc, n_top):
    ncp = kc_ref.shape[2]
    rows = HEADS_PER_GROUP * Q_BLOCK
    jrow = lax.broadcasted_iota(jnp.int32, (n_slc, Q_BLOCK), 0)
    jrow_f = jrow.astype(F32)
    low = _lane_half_mask(0)
    for sub in range(q_ref.shape[2] // Q_BLOCK):
        rs = slice(sub * Q_BLOCK, (sub + 1) * Q_BLOCK)
        t0 = pl.program_id(1) * q_ref.shape[2] + sub * Q_BLOCK
        row_t = t0 + lax.broadcasted_iota(jnp.int32, (Q_BLOCK, min(LANES, ncp)), 0)
        row_valid = jnp.concatenate([row_t >= CMP_LEN - 1] * HEADS_PER_GROUP, axis=0)
        cur = (t0 + lax.broadcasted_iota(jnp.int32, (n_slc, Q_BLOCK), 1)) // SLC_LEN

        def force(imp):
            forced = jnp.where(jrow == cur, FORCE_SCORE, jnp.where(jrow == cur - 1, FORCE_SCORE, imp))
            return jnp.where(jrow == 0, FORCE_SCORE, forced)

        outs = []
        for g in range(KV_GROUPS):
            own_half = _lane_half_mask(g)
            hs = slice(g * HEADS_PER_GROUP, (g + 1) * HEADS_PER_GROUP)
            qop = jnp.concatenate([jnp.where(own_half, q_ref[p, 0, rs, :], jnp.zeros((Q_BLOCK, LANES), q_ref.dtype))
                                   for p in range(N_QBLK)], axis=0)
            s = _dot_nt(qop, kc_ref[0, 0]) + bias_ref[hs, rs, :].reshape(rows, ncp)
            m = _row_max(s)
            e = [jnp.exp2(c - m) for c in _lane_chunks(s)]
            row_sum = jnp.broadcast_to(jnp.sum(sum(e), axis=-1, keepdims=True), e[0].shape)
            inv = jnp.where(row_valid, 1.0 / row_sum, 0.0)
            p = jnp.concatenate([c * inv for c in e], axis=-1)
            outs.append(_dot(p.astype(BF16), vc_ref[0, 0]))
            psum = p[0:Q_BLOCK]
            for r in range(1, HEADS_PER_GROUP):
                psum = psum + p[r * Q_BLOCK:(r + 1) * Q_BLOCK]
            hi, lo = _split_hi_lo(psum)
            imp_t = (_dot_nt(ovt_ref[...], hi) + _dot_nt(ovt_ref[...], lo))[0:n_slc]
            score = jnp.where(jrow <= cur, force(imp_t), NEG)
            sel_bias = jnp.full((n_slc, Q_BLOCK), NEG, F32)
            for _ in range(n_top):
                best = jnp.max(score, axis=0, keepdims=True)
                first = jnp.min(jnp.where(score == best, jrow_f, float(n_slc)), axis=0, keepdims=True)
                hit = jrow_f == first
                sel_bias = jnp.where(hit, jnp.where(best > NEG / 2, 0.0, NEG), sel_bias)
                score = jnp.where(hit, TAKEN, score)
            if n_slc < HEAD_DIM:
                sel_bias = jnp.concatenate([sel_bias, jnp.full((HEAD_DIM - n_slc, Q_BLOCK), NEG, F32)], axis=0)
            sel_t = jnp.concatenate([sel_bias, sel_bias], axis=0).T.astype(qa_ref.dtype)
            for p in range(N_QBLK):
                qa_ref[g * HEADS_PER_GROUP + p, 0, rs, :] = jnp.where(own_half, q_ref[p, 0, rs, :], sel_t)
        for p in range(N_QBLK):
            prs = slice(p * Q_BLOCK, (p + 1) * Q_BLOCK)
            o_ref[p, 0, rs, :] = jnp.where(low, outs[0][prs], outs[1][prs]).astype(o_ref.dtype)


def _cmp_select(qkv, cmp_kv, bias_c, ovt, n_slc, tq=CMP_TQ):
    _, b, t, _ = qkv.shape
    ncp = cmp_kv.shape[2]
    assert n_slc <= HEAD_DIM
    return pl.pallas_call(
        functools.partial(_cmp_select_kernel, n_slc=n_slc, n_top=min(SLC_TOPK, n_slc)),
        grid=(b, t // tq),
        in_specs=[pl.BlockSpec((N_QBLK, 1, tq, LANES), lambda bi, i: (CB_AQ // N_QBLK, bi, i, 0)),
                  pl.BlockSpec((1, 1, ncp, LANES), lambda bi, i: (0, bi, 0, 0)),
                  pl.BlockSpec((1, 1, ncp, LANES), lambda bi, i: (1, bi, 0, 0)),
                  pl.BlockSpec((N_HEADS, tq, ncp), lambda bi, i: (0, i, 0)),
                  pl.BlockSpec((LANES, ncp), lambda bi, i: (0, 0))],
        out_specs=[pl.BlockSpec((N_QBLK, 1, tq, LANES), lambda bi, i: (0, bi, i, 0)),
                   pl.BlockSpec((N_HEADS, 1, tq, LANES), lambda bi, i: (0, bi, i, 0))],
        out_shape=[jax.ShapeDtypeStruct((N_QBLK, b, t, LANES), BF16),
                   jax.ShapeDtypeStruct((N_HEADS, b, t, LANES), BF16)],
        compiler_params=_cparams("parallel", "parallel"),
        name="nsa_cmp_select",
    )(qkv, cmp_kv, cmp_kv, bias_c, ovt)


def _lane_chunks(s):
    width = min(LANES, s.shape[1])
    return [s[:, c:c + width] for c in range(0, s.shape[1], width)]


def _row_max(s):
    chunks = _lane_chunks(s)
    m = chunks[0]
    for c in chunks[1:]:
        m = jnp.maximum(m, c)
    return jnp.broadcast_to(jnp.max(m, axis=-1, keepdims=True), m.shape)


def _exp2_shifted(s, m):
    return jnp.concatenate([jnp.exp2(c - m) for c in _lane_chunks(s)], axis=-1)


def _flash_start(s, v):
    m = _row_max(s)
    return m, _dot(_exp2_shifted(s, m).astype(BF16), v)


def _flash_update(s, m, acc, v):
    m_new = jnp.maximum(m, _row_max(s))
    return m_new, jnp.exp2(m - m_new) * acc + _dot(_exp2_shifted(s, m_new).astype(BF16), v)


def _normalise(acc):
    return acc / pltpu.roll(acc, HEAD_DIM, 1)


def _augment_keys_values(ka_scr, va_scr, k2, xk, v2, base=0):
    low = _lane_half_mask(0)
    ones = jnp.ones_like(v2)
    ka_scr[base] = jnp.where(low, k2, xk)
    ka_scr[base + 1] = jnp.where(low, xk, k2)
    va_scr[base] = jnp.where(low, v2, ones)
    va_scr[base + 1] = jnp.where(low, ones, v2)


def _store_all(scr, s):
    for n, sn in enumerate(s):
        scr[n] = sn


def _load_all(scr):
    return tuple(scr[n] for n in range(scr.shape[0]))


def _worklist_sweep(n_steps, logits, update, sa_scr, sb_scr):
    def single(n, _):
        _store_all(sa_scr, logits(n))
        update(n, _load_all(sa_scr))
        return 0

    odd = n_steps % 2
    lax.fori_loop(0, odd, single, 0)
    _store_all(sa_scr, logits(odd))

    def pair(jp, _):
        n = odd + 2 * jp
        _store_all(sb_scr, logits(n + 1))
        update(n, _load_all(sa_scr))
        _store_all(sa_scr, logits(jnp.minimum(n + 2, n_steps - 1)))
        update(n + 1, _load_all(sb_scr))
        return 0

    lax.fori_loop(0, n_steps // 2, pair, 0)


SLC_QGROUP = 4


def _slc_kernel(nfar_ref, jt_ref, it_ref, qa_ref, k_ref, v_ref, hot_ref, nb_ref, o_ref,
                ka_scr, va_scr, m_scr, acc_scr, sa_scr, sb_scr, *, tq):
    gi = pl.program_id(1)
    qg = qa_ref.shape[2] // tq
    rows = HEADS_PER_GROUP * tq
    _augment_keys_values(ka_scr, va_scr, k_ref[0, 0], hot_ref[...], v_ref[0, 0])
    heads = [slice(g * HEADS_PER_GROUP, (g + 1) * HEADS_PER_GROUP) for g in range(KV_GROUPS)]

    def tile(n):
        return pl.ds(pl.multiple_of(n * tq, tq), tq)

    def logits(j, il):
        return tuple(_dot_nt(qa_ref[heads[g], 0, tile(il), :].reshape(rows, LANES), ka_scr[g, tile(j), :])
                     for g in range(KV_GROUPS))

    def biased(s, which, g):
        return (s.reshape(HEADS_PER_GROUP, tq, tq) + nb_ref[which, heads[g]]).reshape(rows, tq)

    def diag_update(il, s):
        for g in range(KV_GROUPS):
            m_scr[il, g], acc_scr[il, g] = _flash_start(biased(s[g], 1, g), va_scr[g, tile(gi * qg + il), :])

    _worklist_sweep(qg, lambda il: logits(gi * qg + il, il), diag_update, sa_scr, sb_scr)

    def fold(il, g, sg, j):
        m_scr[il, g], acc_scr[il, g] = _flash_update(sg, m_scr[il, g], acc_scr[il, g], va_scr[g, tile(j), :])

    def prev_update(il, s):
        i = gi * qg + il
        pen = jnp.where(i >= 1, 0.0, NEG)
        for g in range(KV_GROUPS):
            fold(il, g, biased(s[g], 0, g) + pen, jnp.maximum(i - 1, 0))

    _worklist_sweep(qg, lambda il: logits(jnp.maximum(gi * qg + il - 1, 0), il), prev_update, sa_scr, sb_scr)

    def far_update(n, s):
        for g in range(KV_GROUPS):
            fold(it_ref[gi, n], g, s[g], jt_ref[gi, n])

    _worklist_sweep(nfar_ref[gi], lambda n: logits(jt_ref[gi, n], it_ref[gi, n]), far_update, sa_scr, sb_scr)

    low = _lane_half_mask(0)
    for il in range(qg):
        o_g0, o_g1 = _normalise(acc_scr[il, 0]), _normalise(acc_scr[il, 1])
        for p in range(N_QBLK):
            head_rows = slice(p * tq, (p + 1) * tq)
            o_ref[p, 0, il * tq:(il + 1) * tq, :] = jnp.where(low, o_g0[head_rows], o_g1[head_rows]).astype(o_ref.dtype)


def _slc_attention(qa, qkv, hot2, near_bias, qg=SLC_QGROUP):
    _, b, t, _ = qa.shape
    tq = near_bias.shape[2]
    nq = t // tq
    qg = min(qg, nq)
    n_groups = nq // qg
    lists = [[(j, il) for j in range(nq) for il in range(qg) if j <= g * qg + il - 2] for g in range(n_groups)]
    width = max(2, max(len(steps) for steps in lists))
    table = lambda k: jnp.asarray([[s[k] for s in steps] + [0] * (width - len(steps)) for steps in lists], jnp.int32)
    nfar = jnp.asarray([len(steps) for steps in lists], jnp.int32)
    smem = pl.BlockSpec(memory_space=pltpu.SMEM)
    rows = HEADS_PER_GROUP * tq
    return pl.pallas_call(
        functools.partial(_slc_kernel, tq=tq),
        grid=(b, n_groups),
        in_specs=[smem, smem, smem,
                  pl.BlockSpec((N_HEADS, 1, qg * tq, LANES), lambda bi, gi: (0, bi, gi, 0)),
                  pl.BlockSpec((1, 1, t, LANES), lambda bi, gi: (CB_AKS, bi, 0, 0)),
                  pl.BlockSpec((1, 1, t, LANES), lambda bi, gi: (CB_AVS, bi, 0, 0)),
                  pl.BlockSpec(hot2.shape, lambda bi, gi: (0, 0)),
                  pl.BlockSpec(near_bias.shape, lambda bi, gi: (0, 0, 0, 0))],
        out_specs=pl.BlockSpec((N_QBLK, 1, qg * tq, LANES), lambda bi, gi: (0, bi, gi, 0)),
        out_shape=jax.ShapeDtypeStruct((N_QBLK, b, t, LANES), BF16),
        scratch_shapes=[pltpu.VMEM((KV_GROUPS, t, LANES), BF16), pltpu.VMEM((KV_GROUPS, t, LANES), BF16),
                        pltpu.VMEM((qg, KV_GROUPS, rows, LANES), F32), pltpu.VMEM((qg, KV_GROUPS, rows, LANES), F32),
                        pltpu.VMEM((KV_GROUPS, rows, tq), F32), pltpu.VMEM((KV_GROUPS, rows, tq), F32)],
        compiler_params=_cparams("parallel", "parallel"),
        name="nsa_slc_attention",
    )(nfar, table(0), table(1), qa, qkv, qkv, hot2, near_bias)


BAND_TQ = 256


def _banded_kernel(*refs, n_blk, use_sinks):
    q_ref = refs[0]
    k_refs = refs[1:1 + n_blk]
    v_refs = refs[1 + n_blk:1 + 2 * n_blk]
    bias_ref, o_ref = refs[1 + 2 * n_blk:]
    i = pl.program_id(1)
    tq = q_ref.shape[2]
    n_keys = n_blk * Q_BLOCK
    rows = HEADS_PER_GROUP * tq
    first_key = (i + 1) * tq - n_keys
    col = lax.broadcasted_iota(jnp.int32, (1, n_keys), 1)
    pad_pen = jnp.where(first_key + col >= 0, 0.0, NEG)
    kcat = jnp.concatenate([r[0, 0] for r in k_refs], axis=0)
    vcat = jnp.concatenate([r[0, 0] for r in v_refs], axis=0)
    if use_sinks:
        sink_row = lax.broadcasted_iota(jnp.int32, (n_keys, 1), 0) == 0
        kcat = jnp.where(sink_row, jnp.zeros_like(kcat), kcat)
        vcat = jnp.where(sink_row, jnp.zeros_like(vcat), vcat)
        pad_pen = jnp.where(col == 0, 0.0, pad_pen)
    outs = []
    for g in range(KV_GROUPS):
        own_half = _lane_half_mask(g)
        qop = jnp.concatenate([jnp.where(own_half, q_ref[p, 0], jnp.zeros_like(q_ref[p, 0]))
                               for p in range(N_QBLK)], axis=0)
        hs = slice(g * HEADS_PER_GROUP, (g + 1) * HEADS_PER_GROUP)
        s = _dot_nt(qop, kcat) + bias_ref[hs].reshape(rows, n_keys) + pad_pen
        e = _exp2_shifted(s, _row_max(s))
        acc = _dot(e.astype(BF16), jnp.where(own_half, vcat, jnp.ones_like(vcat)))
        outs.append(_normalise(acc))
    _merge_group_outputs(o_ref, outs[0], outs[1])


def _banded_attention(qkv, cb_q, cb_k, cb_v, bias, use_sinks):
    _, b, t, _ = qkv.shape
    tq = bias.shape[1]
    n_blk = bias.shape[2] // Q_BLOCK
    per_tile = tq // Q_BLOCK

    def kv_spec(cb, back):
        return pl.BlockSpec((1, 1, Q_BLOCK, LANES),
                            lambda bi, i: (cb, bi, jnp.maximum((i + 1) * per_tile - 1 - back, 0), 0))

    backs = [n_blk - 1 - jb for jb in range(n_blk)]
    return pl.pallas_call(
        functools.partial(_banded_kernel, n_blk=n_blk, use_sinks=use_sinks),
        grid=(b, t // tq),
        in_specs=([pl.BlockSpec((N_QBLK, 1, tq, LANES), lambda bi, i: (cb_q // N_QBLK, bi, i, 0))]
                  + [kv_spec(cb_k, back) for back in backs] + [kv_spec(cb_v, back) for back in backs]
                  + [pl.BlockSpec(bias.shape, lambda bi, i: (0, 0, 0))]),
        out_specs=pl.BlockSpec((N_QBLK, 1, tq, LANES), lambda bi, i: (0, bi, i, 0)),
        out_shape=jax.ShapeDtypeStruct((N_QBLK, b, t, LANES), BF16),
        compiler_params=_cparams("parallel", "parallel"),
        name="banded_attention",
    )(qkv, *([qkv] * (2 * n_blk)), bias)


FOX_EXTRA = 6


def _fox_prep_kernel(g_ref, fb_ref, pq_ref, pk_ref, oq_ref, ok_ref, xq_ref, xk_ref, carry_scr):
    @pl.when(pl.program_id(1) == 0)
    def _():
        carry_scr[...] = jnp.zeros_like(carry_scr)

    x = g_ref[...] + fb_ref[...]
    logf = jnp.minimum(x, 0.0) - jnp.log(1.0 + jnp.exp(-jnp.abs(x)))
    ch = x.shape[0]
    lower = jnp.where(lax.broadcasted_iota(jnp.int32, (ch, ch), 0)
                      >= lax.broadcasted_iota(jnp.int32, (ch, ch), 1), 1.0, 0.0).astype(BF16)

    def split3(v):
        hi, rest = v.astype(BF16), v - v.astype(BF16).astype(F32)
        return (hi,) + _split_hi_lo(rest)

    cs = sum(_dot(lower, part) for part in split3(logf)) + carry_scr[0:1, :]
    carry_scr[0:1, :] = cs[ch - 1:ch, :]
    parts = split3(cs * LOG2E)
    xq = sum(_dot(part, pq_ref[n]) for n, part in enumerate(parts)) + oq_ref[...]
    xk = sum(_dot(part, pk_ref[n]) for n, part in enumerate(parts)) + ok_ref[...]
    for p in range(N_QBLK):
        xq_ref[p, 0] = xq[:, p * LANES:(p + 1) * LANES].astype(xq_ref.dtype)
        xk_ref[p, 0] = xk[:, p * LANES:(p + 1) * LANES].astype(xk_ref.dtype)


def _fox_prep(graw, fb_row, b, t, chunk=512):
    chunk = min(chunk, t)
    nch = t // chunk
    pq = np.zeros((3, LANES, N_QBLK * LANES), np.float32)
    pk = np.zeros((3, LANES, N_QBLK * LANES), np.float32)
    oq = np.zeros((1, N_QBLK * LANES), np.float32)
    ok = np.zeros((1, N_QBLK * LANES), np.float32)
    for h in range(N_HEADS):
        col = (h // 2) * LANES + (HEAD_DIM if h % 2 == 0 else 0)
        for n in range(3):
            pq[n, N_GATE_COLS + h, col + n] = 1.0
            pk[n, N_GATE_COLS + h, col + 3 + n] = -1.0
        oq[0, col + 3:col + FOX_EXTRA] = 1.0
        ok[0, col:col + 3] = 1.0
    const = lambda a: pl.BlockSpec(a.shape, lambda bi, c: (0,) * a.ndim)
    out_spec = pl.BlockSpec((N_QBLK, 1, chunk, LANES), lambda bi, c: (0, bi, c, 0))
    return pl.pallas_call(
        _fox_prep_kernel,
        grid=(b, nch),
        in_specs=[pl.BlockSpec((chunk, LANES), lambda bi, c: (bi * nch + c, 0)),
                  const(fb_row), const(pq), const(pk), const(oq), const(ok)],
        out_specs=[out_spec, out_spec],
        out_shape=[jax.ShapeDtypeStruct((N_QBLK, b, t, LANES), BF16)] * 2,
        scratch_shapes=[pltpu.VMEM((8, LANES), F32)],
        compiler_params=_cparams("parallel", "arbitrary"),
        name="fox_prep",
    )(graw, fb_row, jnp.asarray(pq, BF16), jnp.asarray(pk, BF16), jnp.asarray(oq), jnp.asarray(ok))


def _fox_kernel(jt_ref, it_ref, q_ref, xq_ref, k_ref, xk_ref, v_ref, o_ref,
                qa_scr, ka_scr, va_scr, m_scr, acc_scr, sa_scr, sb_scr, *, tq, n_off):
    nq = q_ref.shape[2] // tq
    low = _lane_half_mask(0)
    _augment_keys_values(ka_scr, va_scr, k_ref[0, 0], xk_ref[0, 0], v_ref[0, 0])
    q2, xq = q_ref[0, 0], xq_ref[0, 0]
    qa_scr[0] = jnp.where(low, q2, xq)
    qa_scr[1] = jnp.where(low, xq, q2)
    causal_pen = jnp.where(lax.broadcasted_iota(jnp.int32, (tq, tq), 0)
                           >= lax.broadcasted_iota(jnp.int32, (tq, tq), 1), 0.0, NEG)

    def tile(n):
        return pl.ds(pl.multiple_of(n * tq, tq), tq)

    def logits(j, i):
        return tuple(_dot_nt(qa_scr[h, tile(i), :], ka_scr[h, tile(j), :]) for h in range(2))

    def diag_update(i, s):
        for h in range(2):
            m_scr[i, h], acc_scr[i, h] = _flash_start(s[h] + causal_pen, va_scr[h, tile(i), :])

    _worklist_sweep(nq, lambda n: logits(n, n), diag_update, sa_scr, sb_scr)

    def off_update(n, s):
        j, i = jt_ref[n], it_ref[n]
        for h in range(2):
            m_scr[i, h], acc_scr[i, h] = _flash_update(s[h], m_scr[i, h], acc_scr[i, h], va_scr[h, tile(j), :])

    _worklist_sweep(n_off, lambda n: logits(jt_ref[n], it_ref[n]), off_update, sa_scr, sb_scr)

    def finish(i, _):
        o_ref[0, 0, tile(i), :] = jnp.where(low, _normalise(acc_scr[i, 0]), _normalise(acc_scr[i, 1])).astype(o_ref.dtype)
        return 0

    lax.fori_loop(0, nq, finish, 0)


def _fox_attention(qkv, xq, xk, tq):
    _, b, t, _ = qkv.shape
    nq = t // tq
    pairs = [(j, i) for j in range(nq) for i in range(j + 1, nq)]
    n_off = len(pairs)
    pairs += [(0, 0)] * max(0, 2 - n_off)
    jt = jnp.asarray([p[0] for p in pairs], jnp.int32)
    it = jnp.asarray([p[1] for p in pairs], jnp.int32)
    smem = pl.BlockSpec(memory_space=pltpu.SMEM)
    seq = lambda cb: pl.BlockSpec((1, 1, t, LANES), lambda bi, p: (cb + p, bi, 0, 0))
    return pl.pallas_call(
        functools.partial(_fox_kernel, tq=tq, n_off=n_off),
        grid=(b, N_QBLK),
        in_specs=[smem, smem, seq(CB_CQ), seq(0), seq(CB_CK), seq(0), seq(CB_CV)],
        out_specs=pl.BlockSpec((1, 1, t, LANES), lambda bi, p: (p, bi, 0, 0)),
        out_shape=jax.ShapeDtypeStruct((N_QBLK, b, t, LANES), BF16),
        scratch_shapes=[pltpu.VMEM((2, t, LANES), BF16), pltpu.VMEM((2, t, LANES), BF16),
                        pltpu.VMEM((2, t, LANES), BF16),
                        pltpu.VMEM((nq, 2, tq, LANES), F32), pltpu.VMEM((nq, 2, tq, LANES), F32),
                        pltpu.VMEM((2, tq, tq), F32), pltpu.VMEM((2, tq, tq), F32)],
        compiler_params=_cparams("parallel", "parallel"),
        name="fox_attention",
    )(jt, it, qkv, xq, qkv, xk, qkv)


def _merge_kernel(ocmp_ref, oslc_ref, owin_ref, ob_ref, oc_ref, gs_ref, eg_ref, mg_ref, wb_ref, wo_ref, x_ref, o_ref):
    def cat(ref):
        return jnp.concatenate([ref[p] for p in range(N_QBLK)], axis=-1)

    ga = _dot(gs_ref[...], eg_ref[...])
    oa = sum(ga[:, n * MIX_WIDTH:(n + 1) * MIX_WIDTH] * cat(ref).astype(F32)
             for n, ref in enumerate((ocmp_ref, oslc_ref, owin_ref))).astype(BF16)
    branches = (oa, cat(ob_ref), cat(oc_ref))
    d = o_ref.shape[1]
    chunks = [slice(c, c + MM_CHUNK) for c in range(0, d, MM_CHUNK)]
    mix = jnp.concatenate(
        [sum(mg_ref[n, :, cols].astype(F32) * _dot(branches[n], wb_ref[n, :, cols]) for n in range(N_BRANCH))
         .astype(BF16) for cols in chunks], axis=-1)
    for cols in chunks:
        o_ref[:, cols] = x_ref[:, cols] + _dot(mix, wo_ref[:, cols])


def _merge(ocmp, oslc, owin, ob, oc, gsig, eg, mg, wb, wo, x2, tm=512):
    bt, d = x2.shape
    o_spec = pl.BlockSpec((N_QBLK, tm, LANES), lambda i: (0, i, 0))
    return pl.pallas_call(
        _merge_kernel,
        grid=(bt // tm,),
        in_specs=[o_spec] * 5 + [pl.BlockSpec((tm, LANES), lambda i: (i, 0)),
                                 pl.BlockSpec(eg.shape, lambda i: (0, 0)),
                                 pl.BlockSpec((N_BRANCH, tm, d), lambda i: (0, i, 0)),
                                 pl.BlockSpec(wb.shape, lambda i: (0, 0, 0)),
                                 pl.BlockSpec(wo.shape, lambda i: (0, 0)),
                                 pl.BlockSpec((tm, d), lambda i: (i, 0))],
        out_specs=pl.BlockSpec((tm, d), lambda i: (i, 0)),
        out_shape=jax.ShapeDtypeStruct((bt, d), F32),
        compiler_params=_cparams("parallel"),
        name="branch_merge",
    )(ocmp, oslc, owin, ob, oc, gsig, eg, mg, wb, wo, x2)


FFN_TF = 1024


def _ffn_kernel(x_ref, xh_ref, g_ref, gn_ref, wg_ref, wu_ref, cw_ref, cb_ref, wd_ref, o_ref, *rest,
                tm, tiles_per_seq, emit_next):
    hn_ref = rest[0] if emit_next else None
    h_scr, hh_scr, a_scr, acc_scr = rest[-4:]
    i = pl.program_id(0)
    f = pl.program_id(1)

    def norm(x, gain_ref):
        ms = jnp.mean(x * x, axis=-1, keepdims=True)
        return ((x * lax.rsqrt(ms + RMS_EPS)) * gain_ref[...]).astype(BF16)

    @pl.when(f == 0)
    def _():
        h_scr[...] = norm(x_ref[...], g_ref)
        hh_scr[...] = norm(xh_ref[...], g_ref)
        acc_scr[...] = jnp.zeros_like(acc_scr)

    a_scr[0:8, :] = _dot(hh_scr[...], wg_ref[...]) * jnp.where(i % tiles_per_seq == 0, 0.0, 1.0)
    acts = []
    for c in range(wg_ref.shape[1] // MM_CHUNK):
        cols = slice(c * MM_CHUNK, (c + 1) * MM_CHUNK)
        a = _dot(h_scr[...], wg_ref[:, cols])
        a_scr[8:8 + tm, cols] = a
        conv = (cw_ref[0:1, cols] * a_scr[6:6 + tm, cols] + cw_ref[1:2, cols] * a_scr[7:7 + tm, cols]
                + cw_ref[2:3, cols] * a + cb_ref[:, cols])
        acts.append((_gelu_tanh(conv) * _dot(h_scr[...], wu_ref[:, cols])).astype(BF16))
    acc_scr[...] += _dot(jnp.concatenate(acts, axis=-1), wd_ref[...])

    @pl.when(f == pl.num_programs(1) - 1)
    def _():
        out = x_ref[...] + acc_scr[...]
        o_ref[...] = out
        if emit_next:
            hn_ref[...] = norm(out, gn_ref)


def _ffn(x2, g, g_next, wg, wu, cw, cb, wd, t, tm=1024, tf=FFN_TF):
    bt, d = x2.shape
    emit_next = g_next is not None
    g_next = g_next if emit_next else g
    pad = -wg.shape[1] % tf
    wg, wu, cw = [jnp.pad(a, ((0, 0), (0, pad))) for a in (wg, wu, cw)]
    cb, wd = jnp.pad(cb, (0, pad)), jnp.pad(wd, ((0, pad), (0, 0)))
    ff = wg.shape[1]
    halo_blocks = tm // 8
    return pl.pallas_call(
        functools.partial(_ffn_kernel, tm=tm, tiles_per_seq=t // tm, emit_next=emit_next),
        grid=(bt // tm, ff // tf),
        in_specs=[pl.BlockSpec((tm, d), lambda i, f: (i, 0)),
                  pl.BlockSpec((8, d), lambda i, f: (jnp.maximum(i * halo_blocks - 1, 0), 0)),
                  pl.BlockSpec((1, d), lambda i, f: (0, 0)),
                  pl.BlockSpec((1, d), lambda i, f: (0, 0)),
                  pl.BlockSpec((d, tf), lambda i, f: (0, f)),
                  pl.BlockSpec((d, tf), lambda i, f: (0, f)),
                  pl.BlockSpec((CONV_WIDTH, tf), lambda i, f: (0, f)),
                  pl.BlockSpec((1, tf), lambda i, f: (0, f)),
                  pl.BlockSpec((tf, d), lambda i, f: (f, 0))],
        out_specs=[pl.BlockSpec((tm, d), lambda i, f: (i, 0))] * (1 + emit_next),
        out_shape=[jax.ShapeDtypeStruct((bt, d), F32), jax.ShapeDtypeStruct((bt, d), BF16)][:1 + emit_next],
        scratch_shapes=[pltpu.VMEM((tm, d), BF16), pltpu.VMEM((8, d), BF16),
                        pltpu.VMEM((tm + 8, tf), F32), pltpu.VMEM((tm, d), F32)],
        compiler_params=_cparams("parallel", "arbitrary"),
        name="conv_ffn",
    )(x2, x2, g.reshape(1, d), g_next.reshape(1, d), wg, wu, cw, cb.reshape(1, ff), wd)


def _t5_bucket_np(dist):
    max_exact = N_BUCKETS // 2
    d = np.maximum(dist, 0)
    ratio = np.log(np.maximum(d, 1) / max_exact) / math.log(MAX_DISTANCE / max_exact)
    large = np.minimum(max_exact + (ratio * (N_BUCKETS - max_exact)).astype(np.int64), N_BUCKETS - 1)
    return np.where(d < max_exact, d, large)


def _expand_kernel(tab_ref, idx_ref, o_ref):
    tab = tab_ref[...]
    onehot = jnp.where(lax.broadcasted_iota(jnp.int32, (tab.shape[1], idx_ref.shape[1]), 0) == idx_ref[...],
                       1.0, 0.0).astype(BF16)
    hi, rest = tab.astype(BF16), tab - tab.astype(BF16).astype(F32)
    mid, lo = _split_hi_lo(rest)
    heads = tab.shape[0]
    parts = _dot(jnp.concatenate([hi, mid, lo], axis=0), onehot)
    o_ref[...] = parts[0:heads] + parts[heads:2 * heads] + parts[2 * heads:3 * heads]


def _bias_from_dist(tab_t, dist, valid, chunk=4096):
    heads = tab_t.shape[0]
    tab = jnp.zeros((heads, LANES), F32).at[:, :N_BUCKETS].set(tab_t).at[:, N_BUCKETS].set(NEG)
    idx = np.where(valid, _t5_bucket_np(dist), N_BUCKETS).reshape(1, -1).astype(np.int32)
    n = idx.shape[1]
    chunk = min(chunk, n)
    out = pl.pallas_call(
        _expand_kernel,
        grid=(n // chunk,),
        in_specs=[pl.BlockSpec((heads, LANES), lambda i: (0, 0)), pl.BlockSpec((1, chunk), lambda i: (0, i))],
        out_specs=pl.BlockSpec((heads, chunk), lambda i: (0, i)),
        out_shape=jax.ShapeDtypeStruct((heads, n), F32),
        compiler_params=_cparams("parallel"),
        name="rel_bias_expand",
    )(tab, jnp.asarray(idx))
    return out.reshape((heads,) + dist.shape)


def _cmp_bias_kernel(g_ref, o_ref):
    i = pl.program_id(0)
    ncp = o_ref.shape[2]
    for h in range(o_ref.shape[0]):
        o_ref[h] = pltpu.roll(g_ref[h], i * (Q_BLOCK // CMP_STRIDE), 1)[:, :ncp]


def _cmp_bias(g_tab, t):
    heads, _, width = g_tab.shape
    ncp = t // CMP_STRIDE
    return pl.pallas_call(
        _cmp_bias_kernel,
        grid=(t // Q_BLOCK,),
        in_specs=[pl.BlockSpec(g_tab.shape, lambda i: (0, 0, 0))],
        out_specs=pl.BlockSpec((heads, Q_BLOCK, ncp), lambda i: (0, i, 0)),
        out_shape=jax.ShapeDtypeStruct((heads, t, ncp), F32),
        compiler_params=_cparams("parallel"),
        name="cmp_bias_build",
    )(g_tab)


def _static_tables(t):
    ncp = t // CMP_STRIDE
    n_cmp = (t - CMP_LEN) // CMP_STRIDE + 1
    n_slc = t // SLC_LEN
    lane = np.arange(MM_CHUNK)
    bd = (lane[:, None] // HEAD_DIM == lane[None, :] // HEAD_DIM).astype(np.float32)
    eg = np.zeros((LANES, N_BRANCH * MIX_WIDTH), np.float32)
    for h in range(N_HEADS):
        base = (h % N_QBLK) * LANES + (h // N_QBLK) * HEAD_DIM
        for n in range(N_BRANCH):
            eg[h * N_BRANCH + n, n * MIX_WIDTH + base:n * MIX_WIDTH + base + HEAD_DIM] = 1.0
    c_start = np.arange(n_cmp) * CMP_STRIDE
    s_start = np.arange(n_slc) * SLC_LEN
    overlap = ((c_start[:, None] < s_start[None, :] + SLC_LEN) & (c_start[:, None] + CMP_LEN > s_start[None, :]))
    ovt = np.zeros((LANES, ncp), np.float32)
    ovt[:n_slc, :n_cmp] = overlap.T
    key_block = np.zeros((t, LANES), np.float32)
    key_block[np.arange(t), np.arange(t) // SLC_LEN] = 1.0
    key_block[np.arange(t), HEAD_DIM + np.arange(t) // SLC_LEN] = 1.0
    as_bf16 = lambda a: jnp.asarray(a, BF16)
    return dict(bd=as_bf16(bd), eg=as_bf16(eg), ovt=as_bf16(ovt), key_block=as_bf16(key_block), n_slc=n_slc)


def _bias_tables(rel_bias, t):
    n_cmp = (t - CMP_LEN) // CMP_STRIDE + 1
    ncp = t // CMP_STRIDE
    tab_a = rel_bias[:, :N_HEADS].T * LOG2E
    tab_b = rel_bias[:, N_HEADS:].T * LOG2E
    r = np.arange(Q_BLOCK)[:, None]

    def band(window):
        n_prev = -(-(window - 1) // Q_BLOCK)
        n_keys = n_prev * Q_BLOCK + BAND_TQ
        dist = (n_keys - BAND_TQ) + np.arange(BAND_TQ)[:, None] - np.arange(n_keys)[None, :]
        valid = (dist >= 0) & (dist < window)
        assert not valid[:, 0].any()
        return dist, valid

    m = np.arange(2 * ncp)
    m = np.where(m < ncp, m, m - 2 * ncp)[None, :]
    dist_c = r - (m * CMP_STRIDE + CMP_LEN - 1)
    assert n_cmp == ncp - 1 and (dist_c[:, m[0] == Q_BLOCK // CMP_STRIDE - 1] < 0).all()
    tile = np.arange(SLC_TILE)
    dist_n = np.stack([SLC_TILE * (1 - which) + tile[:, None] - tile[None, :] for which in range(2)])
    near = _bias_from_dist(tab_a - tab_a[:, N_BUCKETS - 1:], dist_n, dist_n >= 0)
    return dict(win=_bias_from_dist(tab_a, *band(NSA_WINDOW)),
                swa=_bias_from_dist(tab_b, *band(SWA_WINDOW)),
                cmp=_cmp_bias(_bias_from_dist(tab_a, dist_c, dist_c >= 0), t),
                near=jnp.transpose(near, (1, 0, 2, 3)))


def _pair_cols(w):
    parts = []
    for p in range(N_QBLK):
        parts += [w[..., p * HEAD_DIM:(p + 1) * HEAD_DIM],
                  w[..., (N_QBLK + p) * HEAD_DIM:(N_QBLK + p + 1) * HEAD_DIM]]
    return jnp.concatenate(parts, axis=-1)


def _prep_layer(w_in, qk_gain, w_branch):
    kv = KV_GROUPS * HEAD_DIM
    widths = [('a_q', MIX_WIDTH), ('a_kc', kv), ('a_vc', kv), ('a_ks', kv), ('a_vs', kv), ('a_kw', kv), ('a_vw', kv),
              ('a_gate', N_GATE_COLS), ('b_q', MIX_WIDTH), ('b_k', kv), ('b_v', kv),
              ('c_q', MIX_WIDTH), ('c_k', MIX_WIDTH), ('c_v', MIX_WIDTH), ('c_f', N_HEADS),
              ('merge', N_BRANCH * w_in.shape[0])]
    cols, off = {}, 0
    for name, w in widths:
        cols[name] = w_in[:, off:off + w]
        off += w
    scale = HEAD_DIM ** -0.5 * LOG2E
    tile = lambda gvec, n: jnp.tile(gvec, n)
    zeros = lambda n: jnp.zeros((n,), F32)
    ones = lambda n: jnp.ones((n,), F32)
    pieces = [(_pair_cols(cols['a_q']), tile(qk_gain[0] * scale, N_HEADS), ones(MIX_WIDTH)),
              (_pair_cols(cols['b_q']), tile(qk_gain[2] * scale, N_HEADS), ones(MIX_WIDTH)),
              (cols['c_q'], tile(qk_gain[4] * scale, N_HEADS), ones(MIX_WIDTH)),
              (cols['c_k'], tile(qk_gain[5], N_HEADS), ones(MIX_WIDTH)),
              (cols['a_ks'], tile(qk_gain[1], KV_GROUPS), ones(kv)),
              (cols['a_kw'], tile(qk_gain[1], KV_GROUPS), ones(kv)),
              (cols['b_k'], tile(qk_gain[3], KV_GROUPS), ones(kv)),
              (cols['a_vs'], zeros(kv), zeros(kv)),
              (cols['c_v'], zeros(MIX_WIDTH), zeros(MIX_WIDTH)),
              (cols['a_kc'], zeros(kv), zeros(kv)), (cols['a_vc'], zeros(kv), zeros(kv)),
              (cols['a_vw'], zeros(kv), zeros(kv)), (cols['b_v'], zeros(kv), zeros(kv))]
    w_qkv = jnp.concatenate([p[0] for p in pieces], axis=1).astype(BF16)
    n_qkv = w_qkv.shape[1]
    aux = jnp.zeros((8, n_qkv), F32)
    aux = aux.at[0].set(jnp.concatenate([p[1] for p in pieces])).at[1].set(jnp.concatenate([p[2] for p in pieces]))
    d = w_in.shape[0]
    w_gates = jnp.concatenate([cols['a_gate'], cols['c_f'],
                               jnp.zeros((d, LANES - N_GATE_COLS - N_HEADS), F32)], axis=1).astype(BF16)
    wb = jnp.stack([_pair_cols(w_branch[0].T).T, _pair_cols(w_branch[1].T).T, w_branch[2]]).astype(BF16)
    return w_qkv, aux, w_gates, cols['merge'].astype(BF16), wb


def _prep_compress(cmp_pos, cmp_w1, cmp_w2, k_gain):
    def both_groups(w):
        z = jnp.zeros_like(w)
        return jnp.concatenate([jnp.concatenate([w, z], -1), jnp.concatenate([z, w], -1)], -2)

    pos = jnp.concatenate([cmp_pos, cmp_pos], -1)
    w1 = both_groups(cmp_w1.reshape(2, CMP_LEN, HEAD_DIM, CMP_HIDDEN)).astype(BF16)
    w2 = both_groups(cmp_w2).astype(BF16)
    aux = jnp.zeros((2, 8, LANES), F32)
    aux = aux.at[0, 0].set(jnp.tile(k_gain, KV_GROUPS)).at[0, 1].set(1.0)
    return pos, w1, w2, aux


def kernel(x, rel_bias, norm_mix, norm_ffn, w_in, forget_bias, qk_gain, cmp_pos, cmp_w1, cmp_w2, sinks, w_branch, w_out, w_gate, w_up, conv_w, conv_b, w_down):
    b, t, d = x.shape
    bt = b * t
    depth = w_in.shape[0]
    tabs = _static_tables(t)
    biases = _bias_tables(rel_bias, t)
    fox_tq = min(512, t)
    x2 = x.reshape(bt, d)
    h = _rmsnorm(x2, norm_mix[0])
    for l in range(depth):
        w_qkv, aux, w_gates, w_merge, wb = _prep_layer(w_in[l], qk_gain[l], w_branch[l])
        qkv = _inproj_qkv(h, w_qkv, aux, tabs['bd']).reshape(N_CB, b, t, LANES)
        graw, gsig = _inproj_gates(h, w_gates)
        mg = _inproj_merge(h, w_merge, d)

        cmp_kv = _compress(qkv, *_prep_compress(cmp_pos[l], cmp_w1[l], cmp_w2[l], qk_gain[l, 1]), tabs['bd'])
        o_cmp, qa = _cmp_select(qkv, cmp_kv, biases['cmp'], tabs['ovt'], tabs['n_slc'])
        o_slc = _slc_attention(qa, qkv, tabs['key_block'], biases['near'])
        o_win = _banded_attention(qkv, CB_AQ, CB_AKW, CB_AVW, biases['win'], False)

        swa_bias = biases['swa'].at[:, :, 0].set(sinks[l][:, None] * LOG2E)
        o_b = _banded_attention(qkv, CB_BQ, CB_BK, CB_BV, swa_bias, True)

        fb_row = jnp.zeros((1, LANES), F32).at[0, N_GATE_COLS:N_GATE_COLS + N_HEADS].set(forget_bias[l])
        o_c = _fox_attention(qkv, *_fox_prep(graw, fb_row, b, t), fox_tq)

        flat = lambda o: o.reshape(N_QBLK, bt, LANES)
        x2 = _merge(flat(o_cmp), flat(o_slc), flat(o_win), flat(o_b), flat(o_c), gsig, tabs['eg'], mg, wb,
                    w_out[l].astype(BF16), x2)
        x2, *h = _ffn(x2, norm_ffn[l], norm_mix[l + 1] if l + 1 < depth else None, w_gate[l].astype(BF16),
                      w_up[l].astype(BF16), conv_w[l], conv_b[l], w_down[l].astype(BF16), t)
        h = h[0] if h else None
    return x2.reshape(b, t, d)
```

```python
import functools
import math

import numpy as np
import jax
import jax.numpy as jnp
from jax import lax
from jax.experimental import pallas as pl
from jax.experimental.pallas import tpu as pltpu

F32 = jnp.float32
BF16 = jnp.bfloat16

HEAD_DIM = 64
LANES = 128
N_HEADS = 8
KV_GROUPS = 2
HEADS_PER_GROUP = N_HEADS // KV_GROUPS
N_QBLK = N_HEADS * HEAD_DIM // LANES
MIX_WIDTH = N_HEADS * HEAD_DIM
N_BRANCH = 3
Q_BLOCK = 128
CMP_LEN = 32
CMP_STRIDE = 16
CMP_HIDDEN = 256
SLC_LEN = 64
SLC_TOPK = 8
NSA_WINDOW = 256
SWA_WINDOW = 128
N_BUCKETS = 32
MAX_DISTANCE = 128
CONV_WIDTH = 3
RMS_EPS = 1e-6
LOG2E = math.log2(math.e)
NEG = -1e30
FORCE_SCORE = 1e9
TAKEN = -3e38

VMEM_LIMIT = 48 * 1024 * 1024
MM_CHUNK = 256
MM_ROWS = 512
ROW_TILE = 1024
NORM_ROW_TILE = 2048
QKV_ROW_TILE, QKV_COL_TILE = 2048, 512
MERGE_GATE_COL_TILE = 1024
BRANCH_MERGE_ROW_TILE = 512
FFN_TF = 1024
CMP_TQ = 256
SLC_TILE = 256
SLC_QGROUP = 4
BAND_TQ = 256
FOX_TQ = 512
FOX_PREP_CHUNK = 512
EXPAND_CHUNK = 4096

CB_AQ, CB_BQ, CB_CQ, CB_CK = 0, 4, 8, 12
CB_AKS, CB_AKW, CB_BK, CB_AVS = 16, 17, 18, 19
CB_CV, CB_AKC, CB_AVC, CB_AVW, CB_BV = 20, 24, 25, 26, 27
N_CB = 28
N_NORM_CB = 20
N_GATE_COLS = N_HEADS * N_BRANCH


def _cparams(*sem):
    return pltpu.CompilerParams(dimension_semantics=sem, vmem_limit_bytes=VMEM_LIMIT)


def _split_hi_lo(v):
    hi = v.astype(BF16)
    return hi, (v - hi.astype(F32)).astype(BF16)


def _dot(a, b):
    return jnp.dot(a, b, preferred_element_type=F32)


def _dot_nt(a, b):
    return lax.dot_general(a, b, (((1,), (1,)), ((), ())), preferred_element_type=F32)


def _gelu_tanh(x):
    return x * (0.5 * (1.0 + jnp.tanh(math.sqrt(2.0 / math.pi) * (x + 0.044715 * (x * x * x)))))


def _head_rms(y, bd, gain):
    ms = _dot((y * y).astype(BF16), bd) * (1.0 / HEAD_DIM)
    return (y * lax.rsqrt(ms + RMS_EPS)) * gain


def _lane_half_mask(g):
    return (lax.broadcasted_iota(jnp.int32, (1, LANES), 1) // HEAD_DIM) == g


def _merge_group_outputs(o_ref, o_g0, o_g1):
    low = _lane_half_mask(0)
    tq = o_ref.shape[2]
    for p in range(N_QBLK):
        rows = slice(p * tq, (p + 1) * tq)
        o_ref[p, 0] = jnp.where(low, o_g0[rows], o_g1[rows]).astype(o_ref.dtype)


def _rmsnorm_kernel(x_ref, g_ref, o_ref):
    x = x_ref[...]
    ms = jnp.mean(x * x, axis=-1, keepdims=True)
    o_ref[...] = ((x * lax.rsqrt(ms + RMS_EPS)) * g_ref[...]).astype(o_ref.dtype)


def _rmsnorm(x2, g, tm=NORM_ROW_TILE):
    bt, d = x2.shape
    tm = min(tm, bt)
    return pl.pallas_call(
        _rmsnorm_kernel,
        grid=(bt // tm,),
        in_specs=[pl.BlockSpec((tm, d), lambda i: (i, 0)), pl.BlockSpec((1, d), lambda i: (0, 0))],
        out_specs=pl.BlockSpec((tm, d), lambda i: (i, 0)),
        out_shape=jax.ShapeDtypeStruct((bt, d), BF16),
        compiler_params=_cparams("parallel"),
        name="rmsnorm",
    )(x2, g.reshape(1, d))


def _inproj_qkv_kernel(h_ref, w_ref, aux_ref, bd_ref, o_ref, *, n_norm_tiles):
    per_chunk = MM_CHUNK // LANES

    def emit(normalise):
        for r in range(0, h_ref.shape[0], MM_ROWS):
            rows = slice(r, r + MM_ROWS)
            y_all = _dot(h_ref[rows, :], w_ref[...])
            for c in range(w_ref.shape[1] // MM_CHUNK):
                cols = slice(c * MM_CHUNK, (c + 1) * MM_CHUNK)
                y = y_all[:, cols]
                if normalise:
                    normed = _head_rms(y, bd_ref[...], aux_ref[0:1, cols])
                    y = jnp.where(aux_ref[1:2, cols] > 0.5, normed, y)
                for s in range(per_chunk):
                    o_ref[c * per_chunk + s, rows, :] = y[:, s * LANES:(s + 1) * LANES].astype(o_ref.dtype)

    pl.when(pl.program_id(1) < n_norm_tiles)(lambda: emit(True))
    pl.when(pl.program_id(1) >= n_norm_tiles)(lambda: emit(False))


def _inproj_qkv(h, w, aux, bd, tm=QKV_ROW_TILE, tn=QKV_COL_TILE):
    bt, d = h.shape
    n = w.shape[1]
    n_sub = tn // LANES
    tm = min(tm, bt)
    assert N_NORM_CB % n_sub == 0 and tm % MM_ROWS == 0
    return pl.pallas_call(
        functools.partial(_inproj_qkv_kernel, n_norm_tiles=N_NORM_CB // n_sub),
        grid=(bt // tm, n // tn),
        in_specs=[pl.BlockSpec((tm, d), lambda i, j: (i, 0)),
                  pl.BlockSpec((d, tn), lambda i, j: (0, j)),
                  pl.BlockSpec((8, tn), lambda i, j: (0, j)),
                  pl.BlockSpec(bd.shape, lambda i, j: (0, 0))],
        out_specs=pl.BlockSpec((n_sub, tm, LANES), lambda i, j: (j, i, 0)),
        out_shape=jax.ShapeDtypeStruct((n // LANES, bt, LANES), BF16),
        compiler_params=_cparams("parallel", "arbitrary"),
        name="inproj_qkv",
    )(h, w, aux, bd)


def _inproj_gates_kernel(h_ref, w_ref, raw_ref, sig_ref):
    y = _dot(h_ref[...], w_ref[...])
    raw_ref[...] = y
    sig_ref[...] = jax.nn.sigmoid(y).astype(sig_ref.dtype)


def _inproj_gates(h, w, tm=ROW_TILE):
    bt, d = h.shape
    return pl.pallas_call(
        _inproj_gates_kernel,
        grid=(bt // tm,),
        in_specs=[pl.BlockSpec((tm, d), lambda i: (i, 0)),
                  pl.BlockSpec((d, LANES), lambda i: (0, 0))],
        out_specs=[pl.BlockSpec((tm, LANES), lambda i: (i, 0))] * 2,
        out_shape=[jax.ShapeDtypeStruct((bt, LANES), F32), jax.ShapeDtypeStruct((bt, LANES), BF16)],
        compiler_params=_cparams("parallel"),
        name="inproj_gates",
    )(h, w)


def _inproj_merge_kernel(h_ref, w_ref, o_ref):
    width = 2 * MM_CHUNK
    for c in range(w_ref.shape[1] // width):
        cols = slice(c * width, (c + 1) * width)
        o_ref[0, :, cols] = jax.nn.sigmoid(_dot(h_ref[...], w_ref[:, cols])).astype(o_ref.dtype)


def _inproj_merge(h, w, d_model, tm=ROW_TILE, tn=MERGE_GATE_COL_TILE):
    bt, d = h.shape
    n = w.shape[1]
    per = d_model // tn
    return pl.pallas_call(
        _inproj_merge_kernel,
        grid=(bt // tm, n // tn),
        in_specs=[pl.BlockSpec((tm, d), lambda i, j: (i, 0)),
                  pl.BlockSpec((d, tn), lambda i, j: (0, j))],
        out_specs=pl.BlockSpec((1, tm, tn), lambda i, j: (j // per, i, j % per)),
        out_shape=jax.ShapeDtypeStruct((n // d_model, bt, d_model), BF16),
        compiler_params=_cparams("parallel", "arbitrary"),
        name="inproj_merge",
    )(h, w)


def _compress_kernel(z_ref, pos_ref, w1_ref, w2_ref, aux_ref, bd_ref, o_ref, zf_scr):
    n_chunks = o_ref.shape[2]
    zf_scr[...] = z_ref[0, 0].astype(F32)
    halves = []
    for half in range(CMP_LEN // CMP_STRIDE):
        acc = jnp.zeros((n_chunks, KV_GROUPS * CMP_HIDDEN), F32)
        for l in range(CMP_STRIDE):
            rows = zf_scr[pl.ds(l, n_chunks, stride=CMP_STRIDE), :]
            pos = pos_ref[0, half * CMP_STRIDE + l:half * CMP_STRIDE + l + 1, :]
            acc = acc + _dot((rows + pos).astype(BF16), w1_ref[0, half * CMP_STRIDE + l])
        halves.append(acc)
    hid = _gelu_tanh(halves[0] + pltpu.roll(halves[1], n_chunks - 1, 0))
    y = _dot(hid.astype(BF16), w2_ref[0])
    normed = _head_rms(y, bd_ref[0:LANES, 0:LANES], aux_ref[0, 0:1, :])
    o_ref[0, 0] = jnp.where(aux_ref[0, 1:2, :] > 0.5, normed, y).astype(o_ref.dtype)


def _compress(qkv, pos, w1, w2, aux, bd):
    _, b, t, _ = qkv.shape
    n_chunks = t // CMP_STRIDE
    return pl.pallas_call(
        _compress_kernel,
        grid=(2, b),
        in_specs=[pl.BlockSpec((1, 1, t, LANES), lambda w, i: (CB_AKC + w, i, 0, 0)),
                  pl.BlockSpec((1,) + pos.shape[1:], lambda w, i: (w, 0, 0)),
                  pl.BlockSpec((1,) + w1.shape[1:], lambda w, i: (w, 0, 0, 0)),
                  pl.BlockSpec((1,) + w2.shape[1:], lambda w, i: (w, 0, 0)),
                  pl.BlockSpec((1, 8, LANES), lambda w, i: (w, 0, 0)),
                  pl.BlockSpec(bd.shape, lambda w, i: (0, 0))],
        out_specs=pl.BlockSpec((1, 1, n_chunks, LANES), lambda w, i: (w, i, 0, 0)),
        out_shape=jax.ShapeDtypeStruct((2, b, n_chunks, LANES), BF16),
        scratch_shapes=[pltpu.VMEM((t, LANES), F32)],
        compiler_params=_cparams("parallel", "parallel"),
        name="nsa_compress",
    )(qkv, pos, w1, w2, aux, bd)


def _cmp_select_kernel(q_ref, kc_ref, vc_ref, bias_ref, ovt_ref, o_ref, qa_ref, *, n_slc, n_top):
    ncp = kc_ref.shape[2]
    rows = HEADS_PER_GROUP * Q_BLOCK
    jrow = lax.broadcasted_iota(jnp.int32, (n_slc, Q_BLOCK), 0)
    jrow_f = jrow.astype(F32)
    low = _lane_half_mask(0)
    for sub in range(q_ref.shape[2] // Q_BLOCK):
        rs = slice(sub * Q_BLOCK, (sub + 1) * Q_BLOCK)
        t0 = pl.program_id(1) * q_ref.shape[2] + sub * Q_BLOCK
        row_t = t0 + lax.broadcasted_iota(jnp.int32, (Q_BLOCK, min(LANES, ncp)), 0)
        row_valid = jnp.concatenate([row_t >= CMP_LEN - 1] * HEADS_PER_GROUP, axis=0)
        cur = (t0 + lax.broadcasted_iota(jnp.int32, (n_slc, Q_BLOCK), 1)) // SLC_LEN

        def force(imp):
            forced = jnp.where(jrow == cur, FORCE_SCORE, jnp.where(jrow == cur - 1, FORCE_SCORE, imp))
            return jnp.where(jrow == 0, FORCE_SCORE, forced)

        outs = []
        for g in range(KV_GROUPS):
            own_half = _lane_half_mask(g)
            hs = slice(g * HEADS_PER_GROUP, (g + 1) * HEADS_PER_GROUP)
            qop = jnp.concatenate([jnp.where(own_half, q_ref[p, 0, rs, :], jnp.zeros((Q_BLOCK, LANES), q_ref.dtype))
                                   for p in range(N_QBLK)], axis=0)
            s = _dot_nt(qop, kc_ref[0, 0]) + bias_ref[hs, rs, :].reshape(rows, ncp)
            m = _row_max(s)
            e = [jnp.exp2(c - m) for c in _lane_chunks(s)]
            row_sum = jnp.broadcast_to(jnp.sum(sum(e), axis=-1, keepdims=True), e[0].shape)
            inv = jnp.where(row_valid, 1.0 / row_sum, 0.0)
            p = jnp.concatenate([c * inv for c in e], axis=-1)
            outs.append(_dot(p.astype(BF16), vc_ref[0, 0]))
            psum = p[0:Q_BLOCK]
            for r in range(1, HEADS_PER_GROUP):
                psum = psum + p[r * Q_BLOCK:(r + 1) * Q_BLOCK]
            hi, lo = _split_hi_lo(psum)
            imp_t = (_dot_nt(ovt_ref[...], hi) + _dot_nt(ovt_ref[...], lo))[0:n_slc]
            score = jnp.where(jrow <= cur, force(imp_t), NEG)
            sel_bias = jnp.full((n_slc, Q_BLOCK), NEG, F32)
            for _ in range(n_top):
                best = jnp.max(score, axis=0, keepdims=True)
                first = jnp.min(jnp.where(score == best, jrow_f, float(n_slc)), axis=0, keepdims=True)
                hit = jrow_f == first
                sel_bias = jnp.where(hit, jnp.where(best > NEG / 2, 0.0, NEG), sel_bias)
                score = jnp.where(hit, TAKEN, score)
            if n_slc < HEAD_DIM:
                sel_bias = jnp.concatenate([sel_bias, jnp.full((HEAD_DIM - n_slc, Q_BLOCK), NEG, F32)], axis=0)
            sel_t = jnp.concatenate([sel_bias, sel_bias], axis=0).T.astype(qa_ref.dtype)
            for p in range(N_QBLK):
                qa_ref[g * HEADS_PER_GROUP + p, 0, rs, :] = jnp.where(own_half, q_ref[p, 0, rs, :], sel_t)
        for p in range(N_QBLK):
            prs = slice(p * Q_BLOCK, (p + 1) * Q_BLOCK)
            o_ref[p, 0, rs, :] = jnp.where(low, outs[0][prs], outs[1][prs]).astype(o_ref.dtype)


def _cmp_select(qkv, cmp_kv, bias_c, ovt, n_slc, tq=CMP_TQ):
    _, b, t, _ = qkv.shape
    ncp = cmp_kv.shape[2]
    assert n_slc <= HEAD_DIM
    return pl.pallas_call(
        functools.partial(_cmp_select_kernel, n_slc=n_slc, n_top=min(SLC_TOPK, n_slc)),
        grid=(b, t // tq),
        in_specs=[pl.BlockSpec((N_QBLK, 1, tq, LANES), lambda bi, i: (CB_AQ // N_QBLK, bi, i, 0)),
                  pl.BlockSpec((1, 1, ncp, LANES), lambda bi, i: (0, bi, 0, 0)),
                  pl.BlockSpec((1, 1, ncp, LANES), lambda bi, i: (1, bi, 0, 0)),
                  pl.BlockSpec((N_HEADS, tq, ncp), lambda bi, i: (0, i, 0)),
                  pl.BlockSpec((LANES, ncp), lambda bi, i: (0, 0))],
        out_specs=[pl.BlockSpec((N_QBLK, 1, tq, LANES), lambda bi, i: (0, bi, i, 0)),
                   pl.BlockSpec((N_HEADS, 1, tq, LANES), lambda bi, i: (0, bi, i, 0))],
        out_shape=[jax.ShapeDtypeStruct((N_QBLK, b, t, LANES), BF16),
                   jax.ShapeDtypeStruct((N_HEADS, b, t, LANES), BF16)],
        compiler_params=_cparams("parallel", "parallel"),
        name="nsa_cmp_select",
    )(qkv, cmp_kv, cmp_kv, bias_c, ovt)


def _lane_chunks(s):
    width = min(LANES, s.shape[1])
    return [s[:, c:c + width] for c in range(0, s.shape[1], width)]


def _row_max(s):
    chunks = _lane_chunks(s)
    m = chunks[0]
    for c in chunks[1:]:
        m = jnp.maximum(m, c)
    return jnp.broadcast_to(jnp.max(m, axis=-1, keepdims=True), m.shape)


def _exp2_shifted(s, m):
    return jnp.concatenate([jnp.exp2(c - m).astype(BF16) for c in _lane_chunks(s)], axis=-1)


def _flash_start(s, v):
    m = _row_max(s)
    return m, _dot(_exp2_shifted(s, m), v)


def _flash_update(s, m, acc, v):
    m_new = jnp.maximum(m, _row_max(s))
    return m_new, jnp.exp2(m - m_new) * acc + _dot(_exp2_shifted(s, m_new), v)


def _normalise(acc):
    return acc / pltpu.roll(acc, HEAD_DIM, 1)


def _augment_keys_values(ka_scr, va_scr, k2, xk, v2, base=0):
    low = _lane_half_mask(0)
    ones = jnp.ones_like(v2)
    ka_scr[base] = jnp.where(low, k2, xk)
    ka_scr[base + 1] = jnp.where(low, xk, k2)
    va_scr[base] = jnp.where(low, v2, ones)
    va_scr[base + 1] = jnp.where(low, ones, v2)


def _store_all(scr, s):
    for n, sn in enumerate(s):
        scr[n] = sn


def _load_all(scr):
    return tuple(scr[n] for n in range(scr.shape[0]))


def _worklist_sweep(n_steps, logits, update, sa_scr, sb_scr):
    def single(n, _):
        _store_all(sa_scr, logits(n))
        update(n, _load_all(sa_scr))
        return 0

    odd = n_steps % 2
    lax.fori_loop(0, odd, single, 0)
    _store_all(sa_scr, logits(odd))

    def pair(jp, _):
        n = odd + 2 * jp
        _store_all(sb_scr, logits(n + 1))
        update(n, _load_all(sa_scr))
        _store_all(sa_scr, logits(jnp.minimum(n + 2, n_steps - 1)))
        update(n + 1, _load_all(sb_scr))
        return 0

    lax.fori_loop(0, n_steps // 2, pair, 0)


def _slc_kernel(nfar_ref, jt_ref, it_ref, qa_ref, k_ref, v_ref, hot_ref, nb_ref, o_ref,
                ka_scr, va_scr, m_scr, acc_scr, sa_scr, sb_scr, *, tq):
    gi = pl.program_id(1)
    qg = qa_ref.shape[2] // tq
    rows = HEADS_PER_GROUP * tq
    _augment_keys_values(ka_scr, va_scr, k_ref[0, 0], hot_ref[...], v_ref[0, 0])
    heads = [slice(g * HEADS_PER_GROUP, (g + 1) * HEADS_PER_GROUP) for g in range(KV_GROUPS)]

    def tile(n):
        return pl.ds(pl.multiple_of(n * tq, tq), tq)

    def logits(j, il):
        return tuple(_dot_nt(qa_ref[heads[g], 0, tile(il), :].reshape(rows, LANES), ka_scr[g, tile(j), :])
                     for g in range(KV_GROUPS))

    def biased(s, which, g):
        return (s.reshape(HEADS_PER_GROUP, tq, tq) + nb_ref[which, heads[g]]).reshape(rows, tq)

    def diag_update(il, s):
        for g in range(KV_GROUPS):
            m_scr[il, g], acc_scr[il, g] = _flash_start(biased(s[g], 1, g), va_scr[g, tile(gi * qg + il), :])

    _worklist_sweep(qg, lambda il: logits(gi * qg + il, il), diag_update, sa_scr, sb_scr)

    def fold(il, g, sg, j):
        m_scr[il, g], acc_scr[il, g] = _flash_update(sg, m_scr[il, g], acc_scr[il, g], va_scr[g, tile(j), :])

    def prev_update(il, s):
        i = gi * qg + il
        pen = jnp.where(i >= 1, 0.0, NEG)
        for g in range(KV_GROUPS):
            fold(il, g, biased(s[g], 0, g) + pen, jnp.maximum(i - 1, 0))

    _worklist_sweep(qg, lambda il: logits(jnp.maximum(gi * qg + il - 1, 0), il), prev_update, sa_scr, sb_scr)

    def far_update(n, s):
        for g in range(KV_GROUPS):
            fold(it_ref[gi, n], g, s[g], jt_ref[gi, n])

    _worklist_sweep(nfar_ref[gi], lambda n: logits(jt_ref[gi, n], it_ref[gi, n]), far_update, sa_scr, sb_scr)

    low = _lane_half_mask(0)
    for il in range(qg):
        o_g0, o_g1 = _normalise(acc_scr[il, 0]), _normalise(acc_scr[il, 1])
        for p in range(N_QBLK):
            head_rows = slice(p * tq, (p + 1) * tq)
            o_ref[p, 0, il * tq:(il + 1) * tq, :] = jnp.where(low, o_g0[head_rows], o_g1[head_rows]).astype(o_ref.dtype)


def _slc_attention(qa, qkv, hot2, near_bias, qg=SLC_QGROUP):
    _, b, t, _ = qa.shape
    tq = near_bias.shape[2]
    nq = t // tq
    qg = min(qg, nq)
    n_groups = nq // qg
    lists = [[(j, il) for j in range(nq) for il in range(qg) if j <= g * qg + il - 2] for g in range(n_groups)]
    width = max(2, max(len(steps) for steps in lists))
    table = lambda k: jnp.asarray([[s[k] for s in steps] + [0] * (width - len(steps)) for steps in lists], jnp.int32)
    nfar = jnp.asarray([len(steps) for steps in lists], jnp.int32)
    smem = pl.BlockSpec(memory_space=pltpu.SMEM)
    rows = HEADS_PER_GROUP * tq
    return pl.pallas_call(
        functools.partial(_slc_kernel, tq=tq),
        grid=(b, n_groups),
        in_specs=[smem, smem, smem,
                  pl.BlockSpec((N_HEADS, 1, qg * tq, LANES), lambda bi, gi: (0, bi, gi, 0)),
                  pl.BlockSpec((1, 1, t, LANES), lambda bi, gi: (CB_AKS, bi, 0, 0)),
                  pl.BlockSpec((1, 1, t, LANES), lambda bi, gi: (CB_AVS, bi, 0, 0)),
                  pl.BlockSpec(hot2.shape, lambda bi, gi: (0, 0)),
                  pl.BlockSpec(near_bias.shape, lambda bi, gi: (0, 0, 0, 0))],
        out_specs=pl.BlockSpec((N_QBLK, 1, qg * tq, LANES), lambda bi, gi: (0, bi, gi, 0)),
        out_shape=jax.ShapeDtypeStruct((N_QBLK, b, t, LANES), BF16),
        scratch_shapes=[pltpu.VMEM((KV_GROUPS, t, LANES), BF16), pltpu.VMEM((KV_GROUPS, t, LANES), BF16),
                        pltpu.VMEM((qg, KV_GROUPS, rows, LANES), F32), pltpu.VMEM((qg, KV_GROUPS, rows, LANES), F32),
                        pltpu.VMEM((KV_GROUPS, rows, tq), F32), pltpu.VMEM((KV_GROUPS, rows, tq), F32)],
        compiler_params=_cparams("parallel", "parallel"),
        name="nsa_slc_attention",
    )(nfar, table(0), table(1), qa, qkv, qkv, hot2, near_bias)


def _banded_kernel(*refs, n_blk, use_sinks):
    q_ref = refs[0]
    k_refs = refs[1:1 + n_blk]
    v_refs = refs[1 + n_blk:1 + 2 * n_blk]
    bias_ref, o_ref = refs[1 + 2 * n_blk:]
    i = pl.program_id(1)
    tq = q_ref.shape[2]
    n_keys = n_blk * Q_BLOCK
    rows = HEADS_PER_GROUP * tq
    first_key = (i + 1) * tq - n_keys
    col = lax.broadcasted_iota(jnp.int32, (1, n_keys), 1)
    pad_pen = jnp.where(first_key + col >= 0, 0.0, NEG)
    kcat = jnp.concatenate([r[0, 0] for r in k_refs], axis=0)
    vcat = jnp.concatenate([r[0, 0] for r in v_refs], axis=0)
    if use_sinks:
        sink_row = lax.broadcasted_iota(jnp.int32, (n_keys, 1), 0) == 0
        kcat = jnp.where(sink_row, jnp.zeros_like(kcat), kcat)
        vcat = jnp.where(sink_row, jnp.zeros_like(vcat), vcat)
        pad_pen = jnp.where(col == 0, 0.0, pad_pen)
    outs = []
    for g in range(KV_GROUPS):
        own_half = _lane_half_mask(g)
        qop = jnp.concatenate([jnp.where(own_half, q_ref[p, 0], jnp.zeros_like(q_ref[p, 0]))
                               for p in range(N_QBLK)], axis=0)
        hs = slice(g * HEADS_PER_GROUP, (g + 1) * HEADS_PER_GROUP)
        s = _dot_nt(qop, kcat) + bias_ref[hs].reshape(rows, n_keys) + pad_pen
        e = _exp2_shifted(s, _row_max(s))
        acc = _dot(e, jnp.where(own_half, vcat, jnp.ones_like(vcat)))
        outs.append(_normalise(acc))
    _merge_group_outputs(o_ref, outs[0], outs[1])


def _banded_attention(qkv, cb_q, cb_k, cb_v, bias, use_sinks):
    _, b, t, _ = qkv.shape
    tq = bias.shape[1]
    n_blk = bias.shape[2] // Q_BLOCK
    per_tile = tq // Q_BLOCK

    def kv_spec(cb, back):
        return pl.BlockSpec((1, 1, Q_BLOCK, LANES),
                            lambda bi, i: (cb, bi, jnp.maximum((i + 1) * per_tile - 1 - back, 0), 0))

    backs = [n_blk - 1 - jb for jb in range(n_blk)]
    return pl.pallas_call(
        functools.partial(_banded_kernel, n_blk=n_blk, use_sinks=use_sinks),
        grid=(b, t // tq),
        in_specs=([pl.BlockSpec((N_QBLK, 1, tq, LANES), lambda bi, i: (cb_q // N_QBLK, bi, i, 0))]
                  + [kv_spec(cb_k, back) for back in backs] + [kv_spec(cb_v, back) for back in backs]
                  + [pl.BlockSpec(bias.shape, lambda bi, i: (0, 0, 0))]),
        out_specs=pl.BlockSpec((N_QBLK, 1, tq, LANES), lambda bi, i: (0, bi, i, 0)),
        out_shape=jax.ShapeDtypeStruct((N_QBLK, b, t, LANES), BF16),
        compiler_params=_cparams("parallel", "parallel"),
        name="banded_attention",
    )(qkv, *([qkv] * (2 * n_blk)), bias)


FOX_EXTRA = 6


def _fox_prep_kernel(g_ref, fb_ref, pq_ref, pk_ref, oq_ref, ok_ref, xq_ref, xk_ref, carry_scr):
    @pl.when(pl.program_id(1) == 0)
    def _():
        carry_scr[...] = jnp.zeros_like(carry_scr)

    x = g_ref[...] + fb_ref[...]
    logf = jnp.minimum(x, 0.0) - jnp.log(1.0 + jnp.exp(-jnp.abs(x)))
    ch = x.shape[0]
    lower = jnp.where(lax.broadcasted_iota(jnp.int32, (ch, ch), 0)
                      >= lax.broadcasted_iota(jnp.int32, (ch, ch), 1), 1.0, 0.0).astype(BF16)

    def split3(v):
        hi, rest = v.astype(BF16), v - v.astype(BF16).astype(F32)
        return (hi,) + _split_hi_lo(rest)

    terms = _dot(lower, jnp.concatenate(split3(logf), axis=-1))
    cs = sum(_lane_chunks(terms)) + carry_scr[0:1, :]
    carry_scr[0:1, :] = cs[ch - 1:ch, :]
    parts = jnp.concatenate(split3(cs * LOG2E), axis=-1)
    xq = _dot(parts, pq_ref[...]) + oq_ref[...]
    xk = _dot(parts, pk_ref[...]) + ok_ref[...]
    for p in range(N_QBLK):
        xq_ref[p, 0] = xq[:, p * LANES:(p + 1) * LANES].astype(xq_ref.dtype)
        xk_ref[p, 0] = xk[:, p * LANES:(p + 1) * LANES].astype(xk_ref.dtype)


def _fox_prep(graw, fb_row, b, t, chunk=FOX_PREP_CHUNK):
    chunk = min(chunk, t)
    nch = t // chunk
    pq = np.zeros((3 * LANES, N_QBLK * LANES), np.float32)
    pk = np.zeros((3 * LANES, N_QBLK * LANES), np.float32)
    oq = np.zeros((1, N_QBLK * LANES), np.float32)
    ok = np.zeros((1, N_QBLK * LANES), np.float32)
    for h in range(N_HEADS):
        col = (h // 2) * LANES + (HEAD_DIM if h % 2 == 0 else 0)
        for n in range(3):
            pq[n * LANES + N_GATE_COLS + h, col + n] = 1.0
            pk[n * LANES + N_GATE_COLS + h, col + 3 + n] = -1.0
        oq[0, col + 3:col + FOX_EXTRA] = 1.0
        ok[0, col:col + 3] = 1.0
    const = lambda a: pl.BlockSpec(a.shape, lambda bi, c: (0,) * a.ndim)
    out_spec = pl.BlockSpec((N_QBLK, 1, chunk, LANES), lambda bi, c: (0, bi, c, 0))
    return pl.pallas_call(
        _fox_prep_kernel,
        grid=(b, nch),
        in_specs=[pl.BlockSpec((chunk, LANES), lambda bi, c: (bi * nch + c, 0)),
                  const(fb_row), const(pq), const(pk), const(oq), const(ok)],
        out_specs=[out_spec, out_spec],
        out_shape=[jax.ShapeDtypeStruct((N_QBLK, b, t, LANES), BF16)] * 2,
        scratch_shapes=[pltpu.VMEM((8, LANES), F32)],
        compiler_params=_cparams("parallel", "arbitrary"),
        name="fox_prep",
    )(graw, fb_row, jnp.asarray(pq, BF16), jnp.asarray(pk, BF16), jnp.asarray(oq), jnp.asarray(ok))


def _fox_kernel(jt_ref, it_ref, q_ref, xq_ref, k_ref, xk_ref, v_ref, o_ref,
                qa_scr, ka_scr, va_scr, m_scr, acc_scr, sa_scr, sb_scr, *, tq, n_off):
    nq = q_ref.shape[2] // tq
    low = _lane_half_mask(0)
    _augment_keys_values(ka_scr, va_scr, k_ref[0, 0], xk_ref[0, 0], v_ref[0, 0])
    q2, xq = q_ref[0, 0], xq_ref[0, 0]
    qa_scr[0] = jnp.where(low, q2, xq)
    qa_scr[1] = jnp.where(low, xq, q2)
    causal_pen = jnp.where(lax.broadcasted_iota(jnp.int32, (tq, tq), 0)
                           >= lax.broadcasted_iota(jnp.int32, (tq, tq), 1), 0.0, NEG)

    def tile(n):
        return pl.ds(pl.multiple_of(n * tq, tq), tq)

    def logits(j, i):
        return tuple(_dot_nt(qa_scr[h, tile(i), :], ka_scr[h, tile(j), :]) for h in range(2))

    def diag_update(i, s):
        for h in range(2):
            m_scr[i, h], acc_scr[i, h] = _flash_start(s[h] + causal_pen, va_scr[h, tile(i), :])

    _worklist_sweep(nq, lambda n: logits(n, n), diag_update, sa_scr, sb_scr)

    def off_update(n, s):
        j, i = jt_ref[n], it_ref[n]
        for h in range(2):
            m_scr[i, h], acc_scr[i, h] = _flash_update(s[h], m_scr[i, h], acc_scr[i, h], va_scr[h, tile(j), :])

    _worklist_sweep(n_off, lambda n: logits(jt_ref[n], it_ref[n]), off_update, sa_scr, sb_scr)

    def finish(i, _):
        o_ref[0, 0, tile(i), :] = jnp.where(low, _normalise(acc_scr[i, 0]), _normalise(acc_scr[i, 1])).astype(o_ref.dtype)
        return 0

    lax.fori_loop(0, nq, finish, 0)


def _fox_attention(qkv, xq, xk, tq):
    _, b, t, _ = qkv.shape
    nq = t // tq
    pairs = [(j, i) for j in range(nq) for i in range(j + 1, nq)]
    n_off = len(pairs)
    pairs += [(0, 0)] * max(0, 2 - n_off)
    jt = jnp.asarray([p[0] for p in pairs], jnp.int32)
    it = jnp.asarray([p[1] for p in pairs], jnp.int32)
    smem = pl.BlockSpec(memory_space=pltpu.SMEM)
    seq = lambda cb: pl.BlockSpec((1, 1, t, LANES), lambda bi, p: (cb + p, bi, 0, 0))
    return pl.pallas_call(
        functools.partial(_fox_kernel, tq=tq, n_off=n_off),
        grid=(b, N_QBLK),
        in_specs=[smem, smem, seq(CB_CQ), seq(0), seq(CB_CK), seq(0), seq(CB_CV)],
        out_specs=pl.BlockSpec((1, 1, t, LANES), lambda bi, p: (p, bi, 0, 0)),
        out_shape=jax.ShapeDtypeStruct((N_QBLK, b, t, LANES), BF16),
        scratch_shapes=[pltpu.VMEM((2, t, LANES), BF16), pltpu.VMEM((2, t, LANES), BF16),
                        pltpu.VMEM((2, t, LANES), BF16),
                        pltpu.VMEM((nq, 2, tq, LANES), F32), pltpu.VMEM((nq, 2, tq, LANES), F32),
                        pltpu.VMEM((2, tq, tq), F32), pltpu.VMEM((2, tq, tq), F32)],
        compiler_params=_cparams("parallel", "parallel"),
        name="fox_attention",
    )(jt, it, qkv, xq, qkv, xk, qkv)


def _merge_kernel(ocmp_ref, oslc_ref, owin_ref, ob_ref, oc_ref, gs_ref, eg_ref, mg_ref, wb_ref, wo_ref, x_ref, o_ref):
    def cat(ref):
        return jnp.concatenate([ref[p] for p in range(N_QBLK)], axis=-1)

    ga = _dot(gs_ref[...], eg_ref[...])
    oa = sum(ga[:, n * MIX_WIDTH:(n + 1) * MIX_WIDTH] * cat(ref).astype(F32)
             for n, ref in enumerate((ocmp_ref, oslc_ref, owin_ref))).astype(BF16)
    branches = (oa, cat(ob_ref), cat(oc_ref))
    d = o_ref.shape[1]
    chunks = [slice(c, c + MM_CHUNK) for c in range(0, d, MM_CHUNK)]
    mix = jnp.concatenate(
        [sum(mg_ref[n, :, cols].astype(F32) * _dot(branches[n], wb_ref[n, :, cols]) for n in range(N_BRANCH))
         .astype(BF16) for cols in chunks], axis=-1)
    for cols in chunks:
        o_ref[:, cols] = x_ref[:, cols] + _dot(mix, wo_ref[:, cols])


def _merge(ocmp, oslc, owin, ob, oc, gsig, eg, mg, wb, wo, x2, tm=BRANCH_MERGE_ROW_TILE):
    bt, d = x2.shape
    o_spec = pl.BlockSpec((N_QBLK, tm, LANES), lambda i: (0, i, 0))
    return pl.pallas_call(
        _merge_kernel,
        grid=(bt // tm,),
        in_specs=[o_spec] * 5 + [pl.BlockSpec((tm, LANES), lambda i: (i, 0)),
                                 pl.BlockSpec(eg.shape, lambda i: (0, 0)),
                                 pl.BlockSpec((N_BRANCH, tm, d), lambda i: (0, i, 0)),
                                 pl.BlockSpec(wb.shape, lambda i: (0, 0, 0)),
                                 pl.BlockSpec(wo.shape, lambda i: (0, 0)),
                                 pl.BlockSpec((tm, d), lambda i: (i, 0))],
        out_specs=pl.BlockSpec((tm, d), lambda i: (i, 0)),
        out_shape=jax.ShapeDtypeStruct((bt, d), F32),
        compiler_params=_cparams("parallel"),
        name="branch_merge",
    )(ocmp, oslc, owin, ob, oc, gsig, eg, mg, wb, wo, x2)


def _ffn_kernel(x_ref, xh_ref, g_ref, gn_ref, wg_ref, wu_ref, cw_ref, cb_ref, wd_ref, o_ref, *rest,
                tm, tiles_per_seq, emit_next):
    hn_ref = rest[0] if emit_next else None
    h_scr, hh_scr, a_scr, acc_scr = rest[-4:]
    i = pl.program_id(0)
    f = pl.program_id(1)

    def norm(x, gain_ref):
        ms = jnp.mean(x * x, axis=-1, keepdims=True)
        return ((x * lax.rsqrt(ms + RMS_EPS)) * gain_ref[...]).astype(BF16)

    @pl.when(f == 0)
    def _():
        h_scr[...] = norm(x_ref[...], g_ref)
        hh_scr[...] = norm(xh_ref[...], g_ref)
        acc_scr[...] = jnp.zeros_like(acc_scr)

    a_scr[0:8, :] = _dot(hh_scr[...], wg_ref[...]) * jnp.where(i % tiles_per_seq == 0, 0.0, 1.0)
    acts = []
    for c in range(wg_ref.shape[1] // MM_CHUNK):
        cols = slice(c * MM_CHUNK, (c + 1) * MM_CHUNK)
        a = _dot(h_scr[...], wg_ref[:, cols])
        a_scr[8:8 + tm, cols] = a
        conv = (cw_ref[0:1, cols] * a_scr[6:6 + tm, cols] + cw_ref[1:2, cols] * a_scr[7:7 + tm, cols]
                + cw_ref[2:3, cols] * a + cb_ref[:, cols])
        acts.append((_gelu_tanh(conv) * _dot(h_scr[...], wu_ref[:, cols])).astype(BF16))
    acc_scr[...] += _dot(jnp.concatenate(acts, axis=-1), wd_ref[...])

    @pl.when(f == pl.num_programs(1) - 1)
    def _():
        out = x_ref[...] + acc_scr[...]
        o_ref[...] = out
        if emit_next:
            hn_ref[...] = norm(out, gn_ref)


def _ffn(x2, g, g_next, wg, wu, cw, cb, wd, t, tm=ROW_TILE, tf=FFN_TF):
    bt, d = x2.shape
    emit_next = g_next is not None
    g_next = g_next if emit_next else g
    pad = -wg.shape[1] % tf
    wg, wu, cw = [jnp.pad(a, ((0, 0), (0, pad))) for a in (wg, wu, cw)]
    cb, wd = jnp.pad(cb, (0, pad)), jnp.pad(wd, ((0, pad), (0, 0)))
    ff = wg.shape[1]
    halo_blocks = tm // 8
    return pl.pallas_call(
        functools.partial(_ffn_kernel, tm=tm, tiles_per_seq=t // tm, emit_next=emit_next),
        grid=(bt // tm, ff // tf),
        in_specs=[pl.BlockSpec((tm, d), lambda i, f: (i, 0)),
                  pl.BlockSpec((8, d), lambda i, f: (jnp.maximum(i * halo_blocks - 1, 0), 0)),
                  pl.BlockSpec((1, d), lambda i, f: (0, 0)),
                  pl.BlockSpec((1, d), lambda i, f: (0, 0)),
                  pl.BlockSpec((d, tf), lambda i, f: (0, f)),
                  pl.BlockSpec((d, tf), lambda i, f: (0, f)),
                  pl.BlockSpec((CONV_WIDTH, tf), lambda i, f: (0, f)),
                  pl.BlockSpec((1, tf), lambda i, f: (0, f)),
                  pl.BlockSpec((tf, d), lambda i, f: (f, 0))],
        out_specs=[pl.BlockSpec((tm, d), lambda i, f: (i, 0))] * (1 + emit_next),
        out_shape=[jax.ShapeDtypeStruct((bt, d), F32), jax.ShapeDtypeStruct((bt, d), BF16)][:1 + emit_next],
        scratch_shapes=[pltpu.VMEM((tm, d), BF16), pltpu.VMEM((8, d), BF16),
                        pltpu.VMEM((tm + 8, tf), F32), pltpu.VMEM((tm, d), F32)],
        compiler_params=_cparams("parallel", "arbitrary"),
        name="conv_ffn",
    )(x2, x2, g.reshape(1, d), g_next.reshape(1, d), wg, wu, cw, cb.reshape(1, ff), wd)


def _t5_bucket_np(dist):
    max_exact = N_BUCKETS // 2
    d = np.maximum(dist, 0)
    ratio = np.log(np.maximum(d, 1) / max_exact) / math.log(MAX_DISTANCE / max_exact)
    large = np.minimum(max_exact + (ratio * (N_BUCKETS - max_exact)).astype(np.int64), N_BUCKETS - 1)
    return np.where(d < max_exact, d, large)


def _expand_kernel(tab_ref, idx_ref, o_ref):
    tab = tab_ref[...]
    onehot = jnp.where(lax.broadcasted_iota(jnp.int32, (tab.shape[1], idx_ref.shape[1]), 0) == idx_ref[...],
                       1.0, 0.0).astype(BF16)
    hi, rest = tab.astype(BF16), tab - tab.astype(BF16).astype(F32)
    mid, lo = _split_hi_lo(rest)
    heads = tab.shape[0]
    parts = _dot(jnp.concatenate([hi, mid, lo], axis=0), onehot)
    o_ref[...] = parts[0:heads] + parts[heads:2 * heads] + parts[2 * heads:3 * heads]


def _bias_from_dist(tab_t, dist, valid, chunk=EXPAND_CHUNK):
    heads = tab_t.shape[0]
    tab = jnp.zeros((heads, LANES), F32).at[:, :N_BUCKETS].set(tab_t).at[:, N_BUCKETS].set(NEG)
    idx = np.where(valid, _t5_bucket_np(dist), N_BUCKETS).reshape(1, -1).astype(np.int32)
    n = idx.shape[1]
    chunk = min(chunk, n)
    out = pl.pallas_call(
        _expand_kernel,
        grid=(n // chunk,),
        in_specs=[pl.BlockSpec((heads, LANES), lambda i: (0, 0)), pl.BlockSpec((1, chunk), lambda i: (0, i))],
        out_specs=pl.BlockSpec((heads, chunk), lambda i: (0, i)),
        out_shape=jax.ShapeDtypeStruct((heads, n), F32),
        compiler_params=_cparams("parallel"),
        name="rel_bias_expand",
    )(tab, jnp.asarray(idx))
    return out.reshape((heads,) + dist.shape)


def _cmp_bias_kernel(g_ref, o_ref):
    ncp = o_ref.shape[2]
    per_step = o_ref.shape[1] // Q_BLOCK
    for sub in range(per_step):
        shift = (pl.program_id(0) * per_step + sub) * (Q_BLOCK // CMP_STRIDE)
        for h in range(o_ref.shape[0]):
            o_ref[h, sub * Q_BLOCK:(sub + 1) * Q_BLOCK, :] = pltpu.roll(g_ref[h], shift, 1)[:, :ncp]


def _cmp_bias(g_tab, t):
    heads, _, width = g_tab.shape
    ncp = t // CMP_STRIDE
    return pl.pallas_call(
        _cmp_bias_kernel,
        grid=(t // CMP_TQ,),
        in_specs=[pl.BlockSpec(g_tab.shape, lambda i: (0, 0, 0))],
        out_specs=pl.BlockSpec((heads, CMP_TQ, ncp), lambda i: (0, i, 0)),
        out_shape=jax.ShapeDtypeStruct((heads, t, ncp), F32),
        compiler_params=_cparams("parallel"),
        name="cmp_bias_build",
    )(g_tab)


def _static_tables(t):
    ncp = t // CMP_STRIDE
    n_cmp = (t - CMP_LEN) // CMP_STRIDE + 1
    n_slc = t // SLC_LEN
    lane = np.arange(MM_CHUNK)
    bd = (lane[:, None] // HEAD_DIM == lane[None, :] // HEAD_DIM).astype(np.float32)
    eg = np.zeros((LANES, N_BRANCH * MIX_WIDTH), np.float32)
    for h in range(N_HEADS):
        base = (h % N_QBLK) * LANES + (h // N_QBLK) * HEAD_DIM
        for n in range(N_BRANCH):
            eg[h * N_BRANCH + n, n * MIX_WIDTH + base:n * MIX_WIDTH + base + HEAD_DIM] = 1.0
    c_start = np.arange(n_cmp) * CMP_STRIDE
    s_start = np.arange(n_slc) * SLC_LEN
    overlap = ((c_start[:, None] < s_start[None, :] + SLC_LEN) & (c_start[:, None] + CMP_LEN > s_start[None, :]))
    ovt = np.zeros((LANES, ncp), np.float32)
    ovt[:n_slc, :n_cmp] = overlap.T
    key_block = np.zeros((t, LANES), np.float32)
    key_block[np.arange(t), np.arange(t) // SLC_LEN] = 1.0
    key_block[np.arange(t), HEAD_DIM + np.arange(t) // SLC_LEN] = 1.0
    as_bf16 = lambda a: jnp.asarray(a, BF16)
    return dict(bd=as_bf16(bd), eg=as_bf16(eg), ovt=as_bf16(ovt), key_block=as_bf16(key_block), n_slc=n_slc)


def _bias_tables(rel_bias, t):
    n_cmp = (t - CMP_LEN) // CMP_STRIDE + 1
    ncp = t // CMP_STRIDE
    tab_a = rel_bias[:, :N_HEADS].T * LOG2E
    tab_b = rel_bias[:, N_HEADS:].T * LOG2E
    r = np.arange(Q_BLOCK)[:, None]

    def band(window):
        n_prev = -(-(window - 1) // Q_BLOCK)
        n_keys = n_prev * Q_BLOCK + BAND_TQ
        dist = (n_keys - BAND_TQ) + np.arange(BAND_TQ)[:, None] - np.arange(n_keys)[None, :]
        valid = (dist >= 0) & (dist < window)
        assert not valid[:, 0].any()
        return dist, valid

    m = np.arange(2 * ncp)
    m = np.where(m < ncp, m, m - 2 * ncp)[None, :]
    dist_c = r - (m * CMP_STRIDE + CMP_LEN - 1)
    assert n_cmp == ncp - 1 and (dist_c[:, m[0] == Q_BLOCK // CMP_STRIDE - 1] < 0).all()
    tile = np.arange(SLC_TILE)
    dist_n = np.stack([SLC_TILE * (1 - which) + tile[:, None] - tile[None, :] for which in range(2)])
    near = _bias_from_dist(tab_a - tab_a[:, N_BUCKETS - 1:], dist_n, dist_n >= 0)
    return dict(win=_bias_from_dist(tab_a, *band(NSA_WINDOW)),
                swa=_bias_from_dist(tab_b, *band(SWA_WINDOW)),
                cmp=_cmp_bias(_bias_from_dist(tab_a, dist_c, dist_c >= 0), t),
                near=jnp.transpose(near, (1, 0, 2, 3)))


def _pair_cols(w):
    parts = []
    for p in range(N_QBLK):
        parts += [w[..., p * HEAD_DIM:(p + 1) * HEAD_DIM],
                  w[..., (N_QBLK + p) * HEAD_DIM:(N_QBLK + p + 1) * HEAD_DIM]]
    return jnp.concatenate(parts, axis=-1)


def _prep_layer(w_in, qk_gain, w_branch):
    kv = KV_GROUPS * HEAD_DIM
    widths = [('a_q', MIX_WIDTH), ('a_kc', kv), ('a_vc', kv), ('a_ks', kv), ('a_vs', kv), ('a_kw', kv), ('a_vw', kv),
              ('a_gate', N_GATE_COLS), ('b_q', MIX_WIDTH), ('b_k', kv), ('b_v', kv),
              ('c_q', MIX_WIDTH), ('c_k', MIX_WIDTH), ('c_v', MIX_WIDTH), ('c_f', N_HEADS),
              ('merge', N_BRANCH * w_in.shape[0])]
    cols, off = {}, 0
    for name, w in widths:
        cols[name] = w_in[:, off:off + w]
        off += w
    scale = HEAD_DIM ** -0.5 * LOG2E
    tile = lambda gvec, n: jnp.tile(gvec, n)
    zeros = lambda n: jnp.zeros((n,), F32)
    ones = lambda n: jnp.ones((n,), F32)
    pieces = [(_pair_cols(cols['a_q']), tile(qk_gain[0] * scale, N_HEADS), ones(MIX_WIDTH)),
              (_pair_cols(cols['b_q']), tile(qk_gain[2] * scale, N_HEADS), ones(MIX_WIDTH)),
              (cols['c_q'], tile(qk_gain[4] * scale, N_HEADS), ones(MIX_WIDTH)),
              (cols['c_k'], tile(qk_gain[5], N_HEADS), ones(MIX_WIDTH)),
              (cols['a_ks'], tile(qk_gain[1], KV_GROUPS), ones(kv)),
              (cols['a_kw'], tile(qk_gain[1], KV_GROUPS), ones(kv)),
              (cols['b_k'], tile(qk_gain[3], KV_GROUPS), ones(kv)),
              (cols['a_vs'], zeros(kv), zeros(kv)),
              (cols['c_v'], zeros(MIX_WIDTH), zeros(MIX_WIDTH)),
              (cols['a_kc'], zeros(kv), zeros(kv)), (cols['a_vc'], zeros(kv), zeros(kv)),
              (cols['a_vw'], zeros(kv), zeros(kv)), (cols['b_v'], zeros(kv), zeros(kv))]
    w_qkv = jnp.concatenate([p[0] for p in pieces], axis=1).astype(BF16)
    n_qkv = w_qkv.shape[1]
    aux = jnp.zeros((8, n_qkv), F32)
    aux = aux.at[0].set(jnp.concatenate([p[1] for p in pieces])).at[1].set(jnp.concatenate([p[2] for p in pieces]))
    d = w_in.shape[0]
    w_gates = jnp.concatenate([cols['a_gate'], cols['c_f'],
                               jnp.zeros((d, LANES - N_GATE_COLS - N_HEADS), F32)], axis=1).astype(BF16)
    wb = jnp.stack([_pair_cols(w_branch[0].T).T, _pair_cols(w_branch[1].T).T, w_branch[2]]).astype(BF16)
    return w_qkv, aux, w_gates, cols['merge'].astype(BF16), wb


def _prep_compress(cmp_pos, cmp_w1, cmp_w2, k_gain):
    def both_groups(w):
        z = jnp.zeros_like(w)
        return jnp.concatenate([jnp.concatenate([w, z], -1), jnp.concatenate([z, w], -1)], -2)

    pos = jnp.concatenate([cmp_pos, cmp_pos], -1)
    w1 = both_groups(cmp_w1.reshape(2, CMP_LEN, HEAD_DIM, CMP_HIDDEN)).astype(BF16)
    w2 = both_groups(cmp_w2).astype(BF16)
    aux = jnp.zeros((2, 8, LANES), F32)
    aux = aux.at[0, 0].set(jnp.tile(k_gain, KV_GROUPS)).at[0, 1].set(1.0)
    return pos, w1, w2, aux


def kernel(x, rel_bias, norm_mix, norm_ffn, w_in, forget_bias, qk_gain, cmp_pos, cmp_w1, cmp_w2, sinks, w_branch, w_out, w_gate, w_up, conv_w, conv_b, w_down):
    b, t, d = x.shape
    bt = b * t
    depth = w_in.shape[0]
    tabs = _static_tables(t)
    biases = _bias_tables(rel_bias, t)
    fox_tq = min(FOX_TQ, t)
    x2 = x.reshape(bt, d)
    h = _rmsnorm(x2, norm_mix[0])
    for l in range(depth):
        w_qkv, aux, w_gates, w_merge, wb = _prep_layer(w_in[l], qk_gain[l], w_branch[l])
        qkv = _inproj_qkv(h, w_qkv, aux, tabs['bd']).reshape(N_CB, b, t, LANES)
        graw, gsig = _inproj_gates(h, w_gates)
        mg = _inproj_merge(h, w_merge, d)

        cmp_kv = _compress(qkv, *_prep_compress(cmp_pos[l], cmp_w1[l], cmp_w2[l], qk_gain[l, 1]), tabs['bd'])
        o_cmp, qa = _cmp_select(qkv, cmp_kv, biases['cmp'], tabs['ovt'], tabs['n_slc'])
        o_slc = _slc_attention(qa, qkv, tabs['key_block'], biases['near'])
        o_win = _banded_attention(qkv, CB_AQ, CB_AKW, CB_AVW, biases['win'], False)

        swa_bias = biases['swa'].at[:, :, 0].set(sinks[l][:, None] * LOG2E)
        o_b = _banded_attention(qkv, CB_BQ, CB_BK, CB_BV, swa_bias, True)

        fb_row = jnp.zeros((1, LANES), F32).at[0, N_GATE_COLS:N_GATE_COLS + N_HEADS].set(forget_bias[l])
        o_c = _fox_attention(qkv, *_fox_prep(graw, fb_row, b, t), fox_tq)

        flat = lambda o: o.reshape(N_QBLK, bt, LANES)
        x2 = _merge(flat(o_cmp), flat(o_slc), flat(o_win), flat(o_b), flat(o_c), gsig, tabs['eg'], mg, wb,
                    w_out[l].astype(BF16), x2)
        x2, *h = _ffn(x2, norm_ffn[l], norm_mix[l + 1] if l + 1 < depth else None, w_gate[l].astype(BF16),
                      w_up[l].astype(BF16), conv_w[l], conv_b[l], w_down[l].astype(BF16), t)
        h = h[0] if h else None
    return x2.reshape(b, t, d)
```

```python
import functools
import math

import numpy as np
import jax
import jax.numpy as jnp
from jax import lax
from jax.experimental import pallas as pl
from jax.experimental.pallas import tpu as pltpu

F32 = jnp.float32
BF16 = jnp.bfloat16

HEAD_DIM = 64
LANES = 128
N_HEADS = 8
KV_GROUPS = 2
HEADS_PER_GROUP = N_HEADS // KV_GROUPS
N_QBLK = N_HEADS * HEAD_DIM // LANES
MIX_WIDTH = N_HEADS * HEAD_DIM
N_BRANCH = 3
Q_BLOCK = 128
CMP_LEN = 32
CMP_STRIDE = 16
CMP_HIDDEN = 256
SLC_LEN = 64
SLC_TOPK = 8
NSA_WINDOW = 256
SWA_WINDOW = 128
N_BUCKETS = 32
MAX_DISTANCE = 128
CONV_WIDTH = 3
RMS_EPS = 1e-6
LOG2E = math.log2(math.e)
NEG = -1e30
FORCE_SCORE = 1e9
TAKEN = -3e38

VMEM_LIMIT = 48 * 1024 * 1024
MM_CHUNK = 256
MM_ROWS = 512
ROW_TILE = 1024
NORM_ROW_TILE = 2048
QKV_ROW_TILE, QKV_COL_TILE = 2048, 512
MERGE_GATE_COL_TILE = 1024
BRANCH_MERGE_ROW_TILE = 512
FFN_TF = 1024
CMP_TQ = 256
SLC_TILE = 256
SLC_QGROUP = 8
BAND_TQ = 256
FOX_TQ = 512
FOX_PREP_CHUNK = 512
EXPAND_CHUNK = 4096

CB_AQ, CB_BQ, CB_CQ, CB_CK = 0, 4, 8, 12
CB_AKS, CB_AKW, CB_BK, CB_AVS = 16, 17, 18, 19
CB_CV, CB_AKC, CB_AVC, CB_AVW, CB_BV = 20, 24, 25, 26, 27
N_CB = 28
N_NORM_CB = 20
N_GATE_COLS = N_HEADS * N_BRANCH


def _cparams(*sem):
    return pltpu.CompilerParams(dimension_semantics=sem, vmem_limit_bytes=VMEM_LIMIT)


def _split_hi_lo(v):
    hi = v.astype(BF16)
    return hi, (v - hi.astype(F32)).astype(BF16)


def _dot(a, b):
    return jnp.dot(a, b, preferred_element_type=F32)


def _dot_nt(a, b):
    return lax.dot_general(a, b, (((1,), (1,)), ((), ())), preferred_element_type=F32)


def _gelu_tanh(x):
    return x * (0.5 * (1.0 + jnp.tanh(math.sqrt(2.0 / math.pi) * (x + 0.044715 * (x * x * x)))))


def _head_rms(y, bd, gain):
    ms = _dot((y * y).astype(BF16), bd) * (1.0 / HEAD_DIM)
    return (y * lax.rsqrt(ms + RMS_EPS)) * gain


def _lane_half_mask(g):
    return (lax.broadcasted_iota(jnp.int32, (1, LANES), 1) // HEAD_DIM) == g


def _merge_group_outputs(o_ref, o_g0, o_g1):
    low = _lane_half_mask(0)
    tq = o_ref.shape[2]
    for p in range(N_QBLK):
        rows = slice(p * tq, (p + 1) * tq)
        o_ref[p, 0] = jnp.where(low, o_g0[rows], o_g1[rows]).astype(o_ref.dtype)


def _rmsnorm_kernel(x_ref, g_ref, o_ref):
    x = x_ref[...]
    ms = jnp.mean(x * x, axis=-1, keepdims=True)
    o_ref[...] = ((x * lax.rsqrt(ms + RMS_EPS)) * g_ref[...]).astype(o_ref.dtype)


def _rmsnorm(x2, g, tm=NORM_ROW_TILE):
    bt, d = x2.shape
    tm = min(tm, bt)
    return pl.pallas_call(
        _rmsnorm_kernel,
        grid=(bt // tm,),
        in_specs=[pl.BlockSpec((tm, d), lambda i: (i, 0)), pl.BlockSpec((1, d), lambda i: (0, 0))],
        out_specs=pl.BlockSpec((tm, d), lambda i: (i, 0)),
        out_shape=jax.ShapeDtypeStruct((bt, d), BF16),
        compiler_params=_cparams("parallel"),
        name="rmsnorm",
    )(x2, g.reshape(1, d))


def _inproj_qkv_kernel(h_ref, w_ref, aux_ref, bd_ref, o_ref, *, n_norm_tiles):
    per_chunk = MM_CHUNK // LANES

    def emit(normalise):
        for r in range(0, h_ref.shape[0], MM_ROWS):
            rows = slice(r, r + MM_ROWS)
            y_all = _dot(h_ref[rows, :], w_ref[...])
            for c in range(w_ref.shape[1] // MM_CHUNK):
                cols = slice(c * MM_CHUNK, (c + 1) * MM_CHUNK)
                y = y_all[:, cols]
                if normalise:
                    normed = _head_rms(y, bd_ref[...], aux_ref[0:1, cols])
                    y = jnp.where(aux_ref[1:2, cols] > 0.5, normed, y)
                for s in range(per_chunk):
                    o_ref[c * per_chunk + s, rows, :] = y[:, s * LANES:(s + 1) * LANES].astype(o_ref.dtype)

    pl.when(pl.program_id(1) < n_norm_tiles)(lambda: emit(True))
    pl.when(pl.program_id(1) >= n_norm_tiles)(lambda: emit(False))


def _inproj_qkv(h, w, aux, bd, tm=QKV_ROW_TILE, tn=QKV_COL_TILE):
    bt, d = h.shape
    n = w.shape[1]
    n_sub = tn // LANES
    tm = min(tm, bt)
    assert N_NORM_CB % n_sub == 0 and tm % MM_ROWS == 0
    return pl.pallas_call(
        functools.partial(_inproj_qkv_kernel, n_norm_tiles=N_NORM_CB // n_sub),
        grid=(bt // tm, n // tn),
        in_specs=[pl.BlockSpec((tm, d), lambda i, j: (i, 0)),
                  pl.BlockSpec((d, tn), lambda i, j: (0, j)),
                  pl.BlockSpec((8, tn), lambda i, j: (0, j)),
                  pl.BlockSpec(bd.shape, lambda i, j: (0, 0))],
        out_specs=pl.BlockSpec((n_sub, tm, LANES), lambda i, j: (j, i, 0)),
        out_shape=jax.ShapeDtypeStruct((n // LANES, bt, LANES), BF16),
        compiler_params=_cparams("parallel", "arbitrary"),
        name="inproj_qkv",
    )(h, w, aux, bd)


def _inproj_gates_kernel(h_ref, w_ref, raw_ref, sig_ref):
    y = _dot(h_ref[...], w_ref[...])
    raw_ref[...] = y
    sig_ref[...] = jax.nn.sigmoid(y).astype(sig_ref.dtype)


def _inproj_gates(h, w, tm=ROW_TILE):
    bt, d = h.shape
    return pl.pallas_call(
        _inproj_gates_kernel,
        grid=(bt // tm,),
        in_specs=[pl.BlockSpec((tm, d), lambda i: (i, 0)),
                  pl.BlockSpec((d, LANES), lambda i: (0, 0))],
        out_specs=[pl.BlockSpec((tm, LANES), lambda i: (i, 0))] * 2,
        out_shape=[jax.ShapeDtypeStruct((bt, LANES), F32), jax.ShapeDtypeStruct((bt, LANES), BF16)],
        compiler_params=_cparams("parallel"),
        name="inproj_gates",
    )(h, w)


def _inproj_merge_kernel(h_ref, w_ref, o_ref):
    width = 2 * MM_CHUNK
    for c in range(w_ref.shape[1] // width):
        cols = slice(c * width, (c + 1) * width)
        o_ref[0, :, cols] = jax.nn.sigmoid(_dot(h_ref[...], w_ref[:, cols])).astype(o_ref.dtype)


def _inproj_merge(h, w, d_model, tm=ROW_TILE, tn=MERGE_GATE_COL_TILE):
    bt, d = h.shape
    n = w.shape[1]
    per = d_model // tn
    return pl.pallas_call(
        _inproj_merge_kernel,
        grid=(bt // tm, n // tn),
        in_specs=[pl.BlockSpec((tm, d), lambda i, j: (i, 0)),
                  pl.BlockSpec((d, tn), lambda i, j: (0, j))],
        out_specs=pl.BlockSpec((1, tm, tn), lambda i, j: (j // per, i, j % per)),
        out_shape=jax.ShapeDtypeStruct((n // d_model, bt, d_model), BF16),
        compiler_params=_cparams("parallel", "arbitrary"),
        name="inproj_merge",
    )(h, w)


def _compress_kernel(z_ref, pos_ref, w1_ref, w2_ref, aux_ref, bd_ref, o_ref, zf_scr):
    n_chunks = o_ref.shape[2]
    zf_scr[...] = z_ref[0, 0].astype(F32)
    halves = []
    for half in range(CMP_LEN // CMP_STRIDE):
        acc = jnp.zeros((n_chunks, KV_GROUPS * CMP_HIDDEN), F32)
        for l in range(CMP_STRIDE):
            rows = zf_scr[pl.ds(l, n_chunks, stride=CMP_STRIDE), :]
            pos = pos_ref[0, half * CMP_STRIDE + l:half * CMP_STRIDE + l + 1, :]
            acc = acc + _dot((rows + pos).astype(BF16), w1_ref[0, half * CMP_STRIDE + l])
        halves.append(acc)
    hid = _gelu_tanh(halves[0] + pltpu.roll(halves[1], n_chunks - 1, 0))
    y = _dot(hid.astype(BF16), w2_ref[0])
    normed = _head_rms(y, bd_ref[0:LANES, 0:LANES], aux_ref[0, 0:1, :])
    o_ref[0, 0] = jnp.where(aux_ref[0, 1:2, :] > 0.5, normed, y).astype(o_ref.dtype)


def _compress(qkv, pos, w1, w2, aux, bd):
    _, b, t, _ = qkv.shape
    n_chunks = t // CMP_STRIDE
    return pl.pallas_call(
        _compress_kernel,
        grid=(2, b),
        in_specs=[pl.BlockSpec((1, 1, t, LANES), lambda w, i: (CB_AKC + w, i, 0, 0)),
                  pl.BlockSpec((1,) + pos.shape[1:], lambda w, i: (w, 0, 0)),
                  pl.BlockSpec((1,) + w1.shape[1:], lambda w, i: (w, 0, 0, 0)),
                  pl.BlockSpec((1,) + w2.shape[1:], lambda w, i: (w, 0, 0)),
                  pl.BlockSpec((1, 8, LANES), lambda w, i: (w, 0, 0)),
                  pl.BlockSpec(bd.shape, lambda w, i: (0, 0))],
        out_specs=pl.BlockSpec((1, 1, n_chunks, LANES), lambda w, i: (w, i, 0, 0)),
        out_shape=jax.ShapeDtypeStruct((2, b, n_chunks, LANES), BF16),
        scratch_shapes=[pltpu.VMEM((t, LANES), F32)],
        compiler_params=_cparams("parallel", "parallel"),
        name="nsa_compress",
    )(qkv, pos, w1, w2, aux, bd)


def _cmp_select_kernel(q_ref, kc_ref, vc_ref, bias_ref, ovt_ref, o_ref, qa_ref, *, n_slc, n_top):
    ncp = kc_ref.shape[2]
    rows = HEADS_PER_GROUP * Q_BLOCK
    jrow = lax.broadcasted_iota(jnp.int32, (n_slc, Q_BLOCK), 0)
    jrow_f = jrow.astype(F32)
    low = _lane_half_mask(0)
    for sub in range(q_ref.shape[2] // Q_BLOCK):
        rs = slice(sub * Q_BLOCK, (sub + 1) * Q_BLOCK)
        t0 = pl.program_id(1) * q_ref.shape[2] + sub * Q_BLOCK
        row_t = t0 + lax.broadcasted_iota(jnp.int32, (Q_BLOCK, min(LANES, ncp)), 0)
        row_valid = jnp.concatenate([row_t >= CMP_LEN - 1] * HEADS_PER_GROUP, axis=0)
        cur = (t0 + lax.broadcasted_iota(jnp.int32, (n_slc, Q_BLOCK), 1)) // SLC_LEN

        def force(imp):
            forced = jnp.where(jrow == cur, FORCE_SCORE, jnp.where(jrow == cur - 1, FORCE_SCORE, imp))
            return jnp.where(jrow == 0, FORCE_SCORE, forced)

        outs = []
        for g in range(KV_GROUPS):
            own_half = _lane_half_mask(g)
            hs = slice(g * HEADS_PER_GROUP, (g + 1) * HEADS_PER_GROUP)
            qop = jnp.concatenate([jnp.where(own_half, q_ref[p, 0, rs, :], jnp.zeros((Q_BLOCK, LANES), q_ref.dtype))
                                   for p in range(N_QBLK)], axis=0)
            s = _dot_nt(qop, kc_ref[0, 0]) + bias_ref[hs, rs, :].reshape(rows, ncp)
            m = _row_max(s)
            e = [jnp.exp2(c - m) for c in _lane_chunks(s)]
            row_sum = jnp.broadcast_to(jnp.sum(sum(e), axis=-1, keepdims=True), e[0].shape)
            inv = jnp.where(row_valid, 1.0 / row_sum, 0.0)
            p = jnp.concatenate([c * inv for c in e], axis=-1)
            outs.append(_dot(p.astype(BF16), vc_ref[0, 0]))
            psum = p[0:Q_BLOCK]
            for r in range(1, HEADS_PER_GROUP):
                psum = psum + p[r * Q_BLOCK:(r + 1) * Q_BLOCK]
            hi, lo = _split_hi_lo(psum)
            imp_t = (_dot_nt(ovt_ref[...], hi) + _dot_nt(ovt_ref[...], lo))[0:n_slc]
            score = jnp.where(jrow <= cur, force(imp_t), NEG)
            sel_bias = jnp.full((n_slc, Q_BLOCK), NEG, F32)
            for _ in range(n_top):
                best = jnp.max(score, axis=0, keepdims=True)
                first = jnp.min(jnp.where(score == best, jrow_f, float(n_slc)), axis=0, keepdims=True)
                hit = jrow_f == first
                sel_bias = jnp.where(hit, jnp.where(best > NEG / 2, 0.0, NEG), sel_bias)
                score = jnp.where(hit, TAKEN, score)
            if n_slc < HEAD_DIM:
                sel_bias = jnp.concatenate([sel_bias, jnp.full((HEAD_DIM - n_slc, Q_BLOCK), NEG, F32)], axis=0)
            sel_t = jnp.concatenate([sel_bias, sel_bias], axis=0).T.astype(qa_ref.dtype)
            for p in range(N_QBLK):
                qa_ref[g * HEADS_PER_GROUP + p, 0, rs, :] = jnp.where(own_half, q_ref[p, 0, rs, :], sel_t)
        for p in range(N_QBLK):
            prs = slice(p * Q_BLOCK, (p + 1) * Q_BLOCK)
            o_ref[p, 0, rs, :] = jnp.where(low, outs[0][prs], outs[1][prs]).astype(o_ref.dtype)


def _cmp_select(qkv, cmp_kv, bias_c, ovt, n_slc, tq=CMP_TQ):
    _, b, t, _ = qkv.shape
    ncp = cmp_kv.shape[2]
    assert n_slc <= HEAD_DIM
    return pl.pallas_call(
        functools.partial(_cmp_select_kernel, n_slc=n_slc, n_top=min(SLC_TOPK, n_slc)),
        grid=(b, t // tq),
        in_specs=[pl.BlockSpec((N_QBLK, 1, tq, LANES), lambda bi, i: (CB_AQ // N_QBLK, bi, i, 0)),
                  pl.BlockSpec((1, 1, ncp, LANES), lambda bi, i: (0, bi, 0, 0)),
                  pl.BlockSpec((1, 1, ncp, LANES), lambda bi, i: (1, bi, 0, 0)),
                  pl.BlockSpec((N_HEADS, tq, ncp), lambda bi, i: (0, i, 0)),
                  pl.BlockSpec((LANES, ncp), lambda bi, i: (0, 0))],
        out_specs=[pl.BlockSpec((N_QBLK, 1, tq, LANES), lambda bi, i: (0, bi, i, 0)),
                   pl.BlockSpec((N_HEADS, 1, tq, LANES), lambda bi, i: (0, bi, i, 0))],
        out_shape=[jax.ShapeDtypeStruct((N_QBLK, b, t, LANES), BF16),
                   jax.ShapeDtypeStruct((N_HEADS, b, t, LANES), BF16)],
        compiler_params=_cparams("parallel", "parallel"),
        name="nsa_cmp_select",
    )(qkv, cmp_kv, cmp_kv, bias_c, ovt)


def _lane_chunks(s):
    width = min(LANES, s.shape[1])
    return [s[:, c:c + width] for c in range(0, s.shape[1], width)]


def _row_max(s):
    chunks = _lane_chunks(s)
    m = chunks[0]
    for c in chunks[1:]:
        m = jnp.maximum(m, c)
    return jnp.broadcast_to(jnp.max(m, axis=-1, keepdims=True), m.shape)


def _exp2_shifted(s, m):
    return jnp.concatenate([jnp.exp2(c - m).astype(BF16) for c in _lane_chunks(s)], axis=-1)


def _flash_start(s, v):
    m = _row_max(s)
    return m, _dot(_exp2_shifted(s, m), v)


def _flash_update(s, m, acc, v):
    m_new = jnp.maximum(m, _row_max(s))
    return m_new, jnp.exp2(m - m_new) * acc + _dot(_exp2_shifted(s, m_new), v)


def _normalise(acc):
    return acc / pltpu.roll(acc, HEAD_DIM, 1)


def _augment_keys_values(ka_scr, va_scr, k2, xk, v2, base=0):
    low = _lane_half_mask(0)
    ones = jnp.ones_like(v2)
    ka_scr[base] = jnp.where(low, k2, xk)
    ka_scr[base + 1] = jnp.where(low, xk, k2)
    va_scr[base] = jnp.where(low, v2, ones)
    va_scr[base + 1] = jnp.where(low, ones, v2)


def _store_all(scr, s):
    for n, sn in enumerate(s):
        scr[n] = sn


def _load_all(scr):
    return tuple(scr[n] for n in range(scr.shape[0]))


def _worklist_sweep(n_steps, logits, update, sa_scr, sb_scr):
    def single(n, _):
        _store_all(sa_scr, logits(n))
        update(n, _load_all(sa_scr))
        return 0

    odd = n_steps % 2
    lax.fori_loop(0, odd, single, 0)
    _store_all(sa_scr, logits(odd))

    def pair(jp, _):
        n = odd + 2 * jp
        _store_all(sb_scr, logits(n + 1))
        update(n, _load_all(sa_scr))
        _store_all(sa_scr, logits(jnp.minimum(n + 2, n_steps - 1)))
        update(n + 1, _load_all(sb_scr))
        return 0

    lax.fori_loop(0, n_steps // 2, pair, 0)


def _slc_kernel(nfar_ref, jt_ref, it_ref, qa_ref, k_ref, v_ref, hot_ref, nb_ref, o_ref,
                ka_scr, va_scr, m_scr, acc_scr, sa_scr, sb_scr, *, tq):
    gi = pl.program_id(1)
    qg = qa_ref.shape[2] // tq
    rows = HEADS_PER_GROUP * tq
    @pl.when(gi == 0)
    def _():
        _augment_keys_values(ka_scr, va_scr, k_ref[0, 0], hot_ref[...], v_ref[0, 0])

    heads = [slice(g * HEADS_PER_GROUP, (g + 1) * HEADS_PER_GROUP) for g in range(KV_GROUPS)]

    def tile(n):
        return pl.ds(pl.multiple_of(n * tq, tq), tq)

    def logits(j, il):
        return tuple(_dot_nt(qa_ref[heads[g], 0, tile(il), :].reshape(rows, LANES), ka_scr[g, tile(j), :])
                     for g in range(KV_GROUPS))

    def biased(s, which, g):
        return (s.reshape(HEADS_PER_GROUP, tq, tq) + nb_ref[which, heads[g]]).reshape(rows, tq)

    def diag_update(il, s):
        for g in range(KV_GROUPS):
            m_scr[il, g], acc_scr[il, g] = _flash_start(biased(s[g], 1, g), va_scr[g, tile(gi * qg + il), :])

    _worklist_sweep(qg, lambda il: logits(gi * qg + il, il), diag_update, sa_scr, sb_scr)

    def fold(il, g, sg, j):
        m_scr[il, g], acc_scr[il, g] = _flash_update(sg, m_scr[il, g], acc_scr[il, g], va_scr[g, tile(j), :])

    def prev_update(il, s):
        i = gi * qg + il
        pen = jnp.where(i >= 1, 0.0, NEG)
        for g in range(KV_GROUPS):
            fold(il, g, biased(s[g], 0, g) + pen, jnp.maximum(i - 1, 0))

    _worklist_sweep(qg, lambda il: logits(jnp.maximum(gi * qg + il - 1, 0), il), prev_update, sa_scr, sb_scr)

    def far_update(n, s):
        for g in range(KV_GROUPS):
            fold(it_ref[gi, n], g, s[g], jt_ref[gi, n])

    _worklist_sweep(nfar_ref[gi], lambda n: logits(jt_ref[gi, n], it_ref[gi, n]), far_update, sa_scr, sb_scr)

    low = _lane_half_mask(0)
    for il in range(qg):
        o_g0, o_g1 = _normalise(acc_scr[il, 0]), _normalise(acc_scr[il, 1])
        for p in range(N_QBLK):
            head_rows = slice(p * tq, (p + 1) * tq)
            o_ref[p, 0, il * tq:(il + 1) * tq, :] = jnp.where(low, o_g0[head_rows], o_g1[head_rows]).astype(o_ref.dtype)


def _slc_attention(qa, qkv, hot2, near_bias, qg=SLC_QGROUP):
    _, b, t, _ = qa.shape
    tq = near_bias.shape[2]
    nq = t // tq
    qg = min(qg, nq)
    n_groups = nq // qg
    lists = [[(j, il) for j in range(nq) for il in range(qg) if j <= g * qg + il - 2] for g in range(n_groups)]
    width = max(2, max(len(steps) for steps in lists))
    table = lambda k: jnp.asarray([[s[k] for s in steps] + [0] * (width - len(steps)) for steps in lists], jnp.int32)
    nfar = jnp.asarray([len(steps) for steps in lists], jnp.int32)
    smem = pl.BlockSpec(memory_space=pltpu.SMEM)
    rows = HEADS_PER_GROUP * tq
    return pl.pallas_call(
        functools.partial(_slc_kernel, tq=tq),
        grid=(b, n_groups),
        in_specs=[smem, smem, smem,
                  pl.BlockSpec((N_HEADS, 1, qg * tq, LANES), lambda bi, gi: (0, bi, gi, 0)),
                  pl.BlockSpec((1, 1, t, LANES), lambda bi, gi: (CB_AKS, bi, 0, 0)),
                  pl.BlockSpec((1, 1, t, LANES), lambda bi, gi: (CB_AVS, bi, 0, 0)),
                  pl.BlockSpec(hot2.shape, lambda bi, gi: (0, 0)),
                  pl.BlockSpec(near_bias.shape, lambda bi, gi: (0, 0, 0, 0))],
        out_specs=pl.BlockSpec((N_QBLK, 1, qg * tq, LANES), lambda bi, gi: (0, bi, gi, 0)),
        out_shape=jax.ShapeDtypeStruct((N_QBLK, b, t, LANES), BF16),
        scratch_shapes=[pltpu.VMEM((KV_GROUPS, t, LANES), BF16), pltpu.VMEM((KV_GROUPS, t, LANES), BF16),
                        pltpu.VMEM((qg, KV_GROUPS, rows, LANES), F32), pltpu.VMEM((qg, KV_GROUPS, rows, LANES), F32),
                        pltpu.VMEM((KV_GROUPS, rows, tq), F32), pltpu.VMEM((KV_GROUPS, rows, tq), F32)],
        compiler_params=_cparams("parallel", "arbitrary"),
        name="nsa_slc_attention",
    )(nfar, table(0), table(1), qa, qkv, qkv, hot2, near_bias)


def _banded_kernel(*refs, n_blk, use_sinks):
    q_ref = refs[0]
    k_refs = refs[1:1 + n_blk]
    v_refs = refs[1 + n_blk:1 + 2 * n_blk]
    bias_ref, o_ref = refs[1 + 2 * n_blk:]
    i = pl.program_id(1)
    tq = q_ref.shape[2]
    n_keys = n_blk * Q_BLOCK
    rows = HEADS_PER_GROUP * tq
    first_key = (i + 1) * tq - n_keys
    col = lax.broadcasted_iota(jnp.int32, (1, n_keys), 1)
    pad_pen = jnp.where(first_key + col >= 0, 0.0, NEG)
    kcat = jnp.concatenate([r[0, 0] for r in k_refs], axis=0)
    vcat = jnp.concatenate([r[0, 0] for r in v_refs], axis=0)
    if use_sinks:
        sink_row = lax.broadcasted_iota(jnp.int32, (n_keys, 1), 0) == 0
        kcat = jnp.where(sink_row, jnp.zeros_like(kcat), kcat)
        vcat = jnp.where(sink_row, jnp.zeros_like(vcat), vcat)
        pad_pen = jnp.where(col == 0, 0.0, pad_pen)
    outs = []
    for g in range(KV_GROUPS):
        own_half = _lane_half_mask(g)
        qop = jnp.concatenate([jnp.where(own_half, q_ref[p, 0], jnp.zeros_like(q_ref[p, 0]))
                               for p in range(N_QBLK)], axis=0)
        hs = slice(g * HEADS_PER_GROUP, (g + 1) * HEADS_PER_GROUP)
        s = _dot_nt(qop, kcat) + bias_ref[hs].reshape(rows, n_keys) + pad_pen
        e = _exp2_shifted(s, _row_max(s))
        acc = _dot(e, jnp.where(own_half, vcat, jnp.ones_like(vcat)))
        outs.append(_normalise(acc))
    _merge_group_outputs(o_ref, outs[0], outs[1])


def _banded_attention(qkv, cb_q, cb_k, cb_v, bias, use_sinks):
    _, b, t, _ = qkv.shape
    tq = bias.shape[1]
    n_blk = bias.shape[2] // Q_BLOCK
    per_tile = tq // Q_BLOCK

    def kv_spec(cb, back):
        return pl.BlockSpec((1, 1, Q_BLOCK, LANES),
                            lambda bi, i: (cb, bi, jnp.maximum((i + 1) * per_tile - 1 - back, 0), 0))

    backs = [n_blk - 1 - jb for jb in range(n_blk)]
    return pl.pallas_call(
        functools.partial(_banded_kernel, n_blk=n_blk, use_sinks=use_sinks),
        grid=(b, t // tq),
        in_specs=([pl.BlockSpec((N_QBLK, 1, tq, LANES), lambda bi, i: (cb_q // N_QBLK, bi, i, 0))]
                  + [kv_spec(cb_k, back) for back in backs] + [kv_spec(cb_v, back) for back in backs]
                  + [pl.BlockSpec(bias.shape, lambda bi, i: (0, 0, 0))]),
        out_specs=pl.BlockSpec((N_QBLK, 1, tq, LANES), lambda bi, i: (0, bi, i, 0)),
        out_shape=jax.ShapeDtypeStruct((N_QBLK, b, t, LANES), BF16),
        compiler_params=_cparams("parallel", "parallel"),
        name="banded_attention",
    )(qkv, *([qkv] * (2 * n_blk)), bias)


FOX_EXTRA = 6


def _fox_prep_kernel(g_ref, fb_ref, pq_ref, pk_ref, oq_ref, ok_ref, xq_ref, xk_ref, carry_scr):
    @pl.when(pl.program_id(1) == 0)
    def _():
        carry_scr[...] = jnp.zeros_like(carry_scr)

    x = g_ref[...] + fb_ref[...]
    logf = jnp.minimum(x, 0.0) - jnp.log(1.0 + jnp.exp(-jnp.abs(x)))
    ch = x.shape[0]
    lower = jnp.where(lax.broadcasted_iota(jnp.int32, (ch, ch), 0)
                      >= lax.broadcasted_iota(jnp.int32, (ch, ch), 1), 1.0, 0.0).astype(BF16)

    def split3(v):
        hi, rest = v.astype(BF16), v - v.astype(BF16).astype(F32)
        return (hi,) + _split_hi_lo(rest)

    terms = _dot(lower, jnp.concatenate(split3(logf), axis=-1))
    cs = sum(_lane_chunks(terms)) + carry_scr[0:1, :]
    carry_scr[0:1, :] = cs[ch - 1:ch, :]
    parts = jnp.concatenate(split3(cs * LOG2E), axis=-1)
    xq = _dot(parts, pq_ref[...]) + oq_ref[...]
    xk = _dot(parts, pk_ref[...]) + ok_ref[...]
    for p in range(N_QBLK):
        xq_ref[p, 0] = xq[:, p * LANES:(p + 1) * LANES].astype(xq_ref.dtype)
        xk_ref[p, 0] = xk[:, p * LANES:(p + 1) * LANES].astype(xk_ref.dtype)


def _fox_prep(graw, fb_row, b, t, chunk=FOX_PREP_CHUNK):
    chunk = min(chunk, t)
    nch = t // chunk
    pq = np.zeros((3 * LANES, N_QBLK * LANES), np.float32)
    pk = np.zeros((3 * LANES, N_QBLK * LANES), np.float32)
    oq = np.zeros((1, N_QBLK * LANES), np.float32)
    ok = np.zeros((1, N_QBLK * LANES), np.float32)
    for h in range(N_HEADS):
        col = (h // 2) * LANES + (HEAD_DIM if h % 2 == 0 else 0)
        for n in range(3):
            pq[n * LANES + N_GATE_COLS + h, col + n] = 1.0
            pk[n * LANES + N_GATE_COLS + h, col + 3 + n] = -1.0
        oq[0, col + 3:col + FOX_EXTRA] = 1.0
        ok[0, col:col + 3] = 1.0
    const = lambda a: pl.BlockSpec(a.shape, lambda bi, c: (0,) * a.ndim)
    out_spec = pl.BlockSpec((N_QBLK, 1, chunk, LANES), lambda bi, c: (0, bi, c, 0))
    return pl.pallas_call(
        _fox_prep_kernel,
        grid=(b, nch),
        in_specs=[pl.BlockSpec((chunk, LANES), lambda bi, c: (bi * nch + c, 0)),
                  const(fb_row), const(pq), const(pk), const(oq), const(ok)],
        out_specs=[out_spec, out_spec],
        out_shape=[jax.ShapeDtypeStruct((N_QBLK, b, t, LANES), BF16)] * 2,
        scratch_shapes=[pltpu.VMEM((8, LANES), F32)],
        compiler_params=_cparams("parallel", "arbitrary"),
        name="fox_prep",
    )(graw, fb_row, jnp.asarray(pq, BF16), jnp.asarray(pk, BF16), jnp.asarray(oq), jnp.asarray(ok))


def _fox_kernel(jt_ref, it_ref, q_ref, xq_ref, k_ref, xk_ref, v_ref, o_ref,
                qa_scr, ka_scr, va_scr, m_scr, acc_scr, sa_scr, sb_scr, *, tq, n_off):
    nq = q_ref.shape[2] // tq
    low = _lane_half_mask(0)
    _augment_keys_values(ka_scr, va_scr, k_ref[0, 0], xk_ref[0, 0], v_ref[0, 0])
    q2, xq = q_ref[0, 0], xq_ref[0, 0]
    qa_scr[0] = jnp.where(low, q2, xq)
    qa_scr[1] = jnp.where(low, xq, q2)
    causal_pen = jnp.where(lax.broadcasted_iota(jnp.int32, (tq, tq), 0)
                           >= lax.broadcasted_iota(jnp.int32, (tq, tq), 1), 0.0, NEG)

    def tile(n):
        return pl.ds(pl.multiple_of(n * tq, tq), tq)

    def logits(j, i):
        return tuple(_dot_nt(qa_scr[h, tile(i), :], ka_scr[h, tile(j), :]) for h in range(2))

    def diag_update(i, s):
        for h in range(2):
            m_scr[i, h], acc_scr[i, h] = _flash_start(s[h] + causal_pen, va_scr[h, tile(i), :])

    _worklist_sweep(nq, lambda n: logits(n, n), diag_update, sa_scr, sb_scr)

    def off_update(n, s):
        j, i = jt_ref[n], it_ref[n]
        for h in range(2):
            m_scr[i, h], acc_scr[i, h] = _flash_update(s[h], m_scr[i, h], acc_scr[i, h], va_scr[h, tile(j), :])

    _worklist_sweep(n_off, lambda n: logits(jt_ref[n], it_ref[n]), off_update, sa_scr, sb_scr)

    def finish(i, _):
        o_ref[0, 0, tile(i), :] = jnp.where(low, _normalise(acc_scr[i, 0]), _normalise(acc_scr[i, 1])).astype(o_ref.dtype)
        return 0

    lax.fori_loop(0, nq, finish, 0)


def _fox_attention(qkv, xq, xk, tq):
    _, b, t, _ = qkv.shape
    nq = t // tq
    pairs = [(j, i) for j in range(nq) for i in range(j + 1, nq)]
    n_off = len(pairs)
    pairs += [(0, 0)] * max(0, 2 - n_off)
    jt = jnp.asarray([p[0] for p in pairs], jnp.int32)
    it = jnp.asarray([p[1] for p in pairs], jnp.int32)
    smem = pl.BlockSpec(memory_space=pltpu.SMEM)
    seq = lambda cb: pl.BlockSpec((1, 1, t, LANES), lambda bi, p: (cb + p, bi, 0, 0))
    return pl.pallas_call(
        functools.partial(_fox_kernel, tq=tq, n_off=n_off),
        grid=(b, N_QBLK),
        in_specs=[smem, smem, seq(CB_CQ), seq(0), seq(CB_CK), seq(0), seq(CB_CV)],
        out_specs=pl.BlockSpec((1, 1, t, LANES), lambda bi, p: (p, bi, 0, 0)),
        out_shape=jax.ShapeDtypeStruct((N_QBLK, b, t, LANES), BF16),
        scratch_shapes=[pltpu.VMEM((2, t, LANES), BF16), pltpu.VMEM((2, t, LANES), BF16),
                        pltpu.VMEM((2, t, LANES), BF16),
                        pltpu.VMEM((nq, 2, tq, LANES), F32), pltpu.VMEM((nq, 2, tq, LANES), F32),
                        pltpu.VMEM((2, tq, tq), F32), pltpu.VMEM((2, tq, tq), F32)],
        compiler_params=_cparams("parallel", "parallel"),
        name="fox_attention",
    )(jt, it, qkv, xq, qkv, xk, qkv)


def _merge_kernel(ocmp_ref, oslc_ref, owin_ref, ob_ref, oc_ref, gs_ref, eg_ref, mg_ref, wb_ref, wo_ref, x_ref, o_ref):
    def cat(ref):
        return jnp.concatenate([ref[p] for p in range(N_QBLK)], axis=-1)

    ga = _dot(gs_ref[...], eg_ref[...])
    oa = sum(ga[:, n * MIX_WIDTH:(n + 1) * MIX_WIDTH] * cat(ref).astype(F32)
             for n, ref in enumerate((ocmp_ref, oslc_ref, owin_ref))).astype(BF16)
    branches = (oa, cat(ob_ref), cat(oc_ref))
    d = o_ref.shape[1]
    chunks = [slice(c, c + MM_CHUNK) for c in range(0, d, MM_CHUNK)]
    mix = jnp.concatenate(
        [sum(mg_ref[n, :, cols].astype(F32) * _dot(branches[n], wb_ref[n, :, cols]) for n in range(N_BRANCH))
         .astype(BF16) for cols in chunks], axis=-1)
    for cols in chunks:
        o_ref[:, cols] = x_ref[:, cols] + _dot(mix, wo_ref[:, cols])


def _merge(ocmp, oslc, owin, ob, oc, gsig, eg, mg, wb, wo, x2, tm=BRANCH_MERGE_ROW_TILE):
    bt, d = x2.shape
    o_spec = pl.BlockSpec((N_QBLK, tm, LANES), lambda i: (0, i, 0))
    return pl.pallas_call(
        _merge_kernel,
        grid=(bt // tm,),
        in_specs=[o_spec] * 5 + [pl.BlockSpec((tm, LANES), lambda i: (i, 0)),
                                 pl.BlockSpec(eg.shape, lambda i: (0, 0)),
                                 pl.BlockSpec((N_BRANCH, tm, d), lambda i: (0, i, 0)),
                                 pl.BlockSpec(wb.shape, lambda i: (0, 0, 0)),
                                 pl.BlockSpec(wo.shape, lambda i: (0, 0)),
                                 pl.BlockSpec((tm, d), lambda i: (i, 0))],
        out_specs=pl.BlockSpec((tm, d), lambda i: (i, 0)),
        out_shape=jax.ShapeDtypeStruct((bt, d), F32),
        compiler_params=_cparams("parallel"),
        name="branch_merge",
    )(ocmp, oslc, owin, ob, oc, gsig, eg, mg, wb, wo, x2)


def _ffn_kernel(x_ref, xh_ref, g_ref, gn_ref, wg_ref, wu_ref, cw_ref, cb_ref, wd_ref, o_ref, *rest,
                tm, tiles_per_seq, emit_next):
    hn_ref = rest[0] if emit_next else None
    h_scr, hh_scr, a_scr, acc_scr = rest[-4:]
    i = pl.program_id(0)
    f = pl.program_id(1)

    def norm(x, gain_ref):
        ms = jnp.mean(x * x, axis=-1, keepdims=True)
        return ((x * lax.rsqrt(ms + RMS_EPS)) * gain_ref[...]).astype(BF16)

    @pl.when(f == 0)
    def _():
        h_scr[...] = norm(x_ref[...], g_ref)
        hh_scr[...] = norm(xh_ref[...], g_ref)
        acc_scr[...] = jnp.zeros_like(acc_scr)

    a_scr[0:8, :] = _dot(hh_scr[...], wg_ref[...]) * jnp.where(i % tiles_per_seq == 0, 0.0, 1.0)
    acts = []
    for c in range(wg_ref.shape[1] // MM_CHUNK):
        cols = slice(c * MM_CHUNK, (c + 1) * MM_CHUNK)
        a = _dot(h_scr[...], wg_ref[:, cols])
        a_scr[8:8 + tm, cols] = a
        conv = (cw_ref[0:1, cols] * a_scr[6:6 + tm, cols] + cw_ref[1:2, cols] * a_scr[7:7 + tm, cols]
                + cw_ref[2:3, cols] * a + cb_ref[:, cols])
        acts.append((_gelu_tanh(conv) * _dot(h_scr[...], wu_ref[:, cols])).astype(BF16))
    acc_scr[...] += _dot(jnp.concatenate(acts, axis=-1), wd_ref[...])

    @pl.when(f == pl.num_programs(1) - 1)
    def _():
        out = x_ref[...] + acc_scr[...]
        o_ref[...] = out
        if emit_next:
            hn_ref[...] = norm(out, gn_ref)


def _ffn(x2, g, g_next, wg, wu, cw, cb, wd, t, tm=ROW_TILE, tf=FFN_TF):
    bt, d = x2.shape
    emit_next = g_next is not None
    g_next = g_next if emit_next else g
    pad = -wg.shape[1] % tf
    wg, wu, cw = [jnp.pad(a, ((0, 0), (0, pad))) for a in (wg, wu, cw)]
    cb, wd = jnp.pad(cb, (0, pad)), jnp.pad(wd, ((0, pad), (0, 0)))
    ff = wg.shape[1]
    halo_blocks = tm // 8
    return pl.pallas_call(
        functools.partial(_ffn_kernel, tm=tm, tiles_per_seq=t // tm, emit_next=emit_next),
        grid=(bt // tm, ff // tf),
        in_specs=[pl.BlockSpec((tm, d), lambda i, f: (i, 0)),
                  pl.BlockSpec((8, d), lambda i, f: (jnp.maximum(i * halo_blocks - 1, 0), 0)),
                  pl.BlockSpec((1, d), lambda i, f: (0, 0)),
                  pl.BlockSpec((1, d), lambda i, f: (0, 0)),
                  pl.BlockSpec((d, tf), lambda i, f: (0, f)),
                  pl.BlockSpec((d, tf), lambda i, f: (0, f)),
                  pl.BlockSpec((CONV_WIDTH, tf), lambda i, f: (0, f)),
                  pl.BlockSpec((1, tf), lambda i, f: (0, f)),
                  pl.BlockSpec((tf, d), lambda i, f: (f, 0))],
        out_specs=[pl.BlockSpec((tm, d), lambda i, f: (i, 0))] * (1 + emit_next),
        out_shape=[jax.ShapeDtypeStruct((bt, d), F32), jax.ShapeDtypeStruct((bt, d), BF16)][:1 + emit_next],
        scratch_shapes=[pltpu.VMEM((tm, d), BF16), pltpu.VMEM((8, d), BF16),
                        pltpu.VMEM((tm + 8, tf), F32), pltpu.VMEM((tm, d), F32)],
        compiler_params=_cparams("parallel", "arbitrary"),
        name="conv_ffn",
    )(x2, x2, g.reshape(1, d), g_next.reshape(1, d), wg, wu, cw, cb.reshape(1, ff), wd)


def _t5_bucket_np(dist):
    max_exact = N_BUCKETS // 2
    d = np.maximum(dist, 0)
    ratio = np.log(np.maximum(d, 1) / max_exact) / math.log(MAX_DISTANCE / max_exact)
    large = np.minimum(max_exact + (ratio * (N_BUCKETS - max_exact)).astype(np.int64), N_BUCKETS - 1)
    return np.where(d < max_exact, d, large)


def _expand_kernel(tab_ref, idx_ref, o_ref):
    tab = tab_ref[...]
    onehot = jnp.where(lax.broadcasted_iota(jnp.int32, (tab.shape[1], idx_ref.shape[1]), 0) == idx_ref[...],
                       1.0, 0.0).astype(BF16)
    hi, rest = tab.astype(BF16), tab - tab.astype(BF16).astype(F32)
    mid, lo = _split_hi_lo(rest)
    heads = tab.shape[0]
    parts = _dot(jnp.concatenate([hi, mid, lo], axis=0), onehot)
    o_ref[...] = parts[0:heads] + parts[heads:2 * heads] + parts[2 * heads:3 * heads]


def _bias_from_dist(tab_t, dist, valid, chunk=EXPAND_CHUNK):
    heads = tab_t.shape[0]
    tab = jnp.zeros((heads, LANES), F32).at[:, :N_BUCKETS].set(tab_t).at[:, N_BUCKETS].set(NEG)
    idx = np.where(valid, _t5_bucket_np(dist), N_BUCKETS).reshape(1, -1).astype(np.int32)
    n = idx.shape[1]
    chunk = min(chunk, n)
    out = pl.pallas_call(
        _expand_kernel,
        grid=(n // chunk,),
        in_specs=[pl.BlockSpec((heads, LANES), lambda i: (0, 0)), pl.BlockSpec((1, chunk), lambda i: (0, i))],
        out_specs=pl.BlockSpec((heads, chunk), lambda i: (0, i)),
        out_shape=jax.ShapeDtypeStruct((heads, n), F32),
        compiler_params=_cparams("parallel"),
        name="rel_bias_expand",
    )(tab, jnp.asarray(idx))
    return out.reshape((heads,) + dist.shape)


def _cmp_bias_kernel(g_ref, o_ref):
    ncp = o_ref.shape[2]
    per_step = o_ref.shape[1] // Q_BLOCK
    for sub in range(per_step):
        shift = (pl.program_id(0) * per_step + sub) * (Q_BLOCK // CMP_STRIDE)
        for h in range(o_ref.shape[0]):
            o_ref[h, sub * Q_BLOCK:(sub + 1) * Q_BLOCK, :] = pltpu.roll(g_ref[h], shift, 1)[:, :ncp]


def _cmp_bias(g_tab, t):
    heads, _, width = g_tab.shape
    ncp = t // CMP_STRIDE
    return pl.pallas_call(
        _cmp_bias_kernel,
        grid=(t // CMP_TQ,),
        in_specs=[pl.BlockSpec(g_tab.shape, lambda i: (0, 0, 0))],
        out_specs=pl.BlockSpec((heads, CMP_TQ, ncp), lambda i: (0, i, 0)),
        out_shape=jax.ShapeDtypeStruct((heads, t, ncp), F32),
        compiler_params=_cparams("parallel"),
        name="cmp_bias_build",
    )(g_tab)


def _static_tables(t):
    ncp = t // CMP_STRIDE
    n_cmp = (t - CMP_LEN) // CMP_STRIDE + 1
    n_slc = t // SLC_LEN
    lane = np.arange(MM_CHUNK)
    bd = (lane[:, None] // HEAD_DIM == lane[None, :] // HEAD_DIM).astype(np.float32)
    eg = np.zeros((LANES, N_BRANCH * MIX_WIDTH), np.float32)
    for h in range(N_HEADS):
        base = (h % N_QBLK) * LANES + (h // N_QBLK) * HEAD_DIM
        for n in range(N_BRANCH):
            eg[h * N_BRANCH + n, n * MIX_WIDTH + base:n * MIX_WIDTH + base + HEAD_DIM] = 1.0
    c_start = np.arange(n_cmp) * CMP_STRIDE
    s_start = np.arange(n_slc) * SLC_LEN
    overlap = ((c_start[:, None] < s_start[None, :] + SLC_LEN) & (c_start[:, None] + CMP_LEN > s_start[None, :]))
    ovt = np.zeros((LANES, ncp), np.float32)
    ovt[:n_slc, :n_cmp] = overlap.T
    key_block = np.zeros((t, LANES), np.float32)
    key_block[np.arange(t), np.arange(t) // SLC_LEN] = 1.0
    key_block[np.arange(t), HEAD_DIM + np.arange(t) // SLC_LEN] = 1.0
    as_bf16 = lambda a: jnp.asarray(a, BF16)
    return dict(bd=as_bf16(bd), eg=as_bf16(eg), ovt=as_bf16(ovt), key_block=as_bf16(key_block), n_slc=n_slc)


def _bias_tables(rel_bias, t):
    n_cmp = (t - CMP_LEN) // CMP_STRIDE + 1
    ncp = t // CMP_STRIDE
    tab_a = rel_bias[:, :N_HEADS].T * LOG2E
    tab_b = rel_bias[:, N_HEADS:].T * LOG2E
    r = np.arange(Q_BLOCK)[:, None]

    def band(window):
        n_prev = -(-(window - 1) // Q_BLOCK)
        n_keys = n_prev * Q_BLOCK + BAND_TQ
        dist = (n_keys - BAND_TQ) + np.arange(BAND_TQ)[:, None] - np.arange(n_keys)[None, :]
        valid = (dist >= 0) & (dist < window)
        assert not valid[:, 0].any()
        return dist, valid

    m = np.arange(2 * ncp)
    m = np.where(m < ncp, m, m - 2 * ncp)[None, :]
    dist_c = r - (m * CMP_STRIDE + CMP_LEN - 1)
    assert n_cmp == ncp - 1 and (dist_c[:, m[0] == Q_BLOCK // CMP_STRIDE - 1] < 0).all()
    tile = np.arange(SLC_TILE)
    dist_n = np.stack([SLC_TILE * (1 - which) + tile[:, None] - tile[None, :] for which in range(2)])
    near = _bias_from_dist(tab_a - tab_a[:, N_BUCKETS - 1:], dist_n, dist_n >= 0)
    return dict(win=_bias_from_dist(tab_a, *band(NSA_WINDOW)),
                swa=_bias_from_dist(tab_b, *band(SWA_WINDOW)),
                cmp=_cmp_bias(_bias_from_dist(tab_a, dist_c, dist_c >= 0), t),
                near=jnp.transpose(near, (1, 0, 2, 3)))


def _pair_cols(w):
    parts = []
    for p in range(N_QBLK):
        parts += [w[..., p * HEAD_DIM:(p + 1) * HEAD_DIM],
                  w[..., (N_QBLK + p) * HEAD_DIM:(N_QBLK + p + 1) * HEAD_DIM]]
    return jnp.concatenate(parts, axis=-1)


def _prep_layer(w_in, qk_gain, w_branch):
    kv = KV_GROUPS * HEAD_DIM
    widths = [('a_q', MIX_WIDTH), ('a_kc', kv), ('a_vc', kv), ('a_ks', kv), ('a_vs', kv), ('a_kw', kv), ('a_vw', kv),
              ('a_gate', N_GATE_COLS), ('b_q', MIX_WIDTH), ('b_k', kv), ('b_v', kv),
              ('c_q', MIX_WIDTH), ('c_k', MIX_WIDTH), ('c_v', MIX_WIDTH), ('c_f', N_HEADS),
              ('merge', N_BRANCH * w_in.shape[0])]
    cols, off = {}, 0
    for name, w in widths:
        cols[name] = w_in[:, off:off + w]
        off += w
    scale = HEAD_DIM ** -0.5 * LOG2E
    tile = lambda gvec, n: jnp.tile(gvec, n)
    zeros = lambda n: jnp.zeros((n,), F32)
    ones = lambda n: jnp.ones((n,), F32)
    pieces = [(_pair_cols(cols['a_q']), tile(qk_gain[0] * scale, N_HEADS), ones(MIX_WIDTH)),
              (_pair_cols(cols['b_q']), tile(qk_gain[2] * scale, N_HEADS), ones(MIX_WIDTH)),
              (cols['c_q'], tile(qk_gain[4] * scale, N_HEADS), ones(MIX_WIDTH)),
              (cols['c_k'], tile(qk_gain[5], N_HEADS), ones(MIX_WIDTH)),
              (cols['a_ks'], tile(qk_gain[1], KV_GROUPS), ones(kv)),
              (cols['a_kw'], tile(qk_gain[1], KV_GROUPS), ones(kv)),
              (cols['b_k'], tile(qk_gain[3], KV_GROUPS), ones(kv)),
              (cols['a_vs'], zeros(kv), zeros(kv)),
              (cols['c_v'], zeros(MIX_WIDTH), zeros(MIX_WIDTH)),
              (cols['a_kc'], zeros(kv), zeros(kv)), (cols['a_vc'], zeros(kv), zeros(kv)),
              (cols['a_vw'], zeros(kv), zeros(kv)), (cols['b_v'], zeros(kv), zeros(kv))]
    w_qkv = jnp.concatenate([p[0] for p in pieces], axis=1).astype(BF16)
    n_qkv = w_qkv.shape[1]
    aux = jnp.zeros((8, n_qkv), F32)
    aux = aux.at[0].set(jnp.concatenate([p[1] for p in pieces])).at[1].set(jnp.concatenate([p[2] for p in pieces]))
    d = w_in.shape[0]
    w_gates = jnp.concatenate([cols['a_gate'], cols['c_f'],
                               jnp.zeros((d, LANES - N_GATE_COLS - N_HEADS), F32)], axis=1).astype(BF16)
    wb = jnp.stack([_pair_cols(w_branch[0].T).T, _pair_cols(w_branch[1].T).T, w_branch[2]]).astype(BF16)
    return w_qkv, aux, w_gates, cols['merge'].astype(BF16), wb


def _prep_compress(cmp_pos, cmp_w1, cmp_w2, k_gain):
    def both_groups(w):
        z = jnp.zeros_like(w)
        return jnp.concatenate([jnp.concatenate([w, z], -1), jnp.concatenate([z, w], -1)], -2)

    pos = jnp.concatenate([cmp_pos, cmp_pos], -1)
    w1 = both_groups(cmp_w1.reshape(2, CMP_LEN, HEAD_DIM, CMP_HIDDEN)).astype(BF16)
    w2 = both_groups(cmp_w2).astype(BF16)
    aux = jnp.zeros((2, 8, LANES), F32)
    aux = aux.at[0, 0].set(jnp.tile(k_gain, KV_GROUPS)).at[0, 1].set(1.0)
    return pos, w1, w2, aux


def kernel(x, rel_bias, norm_mix, norm_ffn, w_in, forget_bias, qk_gain, cmp_pos, cmp_w1, cmp_w2, sinks, w_branch, w_out, w_gate, w_up, conv_w, conv_b, w_down):
    b, t, d = x.shape
    bt = b * t
    depth = w_in.shape[0]
    tabs = _static_tables(t)
    biases = _bias_tables(rel_bias, t)
    fox_tq = min(FOX_TQ, t)
    x2 = x.reshape(bt, d)
    h = _rmsnorm(x2, norm_mix[0])
    for l in range(depth):
        w_qkv, aux, w_gates, w_merge, wb = _prep_layer(w_in[l], qk_gain[l], w_branch[l])
        qkv = _inproj_qkv(h, w_qkv, aux, tabs['bd']).reshape(N_CB, b, t, LANES)
        graw, gsig = _inproj_gates(h, w_gates)
        mg = _inproj_merge(h, w_merge, d)

        cmp_kv = _compress(qkv, *_prep_compress(cmp_pos[l], cmp_w1[l], cmp_w2[l], qk_gain[l, 1]), tabs['bd'])
        o_cmp, qa = _cmp_select(qkv, cmp_kv, biases['cmp'], tabs['ovt'], tabs['n_slc'])
        o_slc = _slc_attention(qa, qkv, tabs['key_block'], biases['near'])
        o_win = _banded_attention(qkv, CB_AQ, CB_AKW, CB_AVW, biases['win'], False)

        swa_bias = biases['swa'].at[:, :, 0].set(sinks[l][:, None] * LOG2E)
        o_b = _banded_attention(qkv, CB_BQ, CB_BK, CB_BV, swa_bias, True)

        fb_row = jnp.zeros((1, LANES), F32).at[0, N_GATE_COLS:N_GATE_COLS + N_HEADS].set(forget_bias[l])
        o_c = _fox_attention(qkv, *_fox_prep(graw, fb_row, b, t), fox_tq)

        flat = lambda o: o.reshape(N_QBLK, bt, LANES)
        x2 = _merge(flat(o_cmp), flat(o_slc), flat(o_win), flat(o_b), flat(o_c), gsig, tabs['eg'], mg, wb,
                    w_out[l].astype(BF16), x2)
        x2, *h = _ffn(x2, norm_ffn[l], norm_mix[l + 1] if l + 1 < depth else None, w_gate[l].astype(BF16),
                      w_up[l].astype(BF16), conv_w[l], conv_b[l], w_down[l].astype(BF16), t)
        h = h[0] if h else None
    return x2.reshape(b, t, d)
```

```python
import functools
import math

import numpy as np
import jax
import jax.numpy as jnp
from jax import lax
from jax.experimental import pallas as pl
from jax.experimental.pallas import tpu as pltpu

F32 = jnp.float32
BF16 = jnp.bfloat16

HEAD_DIM = 64
LANES = 128
N_HEADS = 8
KV_GROUPS = 2
HEADS_PER_GROUP = N_HEADS // KV_GROUPS
N_QBLK = N_HEADS * HEAD_DIM // LANES
MIX_WIDTH = N_HEADS * HEAD_DIM
N_BRANCH = 3
Q_BLOCK = 128
CMP_LEN = 32
CMP_STRIDE = 16
CMP_HIDDEN = 256
SLC_LEN = 64
SLC_TOPK = 8
NSA_WINDOW = 256
SWA_WINDOW = 128
N_BUCKETS = 32
MAX_DISTANCE = 128
CONV_WIDTH = 3
RMS_EPS = 1e-6
LOG2E = math.log2(math.e)
NEG = -1e30
FORCE_SCORE = 1e9
TAKEN = -3e38

VMEM_LIMIT = 48 * 1024 * 1024
MM_CHUNK = 256
MM_ROWS = 512
ROW_TILE = 1024
NORM_ROW_TILE = 2048
QKV_ROW_TILE, QKV_COL_TILE = 2048, 512
MERGE_GATE_ROW_TILE, MERGE_GATE_COL_TILE = 2048, 1024
BRANCH_MERGE_ROW_TILE = 512
FFN_TF = 1024
CMP_TQ = 256
SLC_TILE = 256
SLC_QGROUP = 8
BAND_TQ = 256
FOX_TQ = 512
FOX_PREP_CHUNK = 512
EXPAND_CHUNK = 4096

CB_AQ, CB_BQ, CB_CQ, CB_CK = 0, 4, 8, 12
CB_AKS, CB_AKW, CB_BK, CB_AVS = 16, 17, 18, 19
CB_CV, CB_AKC, CB_AVC, CB_AVW, CB_BV = 20, 24, 25, 26, 27
N_CB = 28
N_NORM_CB = 20
N_GATE_COLS = N_HEADS * N_BRANCH


def _cparams(*sem):
    return pltpu.CompilerParams(dimension_semantics=sem, vmem_limit_bytes=VMEM_LIMIT)


def _split_hi_lo(v):
    hi = v.astype(BF16)
    return hi, (v - hi.astype(F32)).astype(BF16)


def _dot(a, b):
    return jnp.dot(a, b, preferred_element_type=F32)


def _dot_nt(a, b):
    return lax.dot_general(a, b, (((1,), (1,)), ((), ())), preferred_element_type=F32)


def _sigmoid(x):
    return 0.5 * jnp.tanh(0.5 * x) + 0.5


def _gelu_tanh(x):
    return x * (0.5 * (1.0 + jnp.tanh(math.sqrt(2.0 / math.pi) * (x + 0.044715 * (x * x * x)))))


def _head_rms(y, bd, gain):
    ms = _dot((y * y).astype(BF16), bd) * (1.0 / HEAD_DIM)
    return (y * lax.rsqrt(ms + RMS_EPS)) * gain


def _lane_half_mask(g):
    return (lax.broadcasted_iota(jnp.int32, (1, LANES), 1) // HEAD_DIM) == g


def _merge_group_outputs(o_ref, o_g0, o_g1):
    low = _lane_half_mask(0)
    tq = o_ref.shape[2]
    for p in range(N_QBLK):
        rows = slice(p * tq, (p + 1) * tq)
        o_ref[p, 0] = jnp.where(low, o_g0[rows], o_g1[rows]).astype(o_ref.dtype)


def _rmsnorm_kernel(x_ref, g_ref, o_ref):
    x = x_ref[...]
    ms = jnp.mean(x * x, axis=-1, keepdims=True)
    o_ref[...] = ((x * lax.rsqrt(ms + RMS_EPS)) * g_ref[...]).astype(o_ref.dtype)


def _rmsnorm(x2, g, tm=NORM_ROW_TILE):
    bt, d = x2.shape
    tm = min(tm, bt)
    return pl.pallas_call(
        _rmsnorm_kernel,
        grid=(bt // tm,),
        in_specs=[pl.BlockSpec((tm, d), lambda i: (i, 0)), pl.BlockSpec((1, d), lambda i: (0, 0))],
        out_specs=pl.BlockSpec((tm, d), lambda i: (i, 0)),
        out_shape=jax.ShapeDtypeStruct((bt, d), BF16),
        compiler_params=_cparams("parallel"),
        name="rmsnorm",
    )(x2, g.reshape(1, d))


def _inproj_qkv_kernel(h_ref, w_ref, aux_ref, bd_ref, o_ref, *, n_norm_tiles):
    per_chunk = MM_CHUNK // LANES

    def emit(normalise):
        for r in range(0, h_ref.shape[0], MM_ROWS):
            rows = slice(r, r + MM_ROWS)
            y_all = _dot(h_ref[rows, :], w_ref[...])
            for c in range(w_ref.shape[1] // MM_CHUNK):
                cols = slice(c * MM_CHUNK, (c + 1) * MM_CHUNK)
                y = y_all[:, cols]
                if normalise:
                    normed = _head_rms(y, bd_ref[...], aux_ref[0:1, cols])
                    y = jnp.where(aux_ref[1:2, cols] > 0.5, normed, y)
                for s in range(per_chunk):
                    o_ref[c * per_chunk + s, rows, :] = y[:, s * LANES:(s + 1) * LANES].astype(o_ref.dtype)

    pl.when(pl.program_id(1) < n_norm_tiles)(lambda: emit(True))
    pl.when(pl.program_id(1) >= n_norm_tiles)(lambda: emit(False))


def _inproj_qkv(h, w, aux, bd, tm=QKV_ROW_TILE, tn=QKV_COL_TILE):
    bt, d = h.shape
    n = w.shape[1]
    n_sub = tn // LANES
    tm = min(tm, bt)
    assert N_NORM_CB % n_sub == 0 and tm % MM_ROWS == 0
    return pl.pallas_call(
        functools.partial(_inproj_qkv_kernel, n_norm_tiles=N_NORM_CB // n_sub),
        grid=(bt // tm, n // tn),
        in_specs=[pl.BlockSpec((tm, d), lambda i, j: (i, 0)),
                  pl.BlockSpec((d, tn), lambda i, j: (0, j)),
                  pl.BlockSpec((8, tn), lambda i, j: (0, j)),
                  pl.BlockSpec(bd.shape, lambda i, j: (0, 0))],
        out_specs=pl.BlockSpec((n_sub, tm, LANES), lambda i, j: (j, i, 0)),
        out_shape=jax.ShapeDtypeStruct((n // LANES, bt, LANES), BF16),
        compiler_params=_cparams("parallel", "arbitrary"),
        name="inproj_qkv",
    )(h, w, aux, bd)


def _inproj_gates_kernel(h_ref, w_ref, raw_ref, sig_ref):
    y = _dot(h_ref[...], w_ref[...])
    raw_ref[...] = y
    sig_ref[...] = _sigmoid(y).astype(sig_ref.dtype)


def _inproj_gates(h, w, tm=ROW_TILE):
    bt, d = h.shape
    return pl.pallas_call(
        _inproj_gates_kernel,
        grid=(bt // tm,),
        in_specs=[pl.BlockSpec((tm, d), lambda i: (i, 0)),
                  pl.BlockSpec((d, LANES), lambda i: (0, 0))],
        out_specs=[pl.BlockSpec((tm, LANES), lambda i: (i, 0))] * 2,
        out_shape=[jax.ShapeDtypeStruct((bt, LANES), F32), jax.ShapeDtypeStruct((bt, LANES), BF16)],
        compiler_params=_cparams("parallel"),
        name="inproj_gates",
    )(h, w)


def _inproj_merge_kernel(h_ref, w_ref, o_ref):
    width = 2 * MM_CHUNK
    for c in range(w_ref.shape[1] // width):
        cols = slice(c * width, (c + 1) * width)
        o_ref[0, :, cols] = _sigmoid(_dot(h_ref[...], w_ref[:, cols])).astype(o_ref.dtype)


def _inproj_merge(h, w, d_model, tm=MERGE_GATE_ROW_TILE, tn=MERGE_GATE_COL_TILE):
    bt, d = h.shape
    tm = min(tm, bt)
    n = w.shape[1]
    per = d_model // tn
    return pl.pallas_call(
        _inproj_merge_kernel,
        grid=(bt // tm, n // tn),
        in_specs=[pl.BlockSpec((tm, d), lambda i, j: (i, 0)),
                  pl.BlockSpec((d, tn), lambda i, j: (0, j))],
        out_specs=pl.BlockSpec((1, tm, tn), lambda i, j: (j // per, i, j % per)),
        out_shape=jax.ShapeDtypeStruct((n // d_model, bt, d_model), BF16),
        compiler_params=_cparams("parallel", "arbitrary"),
        name="inproj_merge",
    )(h, w)


def _compress_kernel(z_ref, pos_ref, w1_ref, w2_ref, aux_ref, bd_ref, o_ref, zf_scr):
    n_chunks = o_ref.shape[2]
    zf_scr[...] = z_ref[0, 0].astype(F32)
    halves = []
    for half in range(CMP_LEN // CMP_STRIDE):
        acc = jnp.zeros((n_chunks, KV_GROUPS * CMP_HIDDEN), F32)
        for l in range(CMP_STRIDE):
            rows = zf_scr[pl.ds(l, n_chunks, stride=CMP_STRIDE), :]
            pos = pos_ref[0, half * CMP_STRIDE + l:half * CMP_STRIDE + l + 1, :]
            acc = acc + _dot((rows + pos).astype(BF16), w1_ref[0, half * CMP_STRIDE + l])
        halves.append(acc)
    hid = _gelu_tanh(halves[0] + pltpu.roll(halves[1], n_chunks - 1, 0))
    y = _dot(hid.astype(BF16), w2_ref[0])
    normed = _head_rms(y, bd_ref[0:LANES, 0:LANES], aux_ref[0, 0:1, :])
    o_ref[0, 0] = jnp.where(aux_ref[0, 1:2, :] > 0.5, normed, y).astype(o_ref.dtype)


def _compress(qkv, pos, w1, w2, aux, bd):
    _, b, t, _ = qkv.shape
    n_chunks = t // CMP_STRIDE
    return pl.pallas_call(
        _compress_kernel,
        grid=(2, b),
        in_specs=[pl.BlockSpec((1, 1, t, LANES), lambda w, i: (CB_AKC + w, i, 0, 0)),
                  pl.BlockSpec((1,) + pos.shape[1:], lambda w, i: (w, 0, 0)),
                  pl.BlockSpec((1,) + w1.shape[1:], lambda w, i: (w, 0, 0, 0)),
                  pl.BlockSpec((1,) + w2.shape[1:], lambda w, i: (w, 0, 0)),
                  pl.BlockSpec((1, 8, LANES), lambda w, i: (w, 0, 0)),
                  pl.BlockSpec(bd.shape, lambda w, i: (0, 0))],
        out_specs=pl.BlockSpec((1, 1, n_chunks, LANES), lambda w, i: (w, i, 0, 0)),
        out_shape=jax.ShapeDtypeStruct((2, b, n_chunks, LANES), BF16),
        scratch_shapes=[pltpu.VMEM((t, LANES), F32)],
        compiler_params=_cparams("parallel", "parallel"),
        name="nsa_compress",
    )(qkv, pos, w1, w2, aux, bd)


def _cmp_select_kernel(q_ref, kc_ref, vc_ref, bias_ref, ovt_ref, o_ref, qa_ref, *, n_slc, n_top):
    ncp = kc_ref.shape[2]
    rows = HEADS_PER_GROUP * Q_BLOCK
    jrow = lax.broadcasted_iota(jnp.int32, (n_slc, Q_BLOCK), 0)
    jrow_f = jrow.astype(F32)
    low = _lane_half_mask(0)
    for sub in range(q_ref.shape[2] // Q_BLOCK):
        rs = slice(sub * Q_BLOCK, (sub + 1) * Q_BLOCK)
        t0 = pl.program_id(1) * q_ref.shape[2] + sub * Q_BLOCK
        row_t = t0 + lax.broadcasted_iota(jnp.int32, (Q_BLOCK, min(LANES, ncp)), 0)
        row_valid = jnp.concatenate([row_t >= CMP_LEN - 1] * HEADS_PER_GROUP, axis=0)
        cur = (t0 + lax.broadcasted_iota(jnp.int32, (n_slc, Q_BLOCK), 1)) // SLC_LEN

        def force(imp):
            forced = jnp.where(jrow == cur, FORCE_SCORE, jnp.where(jrow == cur - 1, FORCE_SCORE, imp))
            return jnp.where(jrow == 0, FORCE_SCORE, forced)

        outs = []
        for g in range(KV_GROUPS):
            own_half = _lane_half_mask(g)
            hs = slice(g * HEADS_PER_GROUP, (g + 1) * HEADS_PER_GROUP)
            qop = jnp.concatenate([jnp.where(own_half, q_ref[p, 0, rs, :], jnp.zeros((Q_BLOCK, LANES), q_ref.dtype))
                                   for p in range(N_QBLK)], axis=0)
            s = _dot_nt(qop, kc_ref[0, 0]) + bias_ref[hs, rs, :].reshape(rows, ncp)
            m = _row_max(s)
            e = [jnp.exp2(c - m) for c in _lane_chunks(s)]
            row_sum = jnp.broadcast_to(jnp.sum(sum(e), axis=-1, keepdims=True), e[0].shape)
            inv = jnp.where(row_valid, 1.0 / row_sum, 0.0)
            p = jnp.concatenate([c * inv for c in e], axis=-1)
            outs.append(_dot(p.astype(BF16), vc_ref[0, 0]))
            psum = p[0:Q_BLOCK]
            for r in range(1, HEADS_PER_GROUP):
                psum = psum + p[r * Q_BLOCK:(r + 1) * Q_BLOCK]
            hi, lo = _split_hi_lo(psum)
            imp_t = (_dot_nt(ovt_ref[...], hi) + _dot_nt(ovt_ref[...], lo))[0:n_slc]
            score = jnp.where(jrow <= cur, force(imp_t), NEG)
            sel_bias = jnp.full((n_slc, Q_BLOCK), NEG, F32)
            for _ in range(n_top):
                best = jnp.max(score, axis=0, keepdims=True)
                first = jnp.min(jnp.where(score == best, jrow_f, float(n_slc)), axis=0, keepdims=True)
                hit = jrow_f == first
                sel_bias = jnp.where(hit, jnp.where(best > NEG / 2, 0.0, NEG), sel_bias)
                score = jnp.where(hit, TAKEN, score)
            if n_slc < HEAD_DIM:
                sel_bias = jnp.concatenate([sel_bias, jnp.full((HEAD_DIM - n_slc, Q_BLOCK), NEG, F32)], axis=0)
            sel_t = jnp.concatenate([sel_bias, sel_bias], axis=0).T.astype(qa_ref.dtype)
            for p in range(N_QBLK):
                qa_ref[g * HEADS_PER_GROUP + p, 0, rs, :] = jnp.where(own_half, q_ref[p, 0, rs, :], sel_t)
        for p in range(N_QBLK):
            prs = slice(p * Q_BLOCK, (p + 1) * Q_BLOCK)
            o_ref[p, 0, rs, :] = jnp.where(low, outs[0][prs], outs[1][prs]).astype(o_ref.dtype)


def _cmp_select(qkv, cmp_kv, bias_c, ovt, n_slc, tq=CMP_TQ):
    _, b, t, _ = qkv.shape
    ncp = cmp_kv.shape[2]
    assert n_slc <= HEAD_DIM
    return pl.pallas_call(
        functools.partial(_cmp_select_kernel, n_slc=n_slc, n_top=min(SLC_TOPK, n_slc)),
        grid=(b, t // tq),
        in_specs=[pl.BlockSpec((N_QBLK, 1, tq, LANES), lambda bi, i: (CB_AQ // N_QBLK, bi, i, 0)),
                  pl.BlockSpec((1, 1, ncp, LANES), lambda bi, i: (0, bi, 0, 0)),
                  pl.BlockSpec((1, 1, ncp, LANES), lambda bi, i: (1, bi, 0, 0)),
                  pl.BlockSpec((N_HEADS, tq, ncp), lambda bi, i: (0, i, 0)),
                  pl.BlockSpec((LANES, ncp), lambda bi, i: (0, 0))],
        out_specs=[pl.BlockSpec((N_QBLK, 1, tq, LANES), lambda bi, i: (0, bi, i, 0)),
                   pl.BlockSpec((N_HEADS, 1, tq, LANES), lambda bi, i: (0, bi, i, 0))],
        out_shape=[jax.ShapeDtypeStruct((N_QBLK, b, t, LANES), BF16),
                   jax.ShapeDtypeStruct((N_HEADS, b, t, LANES), BF16)],
        compiler_params=_cparams("parallel", "parallel"),
        name="nsa_cmp_select",
    )(qkv, cmp_kv, cmp_kv, bias_c, ovt)


def _lane_chunks(s):
    width = min(LANES, s.shape[1])
    return [s[:, c:c + width] for c in range(0, s.shape[1], width)]


def _row_max(s):
    chunks = _lane_chunks(s)
    m = chunks[0]
    for c in chunks[1:]:
        m = jnp.maximum(m, c)
    return jnp.broadcast_to(jnp.max(m, axis=-1, keepdims=True), m.shape)


def _exp2_shifted(s, m):
    return jnp.concatenate([jnp.exp2(c - m).astype(BF16) for c in _lane_chunks(s)], axis=-1)


def _flash_start(s, v):
    m = _row_max(s)
    return m, _dot(_exp2_shifted(s, m), v)


def _flash_update(s, m, acc, v):
    m_new = jnp.maximum(m, _row_max(s))
    return m_new, jnp.exp2(m - m_new) * acc + _dot(_exp2_shifted(s, m_new), v)


def _normalise(acc):
    return acc / pltpu.roll(acc, HEAD_DIM, 1)


def _augment_keys_values(ka_scr, va_scr, k2, xk, v2, base=0):
    low = _lane_half_mask(0)
    ones = jnp.ones_like(v2)
    ka_scr[base] = jnp.where(low, k2, xk)
    ka_scr[base + 1] = jnp.where(low, xk, k2)
    va_scr[base] = jnp.where(low, v2, ones)
    va_scr[base + 1] = jnp.where(low, ones, v2)


def _store_all(scr, s):
    for n, sn in enumerate(s):
        scr[n] = sn


def _load_all(scr):
    return tuple(scr[n] for n in range(scr.shape[0]))


def _worklist_sweep(n_steps, logits, update, sa_scr, sb_scr):
    def single(n, _):
        _store_all(sa_scr, logits(n))
        update(n, _load_all(sa_scr))
        return 0

    odd = n_steps % 2
    lax.fori_loop(0, odd, single, 0)
    _store_all(sa_scr, logits(odd))

    def pair(jp, _):
        n = odd + 2 * jp
        _store_all(sb_scr, logits(n + 1))
        update(n, _load_all(sa_scr))
        _store_all(sa_scr, logits(jnp.minimum(n + 2, n_steps - 1)))
        update(n + 1, _load_all(sb_scr))
        return 0

    lax.fori_loop(0, n_steps // 2, pair, 0)


def _slc_kernel(nfar_ref, jt_ref, it_ref, qa_ref, k_ref, v_ref, hot_ref, nb_ref, o_ref,
                ka_scr, va_scr, m_scr, acc_scr, sa_scr, sb_scr, *, tq):
    gi = pl.program_id(1)
    qg = qa_ref.shape[2] // tq
    rows = HEADS_PER_GROUP * tq
    @pl.when(gi == 0)
    def _():
        _augment_keys_values(ka_scr, va_scr, k_ref[0, 0], hot_ref[...], v_ref[0, 0])

    heads = [slice(g * HEADS_PER_GROUP, (g + 1) * HEADS_PER_GROUP) for g in range(KV_GROUPS)]

    def tile(n):
        return pl.ds(pl.multiple_of(n * tq, tq), tq)

    def logits(j, il):
        return tuple(_dot_nt(qa_ref[heads[g], 0, tile(il), :].reshape(rows, LANES), ka_scr[g, tile(j), :])
                     for g in range(KV_GROUPS))

    def biased(s, which, g):
        return (s.reshape(HEADS_PER_GROUP, tq, tq) + nb_ref[which, heads[g]]).reshape(rows, tq)

    def diag_update(il, s):
        for g in range(KV_GROUPS):
            m_scr[il, g], acc_scr[il, g] = _flash_start(biased(s[g], 1, g), va_scr[g, tile(gi * qg + il), :])

    _worklist_sweep(qg, lambda il: logits(gi * qg + il, il), diag_update, sa_scr, sb_scr)

    def fold(il, g, sg, j):
        m_scr[il, g], acc_scr[il, g] = _flash_update(sg, m_scr[il, g], acc_scr[il, g], va_scr[g, tile(j), :])

    def prev_update(il, s):
        i = gi * qg + il
        pen = jnp.where(i >= 1, 0.0, NEG)
        for g in range(KV_GROUPS):
            fold(il, g, biased(s[g], 0, g) + pen, jnp.maximum(i - 1, 0))

    _worklist_sweep(qg, lambda il: logits(jnp.maximum(gi * qg + il - 1, 0), il), prev_update, sa_scr, sb_scr)

    def far_update(n, s):
        for g in range(KV_GROUPS):
            fold(it_ref[gi, n], g, s[g], jt_ref[gi, n])

    _worklist_sweep(nfar_ref[gi], lambda n: logits(jt_ref[gi, n], it_ref[gi, n]), far_update, sa_scr, sb_scr)

    low = _lane_half_mask(0)
    for il in range(qg):
        o_g0, o_g1 = _normalise(acc_scr[il, 0]), _normalise(acc_scr[il, 1])
        for p in range(N_QBLK):
            head_rows = slice(p * tq, (p + 1) * tq)
            o_ref[p, 0, il * tq:(il + 1) * tq, :] = jnp.where(low, o_g0[head_rows], o_g1[head_rows]).astype(o_ref.dtype)


def _slc_attention(qa, qkv, hot2, near_bias, qg=SLC_QGROUP):
    _, b, t, _ = qa.shape
    tq = near_bias.shape[2]
    nq = t // tq
    qg = min(qg, nq)
    n_groups = nq // qg
    lists = [[(j, il) for j in range(nq) for il in range(qg) if j <= g * qg + il - 2] for g in range(n_groups)]
    width = max(2, max(len(steps) for steps in lists))
    table = lambda k: jnp.asarray([[s[k] for s in steps] + [0] * (width - len(steps)) for steps in lists], jnp.int32)
    nfar = jnp.asarray([len(steps) for steps in lists], jnp.int32)
    smem = pl.BlockSpec(memory_space=pltpu.SMEM)
    rows = HEADS_PER_GROUP * tq
    return pl.pallas_call(
        functools.partial(_slc_kernel, tq=tq),
        grid=(b, n_groups),
        in_specs=[smem, smem, smem,
                  pl.BlockSpec((N_HEADS, 1, qg * tq, LANES), lambda bi, gi: (0, bi, gi, 0)),
                  pl.BlockSpec((1, 1, t, LANES), lambda bi, gi: (CB_AKS, bi, 0, 0)),
                  pl.BlockSpec((1, 1, t, LANES), lambda bi, gi: (CB_AVS, bi, 0, 0)),
                  pl.BlockSpec(hot2.shape, lambda bi, gi: (0, 0)),
                  pl.BlockSpec(near_bias.shape, lambda bi, gi: (0, 0, 0, 0))],
        out_specs=pl.BlockSpec((N_QBLK, 1, qg * tq, LANES), lambda bi, gi: (0, bi, gi, 0)),
        out_shape=jax.ShapeDtypeStruct((N_QBLK, b, t, LANES), BF16),
        scratch_shapes=[pltpu.VMEM((KV_GROUPS, t, LANES), BF16), pltpu.VMEM((KV_GROUPS, t, LANES), BF16),
                        pltpu.VMEM((qg, KV_GROUPS, rows, LANES), F32), pltpu.VMEM((qg, KV_GROUPS, rows, LANES), F32),
                        pltpu.VMEM((KV_GROUPS, rows, tq), F32), pltpu.VMEM((KV_GROUPS, rows, tq), F32)],
        compiler_params=_cparams("parallel", "arbitrary"),
        name="nsa_slc_attention",
    )(nfar, table(0), table(1), qa, qkv, qkv, hot2, near_bias)


def _banded_kernel(*refs, n_blk, use_sinks):
    q_ref = refs[0]
    k_refs = refs[1:1 + n_blk]
    v_refs = refs[1 + n_blk:1 + 2 * n_blk]
    bias_ref, o_ref = refs[1 + 2 * n_blk:]
    i = pl.program_id(1)
    tq = q_ref.shape[2]
    n_keys = n_blk * Q_BLOCK
    rows = HEADS_PER_GROUP * tq
    first_key = (i + 1) * tq - n_keys
    col = lax.broadcasted_iota(jnp.int32, (1, n_keys), 1)
    pad_pen = jnp.where(first_key + col >= 0, 0.0, NEG)
    kcat = jnp.concatenate([r[0, 0] for r in k_refs], axis=0)
    vcat = jnp.concatenate([r[0, 0] for r in v_refs], axis=0)
    if use_sinks:
        sink_row = lax.broadcasted_iota(jnp.int32, (n_keys, 1), 0) == 0
        kcat = jnp.where(sink_row, jnp.zeros_like(kcat), kcat)
        vcat = jnp.where(sink_row, jnp.zeros_like(vcat), vcat)
        pad_pen = jnp.where(col == 0, 0.0, pad_pen)
    outs = []
    for g in range(KV_GROUPS):
        own_half = _lane_half_mask(g)
        qop = jnp.concatenate([jnp.where(own_half, q_ref[p, 0], jnp.zeros_like(q_ref[p, 0]))
                               for p in range(N_QBLK)], axis=0)
        hs = slice(g * HEADS_PER_GROUP, (g + 1) * HEADS_PER_GROUP)
        s = _dot_nt(qop, kcat) + bias_ref[hs].reshape(rows, n_keys) + pad_pen
        e = _exp2_shifted(s, _row_max(s))
        acc = _dot(e, jnp.where(own_half, vcat, jnp.ones_like(vcat)))
        outs.append(_normalise(acc))
    _merge_group_outputs(o_ref, outs[0], outs[1])


def _banded_attention(qkv, cb_q, cb_k, cb_v, bias, use_sinks):
    _, b, t, _ = qkv.shape
    tq = bias.shape[1]
    n_blk = bias.shape[2] // Q_BLOCK
    per_tile = tq // Q_BLOCK

    def kv_spec(cb, back):
        return pl.BlockSpec((1, 1, Q_BLOCK, LANES),
                            lambda bi, i: (cb, bi, jnp.maximum((i + 1) * per_tile - 1 - back, 0), 0))

    backs = [n_blk - 1 - jb for jb in range(n_blk)]
    return pl.pallas_call(
        functools.partial(_banded_kernel, n_blk=n_blk, use_sinks=use_sinks),
        grid=(b, t // tq),
        in_specs=([pl.BlockSpec((N_QBLK, 1, tq, LANES), lambda bi, i: (cb_q // N_QBLK, bi, i, 0))]
                  + [kv_spec(cb_k, back) for back in backs] + [kv_spec(cb_v, back) for back in backs]
                  + [pl.BlockSpec(bias.shape, lambda bi, i: (0, 0, 0))]),
        out_specs=pl.BlockSpec((N_QBLK, 1, tq, LANES), lambda bi, i: (0, bi, i, 0)),
        out_shape=jax.ShapeDtypeStruct((N_QBLK, b, t, LANES), BF16),
        compiler_params=_cparams("parallel", "parallel"),
        name="banded_attention",
    )(qkv, *([qkv] * (2 * n_blk)), bias)


FOX_EXTRA = 6


def _fox_prep_kernel(g_ref, fb_ref, pq_ref, pk_ref, oq_ref, ok_ref, xq_ref, xk_ref, carry_scr):
    @pl.when(pl.program_id(1) == 0)
    def _():
        carry_scr[...] = jnp.zeros_like(carry_scr)

    x = g_ref[...] + fb_ref[...]
    logf = jnp.minimum(x, 0.0) - jnp.log(1.0 + jnp.exp(-jnp.abs(x)))
    ch = x.shape[0]
    lower = jnp.where(lax.broadcasted_iota(jnp.int32, (ch, ch), 0)
                      >= lax.broadcasted_iota(jnp.int32, (ch, ch), 1), 1.0, 0.0).astype(BF16)

    def split3(v):
        hi, rest = v.astype(BF16), v - v.astype(BF16).astype(F32)
        return (hi,) + _split_hi_lo(rest)

    terms = _dot(lower, jnp.concatenate(split3(logf), axis=-1))
    cs = sum(_lane_chunks(terms)) + carry_scr[0:1, :]
    carry_scr[0:1, :] = cs[ch - 1:ch, :]
    parts = jnp.concatenate(split3(cs * LOG2E), axis=-1)
    xq = _dot(parts, pq_ref[...]) + oq_ref[...]
    xk = _dot(parts, pk_ref[...]) + ok_ref[...]
    for p in range(N_QBLK):
        xq_ref[p, 0] = xq[:, p * LANES:(p + 1) * LANES].astype(xq_ref.dtype)
        xk_ref[p, 0] = xk[:, p * LANES:(p + 1) * LANES].astype(xk_ref.dtype)


def _fox_prep(graw, fb_row, b, t, chunk=FOX_PREP_CHUNK):
    chunk = min(chunk, t)
    nch = t // chunk
    pq = np.zeros((3 * LANES, N_QBLK * LANES), np.float32)
    pk = np.zeros((3 * LANES, N_QBLK * LANES), np.float32)
    oq = np.zeros((1, N_QBLK * LANES), np.float32)
    ok = np.zeros((1, N_QBLK * LANES), np.float32)
    for h in range(N_HEADS):
        col = (h // 2) * LANES + (HEAD_DIM if h % 2 == 0 else 0)
        for n in range(3):
            pq[n * LANES + N_GATE_COLS + h, col + n] = 1.0
            pk[n * LANES + N_GATE_COLS + h, col + 3 + n] = -1.0
        oq[0, col + 3:col + FOX_EXTRA] = 1.0
        ok[0, col:col + 3] = 1.0
    const = lambda a: pl.BlockSpec(a.shape, lambda bi, c: (0,) * a.ndim)
    out_spec = pl.BlockSpec((N_QBLK, 1, chunk, LANES), lambda bi, c: (0, bi, c, 0))
    return pl.pallas_call(
        _fox_prep_kernel,
        grid=(b, nch),
        in_specs=[pl.BlockSpec((chunk, LANES), lambda bi, c: (bi * nch + c, 0)),
                  const(fb_row), const(pq), const(pk), const(oq), const(ok)],
        out_specs=[out_spec, out_spec],
        out_shape=[jax.ShapeDtypeStruct((N_QBLK, b, t, LANES), BF16)] * 2,
        scratch_shapes=[pltpu.VMEM((8, LANES), F32)],
        compiler_params=_cparams("parallel", "arbitrary"),
        name="fox_prep",
    )(graw, fb_row, jnp.asarray(pq, BF16), jnp.asarray(pk, BF16), jnp.asarray(oq), jnp.asarray(ok))


def _fox_kernel(jt_ref, it_ref, q_ref, xq_ref, k_ref, xk_ref, v_ref, o_ref,
                qa_scr, ka_scr, va_scr, m_scr, acc_scr, sa_scr, sb_scr, *, tq, n_off):
    nq = q_ref.shape[2] // tq
    low = _lane_half_mask(0)
    _augment_keys_values(ka_scr, va_scr, k_ref[0, 0], xk_ref[0, 0], v_ref[0, 0])
    q2, xq = q_ref[0, 0], xq_ref[0, 0]
    qa_scr[0] = jnp.where(low, q2, xq)
    qa_scr[1] = jnp.where(low, xq, q2)
    causal_pen = jnp.where(lax.broadcasted_iota(jnp.int32, (tq, tq), 0)
                           >= lax.broadcasted_iota(jnp.int32, (tq, tq), 1), 0.0, NEG)

    def tile(n):
        return pl.ds(pl.multiple_of(n * tq, tq), tq)

    def logits(j, i):
        return tuple(_dot_nt(qa_scr[h, tile(i), :], ka_scr[h, tile(j), :]) for h in range(2))

    def diag_update(i, s):
        for h in range(2):
            m_scr[i, h], acc_scr[i, h] = _flash_start(s[h] + causal_pen, va_scr[h, tile(i), :])

    _worklist_sweep(nq, lambda n: logits(n, n), diag_update, sa_scr, sb_scr)

    def off_update(n, s):
        j, i = jt_ref[n], it_ref[n]
        for h in range(2):
            m_scr[i, h], acc_scr[i, h] = _flash_update(s[h], m_scr[i, h], acc_scr[i, h], va_scr[h, tile(j), :])

    _worklist_sweep(n_off, lambda n: logits(jt_ref[n], it_ref[n]), off_update, sa_scr, sb_scr)

    def finish(i, _):
        o_ref[0, 0, tile(i), :] = jnp.where(low, _normalise(acc_scr[i, 0]), _normalise(acc_scr[i, 1])).astype(o_ref.dtype)
        return 0

    lax.fori_loop(0, nq, finish, 0)


def _fox_attention(qkv, xq, xk, tq):
    _, b, t, _ = qkv.shape
    nq = t // tq
    pairs = [(j, i) for j in range(nq) for i in range(j + 1, nq)]
    n_off = len(pairs)
    pairs += [(0, 0)] * max(0, 2 - n_off)
    jt = jnp.asarray([p[0] for p in pairs], jnp.int32)
    it = jnp.asarray([p[1] for p in pairs], jnp.int32)
    smem = pl.BlockSpec(memory_space=pltpu.SMEM)
    seq = lambda cb: pl.BlockSpec((1, 1, t, LANES), lambda bi, p: (cb + p, bi, 0, 0))
    return pl.pallas_call(
        functools.partial(_fox_kernel, tq=tq, n_off=n_off),
        grid=(b, N_QBLK),
        in_specs=[smem, smem, seq(CB_CQ), seq(0), seq(CB_CK), seq(0), seq(CB_CV)],
        out_specs=pl.BlockSpec((1, 1, t, LANES), lambda bi, p: (p, bi, 0, 0)),
        out_shape=jax.ShapeDtypeStruct((N_QBLK, b, t, LANES), BF16),
        scratch_shapes=[pltpu.VMEM((2, t, LANES), BF16), pltpu.VMEM((2, t, LANES), BF16),
                        pltpu.VMEM((2, t, LANES), BF16),
                        pltpu.VMEM((nq, 2, tq, LANES), F32), pltpu.VMEM((nq, 2, tq, LANES), F32),
                        pltpu.VMEM((2, tq, tq), F32), pltpu.VMEM((2, tq, tq), F32)],
        compiler_params=_cparams("parallel", "parallel"),
        name="fox_attention",
    )(jt, it, qkv, xq, qkv, xk, qkv)


def _merge_kernel(ocmp_ref, oslc_ref, owin_ref, ob_ref, oc_ref, gs_ref, eg_ref, mg_ref, wb_ref, wo_ref, x_ref, o_ref):
    def cat(ref):
        return jnp.concatenate([ref[p] for p in range(N_QBLK)], axis=-1)

    ga = _dot(gs_ref[...], eg_ref[...])
    oa = sum(ga[:, n * MIX_WIDTH:(n + 1) * MIX_WIDTH] * cat(ref).astype(F32)
             for n, ref in enumerate((ocmp_ref, oslc_ref, owin_ref))).astype(BF16)
    branches = (oa, cat(ob_ref), cat(oc_ref))
    d = o_ref.shape[1]
    chunks = [slice(c, c + MM_CHUNK) for c in range(0, d, MM_CHUNK)]
    mix = jnp.concatenate(
        [sum(mg_ref[n, :, cols].astype(F32) * _dot(branches[n], wb_ref[n, :, cols]) for n in range(N_BRANCH))
         .astype(BF16) for cols in chunks], axis=-1)
    for cols in chunks:
        o_ref[:, cols] = x_ref[:, cols] + _dot(mix, wo_ref[:, cols])


def _merge(ocmp, oslc, owin, ob, oc, gsig, eg, mg, wb, wo, x2, tm=BRANCH_MERGE_ROW_TILE):
    bt, d = x2.shape
    o_spec = pl.BlockSpec((N_QBLK, tm, LANES), lambda i: (0, i, 0))
    return pl.pallas_call(
        _merge_kernel,
        grid=(bt // tm,),
        in_specs=[o_spec] * 5 + [pl.BlockSpec((tm, LANES), lambda i: (i, 0)),
                                 pl.BlockSpec(eg.shape, lambda i: (0, 0)),
                                 pl.BlockSpec((N_BRANCH, tm, d), lambda i: (0, i, 0)),
                                 pl.BlockSpec(wb.shape, lambda i: (0, 0, 0)),
                                 pl.BlockSpec(wo.shape, lambda i: (0, 0)),
                                 pl.BlockSpec((tm, d), lambda i: (i, 0))],
        out_specs=pl.BlockSpec((tm, d), lambda i: (i, 0)),
        out_shape=jax.ShapeDtypeStruct((bt, d), F32),
        compiler_params=_cparams("parallel"),
        name="branch_merge",
    )(ocmp, oslc, owin, ob, oc, gsig, eg, mg, wb, wo, x2)


def _ffn_kernel(x_ref, xh_ref, g_ref, gn_ref, wg_ref, wu_ref, cw_ref, cb_ref, wd_ref, o_ref, *rest,
                tm, tiles_per_seq, emit_next):
    hn_ref = rest[0] if emit_next else None
    h_scr, hh_scr, a_scr, acc_scr = rest[-4:]
    i = pl.program_id(0)
    f = pl.program_id(1)

    def norm(x, gain_ref):
        ms = jnp.mean(x * x, axis=-1, keepdims=True)
        return ((x * lax.rsqrt(ms + RMS_EPS)) * gain_ref[...]).astype(BF16)

    @pl.when(f == 0)
    def _():
        h_scr[...] = norm(x_ref[...], g_ref)
        hh_scr[...] = norm(xh_ref[...], g_ref)
        acc_scr[...] = jnp.zeros_like(acc_scr)

    a_scr[0:8, :] = _dot(hh_scr[...], wg_ref[...]) * jnp.where(i % tiles_per_seq == 0, 0.0, 1.0)
    acts = []
    for c in range(wg_ref.shape[1] // MM_CHUNK):
        cols = slice(c * MM_CHUNK, (c + 1) * MM_CHUNK)
        a = _dot(h_scr[...], wg_ref[:, cols])
        a_scr[8:8 + tm, cols] = a
        conv = (cw_ref[0:1, cols] * a_scr[6:6 + tm, cols] + cw_ref[1:2, cols] * a_scr[7:7 + tm, cols]
                + cw_ref[2:3, cols] * a + cb_ref[:, cols])
        acts.append((_gelu_tanh(conv) * _dot(h_scr[...], wu_ref[:, cols])).astype(BF16))
    acc_scr[...] += _dot(jnp.concatenate(acts, axis=-1), wd_ref[...])

    @pl.when(f == pl.num_programs(1) - 1)
    def _():
        out = x_ref[...] + acc_scr[...]
        o_ref[...] = out
        if emit_next:
            hn_ref[...] = norm(out, gn_ref)


def _ffn(x2, g, g_next, wg, wu, cw, cb, wd, t, tm=ROW_TILE, tf=FFN_TF):
    bt, d = x2.shape
    emit_next = g_next is not None
    g_next = g_next if emit_next else g
    pad = -wg.shape[1] % tf
    wg, wu, cw = [jnp.pad(a, ((0, 0), (0, pad))) for a in (wg, wu, cw)]
    cb, wd = jnp.pad(cb, (0, pad)), jnp.pad(wd, ((0, pad), (0, 0)))
    ff = wg.shape[1]
    halo_blocks = tm // 8
    return pl.pallas_call(
        functools.partial(_ffn_kernel, tm=tm, tiles_per_seq=t // tm, emit_next=emit_next),
        grid=(bt // tm, ff // tf),
        in_specs=[pl.BlockSpec((tm, d), lambda i, f: (i, 0)),
                  pl.BlockSpec((8, d), lambda i, f: (jnp.maximum(i * halo_blocks - 1, 0), 0)),
                  pl.BlockSpec((1, d), lambda i, f: (0, 0)),
                  pl.BlockSpec((1, d), lambda i, f: (0, 0)),
                  pl.BlockSpec((d, tf), lambda i, f: (0, f)),
                  pl.BlockSpec((d, tf), lambda i, f: (0, f)),
                  pl.BlockSpec((CONV_WIDTH, tf), lambda i, f: (0, f)),
                  pl.BlockSpec((1, tf), lambda i, f: (0, f)),
                  pl.BlockSpec((tf, d), lambda i, f: (f, 0))],
        out_specs=[pl.BlockSpec((tm, d), lambda i, f: (i, 0))] * (1 + emit_next),
        out_shape=[jax.ShapeDtypeStruct((bt, d), F32), jax.ShapeDtypeStruct((bt, d), BF16)][:1 + emit_next],
        scratch_shapes=[pltpu.VMEM((tm, d), BF16), pltpu.VMEM((8, d), BF16),
                        pltpu.VMEM((tm + 8, tf), F32), pltpu.VMEM((tm, d), F32)],
        compiler_params=_cparams("parallel", "arbitrary"),
        name="conv_ffn",
    )(x2, x2, g.reshape(1, d), g_next.reshape(1, d), wg, wu, cw, cb.reshape(1, ff), wd)


def _t5_bucket_np(dist):
    max_exact = N_BUCKETS // 2
    d = np.maximum(dist, 0)
    ratio = np.log(np.maximum(d, 1) / max_exact) / math.log(MAX_DISTANCE / max_exact)
    large = np.minimum(max_exact + (ratio * (N_BUCKETS - max_exact)).astype(np.int64), N_BUCKETS - 1)
    return np.where(d < max_exact, d, large)


def _expand_kernel(tab_ref, idx_ref, o_ref):
    tab = tab_ref[...]
    onehot = jnp.where(lax.broadcasted_iota(jnp.int32, (tab.shape[1], idx_ref.shape[1]), 0) == idx_ref[...],
                       1.0, 0.0).astype(BF16)
    hi, rest = tab.astype(BF16), tab - tab.astype(BF16).astype(F32)
    mid, lo = _split_hi_lo(rest)
    heads = tab.shape[0]
    parts = _dot(jnp.concatenate([hi, mid, lo], axis=0), onehot)
    o_ref[...] = parts[0:heads] + parts[heads:2 * heads] + parts[2 * heads:3 * heads]


def _bias_from_dist(tab_t, dist, valid, chunk=EXPAND_CHUNK):
    heads = tab_t.shape[0]
    tab = jnp.zeros((heads, LANES), F32).at[:, :N_BUCKETS].set(tab_t).at[:, N_BUCKETS].set(NEG)
    idx = np.where(valid, _t5_bucket_np(dist), N_BUCKETS).reshape(1, -1).astype(np.int32)
    n = idx.shape[1]
    chunk = min(chunk, n)
    out = pl.pallas_call(
        _expand_kernel,
        grid=(n // chunk,),
        in_specs=[pl.BlockSpec((heads, LANES), lambda i: (0, 0)), pl.BlockSpec((1, chunk), lambda i: (0, i))],
        out_specs=pl.BlockSpec((heads, chunk), lambda i: (0, i)),
        out_shape=jax.ShapeDtypeStruct((heads, n), F32),
        compiler_params=_cparams("parallel"),
        name="rel_bias_expand",
    )(tab, jnp.asarray(idx))
    return out.reshape((heads,) + dist.shape)


def _cmp_bias_kernel(g_ref, o_ref):
    ncp = o_ref.shape[2]
    per_step = o_ref.shape[1] // Q_BLOCK
    for sub in range(per_step):
        shift = (pl.program_id(0) * per_step + sub) * (Q_BLOCK // CMP_STRIDE)
        for h in range(o_ref.shape[0]):
            o_ref[h, sub * Q_BLOCK:(sub + 1) * Q_BLOCK, :] = pltpu.roll(g_ref[h], shift, 1)[:, :ncp]


def _cmp_bias(g_tab, t):
    heads, _, width = g_tab.shape
    ncp = t // CMP_STRIDE
    return pl.pallas_call(
        _cmp_bias_kernel,
        grid=(t // CMP_TQ,),
        in_specs=[pl.BlockSpec(g_tab.shape, lambda i: (0, 0, 0))],
        out_specs=pl.BlockSpec((heads, CMP_TQ, ncp), lambda i: (0, i, 0)),
        out_shape=jax.ShapeDtypeStruct((heads, t, ncp), F32),
        compiler_params=_cparams("parallel"),
        name="cmp_bias_build",
    )(g_tab)


def _static_tables(t):
    ncp = t // CMP_STRIDE
    n_cmp = (t - CMP_LEN) // CMP_STRIDE + 1
    n_slc = t // SLC_LEN
    lane = np.arange(MM_CHUNK)
    bd = (lane[:, None] // HEAD_DIM == lane[None, :] // HEAD_DIM).astype(np.float32)
    eg = np.zeros((LANES, N_BRANCH * MIX_WIDTH), np.float32)
    for h in range(N_HEADS):
        base = (h % N_QBLK) * LANES + (h // N_QBLK) * HEAD_DIM
        for n in range(N_BRANCH):
            eg[h * N_BRANCH + n, n * MIX_WIDTH + base:n * MIX_WIDTH + base + HEAD_DIM] = 1.0
    c_start = np.arange(n_cmp) * CMP_STRIDE
    s_start = np.arange(n_slc) * SLC_LEN
    overlap = ((c_start[:, None] < s_start[None, :] + SLC_LEN) & (c_start[:, None] + CMP_LEN > s_start[None, :]))
    ovt = np.zeros((LANES, ncp), np.float32)
    ovt[:n_slc, :n_cmp] = overlap.T
    key_block = np.zeros((t, LANES), np.float32)
    key_block[np.arange(t), np.arange(t) // SLC_LEN] = 1.0
    key_block[np.arange(t), HEAD_DIM + np.arange(t) // SLC_LEN] = 1.0
    as_bf16 = lambda a: jnp.asarray(a, BF16)
    return dict(bd=as_bf16(bd), eg=as_bf16(eg), ovt=as_bf16(ovt), key_block=as_bf16(key_block), n_slc=n_slc)


def _bias_tables(rel_bias, t):
    n_cmp = (t - CMP_LEN) // CMP_STRIDE + 1
    ncp = t // CMP_STRIDE
    tab_a = rel_bias[:, :N_HEADS].T * LOG2E
    tab_b = rel_bias[:, N_HEADS:].T * LOG2E
    r = np.arange(Q_BLOCK)[:, None]

    def band(window):
        n_prev = -(-(window - 1) // Q_BLOCK)
        n_keys = n_prev * Q_BLOCK + BAND_TQ
        dist = (n_keys - BAND_TQ) + np.arange(BAND_TQ)[:, None] - np.arange(n_keys)[None, :]
        valid = (dist >= 0) & (dist < window)
        assert not valid[:, 0].any()
        return dist, valid

    m = np.arange(2 * ncp)
    m = np.where(m < ncp, m, m - 2 * ncp)[None, :]
    dist_c = r - (m * CMP_STRIDE + CMP_LEN - 1)
    assert n_cmp == ncp - 1 and (dist_c[:, m[0] == Q_BLOCK // CMP_STRIDE - 1] < 0).all()
    tile = np.arange(SLC_TILE)
    dist_n = np.stack([SLC_TILE * (1 - which) + tile[:, None] - tile[None, :] for which in range(2)])
    near = _bias_from_dist(tab_a - tab_a[:, N_BUCKETS - 1:], dist_n, dist_n >= 0)
    return dict(win=_bias_from_dist(tab_a, *band(NSA_WINDOW)),
                swa=_bias_from_dist(tab_b, *band(SWA_WINDOW)),
                cmp=_cmp_bias(_bias_from_dist(tab_a, dist_c, dist_c >= 0), t),
                near=jnp.transpose(near, (1, 0, 2, 3)))


def _pair_cols(w):
    parts = []
    for p in range(N_QBLK):
        parts += [w[..., p * HEAD_DIM:(p + 1) * HEAD_DIM],
                  w[..., (N_QBLK + p) * HEAD_DIM:(N_QBLK + p + 1) * HEAD_DIM]]
    return jnp.concatenate(parts, axis=-1)


def _prep_layer(w_in, qk_gain, w_branch):
    kv = KV_GROUPS * HEAD_DIM
    widths = [('a_q', MIX_WIDTH), ('a_kc', kv), ('a_vc', kv), ('a_ks', kv), ('a_vs', kv), ('a_kw', kv), ('a_vw', kv),
              ('a_gate', N_GATE_COLS), ('b_q', MIX_WIDTH), ('b_k', kv), ('b_v', kv),
              ('c_q', MIX_WIDTH), ('c_k', MIX_WIDTH), ('c_v', MIX_WIDTH), ('c_f', N_HEADS),
              ('merge', N_BRANCH * w_in.shape[0])]
    cols, off = {}, 0
    for name, w in widths:
        cols[name] = w_in[:, off:off + w]
        off += w
    scale = HEAD_DIM ** -0.5 * LOG2E
    tile = lambda gvec, n: jnp.tile(gvec, n)
    zeros = lambda n: jnp.zeros((n,), F32)
    ones = lambda n: jnp.ones((n,), F32)
    pieces = [(_pair_cols(cols['a_q']), tile(qk_gain[0] * scale, N_HEADS), ones(MIX_WIDTH)),
              (_pair_cols(cols['b_q']), tile(qk_gain[2] * scale, N_HEADS), ones(MIX_WIDTH)),
              (cols['c_q'], tile(qk_gain[4] * scale, N_HEADS), ones(MIX_WIDTH)),
              (cols['c_k'], tile(qk_gain[5], N_HEADS), ones(MIX_WIDTH)),
              (cols['a_ks'], tile(qk_gain[1], KV_GROUPS), ones(kv)),
              (cols['a_kw'], tile(qk_gain[1], KV_GROUPS), ones(kv)),
              (cols['b_k'], tile(qk_gain[3], KV_GROUPS), ones(kv)),
              (cols['a_vs'], zeros(kv), zeros(kv)),
              (cols['c_v'], zeros(MIX_WIDTH), zeros(MIX_WIDTH)),
              (cols['a_kc'], zeros(kv), zeros(kv)), (cols['a_vc'], zeros(kv), zeros(kv)),
              (cols['a_vw'], zeros(kv), zeros(kv)), (cols['b_v'], zeros(kv), zeros(kv))]
    w_qkv = jnp.concatenate([p[0] for p in pieces], axis=1).astype(BF16)
    n_qkv = w_qkv.shape[1]
    aux = jnp.zeros((8, n_qkv), F32)
    aux = aux.at[0].set(jnp.concatenate([p[1] for p in pieces])).at[1].set(jnp.concatenate([p[2] for p in pieces]))
    d = w_in.shape[0]
    w_gates = jnp.concatenate([cols['a_gate'], cols['c_f'],
                               jnp.zeros((d, LANES - N_GATE_COLS - N_HEADS), F32)], axis=1).astype(BF16)
    wb = jnp.stack([_pair_cols(w_branch[0].T).T, _pair_cols(w_branch[1].T).T, w_branch[2]]).astype(BF16)
    return w_qkv, aux, w_gates, cols['merge'].astype(BF16), wb


def _prep_compress(cmp_pos, cmp_w1, cmp_w2, k_gain):
    def both_groups(w):
        z = jnp.zeros_like(w)
        return jnp.concatenate([jnp.concatenate([w, z], -1), jnp.concatenate([z, w], -1)], -2)

    pos = jnp.concatenate([cmp_pos, cmp_pos], -1)
    w1 = both_groups(cmp_w1.reshape(2, CMP_LEN, HEAD_DIM, CMP_HIDDEN)).astype(BF16)
    w2 = both_groups(cmp_w2).astype(BF16)
    aux = jnp.zeros((2, 8, LANES), F32)
    aux = aux.at[0, 0].set(jnp.tile(k_gain, KV_GROUPS)).at[0, 1].set(1.0)
    return pos, w1, w2, aux


def kernel(x, rel_bias, norm_mix, norm_ffn, w_in, forget_bias, qk_gain, cmp_pos, cmp_w1, cmp_w2, sinks, w_branch, w_out, w_gate, w_up, conv_w, conv_b, w_down):
    b, t, d = x.shape
    bt = b * t
    depth = w_in.shape[0]
    tabs = _static_tables(t)
    biases = _bias_tables(rel_bias, t)
    fox_tq = min(FOX_TQ, t)
    x2 = x.reshape(bt, d)
    h = _rmsnorm(x2, norm_mix[0])
    for l in range(depth):
        w_qkv, aux, w_gates, w_merge, wb = _prep_layer(w_in[l], qk_gain[l], w_branch[l])
        qkv = _inproj_qkv(h, w_qkv, aux, tabs['bd']).reshape(N_CB, b, t, LANES)
        graw, gsig = _inproj_gates(h, w_gates)
        mg = _inproj_merge(h, w_merge, d)

        cmp_kv = _compress(qkv, *_prep_compress(cmp_pos[l], cmp_w1[l], cmp_w2[l], qk_gain[l, 1]), tabs['bd'])
        o_cmp, qa = _cmp_select(qkv, cmp_kv, biases['cmp'], tabs['ovt'], tabs['n_slc'])
        o_slc = _slc_attention(qa, qkv, tabs['key_block'], biases['near'])
        o_win = _banded_attention(qkv, CB_AQ, CB_AKW, CB_AVW, biases['win'], False)

        swa_bias = biases['swa'].at[:, :, 0].set(sinks[l][:, None] * LOG2E)
        o_b = _banded_attention(qkv, CB_BQ, CB_BK, CB_BV, swa_bias, True)

        fb_row = jnp.zeros((1, LANES), F32).at[0, N_GATE_COLS:N_GATE_COLS + N_HEADS].set(forget_bias[l])
        o_c = _fox_attention(qkv, *_fox_prep(graw, fb_row, b, t), fox_tq)

        flat = lambda o: o.reshape(N_QBLK, bt, LANES)
        x2 = _merge(flat(o_cmp), flat(o_slc), flat(o_win), flat(o_b), flat(o_c), gsig, tabs['eg'], mg, wb,
                    w_out[l].astype(BF16), x2)
        x2, *h = _ffn(x2, norm_ffn[l], norm_mix[l + 1] if l + 1 < depth else None, w_gate[l].astype(BF16),
                      w_up[l].astype(BF16), conv_w[l], conv_b[l], w_down[l].astype(BF16), t)
        h = h[0] if h else None
    return x2.reshape(b, t, d)
```

```python
import functools
import math

import numpy as np
import jax
import jax.numpy as jnp
from jax import lax
from jax.experimental import pallas as pl
from jax.experimental.pallas import tpu as pltpu

F32 = jnp.float32
BF16 = jnp.bfloat16

HEAD_DIM = 64
LANES = 128
N_HEADS = 8
KV_GROUPS = 2
HEADS_PER_GROUP = N_HEADS // KV_GROUPS
N_QBLK = N_HEADS * HEAD_DIM // LANES
MIX_WIDTH = N_HEADS * HEAD_DIM
N_BRANCH = 3
Q_BLOCK = 128
CMP_LEN = 32
CMP_STRIDE = 16
CMP_HIDDEN = 256
SLC_LEN = 64
SLC_TOPK = 8
NSA_WINDOW = 256
SWA_WINDOW = 128
N_BUCKETS = 32
MAX_DISTANCE = 128
CONV_WIDTH = 3
RMS_EPS = 1e-6
LOG2E = math.log2(math.e)
NEG = -1e30
FORCE_SCORE = 1e9
TAKEN = -3e38

VMEM_LIMIT = 48 * 1024 * 1024
MM_CHUNK = 256
MM_ROWS = 512
ROW_TILE = 1024
NORM_ROW_TILE = 2048
QKV_ROW_TILE, QKV_COL_TILE = 2048, 512
MERGE_GATE_ROW_TILE, MERGE_GATE_COL_TILE = 2048, 1024
BRANCH_MERGE_ROW_TILE = 512
FFN_TF = 1024
CMP_TQ = 512
SLC_TILE = 256
SLC_QGROUP = 8
BAND_TQ = 256
FOX_TQ = 512
FOX_PREP_CHUNK = 512
EXPAND_CHUNK = 4096

CB_AQ, CB_BQ, CB_CQ, CB_CK = 0, 4, 8, 12
CB_AKS, CB_AKW, CB_BK, CB_AVS = 16, 17, 18, 19
CB_CV, CB_AKC, CB_AVC, CB_AVW, CB_BV = 20, 24, 25, 26, 27
N_CB = 28
N_NORM_CB = 20
N_GATE_COLS = N_HEADS * N_BRANCH


def _cparams(*sem):
    return pltpu.CompilerParams(dimension_semantics=sem, vmem_limit_bytes=VMEM_LIMIT)


def _split_hi_lo(v):
    hi = v.astype(BF16)
    return hi, (v - hi.astype(F32)).astype(BF16)


def _dot(a, b):
    return jnp.dot(a, b, preferred_element_type=F32)


def _dot_nt(a, b):
    return lax.dot_general(a, b, (((1,), (1,)), ((), ())), preferred_element_type=F32)


def _sigmoid(x):
    return 0.5 * jnp.tanh(0.5 * x) + 0.5


def _gelu_tanh(x):
    return x * (0.5 * (1.0 + jnp.tanh(math.sqrt(2.0 / math.pi) * (x + 0.044715 * (x * x * x)))))


def _head_rms(y, bd, gain):
    ms = _dot((y * y).astype(BF16), bd) * (1.0 / HEAD_DIM)
    return (y * lax.rsqrt(ms + RMS_EPS)) * gain


def _lane_half_mask(g):
    return (lax.broadcasted_iota(jnp.int32, (1, LANES), 1) // HEAD_DIM) == g


def _merge_group_outputs(o_ref, o_g0, o_g1):
    low = _lane_half_mask(0)
    tq = o_ref.shape[2]
    for p in range(N_QBLK):
        rows = slice(p * tq, (p + 1) * tq)
        o_ref[p, 0] = jnp.where(low, o_g0[rows], o_g1[rows]).astype(o_ref.dtype)


def _rmsnorm_kernel(x_ref, g_ref, o_ref):
    x = x_ref[...]
    ms = jnp.mean(x * x, axis=-1, keepdims=True)
    o_ref[...] = ((x * lax.rsqrt(ms + RMS_EPS)) * g_ref[...]).astype(o_ref.dtype)


def _rmsnorm(x2, g, tm=NORM_ROW_TILE):
    bt, d = x2.shape
    tm = min(tm, bt)
    return pl.pallas_call(
        _rmsnorm_kernel,
        grid=(bt // tm,),
        in_specs=[pl.BlockSpec((tm, d), lambda i: (i, 0)), pl.BlockSpec((1, d), lambda i: (0, 0))],
        out_specs=pl.BlockSpec((tm, d), lambda i: (i, 0)),
        out_shape=jax.ShapeDtypeStruct((bt, d), BF16),
        compiler_params=_cparams("parallel"),
        name="rmsnorm",
    )(x2, g.reshape(1, d))


def _inproj_qkv_kernel(h_ref, w_ref, aux_ref, bd_ref, o_ref, *, n_norm_tiles):
    per_chunk = MM_CHUNK // LANES

    def emit(normalise):
        for r in range(0, h_ref.shape[0], MM_ROWS):
            rows = slice(r, r + MM_ROWS)
            y_all = _dot(h_ref[rows, :], w_ref[...])
            for c in range(w_ref.shape[1] // MM_CHUNK):
                cols = slice(c * MM_CHUNK, (c + 1) * MM_CHUNK)
                y = y_all[:, cols]
                if normalise:
                    normed = _head_rms(y, bd_ref[...], aux_ref[0:1, cols])
                    y = jnp.where(aux_ref[1:2, cols] > 0.5, normed, y)
                for s in range(per_chunk):
                    o_ref[c * per_chunk + s, rows, :] = y[:, s * LANES:(s + 1) * LANES].astype(o_ref.dtype)

    pl.when(pl.program_id(1) < n_norm_tiles)(lambda: emit(True))
    pl.when(pl.program_id(1) >= n_norm_tiles)(lambda: emit(False))


def _inproj_qkv(h, w, aux, bd, tm=QKV_ROW_TILE, tn=QKV_COL_TILE):
    bt, d = h.shape
    n = w.shape[1]
    n_sub = tn // LANES
    tm = min(tm, bt)
    assert N_NORM_CB % n_sub == 0 and tm % MM_ROWS == 0
    return pl.pallas_call(
        functools.partial(_inproj_qkv_kernel, n_norm_tiles=N_NORM_CB // n_sub),
        grid=(bt // tm, n // tn),
        in_specs=[pl.BlockSpec((tm, d), lambda i, j: (i, 0)),
                  pl.BlockSpec((d, tn), lambda i, j: (0, j)),
                  pl.BlockSpec((8, tn), lambda i, j: (0, j)),
                  pl.BlockSpec(bd.shape, lambda i, j: (0, 0))],
        out_specs=pl.BlockSpec((n_sub, tm, LANES), lambda i, j: (j, i, 0)),
        out_shape=jax.ShapeDtypeStruct((n // LANES, bt, LANES), BF16),
        compiler_params=_cparams("parallel", "arbitrary"),
        name="inproj_qkv",
    )(h, w, aux, bd)


def _inproj_gates_kernel(h_ref, w_ref, raw_ref, sig_ref):
    y = _dot(h_ref[...], w_ref[...])
    raw_ref[...] = y
    sig_ref[...] = _sigmoid(y).astype(sig_ref.dtype)


def _inproj_gates(h, w, tm=ROW_TILE):
    bt, d = h.shape
    return pl.pallas_call(
        _inproj_gates_kernel,
        grid=(bt // tm,),
        in_specs=[pl.BlockSpec((tm, d), lambda i: (i, 0)),
                  pl.BlockSpec((d, LANES), lambda i: (0, 0))],
        out_specs=[pl.BlockSpec((tm, LANES), lambda i: (i, 0))] * 2,
        out_shape=[jax.ShapeDtypeStruct((bt, LANES), F32), jax.ShapeDtypeStruct((bt, LANES), BF16)],
        compiler_params=_cparams("parallel"),
        name="inproj_gates",
    )(h, w)


def _inproj_merge_kernel(h_ref, w_ref, o_ref):
    width = 2 * MM_CHUNK
    for c in range(w_ref.shape[1] // width):
        cols = slice(c * width, (c + 1) * width)
        o_ref[0, :, cols] = _sigmoid(_dot(h_ref[...], w_ref[:, cols])).astype(o_ref.dtype)


def _inproj_merge(h, w, d_model, tm=MERGE_GATE_ROW_TILE, tn=MERGE_GATE_COL_TILE):
    bt, d = h.shape
    tm = min(tm, bt)
    n = w.shape[1]
    per = d_model // tn
    return pl.pallas_call(
        _inproj_merge_kernel,
        grid=(bt // tm, n // tn),
        in_specs=[pl.BlockSpec((tm, d), lambda i, j: (i, 0)),
                  pl.BlockSpec((d, tn), lambda i, j: (0, j))],
        out_specs=pl.BlockSpec((1, tm, tn), lambda i, j: (j // per, i, j % per)),
        out_shape=jax.ShapeDtypeStruct((n // d_model, bt, d_model), BF16),
        compiler_params=_cparams("parallel", "arbitrary"),
        name="inproj_merge",
    )(h, w)


def _compress_kernel(z_ref, pos_ref, w1_ref, w2_ref, aux_ref, bd_ref, o_ref, zf_scr):
    n_chunks = o_ref.shape[2]
    zf_scr[...] = z_ref[0, 0].astype(F32)
    halves = []
    for half in range(CMP_LEN // CMP_STRIDE):
        acc = jnp.zeros((n_chunks, KV_GROUPS * CMP_HIDDEN), F32)
        for l in range(CMP_STRIDE):
            rows = zf_scr[pl.ds(l, n_chunks, stride=CMP_STRIDE), :]
            pos = pos_ref[0, half * CMP_STRIDE + l:half * CMP_STRIDE + l + 1, :]
            acc = acc + _dot((rows + pos).astype(BF16), w1_ref[0, half * CMP_STRIDE + l])
        halves.append(acc)
    hid = _gelu_tanh(halves[0] + pltpu.roll(halves[1], n_chunks - 1, 0))
    y = _dot(hid.astype(BF16), w2_ref[0])
    normed = _head_rms(y, bd_ref[0:LANES, 0:LANES], aux_ref[0, 0:1, :])
    o_ref[0, 0] = jnp.where(aux_ref[0, 1:2, :] > 0.5, normed, y).astype(o_ref.dtype)


def _compress(qkv, pos, w1, w2, aux, bd):
    _, b, t, _ = qkv.shape
    n_chunks = t // CMP_STRIDE
    return pl.pallas_call(
        _compress_kernel,
        grid=(2, b),
        in_specs=[pl.BlockSpec((1, 1, t, LANES), lambda w, i: (CB_AKC + w, i, 0, 0)),
                  pl.BlockSpec((1,) + pos.shape[1:], lambda w, i: (w, 0, 0)),
                  pl.BlockSpec((1,) + w1.shape[1:], lambda w, i: (w, 0, 0, 0)),
                  pl.BlockSpec((1,) + w2.shape[1:], lambda w, i: (w, 0, 0)),
                  pl.BlockSpec((1, 8, LANES), lambda w, i: (w, 0, 0)),
                  pl.BlockSpec(bd.shape, lambda w, i: (0, 0))],
        out_specs=pl.BlockSpec((1, 1, n_chunks, LANES), lambda w, i: (w, i, 0, 0)),
        out_shape=jax.ShapeDtypeStruct((2, b, n_chunks, LANES), BF16),
        scratch_shapes=[pltpu.VMEM((t, LANES), F32)],
        compiler_params=_cparams("parallel", "parallel"),
        name="nsa_compress",
    )(qkv, pos, w1, w2, aux, bd)


def _cmp_select_kernel(q_ref, kc_ref, vc_ref, bias_ref, ovt_ref, o_ref, qa_ref, *, n_slc, n_top):
    ncp = kc_ref.shape[2]
    rows = HEADS_PER_GROUP * Q_BLOCK
    jrow = lax.broadcasted_iota(jnp.int32, (n_slc, Q_BLOCK), 0)
    jrow_f = jrow.astype(F32)
    low = _lane_half_mask(0)
    for sub in range(q_ref.shape[2] // Q_BLOCK):
        rs = slice(sub * Q_BLOCK, (sub + 1) * Q_BLOCK)
        t0 = pl.program_id(1) * q_ref.shape[2] + sub * Q_BLOCK
        row_t = t0 + lax.broadcasted_iota(jnp.int32, (Q_BLOCK, min(LANES, ncp)), 0)
        row_valid = jnp.concatenate([row_t >= CMP_LEN - 1] * HEADS_PER_GROUP, axis=0)
        cur = (t0 + lax.broadcasted_iota(jnp.int32, (n_slc, Q_BLOCK), 1)) // SLC_LEN

        def force(imp):
            forced = jnp.where(jrow == cur, FORCE_SCORE, jnp.where(jrow == cur - 1, FORCE_SCORE, imp))
            return jnp.where(jrow == 0, FORCE_SCORE, forced)

        outs = []
        for g in range(KV_GROUPS):
            own_half = _lane_half_mask(g)
            hs = slice(g * HEADS_PER_GROUP, (g + 1) * HEADS_PER_GROUP)
            qop = jnp.concatenate([jnp.where(own_half, q_ref[p, 0, rs, :], jnp.zeros((Q_BLOCK, LANES), q_ref.dtype))
                                   for p in range(N_QBLK)], axis=0)
            s = _dot_nt(qop, kc_ref[0, 0]) + bias_ref[hs, rs, :].reshape(rows, ncp)
            m = _row_max(s)
            e = [jnp.exp2(c - m) for c in _lane_chunks(s)]
            row_sum = jnp.broadcast_to(jnp.sum(sum(e), axis=-1, keepdims=True), e[0].shape)
            inv = jnp.where(row_valid, 1.0 / row_sum, 0.0)
            p = jnp.concatenate([c * inv for c in e], axis=-1)
            outs.append(_dot(p.astype(BF16), vc_ref[0, 0]))
            psum = p[0:Q_BLOCK]
            for r in range(1, HEADS_PER_GROUP):
                psum = psum + p[r * Q_BLOCK:(r + 1) * Q_BLOCK]
            hi, lo = _split_hi_lo(psum)
            imp_t = (_dot_nt(ovt_ref[...], hi) + _dot_nt(ovt_ref[...], lo))[0:n_slc]
            score = jnp.where(jrow <= cur, force(imp_t), NEG)
            sel_bias = jnp.full((n_slc, Q_BLOCK), NEG, F32)
            for _ in range(n_top):
                best = jnp.max(score, axis=0, keepdims=True)
                first = jnp.min(jnp.where(score == best, jrow_f, float(n_slc)), axis=0, keepdims=True)
                hit = jrow_f == first
                sel_bias = jnp.where(hit, jnp.where(best > NEG / 2, 0.0, NEG), sel_bias)
                score = jnp.where(hit, TAKEN, score)
            if n_slc < HEAD_DIM:
                sel_bias = jnp.concatenate([sel_bias, jnp.full((HEAD_DIM - n_slc, Q_BLOCK), NEG, F32)], axis=0)
            sel_t = jnp.concatenate([sel_bias, sel_bias], axis=0).T.astype(qa_ref.dtype)
            for p in range(N_QBLK):
                qa_ref[g * HEADS_PER_GROUP + p, 0, rs, :] = jnp.where(own_half, q_ref[p, 0, rs, :], sel_t)
        for p in range(N_QBLK):
            prs = slice(p * Q_BLOCK, (p + 1) * Q_BLOCK)
            o_ref[p, 0, rs, :] = jnp.where(low, outs[0][prs], outs[1][prs]).astype(o_ref.dtype)


def _cmp_select(qkv, cmp_kv, bias_c, ovt, n_slc, tq=CMP_TQ):
    _, b, t, _ = qkv.shape
    ncp = cmp_kv.shape[2]
    assert n_slc <= HEAD_DIM
    return pl.pallas_call(
        functools.partial(_cmp_select_kernel, n_slc=n_slc, n_top=min(SLC_TOPK, n_slc)),
        grid=(b, t // tq),
        in_specs=[pl.BlockSpec((N_QBLK, 1, tq, LANES), lambda bi, i: (CB_AQ // N_QBLK, bi, i, 0)),
                  pl.BlockSpec((1, 1, ncp, LANES), lambda bi, i: (0, bi, 0, 0)),
                  pl.BlockSpec((1, 1, ncp, LANES), lambda bi, i: (1, bi, 0, 0)),
                  pl.BlockSpec((N_HEADS, tq, ncp), lambda bi, i: (0, i, 0)),
                  pl.BlockSpec((LANES, ncp), lambda bi, i: (0, 0))],
        out_specs=[pl.BlockSpec((N_QBLK, 1, tq, LANES), lambda bi, i: (0, bi, i, 0)),
                   pl.BlockSpec((N_HEADS, 1, tq, LANES), lambda bi, i: (0, bi, i, 0))],
        out_shape=[jax.ShapeDtypeStruct((N_QBLK, b, t, LANES), BF16),
                   jax.ShapeDtypeStruct((N_HEADS, b, t, LANES), BF16)],
        compiler_params=_cparams("parallel", "parallel"),
        name="nsa_cmp_select",
    )(qkv, cmp_kv, cmp_kv, bias_c, ovt)


def _lane_chunks(s):
    width = min(LANES, s.shape[1])
    return [s[:, c:c + width] for c in range(0, s.shape[1], width)]


def _row_max(s):
    chunks = _lane_chunks(s)
    m = chunks[0]
    for c in chunks[1:]:
        m = jnp.maximum(m, c)
    return jnp.broadcast_to(jnp.max(m, axis=-1, keepdims=True), m.shape)


def _exp2_shifted(s, m):
    return jnp.concatenate([jnp.exp2(c - m).astype(BF16) for c in _lane_chunks(s)], axis=-1)


def _flash_start(s, v):
    m = _row_max(s)
    return m, _dot(_exp2_shifted(s, m), v)


def _flash_update(s, m, acc, v):
    m_new = jnp.maximum(m, _row_max(s))
    return m_new, jnp.exp2(m - m_new) * acc + _dot(_exp2_shifted(s, m_new), v)


def _normalise(acc):
    return acc / pltpu.roll(acc, HEAD_DIM, 1)


def _augment_keys_values(ka_scr, va_scr, k2, xk, v2, base=0):
    low = _lane_half_mask(0)
    ones = jnp.ones_like(v2)
    ka_scr[base] = jnp.where(low, k2, xk)
    ka_scr[base + 1] = jnp.where(low, xk, k2)
    va_scr[base] = jnp.where(low, v2, ones)
    va_scr[base + 1] = jnp.where(low, ones, v2)


def _store_all(scr, s):
    for n, sn in enumerate(s):
        scr[n] = sn


def _load_all(scr):
    return tuple(scr[n] for n in range(scr.shape[0]))


def _worklist_sweep(n_steps, logits, update, sa_scr, sb_scr):
    def single(n, _):
        _store_all(sa_scr, logits(n))
        update(n, _load_all(sa_scr))
        return 0

    odd = n_steps % 2
    lax.fori_loop(0, odd, single, 0)
    _store_all(sa_scr, logits(odd))

    def pair(jp, _):
        n = odd + 2 * jp
        _store_all(sb_scr, logits(n + 1))
        update(n, _load_all(sa_scr))
        _store_all(sa_scr, logits(jnp.minimum(n + 2, n_steps - 1)))
        update(n + 1, _load_all(sb_scr))
        return 0

    lax.fori_loop(0, n_steps // 2, pair, 0)


def _slc_kernel(nfar_ref, jt_ref, it_ref, qa_ref, k_ref, v_ref, hot_ref, nb_ref, o_ref,
                ka_scr, va_scr, m_scr, acc_scr, sa_scr, sb_scr, *, tq):
    gi = pl.program_id(1)
    qg = qa_ref.shape[2] // tq
    rows = HEADS_PER_GROUP * tq
    @pl.when(gi == 0)
    def _():
        _augment_keys_values(ka_scr, va_scr, k_ref[0, 0], hot_ref[...], v_ref[0, 0])

    heads = [slice(g * HEADS_PER_GROUP, (g + 1) * HEADS_PER_GROUP) for g in range(KV_GROUPS)]

    def tile(n):
        return pl.ds(pl.multiple_of(n * tq, tq), tq)

    def logits(j, il):
        return tuple(_dot_nt(qa_ref[heads[g], 0, tile(il), :].reshape(rows, LANES), ka_scr[g, tile(j), :])
                     for g in range(KV_GROUPS))

    def biased(s, which, g):
        return (s.reshape(HEADS_PER_GROUP, tq, tq) + nb_ref[which, heads[g]]).reshape(rows, tq)

    def diag_update(il, s):
        for g in range(KV_GROUPS):
            m_scr[il, g], acc_scr[il, g] = _flash_start(biased(s[g], 1, g), va_scr[g, tile(gi * qg + il), :])

    _worklist_sweep(qg, lambda il: logits(gi * qg + il, il), diag_update, sa_scr, sb_scr)

    def fold(il, g, sg, j):
        m_scr[il, g], acc_scr[il, g] = _flash_update(sg, m_scr[il, g], acc_scr[il, g], va_scr[g, tile(j), :])

    def prev_update(il, s):
        i = gi * qg + il
        pen = jnp.where(i >= 1, 0.0, NEG)
        for g in range(KV_GROUPS):
            fold(il, g, biased(s[g], 0, g) + pen, jnp.maximum(i - 1, 0))

    _worklist_sweep(qg, lambda il: logits(jnp.maximum(gi * qg + il - 1, 0), il), prev_update, sa_scr, sb_scr)

    def far_update(n, s):
        for g in range(KV_GROUPS):
            fold(it_ref[gi, n], g, s[g], jt_ref[gi, n])

    _worklist_sweep(nfar_ref[gi], lambda n: logits(jt_ref[gi, n], it_ref[gi, n]), far_update, sa_scr, sb_scr)

    low = _lane_half_mask(0)
    for il in range(qg):
        o_g0, o_g1 = _normalise(acc_scr[il, 0]), _normalise(acc_scr[il, 1])
        for p in range(N_QBLK):
            head_rows = slice(p * tq, (p + 1) * tq)
            o_ref[p, 0, il * tq:(il + 1) * tq, :] = jnp.where(low, o_g0[head_rows], o_g1[head_rows]).astype(o_ref.dtype)


def _slc_attention(qa, qkv, hot2, near_bias, qg=SLC_QGROUP):
    _, b, t, _ = qa.shape
    tq = near_bias.shape[2]
    nq = t // tq
    qg = min(qg, nq)
    n_groups = nq // qg
    lists = [[(j, il) for j in range(nq) for il in range(qg) if j <= g * qg + il - 2] for g in range(n_groups)]
    width = max(2, max(len(steps) for steps in lists))
    table = lambda k: jnp.asarray([[s[k] for s in steps] + [0] * (width - len(steps)) for steps in lists], jnp.int32)
    nfar = jnp.asarray([len(steps) for steps in lists], jnp.int32)
    smem = pl.BlockSpec(memory_space=pltpu.SMEM)
    rows = HEADS_PER_GROUP * tq
    return pl.pallas_call(
        functools.partial(_slc_kernel, tq=tq),
        grid=(b, n_groups),
        in_specs=[smem, smem, smem,
                  pl.BlockSpec((N_HEADS, 1, qg * tq, LANES), lambda bi, gi: (0, bi, gi, 0)),
                  pl.BlockSpec((1, 1, t, LANES), lambda bi, gi: (CB_AKS, bi, 0, 0)),
                  pl.BlockSpec((1, 1, t, LANES), lambda bi, gi: (CB_AVS, bi, 0, 0)),
                  pl.BlockSpec(hot2.shape, lambda bi, gi: (0, 0)),
                  pl.BlockSpec(near_bias.shape, lambda bi, gi: (0, 0, 0, 0))],
        out_specs=pl.BlockSpec((N_QBLK, 1, qg * tq, LANES), lambda bi, gi: (0, bi, gi, 0)),
        out_shape=jax.ShapeDtypeStruct((N_QBLK, b, t, LANES), BF16),
        scratch_shapes=[pltpu.VMEM((KV_GROUPS, t, LANES), BF16), pltpu.VMEM((KV_GROUPS, t, LANES), BF16),
                        pltpu.VMEM((qg, KV_GROUPS, rows, LANES), F32), pltpu.VMEM((qg, KV_GROUPS, rows, LANES), F32),
                        pltpu.VMEM((KV_GROUPS, rows, tq), F32), pltpu.VMEM((KV_GROUPS, rows, tq), F32)],
        compiler_params=_cparams("parallel", "arbitrary"),
        name="nsa_slc_attention",
    )(nfar, table(0), table(1), qa, qkv, qkv, hot2, near_bias)


def _banded_kernel(*refs, n_blk, use_sinks):
    q_ref = refs[0]
    k_refs = refs[1:1 + n_blk]
    v_refs = refs[1 + n_blk:1 + 2 * n_blk]
    bias_ref, o_ref = refs[1 + 2 * n_blk:]
    i = pl.program_id(1)
    tq = q_ref.shape[2]
    n_keys = n_blk * Q_BLOCK
    rows = HEADS_PER_GROUP * tq
    first_key = (i + 1) * tq - n_keys
    col = lax.broadcasted_iota(jnp.int32, (1, n_keys), 1)
    pad_pen = jnp.where(first_key + col >= 0, 0.0, NEG)
    kcat = jnp.concatenate([r[0, 0] for r in k_refs], axis=0)
    vcat = jnp.concatenate([r[0, 0] for r in v_refs], axis=0)
    if use_sinks:
        sink_row = lax.broadcasted_iota(jnp.int32, (n_keys, 1), 0) == 0
        kcat = jnp.where(sink_row, jnp.zeros_like(kcat), kcat)
        vcat = jnp.where(sink_row, jnp.zeros_like(vcat), vcat)
        pad_pen = jnp.where(col == 0, 0.0, pad_pen)
    outs = []
    for g in range(KV_GROUPS):
        own_half = _lane_half_mask(g)
        qop = jnp.concatenate([jnp.where(own_half, q_ref[p, 0], jnp.zeros_like(q_ref[p, 0]))
                               for p in range(N_QBLK)], axis=0)
        hs = slice(g * HEADS_PER_GROUP, (g + 1) * HEADS_PER_GROUP)
        s = _dot_nt(qop, kcat) + bias_ref[hs].reshape(rows, n_keys) + pad_pen
        e = _exp2_shifted(s, _row_max(s))
        acc = _dot(e, jnp.where(own_half, vcat, jnp.ones_like(vcat)))
        outs.append(_normalise(acc))
    _merge_group_outputs(o_ref, outs[0], outs[1])


def _banded_attention(qkv, cb_q, cb_k, cb_v, bias, use_sinks):
    _, b, t, _ = qkv.shape
    tq = bias.shape[1]
    n_blk = bias.shape[2] // Q_BLOCK
    per_tile = tq // Q_BLOCK

    def kv_spec(cb, back):
        return pl.BlockSpec((1, 1, Q_BLOCK, LANES),
                            lambda bi, i: (cb, bi, jnp.maximum((i + 1) * per_tile - 1 - back, 0), 0))

    backs = [n_blk - 1 - jb for jb in range(n_blk)]
    return pl.pallas_call(
        functools.partial(_banded_kernel, n_blk=n_blk, use_sinks=use_sinks),
        grid=(b, t // tq),
        in_specs=([pl.BlockSpec((N_QBLK, 1, tq, LANES), lambda bi, i: (cb_q // N_QBLK, bi, i, 0))]
                  + [kv_spec(cb_k, back) for back in backs] + [kv_spec(cb_v, back) for back in backs]
                  + [pl.BlockSpec(bias.shape, lambda bi, i: (0, 0, 0))]),
        out_specs=pl.BlockSpec((N_QBLK, 1, tq, LANES), lambda bi, i: (0, bi, i, 0)),
        out_shape=jax.ShapeDtypeStruct((N_QBLK, b, t, LANES), BF16),
        compiler_params=_cparams("parallel", "parallel"),
        name="banded_attention",
    )(qkv, *([qkv] * (2 * n_blk)), bias)


FOX_EXTRA = 6


def _fox_prep_kernel(g_ref, fb_ref, pq_ref, pk_ref, oq_ref, ok_ref, xq_ref, xk_ref, carry_scr):
    @pl.when(pl.program_id(1) == 0)
    def _():
        carry_scr[...] = jnp.zeros_like(carry_scr)

    x = g_ref[...] + fb_ref[...]
    logf = jnp.minimum(x, 0.0) - jnp.log(1.0 + jnp.exp(-jnp.abs(x)))
    ch = x.shape[0]
    lower = jnp.where(lax.broadcasted_iota(jnp.int32, (ch, ch), 0)
                      >= lax.broadcasted_iota(jnp.int32, (ch, ch), 1), 1.0, 0.0).astype(BF16)

    def split3(v):
        hi, rest = v.astype(BF16), v - v.astype(BF16).astype(F32)
        return (hi,) + _split_hi_lo(rest)

    terms = _dot(lower, jnp.concatenate(split3(logf), axis=-1))
    cs = sum(_lane_chunks(terms)) + carry_scr[0:1, :]
    carry_scr[0:1, :] = cs[ch - 1:ch, :]
    parts = jnp.concatenate(split3(cs * LOG2E), axis=-1)
    xq = _dot(parts, pq_ref[...]) + oq_ref[...]
    xk = _dot(parts, pk_ref[...]) + ok_ref[...]
    for p in range(N_QBLK):
        xq_ref[p, 0] = xq[:, p * LANES:(p + 1) * LANES].astype(xq_ref.dtype)
        xk_ref[p, 0] = xk[:, p * LANES:(p + 1) * LANES].astype(xk_ref.dtype)


def _fox_prep(graw, fb_row, b, t, chunk=FOX_PREP_CHUNK):
    chunk = min(chunk, t)
    nch = t // chunk
    pq = np.zeros((3 * LANES, N_QBLK * LANES), np.float32)
    pk = np.zeros((3 * LANES, N_QBLK * LANES), np.float32)
    oq = np.zeros((1, N_QBLK * LANES), np.float32)
    ok = np.zeros((1, N_QBLK * LANES), np.float32)
    for h in range(N_HEADS):
        col = (h // 2) * LANES + (HEAD_DIM if h % 2 == 0 else 0)
        for n in range(3):
            pq[n * LANES + N_GATE_COLS + h, col + n] = 1.0
            pk[n * LANES + N_GATE_COLS + h, col + 3 + n] = -1.0
        oq[0, col + 3:col + FOX_EXTRA] = 1.0
        ok[0, col:col + 3] = 1.0
    const = lambda a: pl.BlockSpec(a.shape, lambda bi, c: (0,) * a.ndim)
    out_spec = pl.BlockSpec((N_QBLK, 1, chunk, LANES), lambda bi, c: (0, bi, c, 0))
    return pl.pallas_call(
        _fox_prep_kernel,
        grid=(b, nch),
        in_specs=[pl.BlockSpec((chunk, LANES), lambda bi, c: (bi * nch + c, 0)),
                  const(fb_row), const(pq), const(pk), const(oq), const(ok)],
        out_specs=[out_spec, out_spec],
        out_shape=[jax.ShapeDtypeStruct((N_QBLK, b, t, LANES), BF16)] * 2,
        scratch_shapes=[pltpu.VMEM((8, LANES), F32)],
        compiler_params=_cparams("parallel", "arbitrary"),
        name="fox_prep",
    )(graw, fb_row, jnp.asarray(pq, BF16), jnp.asarray(pk, BF16), jnp.asarray(oq), jnp.asarray(ok))


def _fox_kernel(jt_ref, it_ref, q_ref, xq_ref, k_ref, xk_ref, v_ref, o_ref,
                qa_scr, ka_scr, va_scr, m_scr, acc_scr, sa_scr, sb_scr, *, tq, n_off):
    nq = q_ref.shape[2] // tq
    low = _lane_half_mask(0)
    _augment_keys_values(ka_scr, va_scr, k_ref[0, 0], xk_ref[0, 0], v_ref[0, 0])
    q2, xq = q_ref[0, 0], xq_ref[0, 0]
    qa_scr[0] = jnp.where(low, q2, xq)
    qa_scr[1] = jnp.where(low, xq, q2)
    causal_pen = jnp.where(lax.broadcasted_iota(jnp.int32, (tq, tq), 0)
                           >= lax.broadcasted_iota(jnp.int32, (tq, tq), 1), 0.0, NEG)

    def tile(n):
        return pl.ds(pl.multiple_of(n * tq, tq), tq)

    def logits(j, i):
        return tuple(_dot_nt(qa_scr[h, tile(i), :], ka_scr[h, tile(j), :]) for h in range(2))

    def diag_update(i, s):
        for h in range(2):
            m_scr[i, h], acc_scr[i, h] = _flash_start(s[h] + causal_pen, va_scr[h, tile(i), :])

    _worklist_sweep(nq, lambda n: logits(n, n), diag_update, sa_scr, sb_scr)

    def off_update(n, s):
        j, i = jt_ref[n], it_ref[n]
        for h in range(2):
            m_scr[i, h], acc_scr[i, h] = _flash_update(s[h], m_scr[i, h], acc_scr[i, h], va_scr[h, tile(j), :])

    _worklist_sweep(n_off, lambda n: logits(jt_ref[n], it_ref[n]), off_update, sa_scr, sb_scr)

    def finish(i, _):
        o_ref[0, 0, tile(i), :] = jnp.where(low, _normalise(acc_scr[i, 0]), _normalise(acc_scr[i, 1])).astype(o_ref.dtype)
        return 0

    lax.fori_loop(0, nq, finish, 0)


def _fox_attention(qkv, xq, xk, tq):
    _, b, t, _ = qkv.shape
    nq = t // tq
    pairs = [(j, i) for j in range(nq) for i in range(j + 1, nq)]
    n_off = len(pairs)
    pairs += [(0, 0)] * max(0, 2 - n_off)
    jt = jnp.asarray([p[0] for p in pairs], jnp.int32)
    it = jnp.asarray([p[1] for p in pairs], jnp.int32)
    smem = pl.BlockSpec(memory_space=pltpu.SMEM)
    seq = lambda cb: pl.BlockSpec((1, 1, t, LANES), lambda bi, p: (cb + p, bi, 0, 0))
    return pl.pallas_call(
        functools.partial(_fox_kernel, tq=tq, n_off=n_off),
        grid=(b, N_QBLK),
        in_specs=[smem, smem, seq(CB_CQ), seq(0), seq(CB_CK), seq(0), seq(CB_CV)],
        out_specs=pl.BlockSpec((1, 1, t, LANES), lambda bi, p: (p, bi, 0, 0)),
        out_shape=jax.ShapeDtypeStruct((N_QBLK, b, t, LANES), BF16),
        scratch_shapes=[pltpu.VMEM((2, t, LANES), BF16), pltpu.VMEM((2, t, LANES), BF16),
                        pltpu.VMEM((2, t, LANES), BF16),
                        pltpu.VMEM((nq, 2, tq, LANES), F32), pltpu.VMEM((nq, 2, tq, LANES), F32),
                        pltpu.VMEM((2, tq, tq), F32), pltpu.VMEM((2, tq, tq), F32)],
        compiler_params=_cparams("parallel", "parallel"),
        name="fox_attention",
    )(jt, it, qkv, xq, qkv, xk, qkv)


def _merge_kernel(ocmp_ref, oslc_ref, owin_ref, ob_ref, oc_ref, gs_ref, eg_ref, mg_ref, wb_ref, wo_ref, x_ref, o_ref):
    def cat(ref):
        return jnp.concatenate([ref[p] for p in range(N_QBLK)], axis=-1)

    ga = _dot(gs_ref[...], eg_ref[...])
    oa = sum(ga[:, n * MIX_WIDTH:(n + 1) * MIX_WIDTH] * cat(ref).astype(F32)
             for n, ref in enumerate((ocmp_ref, oslc_ref, owin_ref))).astype(BF16)
    branches = (oa, cat(ob_ref), cat(oc_ref))
    d = o_ref.shape[1]
    chunks = [slice(c, c + MM_CHUNK) for c in range(0, d, MM_CHUNK)]
    mix = jnp.concatenate(
        [sum(mg_ref[n, :, cols].astype(F32) * _dot(branches[n], wb_ref[n, :, cols]) for n in range(N_BRANCH))
         .astype(BF16) for cols in chunks], axis=-1)
    for cols in chunks:
        o_ref[:, cols] = x_ref[:, cols] + _dot(mix, wo_ref[:, cols])


def _merge(ocmp, oslc, owin, ob, oc, gsig, eg, mg, wb, wo, x2, tm=BRANCH_MERGE_ROW_TILE):
    bt, d = x2.shape
    o_spec = pl.BlockSpec((N_QBLK, tm, LANES), lambda i: (0, i, 0))
    return pl.pallas_call(
        _merge_kernel,
        grid=(bt // tm,),
        in_specs=[o_spec] * 5 + [pl.BlockSpec((tm, LANES), lambda i: (i, 0)),
                                 pl.BlockSpec(eg.shape, lambda i: (0, 0)),
                                 pl.BlockSpec((N_BRANCH, tm, d), lambda i: (0, i, 0)),
                                 pl.BlockSpec(wb.shape, lambda i: (0, 0, 0)),
                                 pl.BlockSpec(wo.shape, lambda i: (0, 0)),
                                 pl.BlockSpec((tm, d), lambda i: (i, 0))],
        out_specs=pl.BlockSpec((tm, d), lambda i: (i, 0)),
        out_shape=jax.ShapeDtypeStruct((bt, d), F32),
        compiler_params=_cparams("parallel"),
        name="branch_merge",
    )(ocmp, oslc, owin, ob, oc, gsig, eg, mg, wb, wo, x2)


def _ffn_kernel(x_ref, xh_ref, g_ref, gn_ref, wg_ref, wu_ref, cw_ref, cb_ref, wd_ref, o_ref, *rest,
                tm, tiles_per_seq, emit_next):
    hn_ref = rest[0] if emit_next else None
    h_scr, hh_scr, a_scr, acc_scr = rest[-4:]
    i = pl.program_id(0)
    f = pl.program_id(1)

    def norm(x, gain_ref):
        ms = jnp.mean(x * x, axis=-1, keepdims=True)
        return ((x * lax.rsqrt(ms + RMS_EPS)) * gain_ref[...]).astype(BF16)

    @pl.when(f == 0)
    def _():
        h_scr[...] = norm(x_ref[...], g_ref)
        hh_scr[...] = norm(xh_ref[...], g_ref)
        acc_scr[...] = jnp.zeros_like(acc_scr)

    a_scr[0:8, :] = _dot(hh_scr[...], wg_ref[...]) * jnp.where(i % tiles_per_seq == 0, 0.0, 1.0)
    acts = []
    for c in range(wg_ref.shape[1] // MM_CHUNK):
        cols = slice(c * MM_CHUNK, (c + 1) * MM_CHUNK)
        a = _dot(h_scr[...], wg_ref[:, cols])
        a_scr[8:8 + tm, cols] = a
        conv = (cw_ref[0:1, cols] * a_scr[6:6 + tm, cols] + cw_ref[1:2, cols] * a_scr[7:7 + tm, cols]
                + cw_ref[2:3, cols] * a + cb_ref[:, cols])
        acts.append((_gelu_tanh(conv) * _dot(h_scr[...], wu_ref[:, cols])).astype(BF16))
    acc_scr[...] += _dot(jnp.concatenate(acts, axis=-1), wd_ref[...])

    @pl.when(f == pl.num_programs(1) - 1)
    def _():
        out = x_ref[...] + acc_scr[...]
        o_ref[...] = out
        if emit_next:
            hn_ref[...] = norm(out, gn_ref)


def _ffn(x2, g, g_next, wg, wu, cw, cb, wd, t, tm=ROW_TILE, tf=FFN_TF):
    bt, d = x2.shape
    emit_next = g_next is not None
    g_next = g_next if emit_next else g
    pad = -wg.shape[1] % tf
    wg, wu, cw = [jnp.pad(a, ((0, 0), (0, pad))) for a in (wg, wu, cw)]
    cb, wd = jnp.pad(cb, (0, pad)), jnp.pad(wd, ((0, pad), (0, 0)))
    ff = wg.shape[1]
    halo_blocks = tm // 8
    return pl.pallas_call(
        functools.partial(_ffn_kernel, tm=tm, tiles_per_seq=t // tm, emit_next=emit_next),
        grid=(bt // tm, ff // tf),
        in_specs=[pl.BlockSpec((tm, d), lambda i, f: (i, 0)),
                  pl.BlockSpec((8, d), lambda i, f: (jnp.maximum(i * halo_blocks - 1, 0), 0)),
                  pl.BlockSpec((1, d), lambda i, f: (0, 0)),
                  pl.BlockSpec((1, d), lambda i, f: (0, 0)),
                  pl.BlockSpec((d, tf), lambda i, f: (0, f)),
                  pl.BlockSpec((d, tf), lambda i, f: (0, f)),
                  pl.BlockSpec((CONV_WIDTH, tf), lambda i, f: (0, f)),
                  pl.BlockSpec((1, tf), lambda i, f: (0, f)),
                  pl.BlockSpec((tf, d), lambda i, f: (f, 0))],
        out_specs=[pl.BlockSpec((tm, d), lambda i, f: (i, 0))] * (1 + emit_next),
        out_shape=[jax.ShapeDtypeStruct((bt, d), F32), jax.ShapeDtypeStruct((bt, d), BF16)][:1 + emit_next],
        scratch_shapes=[pltpu.VMEM((tm, d), BF16), pltpu.VMEM((8, d), BF16),
                        pltpu.VMEM((tm + 8, tf), F32), pltpu.VMEM((tm, d), F32)],
        compiler_params=_cparams("parallel", "arbitrary"),
        name="conv_ffn",
    )(x2, x2, g.reshape(1, d), g_next.reshape(1, d), wg, wu, cw, cb.reshape(1, ff), wd)


def _t5_bucket_np(dist):
    max_exact = N_BUCKETS // 2
    d = np.maximum(dist, 0)
    ratio = np.log(np.maximum(d, 1) / max_exact) / math.log(MAX_DISTANCE / max_exact)
    large = np.minimum(max_exact + (ratio * (N_BUCKETS - max_exact)).astype(np.int64), N_BUCKETS - 1)
    return np.where(d < max_exact, d, large)


def _expand_kernel(tab_ref, idx_ref, o_ref):
    tab = tab_ref[...]
    onehot = jnp.where(lax.broadcasted_iota(jnp.int32, (tab.shape[1], idx_ref.shape[1]), 0) == idx_ref[...],
                       1.0, 0.0).astype(BF16)
    hi, rest = tab.astype(BF16), tab - tab.astype(BF16).astype(F32)
    mid, lo = _split_hi_lo(rest)
    heads = tab.shape[0]
    parts = _dot(jnp.concatenate([hi, mid, lo], axis=0), onehot)
    o_ref[...] = parts[0:heads] + parts[heads:2 * heads] + parts[2 * heads:3 * heads]


def _bias_from_dist(tab_t, dist, valid, chunk=EXPAND_CHUNK):
    heads = tab_t.shape[0]
    tab = jnp.zeros((heads, LANES), F32).at[:, :N_BUCKETS].set(tab_t).at[:, N_BUCKETS].set(NEG)
    idx = np.where(valid, _t5_bucket_np(dist), N_BUCKETS).reshape(1, -1).astype(np.int32)
    n = idx.shape[1]
    chunk = min(chunk, n)
    out = pl.pallas_call(
        _expand_kernel,
        grid=(n // chunk,),
        in_specs=[pl.BlockSpec((heads, LANES), lambda i: (0, 0)), pl.BlockSpec((1, chunk), lambda i: (0, i))],
        out_specs=pl.BlockSpec((heads, chunk), lambda i: (0, i)),
        out_shape=jax.ShapeDtypeStruct((heads, n), F32),
        compiler_params=_cparams("parallel"),
        name="rel_bias_expand",
    )(tab, jnp.asarray(idx))
    return out.reshape((heads,) + dist.shape)


def _cmp_bias_kernel(g_ref, o_ref):
    ncp = o_ref.shape[2]
    per_step = o_ref.shape[1] // Q_BLOCK
    for sub in range(per_step):
        shift = (pl.program_id(0) * per_step + sub) * (Q_BLOCK // CMP_STRIDE)
        for h in range(o_ref.shape[0]):
            o_ref[h, sub * Q_BLOCK:(sub + 1) * Q_BLOCK, :] = pltpu.roll(g_ref[h], shift, 1)[:, :ncp]


def _cmp_bias(g_tab, t):
    heads, _, width = g_tab.shape
    ncp = t // CMP_STRIDE
    return pl.pallas_call(
        _cmp_bias_kernel,
        grid=(t // CMP_TQ,),
        in_specs=[pl.BlockSpec(g_tab.shape, lambda i: (0, 0, 0))],
        out_specs=pl.BlockSpec((heads, CMP_TQ, ncp), lambda i: (0, i, 0)),
        out_shape=jax.ShapeDtypeStruct((heads, t, ncp), F32),
        compiler_params=_cparams("parallel"),
        name="cmp_bias_build",
    )(g_tab)


def _static_tables(t):
    ncp = t // CMP_STRIDE
    n_cmp = (t - CMP_LEN) // CMP_STRIDE + 1
    n_slc = t // SLC_LEN
    lane = np.arange(MM_CHUNK)
    bd = (lane[:, None] // HEAD_DIM == lane[None, :] // HEAD_DIM).astype(np.float32)
    eg = np.zeros((LANES, N_BRANCH * MIX_WIDTH), np.float32)
    for h in range(N_HEADS):
        base = (h % N_QBLK) * LANES + (h // N_QBLK) * HEAD_DIM
        for n in range(N_BRANCH):
            eg[h * N_BRANCH + n, n * MIX_WIDTH + base:n * MIX_WIDTH + base + HEAD_DIM] = 1.0
    c_start = np.arange(n_cmp) * CMP_STRIDE
    s_start = np.arange(n_slc) * SLC_LEN
    overlap = ((c_start[:, None] < s_start[None, :] + SLC_LEN) & (c_start[:, None] + CMP_LEN > s_start[None, :]))
    ovt = np.zeros((LANES, ncp), np.float32)
    ovt[:n_slc, :n_cmp] = overlap.T
    key_block = np.zeros((t, LANES), np.float32)
    key_block[np.arange(t), np.arange(t) // SLC_LEN] = 1.0
    key_block[np.arange(t), HEAD_DIM + np.arange(t) // SLC_LEN] = 1.0
    as_bf16 = lambda a: jnp.asarray(a, BF16)
    return dict(bd=as_bf16(bd), eg=as_bf16(eg), ovt=as_bf16(ovt), key_block=as_bf16(key_block), n_slc=n_slc)


def _bias_tables(rel_bias, t):
    n_cmp = (t - CMP_LEN) // CMP_STRIDE + 1
    ncp = t // CMP_STRIDE
    tab_a = rel_bias[:, :N_HEADS].T * LOG2E
    tab_b = rel_bias[:, N_HEADS:].T * LOG2E
    r = np.arange(Q_BLOCK)[:, None]

    def band(window):
        n_prev = -(-(window - 1) // Q_BLOCK)
        n_keys = n_prev * Q_BLOCK + BAND_TQ
        dist = (n_keys - BAND_TQ) + np.arange(BAND_TQ)[:, None] - np.arange(n_keys)[None, :]
        valid = (dist >= 0) & (dist < window)
        assert not valid[:, 0].any()
        return dist, valid

    m = np.arange(2 * ncp)
    m = np.where(m < ncp, m, m - 2 * ncp)[None, :]
    dist_c = r - (m * CMP_STRIDE + CMP_LEN - 1)
    assert n_cmp == ncp - 1 and (dist_c[:, m[0] == Q_BLOCK // CMP_STRIDE - 1] < 0).all()
    tile = np.arange(SLC_TILE)
    dist_n = np.stack([SLC_TILE * (1 - which) + tile[:, None] - tile[None, :] for which in range(2)])
    near = _bias_from_dist(tab_a - tab_a[:, N_BUCKETS - 1:], dist_n, dist_n >= 0)
    return dict(win=_bias_from_dist(tab_a, *band(NSA_WINDOW)),
                swa=_bias_from_dist(tab_b, *band(SWA_WINDOW)),
                cmp=_cmp_bias(_bias_from_dist(tab_a, dist_c, dist_c >= 0), t),
                near=jnp.transpose(near, (1, 0, 2, 3)))


def _pair_cols(w):
    parts = []
    for p in range(N_QBLK):
        parts += [w[..., p * HEAD_DIM:(p + 1) * HEAD_DIM],
                  w[..., (N_QBLK + p) * HEAD_DIM:(N_QBLK + p + 1) * HEAD_DIM]]
    return jnp.concatenate(parts, axis=-1)


def _prep_layer(w_in, qk_gain, w_branch):
    kv = KV_GROUPS * HEAD_DIM
    widths = [('a_q', MIX_WIDTH), ('a_kc', kv), ('a_vc', kv), ('a_ks', kv), ('a_vs', kv), ('a_kw', kv), ('a_vw', kv),
              ('a_gate', N_GATE_COLS), ('b_q', MIX_WIDTH), ('b_k', kv), ('b_v', kv),
              ('c_q', MIX_WIDTH), ('c_k', MIX_WIDTH), ('c_v', MIX_WIDTH), ('c_f', N_HEADS),
              ('merge', N_BRANCH * w_in.shape[0])]
    cols, off = {}, 0
    for name, w in widths:
        cols[name] = w_in[:, off:off + w]
        off += w
    scale = HEAD_DIM ** -0.5 * LOG2E
    tile = lambda gvec, n: jnp.tile(gvec, n)
    zeros = lambda n: jnp.zeros((n,), F32)
    ones = lambda n: jnp.ones((n,), F32)
    pieces = [(_pair_cols(cols['a_q']), tile(qk_gain[0] * scale, N_HEADS), ones(MIX_WIDTH)),
              (_pair_cols(cols['b_q']), tile(qk_gain[2] * scale, N_HEADS), ones(MIX_WIDTH)),
              (cols['c_q'], tile(qk_gain[4] * scale, N_HEADS), ones(MIX_WIDTH)),
              (cols['c_k'], tile(qk_gain[5], N_HEADS), ones(MIX_WIDTH)),
              (cols['a_ks'], tile(qk_gain[1], KV_GROUPS), ones(kv)),
              (cols['a_kw'], tile(qk_gain[1], KV_GROUPS), ones(kv)),
              (cols['b_k'], tile(qk_gain[3], KV_GROUPS), ones(kv)),
              (cols['a_vs'], zeros(kv), zeros(kv)),
              (cols['c_v'], zeros(MIX_WIDTH), zeros(MIX_WIDTH)),
              (cols['a_kc'], zeros(kv), zeros(kv)), (cols['a_vc'], zeros(kv), zeros(kv)),
              (cols['a_vw'], zeros(kv), zeros(kv)), (cols['b_v'], zeros(kv), zeros(kv))]
    w_qkv = jnp.concatenate([p[0] for p in pieces], axis=1).astype(BF16)
    n_qkv = w_qkv.shape[1]
    aux = jnp.zeros((8, n_qkv), F32)
    aux = aux.at[0].set(jnp.concatenate([p[1] for p in pieces])).at[1].set(jnp.concatenate([p[2] for p in pieces]))
    d = w_in.shape[0]
    w_gates = jnp.concatenate([cols['a_gate'], cols['c_f'],
                               jnp.zeros((d, LANES - N_GATE_COLS - N_HEADS), F32)], axis=1).astype(BF16)
    wb = jnp.stack([_pair_cols(w_branch[0].T).T, _pair_cols(w_branch[1].T).T, w_branch[2]]).astype(BF16)
    return w_qkv, aux, w_gates, cols['merge'].astype(BF16), wb


def _prep_compress(cmp_pos, cmp_w1, cmp_w2, k_gain):
    def both_groups(w):
        z = jnp.zeros_like(w)
        return jnp.concatenate([jnp.concatenate([w, z], -1), jnp.concatenate([z, w], -1)], -2)

    pos = jnp.concatenate([cmp_pos, cmp_pos], -1)
    w1 = both_groups(cmp_w1.reshape(2, CMP_LEN, HEAD_DIM, CMP_HIDDEN)).astype(BF16)
    w2 = both_groups(cmp_w2).astype(BF16)
    aux = jnp.zeros((2, 8, LANES), F32)
    aux = aux.at[0, 0].set(jnp.tile(k_gain, KV_GROUPS)).at[0, 1].set(1.0)
    return pos, w1, w2, aux


def kernel(x, rel_bias, norm_mix, norm_ffn, w_in, forget_bias, qk_gain, cmp_pos, cmp_w1, cmp_w2, sinks, w_branch, w_out, w_gate, w_up, conv_w, conv_b, w_down):
    b, t, d = x.shape
    bt = b * t
    depth = w_in.shape[0]
    tabs = _static_tables(t)
    biases = _bias_tables(rel_bias, t)
    fox_tq = min(FOX_TQ, t)
    x2 = x.reshape(bt, d)
    h = _rmsnorm(x2, norm_mix[0])
    for l in range(depth):
        w_qkv, aux, w_gates, w_merge, wb = _prep_layer(w_in[l], qk_gain[l], w_branch[l])
        qkv = _inproj_qkv(h, w_qkv, aux, tabs['bd']).reshape(N_CB, b, t, LANES)
        graw, gsig = _inproj_gates(h, w_gates)
        mg = _inproj_merge(h, w_merge, d)

        cmp_kv = _compress(qkv, *_prep_compress(cmp_pos[l], cmp_w1[l], cmp_w2[l], qk_gain[l, 1]), tabs['bd'])
        o_cmp, qa = _cmp_select(qkv, cmp_kv, biases['cmp'], tabs['ovt'], tabs['n_slc'])
        o_slc = _slc_attention(qa, qkv, tabs['key_block'], biases['near'])
        o_win = _banded_attention(qkv, CB_AQ, CB_AKW, CB_AVW, biases['win'], False)

        swa_bias = biases['swa'].at[:, :, 0].set(sinks[l][:, None] * LOG2E)
        o_b = _banded_attention(qkv, CB_BQ, CB_BK, CB_BV, swa_bias, True)

        fb_row = jnp.zeros((1, LANES), F32).at[0, N_GATE_COLS:N_GATE_COLS + N_HEADS].set(forget_bias[l])
        o_c = _fox_attention(qkv, *_fox_prep(graw, fb_row, b, t), fox_tq)

        flat = lambda o: o.reshape(N_QBLK, bt, LANES)
        x2 = _merge(flat(o_cmp), flat(o_slc), flat(o_win), flat(o_b), flat(o_c), gsig, tabs['eg'], mg, wb,
                    w_out[l].astype(BF16), x2)
        x2, *h = _ffn(x2, norm_ffn[l], norm_mix[l + 1] if l + 1 < depth else None, w_gate[l].astype(BF16),
                      w_up[l].astype(BF16), conv_w[l], conv_b[l], w_down[l].astype(BF16), t)
        h = h[0] if h else None
    return x2.reshape(b, t, d)
```

```python
import functools
import math

import numpy as np
import jax
import jax.numpy as jnp
from jax import lax
from jax.experimental import pallas as pl
from jax.experimental.pallas import tpu as pltpu

F32 = jnp.float32
BF16 = jnp.bfloat16

HEAD_DIM = 64
LANES = 128
N_HEADS = 8
KV_GROUPS = 2
HEADS_PER_GROUP = N_HEADS // KV_GROUPS
N_QBLK = N_HEADS * HEAD_DIM // LANES
MIX_WIDTH = N_HEADS * HEAD_DIM
N_BRANCH = 3
Q_BLOCK = 128
CMP_LEN = 32
CMP_STRIDE = 16
CMP_HIDDEN = 256
SLC_LEN = 64
SLC_TOPK = 8
NSA_WINDOW = 256
SWA_WINDOW = 128
N_BUCKETS = 32
MAX_DISTANCE = 128
CONV_WIDTH = 3
RMS_EPS = 1e-6
LOG2E = math.log2(math.e)
NEG = -1e30
FORCE_SCORE = 1e9
TAKEN = -3e38

VMEM_LIMIT = 48 * 1024 * 1024
MM_CHUNK = 256
MM_ROWS = 512
ROW_TILE = 1024
NORM_ROW_TILE = 2048
QKV_ROW_TILE, QKV_COL_TILE = 4096, 512
MERGE_GATE_ROW_TILE, MERGE_GATE_COL_TILE = 2048, 1024
BRANCH_MERGE_ROW_TILE = 512
FFN_TF = 1024
CMP_TQ = 512
SLC_TILE = 256
SLC_QGROUP = 8
BAND_TQ = 256
FOX_TQ = 512
FOX_PREP_CHUNK = 512
EXPAND_CHUNK = 4096

CB_AQ, CB_BQ, CB_CQ, CB_CK = 0, 4, 8, 12
CB_AKS, CB_AKW, CB_BK, CB_AVS = 16, 17, 18, 19
CB_CV, CB_AKC, CB_AVC, CB_AVW, CB_BV = 20, 24, 25, 26, 27
N_CB = 28
N_NORM_CB = 20
N_GATE_COLS = N_HEADS * N_BRANCH


def _cparams(*sem):
    return pltpu.CompilerParams(dimension_semantics=sem, vmem_limit_bytes=VMEM_LIMIT)


def _split_hi_lo(v):
    hi = v.astype(BF16)
    return hi, (v - hi.astype(F32)).astype(BF16)


def _dot(a, b):
    return jnp.dot(a, b, preferred_element_type=F32)


def _dot_nt(a, b):
    return lax.dot_general(a, b, (((1,), (1,)), ((), ())), preferred_element_type=F32)


def _sigmoid(x):
    return 0.5 * jnp.tanh(0.5 * x) + 0.5


def _gelu_tanh(x):
    return x * (0.5 * (1.0 + jnp.tanh(math.sqrt(2.0 / math.pi) * (x + 0.044715 * (x * x * x)))))


def _head_rms(y, bd, gain):
    ms = _dot((y * y).astype(BF16), bd) * (1.0 / HEAD_DIM)
    return (y * lax.rsqrt(ms + RMS_EPS)) * gain


def _lane_half_mask(g):
    return (lax.broadcasted_iota(jnp.int32, (1, LANES), 1) // HEAD_DIM) == g


def _merge_group_outputs(o_ref, o_g0, o_g1):
    low = _lane_half_mask(0)
    tq = o_ref.shape[2]
    for p in range(N_QBLK):
        rows = slice(p * tq, (p + 1) * tq)
        o_ref[p, 0] = jnp.where(low, o_g0[rows], o_g1[rows]).astype(o_ref.dtype)


def _rmsnorm_kernel(x_ref, g_ref, o_ref):
    x = x_ref[...]
    ms = jnp.mean(x * x, axis=-1, keepdims=True)
    o_ref[...] = ((x * lax.rsqrt(ms + RMS_EPS)) * g_ref[...]).astype(o_ref.dtype)


def _rmsnorm(x2, g, tm=NORM_ROW_TILE):
    bt, d = x2.shape
    tm = min(tm, bt)
    return pl.pallas_call(
        _rmsnorm_kernel,
        grid=(bt // tm,),
        in_specs=[pl.BlockSpec((tm, d), lambda i: (i, 0)), pl.BlockSpec((1, d), lambda i: (0, 0))],
        out_specs=pl.BlockSpec((tm, d), lambda i: (i, 0)),
        out_shape=jax.ShapeDtypeStruct((bt, d), BF16),
        compiler_params=_cparams("parallel"),
        name="rmsnorm",
    )(x2, g.reshape(1, d))


def _inproj_qkv_kernel(h_ref, w_ref, aux_ref, bd_ref, o_ref, *, n_norm_tiles):
    per_chunk = MM_CHUNK // LANES

    def emit(normalise):
        for r in range(0, h_ref.shape[0], MM_ROWS):
            rows = slice(r, r + MM_ROWS)
            y_all = _dot(h_ref[rows, :], w_ref[...])
            for c in range(w_ref.shape[1] // MM_CHUNK):
                cols = slice(c * MM_CHUNK, (c + 1) * MM_CHUNK)
                y = y_all[:, cols]
                if normalise:
                    normed = _head_rms(y, bd_ref[...], aux_ref[0:1, cols])
                    y = jnp.where(aux_ref[1:2, cols] > 0.5, normed, y)
                for s in range(per_chunk):
                    o_ref[c * per_chunk + s, rows, :] = y[:, s * LANES:(s + 1) * LANES].astype(o_ref.dtype)

    pl.when(pl.program_id(1) < n_norm_tiles)(lambda: emit(True))
    pl.when(pl.program_id(1) >= n_norm_tiles)(lambda: emit(False))


def _inproj_qkv(h, w, aux, bd, tm=QKV_ROW_TILE, tn=QKV_COL_TILE):
    bt, d = h.shape
    n = w.shape[1]
    n_sub = tn // LANES
    tm = min(tm, bt)
    assert N_NORM_CB % n_sub == 0 and tm % MM_ROWS == 0
    return pl.pallas_call(
        functools.partial(_inproj_qkv_kernel, n_norm_tiles=N_NORM_CB // n_sub),
        grid=(bt // tm, n // tn),
        in_specs=[pl.BlockSpec((tm, d), lambda i, j: (i, 0)),
                  pl.BlockSpec((d, tn), lambda i, j: (0, j)),
                  pl.BlockSpec((8, tn), lambda i, j: (0, j)),
                  pl.BlockSpec(bd.shape, lambda i, j: (0, 0))],
        out_specs=pl.BlockSpec((n_sub, tm, LANES), lambda i, j: (j, i, 0)),
        out_shape=jax.ShapeDtypeStruct((n // LANES, bt, LANES), BF16),
        compiler_params=_cparams("parallel", "arbitrary"),
        name="inproj_qkv",
    )(h, w, aux, bd)


def _inproj_gates_kernel(h_ref, w_ref, raw_ref, sig_ref):
    y = _dot(h_ref[...], w_ref[...])
    raw_ref[...] = y
    sig_ref[...] = _sigmoid(y).astype(sig_ref.dtype)


def _inproj_gates(h, w, tm=NORM_ROW_TILE):
    bt, d = h.shape
    tm = min(tm, bt)
    return pl.pallas_call(
        _inproj_gates_kernel,
        grid=(bt // tm,),
        in_specs=[pl.BlockSpec((tm, d), lambda i: (i, 0)),
                  pl.BlockSpec((d, LANES), lambda i: (0, 0))],
        out_specs=[pl.BlockSpec((tm, LANES), lambda i: (i, 0))] * 2,
        out_shape=[jax.ShapeDtypeStruct((bt, LANES), F32), jax.ShapeDtypeStruct((bt, LANES), BF16)],
        compiler_params=_cparams("parallel"),
        name="inproj_gates",
    )(h, w)


def _inproj_merge_kernel(h_ref, w_ref, o_ref):
    width = 2 * MM_CHUNK
    for c in range(w_ref.shape[1] // width):
        cols = slice(c * width, (c + 1) * width)
        o_ref[0, :, cols] = _sigmoid(_dot(h_ref[...], w_ref[:, cols])).astype(o_ref.dtype)


def _inproj_merge(h, w, d_model, tm=MERGE_GATE_ROW_TILE, tn=MERGE_GATE_COL_TILE):
    bt, d = h.shape
    tm = min(tm, bt)
    n = w.shape[1]
    per = d_model // tn
    return pl.pallas_call(
        _inproj_merge_kernel,
        grid=(bt // tm, n // tn),
        in_specs=[pl.BlockSpec((tm, d), lambda i, j: (i, 0)),
                  pl.BlockSpec((d, tn), lambda i, j: (0, j))],
        out_specs=pl.BlockSpec((1, tm, tn), lambda i, j: (j // per, i, j % per)),
        out_shape=jax.ShapeDtypeStruct((n // d_model, bt, d_model), BF16),
        compiler_params=_cparams("parallel", "arbitrary"),
        name="inproj_merge",
    )(h, w)


def _compress_kernel(z_ref, pos_ref, w1_ref, w2_ref, aux_ref, bd_ref, o_ref, zf_scr):
    n_chunks = o_ref.shape[2]
    zf_scr[...] = z_ref[0, 0].astype(F32)
    halves = []
    for half in range(CMP_LEN // CMP_STRIDE):
        acc = jnp.zeros((n_chunks, KV_GROUPS * CMP_HIDDEN), F32)
        for l in range(CMP_STRIDE):
            rows = zf_scr[pl.ds(l, n_chunks, stride=CMP_STRIDE), :]
            pos = pos_ref[0, half * CMP_STRIDE + l:half * CMP_STRIDE + l + 1, :]
            acc = acc + _dot((rows + pos).astype(BF16), w1_ref[0, half * CMP_STRIDE + l])
        halves.append(acc)
    hid = _gelu_tanh(halves[0] + pltpu.roll(halves[1], n_chunks - 1, 0))
    y = _dot(hid.astype(BF16), w2_ref[0])
    normed = _head_rms(y, bd_ref[0:LANES, 0:LANES], aux_ref[0, 0:1, :])
    o_ref[0, 0] = jnp.where(aux_ref[0, 1:2, :] > 0.5, normed, y).astype(o_ref.dtype)


def _compress(qkv, pos, w1, w2, aux, bd):
    _, b, t, _ = qkv.shape
    n_chunks = t // CMP_STRIDE
    return pl.pallas_call(
        _compress_kernel,
        grid=(2, b),
        in_specs=[pl.BlockSpec((1, 1, t, LANES), lambda w, i: (CB_AKC + w, i, 0, 0)),
                  pl.BlockSpec((1,) + pos.shape[1:], lambda w, i: (w, 0, 0)),
                  pl.BlockSpec((1,) + w1.shape[1:], lambda w, i: (w, 0, 0, 0)),
                  pl.BlockSpec((1,) + w2.shape[1:], lambda w, i: (w, 0, 0)),
                  pl.BlockSpec((1, 8, LANES), lambda w, i: (w, 0, 0)),
                  pl.BlockSpec(bd.shape, lambda w, i: (0, 0))],
        out_specs=pl.BlockSpec((1, 1, n_chunks, LANES), lambda w, i: (w, i, 0, 0)),
        out_shape=jax.ShapeDtypeStruct((2, b, n_chunks, LANES), BF16),
        scratch_shapes=[pltpu.VMEM((t, LANES), F32)],
        compiler_params=_cparams("parallel", "parallel"),
        name="nsa_compress",
    )(qkv, pos, w1, w2, aux, bd)


def _cmp_select_kernel(q_ref, kc_ref, vc_ref, bias_ref, ovt_ref, o_ref, qa_ref, *, n_slc, n_top):
    ncp = kc_ref.shape[2]
    rows = HEADS_PER_GROUP * Q_BLOCK
    jrow = lax.broadcasted_iota(jnp.int32, (n_slc, Q_BLOCK), 0)
    jrow_f = jrow.astype(F32)
    low = _lane_half_mask(0)
    for sub in range(q_ref.shape[2] // Q_BLOCK):
        rs = slice(sub * Q_BLOCK, (sub + 1) * Q_BLOCK)
        t0 = pl.program_id(1) * q_ref.shape[2] + sub * Q_BLOCK
        row_t = t0 + lax.broadcasted_iota(jnp.int32, (Q_BLOCK, min(LANES, ncp)), 0)
        row_valid = jnp.concatenate([row_t >= CMP_LEN - 1] * HEADS_PER_GROUP, axis=0)
        cur = (t0 + lax.broadcasted_iota(jnp.int32, (n_slc, Q_BLOCK), 1)) // SLC_LEN

        def force(imp):
            forced = jnp.where(jrow == cur, FORCE_SCORE, jnp.where(jrow == cur - 1, FORCE_SCORE, imp))
            return jnp.where(jrow == 0, FORCE_SCORE, forced)

        outs = []
        for g in range(KV_GROUPS):
            own_half = _lane_half_mask(g)
            hs = slice(g * HEADS_PER_GROUP, (g + 1) * HEADS_PER_GROUP)
            qop = jnp.concatenate([jnp.where(own_half, q_ref[p, 0, rs, :], jnp.zeros((Q_BLOCK, LANES), q_ref.dtype))
                                   for p in range(N_QBLK)], axis=0)
            s = _dot_nt(qop, kc_ref[0, 0]) + bias_ref[hs, rs, :].reshape(rows, ncp)
            m = _row_max(s)
            e = [jnp.exp2(c - m) for c in _lane_chunks(s)]
            row_sum = jnp.broadcast_to(jnp.sum(sum(e), axis=-1, keepdims=True), e[0].shape)
            inv = jnp.where(row_valid, 1.0 / row_sum, 0.0)
            p = jnp.concatenate([c * inv for c in e], axis=-1)
            outs.append(_dot(p.astype(BF16), vc_ref[0, 0]))
            psum = p[0:Q_BLOCK]
            for r in range(1, HEADS_PER_GROUP):
                psum = psum + p[r * Q_BLOCK:(r + 1) * Q_BLOCK]
            hi, lo = _split_hi_lo(psum)
            imp_t = (_dot_nt(ovt_ref[...], hi) + _dot_nt(ovt_ref[...], lo))[0:n_slc]
            score = jnp.where(jrow <= cur, force(imp_t), NEG)
            sel_bias = jnp.full((n_slc, Q_BLOCK), NEG, F32)
            for _ in range(n_top):
                best = jnp.max(score, axis=0, keepdims=True)
                first = jnp.min(jnp.where(score == best, jrow_f, float(n_slc)), axis=0, keepdims=True)
                hit = jrow_f == first
                sel_bias = jnp.where(hit, jnp.where(best > NEG / 2, 0.0, NEG), sel_bias)
                score = jnp.where(hit, TAKEN, score)
            if n_slc < HEAD_DIM:
                sel_bias = jnp.concatenate([sel_bias, jnp.full((HEAD_DIM - n_slc, Q_BLOCK), NEG, F32)], axis=0)
            sel_t = jnp.concatenate([sel_bias, sel_bias], axis=0).T.astype(qa_ref.dtype)
            for p in range(N_QBLK):
                qa_ref[g * HEADS_PER_GROUP + p, 0, rs, :] = jnp.where(own_half, q_ref[p, 0, rs, :], sel_t)
        for p in range(N_QBLK):
            prs = slice(p * Q_BLOCK, (p + 1) * Q_BLOCK)
            o_ref[p, 0, rs, :] = jnp.where(low, outs[0][prs], outs[1][prs]).astype(o_ref.dtype)


def _cmp_select(qkv, cmp_kv, bias_c, ovt, n_slc, tq=CMP_TQ):
    _, b, t, _ = qkv.shape
    ncp = cmp_kv.shape[2]
    assert n_slc <= HEAD_DIM
    return pl.pallas_call(
        functools.partial(_cmp_select_kernel, n_slc=n_slc, n_top=min(SLC_TOPK, n_slc)),
        grid=(b, t // tq),
        in_specs=[pl.BlockSpec((N_QBLK, 1, tq, LANES), lambda bi, i: (CB_AQ // N_QBLK, bi, i, 0)),
                  pl.BlockSpec((1, 1, ncp, LANES), lambda bi, i: (0, bi, 0, 0)),
                  pl.BlockSpec((1, 1, ncp, LANES), lambda bi, i: (1, bi, 0, 0)),
                  pl.BlockSpec((N_HEADS, tq, ncp), lambda bi, i: (0, i, 0)),
                  pl.BlockSpec((LANES, ncp), lambda bi, i: (0, 0))],
        out_specs=[pl.BlockSpec((N_QBLK, 1, tq, LANES), lambda bi, i: (0, bi, i, 0)),
                   pl.BlockSpec((N_HEADS, 1, tq, LANES), lambda bi, i: (0, bi, i, 0))],
        out_shape=[jax.ShapeDtypeStruct((N_QBLK, b, t, LANES), BF16),
                   jax.ShapeDtypeStruct((N_HEADS, b, t, LANES), BF16)],
        compiler_params=_cparams("parallel", "parallel"),
        name="nsa_cmp_select",
    )(qkv, cmp_kv, cmp_kv, bias_c, ovt)


def _lane_chunks(s):
    width = min(LANES, s.shape[1])
    return [s[:, c:c + width] for c in range(0, s.shape[1], width)]


def _row_max(s):
    chunks = _lane_chunks(s)
    m = chunks[0]
    for c in chunks[1:]:
        m = jnp.maximum(m, c)
    return jnp.broadcast_to(jnp.max(m, axis=-1, keepdims=True), m.shape)


def _exp2_shifted(s, m):
    return jnp.concatenate([jnp.exp2(c - m).astype(BF16) for c in _lane_chunks(s)], axis=-1)


def _flash_start(s, v):
    m = _row_max(s)
    return m, _dot(_exp2_shifted(s, m), v)


def _flash_update(s, m, acc, v):
    m_new = jnp.maximum(m, _row_max(s))
    return m_new, jnp.exp2(m - m_new) * acc + _dot(_exp2_shifted(s, m_new), v)


def _normalise(acc):
    return acc / pltpu.roll(acc, HEAD_DIM, 1)


def _augment_keys_values(ka_scr, va_scr, k2, xk, v2, base=0):
    low = _lane_half_mask(0)
    ones = jnp.ones_like(v2)
    ka_scr[base] = jnp.where(low, k2, xk)
    ka_scr[base + 1] = jnp.where(low, xk, k2)
    va_scr[base] = jnp.where(low, v2, ones)
    va_scr[base + 1] = jnp.where(low, ones, v2)


def _store_all(scr, s):
    for n, sn in enumerate(s):
        scr[n] = sn


def _load_all(scr):
    return tuple(scr[n] for n in range(scr.shape[0]))


def _worklist_sweep(n_steps, logits, update, sa_scr, sb_scr):
    def single(n, _):
        _store_all(sa_scr, logits(n))
        update(n, _load_all(sa_scr))
        return 0

    odd = n_steps % 2
    lax.fori_loop(0, odd, single, 0)
    _store_all(sa_scr, logits(odd))

    def pair(jp, _):
        n = odd + 2 * jp
        _store_all(sb_scr, logits(n + 1))
        update(n, _load_all(sa_scr))
        _store_all(sa_scr, logits(jnp.minimum(n + 2, n_steps - 1)))
        update(n + 1, _load_all(sb_scr))
        return 0

    lax.fori_loop(0, n_steps // 2, pair, 0)


def _slc_kernel(nfar_ref, jt_ref, it_ref, qa_ref, k_ref, v_ref, hot_ref, nb_ref, o_ref,
                ka_scr, va_scr, m_scr, acc_scr, sa_scr, sb_scr, *, tq):
    gi = pl.program_id(1)
    qg = qa_ref.shape[2] // tq
    rows = HEADS_PER_GROUP * tq
    @pl.when(gi == 0)
    def _():
        _augment_keys_values(ka_scr, va_scr, k_ref[0, 0], hot_ref[...], v_ref[0, 0])

    heads = [slice(g * HEADS_PER_GROUP, (g + 1) * HEADS_PER_GROUP) for g in range(KV_GROUPS)]

    def tile(n):
        return pl.ds(pl.multiple_of(n * tq, tq), tq)

    def logits(j, il):
        return tuple(_dot_nt(qa_ref[heads[g], 0, tile(il), :].reshape(rows, LANES), ka_scr[g, tile(j), :])
                     for g in range(KV_GROUPS))

    def biased(s, which, g):
        return (s.reshape(HEADS_PER_GROUP, tq, tq) + nb_ref[which, heads[g]]).reshape(rows, tq)

    def diag_update(il, s):
        for g in range(KV_GROUPS):
            m_scr[il, g], acc_scr[il, g] = _flash_start(biased(s[g], 1, g), va_scr[g, tile(gi * qg + il), :])

    _worklist_sweep(qg, lambda il: logits(gi * qg + il, il), diag_update, sa_scr, sb_scr)

    def fold(il, g, sg, j):
        m_scr[il, g], acc_scr[il, g] = _flash_update(sg, m_scr[il, g], acc_scr[il, g], va_scr[g, tile(j), :])

    def prev_update(il, s):
        i = gi * qg + il
        pen = jnp.where(i >= 1, 0.0, NEG)
        for g in range(KV_GROUPS):
            fold(il, g, biased(s[g], 0, g) + pen, jnp.maximum(i - 1, 0))

    _worklist_sweep(qg, lambda il: logits(jnp.maximum(gi * qg + il - 1, 0), il), prev_update, sa_scr, sb_scr)

    def far_update(n, s):
        for g in range(KV_GROUPS):
            fold(it_ref[gi, n], g, s[g], jt_ref[gi, n])

    _worklist_sweep(nfar_ref[gi], lambda n: logits(jt_ref[gi, n], it_ref[gi, n]), far_update, sa_scr, sb_scr)

    low = _lane_half_mask(0)
    for il in range(qg):
        o_g0, o_g1 = _normalise(acc_scr[il, 0]), _normalise(acc_scr[il, 1])
        for p in range(N_QBLK):
            head_rows = slice(p * tq, (p + 1) * tq)
            o_ref[p, 0, il * tq:(il + 1) * tq, :] = jnp.where(low, o_g0[head_rows], o_g1[head_rows]).astype(o_ref.dtype)


def _slc_attention(qa, qkv, hot2, near_bias, qg=SLC_QGROUP):
    _, b, t, _ = qa.shape
    tq = near_bias.shape[2]
    nq = t // tq
    qg = min(qg, nq)
    n_groups = nq // qg
    lists = [[(j, il) for j in range(nq) for il in range(qg) if j <= g * qg + il - 2] for g in range(n_groups)]
    width = max(2, max(len(steps) for steps in lists))
    table = lambda k: jnp.asarray([[s[k] for s in steps] + [0] * (width - len(steps)) for steps in lists], jnp.int32)
    nfar = jnp.asarray([len(steps) for steps in lists], jnp.int32)
    smem = pl.BlockSpec(memory_space=pltpu.SMEM)
    rows = HEADS_PER_GROUP * tq
    return pl.pallas_call(
        functools.partial(_slc_kernel, tq=tq),
        grid=(b, n_groups),
        in_specs=[smem, smem, smem,
                  pl.BlockSpec((N_HEADS, 1, qg * tq, LANES), lambda bi, gi: (0, bi, gi, 0)),
                  pl.BlockSpec((1, 1, t, LANES), lambda bi, gi: (CB_AKS, bi, 0, 0)),
                  pl.BlockSpec((1, 1, t, LANES), lambda bi, gi: (CB_AVS, bi, 0, 0)),
                  pl.BlockSpec(hot2.shape, lambda bi, gi: (0, 0)),
                  pl.BlockSpec(near_bias.shape, lambda bi, gi: (0, 0, 0, 0))],
        out_specs=pl.BlockSpec((N_QBLK, 1, qg * tq, LANES), lambda bi, gi: (0, bi, gi, 0)),
        out_shape=jax.ShapeDtypeStruct((N_QBLK, b, t, LANES), BF16),
        scratch_shapes=[pltpu.VMEM((KV_GROUPS, t, LANES), BF16), pltpu.VMEM((KV_GROUPS, t, LANES), BF16),
                        pltpu.VMEM((qg, KV_GROUPS, rows, LANES), F32), pltpu.VMEM((qg, KV_GROUPS, rows, LANES), F32),
                        pltpu.VMEM((KV_GROUPS, rows, tq), F32), pltpu.VMEM((KV_GROUPS, rows, tq), F32)],
        compiler_params=_cparams("parallel", "arbitrary"),
        name="nsa_slc_attention",
    )(nfar, table(0), table(1), qa, qkv, qkv, hot2, near_bias)


def _banded_kernel(*refs, n_blk, use_sinks):
    q_ref = refs[0]
    k_refs = refs[1:1 + n_blk]
    v_refs = refs[1 + n_blk:1 + 2 * n_blk]
    bias_ref, o_ref = refs[1 + 2 * n_blk:]
    i = pl.program_id(1)
    tq = q_ref.shape[2]
    n_keys = n_blk * Q_BLOCK
    rows = HEADS_PER_GROUP * tq
    first_key = (i + 1) * tq - n_keys
    col = lax.broadcasted_iota(jnp.int32, (1, n_keys), 1)
    pad_pen = jnp.where(first_key + col >= 0, 0.0, NEG)
    kcat = jnp.concatenate([r[0, 0] for r in k_refs], axis=0)
    vcat = jnp.concatenate([r[0, 0] for r in v_refs], axis=0)
    if use_sinks:
        sink_row = lax.broadcasted_iota(jnp.int32, (n_keys, 1), 0) == 0
        kcat = jnp.where(sink_row, jnp.zeros_like(kcat), kcat)
        vcat = jnp.where(sink_row, jnp.zeros_like(vcat), vcat)
        pad_pen = jnp.where(col == 0, 0.0, pad_pen)
    outs = []
    for g in range(KV_GROUPS):
        own_half = _lane_half_mask(g)
        qop = jnp.concatenate([jnp.where(own_half, q_ref[p, 0], jnp.zeros_like(q_ref[p, 0]))
                               for p in range(N_QBLK)], axis=0)
        hs = slice(g * HEADS_PER_GROUP, (g + 1) * HEADS_PER_GROUP)
        s = _dot_nt(qop, kcat) + bias_ref[hs].reshape(rows, n_keys) + pad_pen
        e = _exp2_shifted(s, _row_max(s))
        acc = _dot(e, jnp.where(own_half, vcat, jnp.ones_like(vcat)))
        outs.append(_normalise(acc))
    _merge_group_outputs(o_ref, outs[0], outs[1])


def _banded_attention(qkv, cb_q, cb_k, cb_v, bias, use_sinks):
    _, b, t, _ = qkv.shape
    tq = bias.shape[1]
    n_blk = bias.shape[2] // Q_BLOCK
    per_tile = tq // Q_BLOCK

    def kv_spec(cb, back):
        return pl.BlockSpec((1, 1, Q_BLOCK, LANES),
                            lambda bi, i: (cb, bi, jnp.maximum((i + 1) * per_tile - 1 - back, 0), 0))

    backs = [n_blk - 1 - jb for jb in range(n_blk)]
    return pl.pallas_call(
        functools.partial(_banded_kernel, n_blk=n_blk, use_sinks=use_sinks),
        grid=(b, t // tq),
        in_specs=([pl.BlockSpec((N_QBLK, 1, tq, LANES), lambda bi, i: (cb_q // N_QBLK, bi, i, 0))]
                  + [kv_spec(cb_k, back) for back in backs] + [kv_spec(cb_v, back) for back in backs]
                  + [pl.BlockSpec(bias.shape, lambda bi, i: (0, 0, 0))]),
        out_specs=pl.BlockSpec((N_QBLK, 1, tq, LANES), lambda bi, i: (0, bi, i, 0)),
        out_shape=jax.ShapeDtypeStruct((N_QBLK, b, t, LANES), BF16),
        compiler_params=_cparams("parallel", "parallel"),
        name="banded_attention",
    )(qkv, *([qkv] * (2 * n_blk)), bias)


FOX_EXTRA = 6


def _fox_prep_kernel(g_ref, fb_ref, pq_ref, pk_ref, oq_ref, ok_ref, xq_ref, xk_ref, carry_scr):
    @pl.when(pl.program_id(1) == 0)
    def _():
        carry_scr[...] = jnp.zeros_like(carry_scr)

    x = g_ref[...] + fb_ref[...]
    logf = jnp.minimum(x, 0.0) - jnp.log(1.0 + jnp.exp(-jnp.abs(x)))
    ch = x.shape[0]
    lower = jnp.where(lax.broadcasted_iota(jnp.int32, (ch, ch), 0)
                      >= lax.broadcasted_iota(jnp.int32, (ch, ch), 1), 1.0, 0.0).astype(BF16)

    def split3(v):
        hi, rest = v.astype(BF16), v - v.astype(BF16).astype(F32)
        return (hi,) + _split_hi_lo(rest)

    terms = _dot(lower, jnp.concatenate(split3(logf), axis=-1))
    cs = sum(_lane_chunks(terms)) + carry_scr[0:1, :]
    carry_scr[0:1, :] = cs[ch - 1:ch, :]
    parts = jnp.concatenate(split3(cs * LOG2E), axis=-1)
    xq = _dot(parts, pq_ref[...]) + oq_ref[...]
    xk = _dot(parts, pk_ref[...]) + ok_ref[...]
    for p in range(N_QBLK):
        xq_ref[p, 0] = xq[:, p * LANES:(p + 1) * LANES].astype(xq_ref.dtype)
        xk_ref[p, 0] = xk[:, p * LANES:(p + 1) * LANES].astype(xk_ref.dtype)


def _fox_prep(graw, fb_row, b, t, chunk=FOX_PREP_CHUNK):
    chunk = min(chunk, t)
    nch = t // chunk
    pq = np.zeros((3 * LANES, N_QBLK * LANES), np.float32)
    pk = np.zeros((3 * LANES, N_QBLK * LANES), np.float32)
    oq = np.zeros((1, N_QBLK * LANES), np.float32)
    ok = np.zeros((1, N_QBLK * LANES), np.float32)
    for h in range(N_HEADS):
        col = (h // 2) * LANES + (HEAD_DIM if h % 2 == 0 else 0)
        for n in range(3):
            pq[n * LANES + N_GATE_COLS + h, col + n] = 1.0
            pk[n * LANES + N_GATE_COLS + h, col + 3 + n] = -1.0
        oq[0, col + 3:col + FOX_EXTRA] = 1.0
        ok[0, col:col + 3] = 1.0
    const = lambda a: pl.BlockSpec(a.shape, lambda bi, c: (0,) * a.ndim)
    out_spec = pl.BlockSpec((N_QBLK, 1, chunk, LANES), lambda bi, c: (0, bi, c, 0))
    return pl.pallas_call(
        _fox_prep_kernel,
        grid=(b, nch),
        in_specs=[pl.BlockSpec((chunk, LANES), lambda bi, c: (bi * nch + c, 0)),
                  const(fb_row), const(pq), const(pk), const(oq), const(ok)],
        out_specs=[out_spec, out_spec],
        out_shape=[jax.ShapeDtypeStruct((N_QBLK, b, t, LANES), BF16)] * 2,
        scratch_shapes=[pltpu.VMEM((8, LANES), F32)],
        compiler_params=_cparams("parallel", "arbitrary"),
        name="fox_prep",
    )(graw, fb_row, jnp.asarray(pq, BF16), jnp.asarray(pk, BF16), jnp.asarray(oq), jnp.asarray(ok))


def _fox_kernel(jt_ref, it_ref, q_ref, xq_ref, k_ref, xk_ref, v_ref, o_ref,
                qa_scr, ka_scr, va_scr, m_scr, acc_scr, sa_scr, sb_scr, *, tq, n_off):
    nq = q_ref.shape[2] // tq
    low = _lane_half_mask(0)
    _augment_keys_values(ka_scr, va_scr, k_ref[0, 0], xk_ref[0, 0], v_ref[0, 0])
    q2, xq = q_ref[0, 0], xq_ref[0, 0]
    qa_scr[0] = jnp.where(low, q2, xq)
    qa_scr[1] = jnp.where(low, xq, q2)
    causal_pen = jnp.where(lax.broadcasted_iota(jnp.int32, (tq, tq), 0)
                           >= lax.broadcasted_iota(jnp.int32, (tq, tq), 1), 0.0, NEG)

    def tile(n):
        return pl.ds(pl.multiple_of(n * tq, tq), tq)

    def logits(j, i):
        return tuple(_dot_nt(qa_scr[h, tile(i), :], ka_scr[h, tile(j), :]) for h in range(2))

    def diag_update(i, s):
        for h in range(2):
            m_scr[i, h], acc_scr[i, h] = _flash_start(s[h] + causal_pen, va_scr[h, tile(i), :])

    _worklist_sweep(nq, lambda n: logits(n, n), diag_update, sa_scr, sb_scr)

    def off_update(n, s):
        j, i = jt_ref[n], it_ref[n]
        for h in range(2):
            m_scr[i, h], acc_scr[i, h] = _flash_update(s[h], m_scr[i, h], acc_scr[i, h], va_scr[h, tile(j), :])

    _worklist_sweep(n_off, lambda n: logits(jt_ref[n], it_ref[n]), off_update, sa_scr, sb_scr)

    def finish(i, _):
        o_ref[0, 0, tile(i), :] = jnp.where(low, _normalise(acc_scr[i, 0]), _normalise(acc_scr[i, 1])).astype(o_ref.dtype)
        return 0

    lax.fori_loop(0, nq, finish, 0)


def _fox_attention(qkv, xq, xk, tq):
    _, b, t, _ = qkv.shape
    nq = t // tq
    pairs = [(j, i) for j in range(nq) for i in range(j + 1, nq)]
    n_off = len(pairs)
    pairs += [(0, 0)] * max(0, 2 - n_off)
    jt = jnp.asarray([p[0] for p in pairs], jnp.int32)
    it = jnp.asarray([p[1] for p in pairs], jnp.int32)
    smem = pl.BlockSpec(memory_space=pltpu.SMEM)
    seq = lambda cb: pl.BlockSpec((1, 1, t, LANES), lambda bi, p: (cb + p, bi, 0, 0))
    return pl.pallas_call(
        functools.partial(_fox_kernel, tq=tq, n_off=n_off),
        grid=(b, N_QBLK),
        in_specs=[smem, smem, seq(CB_CQ), seq(0), seq(CB_CK), seq(0), seq(CB_CV)],
        out_specs=pl.BlockSpec((1, 1, t, LANES), lambda bi, p: (p, bi, 0, 0)),
        out_shape=jax.ShapeDtypeStruct((N_QBLK, b, t, LANES), BF16),
        scratch_shapes=[pltpu.VMEM((2, t, LANES), BF16), pltpu.VMEM((2, t, LANES), BF16),
                        pltpu.VMEM((2, t, LANES), BF16),
                        pltpu.VMEM((nq, 2, tq, LANES), F32), pltpu.VMEM((nq, 2, tq, LANES), F32),
                        pltpu.VMEM((2, tq, tq), F32), pltpu.VMEM((2, tq, tq), F32)],
        compiler_params=_cparams("parallel", "parallel"),
        name="fox_attention",
    )(jt, it, qkv, xq, qkv, xk, qkv)


def _merge_kernel(ocmp_ref, oslc_ref, owin_ref, ob_ref, oc_ref, gs_ref, eg_ref, mg_ref, wb_ref, wo_ref, x_ref, o_ref):
    def cat(ref):
        return jnp.concatenate([ref[p] for p in range(N_QBLK)], axis=-1)

    ga = _dot(gs_ref[...], eg_ref[...])
    oa = sum(ga[:, n * MIX_WIDTH:(n + 1) * MIX_WIDTH] * cat(ref).astype(F32)
             for n, ref in enumerate((ocmp_ref, oslc_ref, owin_ref))).astype(BF16)
    branches = (oa, cat(ob_ref), cat(oc_ref))
    d = o_ref.shape[1]
    chunks = [slice(c, c + MM_CHUNK) for c in range(0, d, MM_CHUNK)]
    mix = jnp.concatenate(
        [sum(mg_ref[n, :, cols].astype(F32) * _dot(branches[n], wb_ref[n, :, cols]) for n in range(N_BRANCH))
         .astype(BF16) for cols in chunks], axis=-1)
    for cols in chunks:
        o_ref[:, cols] = x_ref[:, cols] + _dot(mix, wo_ref[:, cols])


def _merge(ocmp, oslc, owin, ob, oc, gsig, eg, mg, wb, wo, x2, tm=BRANCH_MERGE_ROW_TILE):
    bt, d = x2.shape
    o_spec = pl.BlockSpec((N_QBLK, tm, LANES), lambda i: (0, i, 0))
    return pl.pallas_call(
        _merge_kernel,
        grid=(bt // tm,),
        in_specs=[o_spec] * 5 + [pl.BlockSpec((tm, LANES), lambda i: (i, 0)),
                                 pl.BlockSpec(eg.shape, lambda i: (0, 0)),
                                 pl.BlockSpec((N_BRANCH, tm, d), lambda i: (0, i, 0)),
                                 pl.BlockSpec(wb.shape, lambda i: (0, 0, 0)),
                                 pl.BlockSpec(wo.shape, lambda i: (0, 0)),
                                 pl.BlockSpec((tm, d), lambda i: (i, 0))],
        out_specs=pl.BlockSpec((tm, d), lambda i: (i, 0)),
        out_shape=jax.ShapeDtypeStruct((bt, d), F32),
        compiler_params=_cparams("parallel"),
        name="branch_merge",
    )(ocmp, oslc, owin, ob, oc, gsig, eg, mg, wb, wo, x2)


def _ffn_kernel(x_ref, xh_ref, g_ref, gn_ref, wg_ref, wu_ref, cw_ref, cb_ref, wd_ref, o_ref, *rest,
                tm, tiles_per_seq, emit_next):
    hn_ref = rest[0] if emit_next else None
    h_scr, hh_scr, a_scr, acc_scr = rest[-4:]
    i = pl.program_id(0)
    f = pl.program_id(1)

    def norm(x, gain_ref):
        ms = jnp.mean(x * x, axis=-1, keepdims=True)
        return ((x * lax.rsqrt(ms + RMS_EPS)) * gain_ref[...]).astype(BF16)

    @pl.when(f == 0)
    def _():
        h_scr[...] = norm(x_ref[...], g_ref)
        hh_scr[...] = norm(xh_ref[...], g_ref)
        acc_scr[...] = jnp.zeros_like(acc_scr)

    a_scr[0:8, :] = _dot(hh_scr[...], wg_ref[...]) * jnp.where(i % tiles_per_seq == 0, 0.0, 1.0)
    acts = []
    for c in range(wg_ref.shape[1] // MM_CHUNK):
        cols = slice(c * MM_CHUNK, (c + 1) * MM_CHUNK)
        a = _dot(h_scr[...], wg_ref[:, cols])
        a_scr[8:8 + tm, cols] = a
        conv = (cw_ref[0:1, cols] * a_scr[6:6 + tm, cols] + cw_ref[1:2, cols] * a_scr[7:7 + tm, cols]
                + cw_ref[2:3, cols] * a + cb_ref[:, cols])
        acts.append((_gelu_tanh(conv) * _dot(h_scr[...], wu_ref[:, cols])).astype(BF16))
    acc_scr[...] += _dot(jnp.concatenate(acts, axis=-1), wd_ref[...])

    @pl.when(f == pl.num_programs(1) - 1)
    def _():
        out = x_ref[...] + acc_scr[...]
        o_ref[...] = out
        if emit_next:
            hn_ref[...] = norm(out, gn_ref)


def _ffn(x2, g, g_next, wg, wu, cw, cb, wd, t, tm=ROW_TILE, tf=FFN_TF):
    bt, d = x2.shape
    emit_next = g_next is not None
    g_next = g_next if emit_next else g
    pad = -wg.shape[1] % tf
    wg, wu, cw = [jnp.pad(a, ((0, 0), (0, pad))) for a in (wg, wu, cw)]
    cb, wd = jnp.pad(cb, (0, pad)), jnp.pad(wd, ((0, pad), (0, 0)))
    ff = wg.shape[1]
    halo_blocks = tm // 8
    return pl.pallas_call(
        functools.partial(_ffn_kernel, tm=tm, tiles_per_seq=t // tm, emit_next=emit_next),
        grid=(bt // tm, ff // tf),
        in_specs=[pl.BlockSpec((tm, d), lambda i, f: (i, 0)),
                  pl.BlockSpec((8, d), lambda i, f: (jnp.maximum(i * halo_blocks - 1, 0), 0)),
                  pl.BlockSpec((1, d), lambda i, f: (0, 0)),
                  pl.BlockSpec((1, d), lambda i, f: (0, 0)),
                  pl.BlockSpec((d, tf), lambda i, f: (0, f)),
                  pl.BlockSpec((d, tf), lambda i, f: (0, f)),
                  pl.BlockSpec((CONV_WIDTH, tf), lambda i, f: (0, f)),
                  pl.BlockSpec((1, tf), lambda i, f: (0, f)),
                  pl.BlockSpec((tf, d), lambda i, f: (f, 0))],
        out_specs=[pl.BlockSpec((tm, d), lambda i, f: (i, 0))] * (1 + emit_next),
        out_shape=[jax.ShapeDtypeStruct((bt, d), F32), jax.ShapeDtypeStruct((bt, d), BF16)][:1 + emit_next],
        scratch_shapes=[pltpu.VMEM((tm, d), BF16), pltpu.VMEM((8, d), BF16),
                        pltpu.VMEM((tm + 8, tf), F32), pltpu.VMEM((tm, d), F32)],
        compiler_params=_cparams("parallel", "arbitrary"),
        name="conv_ffn",
    )(x2, x2, g.reshape(1, d), g_next.reshape(1, d), wg, wu, cw, cb.reshape(1, ff), wd)


def _t5_bucket_np(dist):
    max_exact = N_BUCKETS // 2
    d = np.maximum(dist, 0)
    ratio = np.log(np.maximum(d, 1) / max_exact) / math.log(MAX_DISTANCE / max_exact)
    large = np.minimum(max_exact + (ratio * (N_BUCKETS - max_exact)).astype(np.int64), N_BUCKETS - 1)
    return np.where(d < max_exact, d, large)


def _expand_kernel(tab_ref, idx_ref, o_ref):
    tab = tab_ref[...]
    onehot = jnp.where(lax.broadcasted_iota(jnp.int32, (tab.shape[1], idx_ref.shape[1]), 0) == idx_ref[...],
                       1.0, 0.0).astype(BF16)
    hi, rest = tab.astype(BF16), tab - tab.astype(BF16).astype(F32)
    mid, lo = _split_hi_lo(rest)
    heads = tab.shape[0]
    parts = _dot(jnp.concatenate([hi, mid, lo], axis=0), onehot)
    o_ref[...] = parts[0:heads] + parts[heads:2 * heads] + parts[2 * heads:3 * heads]


def _bias_from_dist(tab_t, dist, valid, chunk=EXPAND_CHUNK):
    heads = tab_t.shape[0]
    tab = jnp.zeros((heads, LANES), F32).at[:, :N_BUCKETS].set(tab_t).at[:, N_BUCKETS].set(NEG)
    idx = np.where(valid, _t5_bucket_np(dist), N_BUCKETS).reshape(1, -1).astype(np.int32)
    n = idx.shape[1]
    chunk = min(chunk, n)
    out = pl.pallas_call(
        _expand_kernel,
        grid=(n // chunk,),
        in_specs=[pl.BlockSpec((heads, LANES), lambda i: (0, 0)), pl.BlockSpec((1, chunk), lambda i: (0, i))],
        out_specs=pl.BlockSpec((heads, chunk), lambda i: (0, i)),
        out_shape=jax.ShapeDtypeStruct((heads, n), F32),
        compiler_params=_cparams("parallel"),
        name="rel_bias_expand",
    )(tab, jnp.asarray(idx))
    return out.reshape((heads,) + dist.shape)


def _cmp_bias_kernel(g_ref, o_ref):
    ncp = o_ref.shape[2]
    per_step = o_ref.shape[1] // Q_BLOCK
    for sub in range(per_step):
        shift = (pl.program_id(0) * per_step + sub) * (Q_BLOCK // CMP_STRIDE)
        for h in range(o_ref.shape[0]):
            o_ref[h, sub * Q_BLOCK:(sub + 1) * Q_BLOCK, :] = pltpu.roll(g_ref[h], shift, 1)[:, :ncp]


def _cmp_bias(g_tab, t):
    heads, _, width = g_tab.shape
    ncp = t // CMP_STRIDE
    return pl.pallas_call(
        _cmp_bias_kernel,
        grid=(t // CMP_TQ,),
        in_specs=[pl.BlockSpec(g_tab.shape, lambda i: (0, 0, 0))],
        out_specs=pl.BlockSpec((heads, CMP_TQ, ncp), lambda i: (0, i, 0)),
        out_shape=jax.ShapeDtypeStruct((heads, t, ncp), F32),
        compiler_params=_cparams("parallel"),
        name="cmp_bias_build",
    )(g_tab)


def _static_tables(t):
    ncp = t // CMP_STRIDE
    n_cmp = (t - CMP_LEN) // CMP_STRIDE + 1
    n_slc = t // SLC_LEN
    lane = np.arange(MM_CHUNK)
    bd = (lane[:, None] // HEAD_DIM == lane[None, :] // HEAD_DIM).astype(np.float32)
    eg = np.zeros((LANES, N_BRANCH * MIX_WIDTH), np.float32)
    for h in range(N_HEADS):
        base = (h % N_QBLK) * LANES + (h // N_QBLK) * HEAD_DIM
        for n in range(N_BRANCH):
            eg[h * N_BRANCH + n, n * MIX_WIDTH + base:n * MIX_WIDTH + base + HEAD_DIM] = 1.0
    c_start = np.arange(n_cmp) * CMP_STRIDE
    s_start = np.arange(n_slc) * SLC_LEN
    overlap = ((c_start[:, None] < s_start[None, :] + SLC_LEN) & (c_start[:, None] + CMP_LEN > s_start[None, :]))
    ovt = np.zeros((LANES, ncp), np.float32)
    ovt[:n_slc, :n_cmp] = overlap.T
    key_block = np.zeros((t, LANES), np.float32)
    key_block[np.arange(t), np.arange(t) // SLC_LEN] = 1.0
    key_block[np.arange(t), HEAD_DIM + np.arange(t) // SLC_LEN] = 1.0
    as_bf16 = lambda a: jnp.asarray(a, BF16)
    return dict(bd=as_bf16(bd), eg=as_bf16(eg), ovt=as_bf16(ovt), key_block=as_bf16(key_block), n_slc=n_slc)


def _bias_tables(rel_bias, t):
    n_cmp = (t - CMP_LEN) // CMP_STRIDE + 1
    ncp = t // CMP_STRIDE
    tab_a = rel_bias[:, :N_HEADS].T * LOG2E
    tab_b = rel_bias[:, N_HEADS:].T * LOG2E
    r = np.arange(Q_BLOCK)[:, None]

    def band(window):
        n_prev = -(-(window - 1) // Q_BLOCK)
        n_keys = n_prev * Q_BLOCK + BAND_TQ
        dist = (n_keys - BAND_TQ) + np.arange(BAND_TQ)[:, None] - np.arange(n_keys)[None, :]
        valid = (dist >= 0) & (dist < window)
        assert not valid[:, 0].any()
        return dist, valid

    m = np.arange(2 * ncp)
    m = np.where(m < ncp, m, m - 2 * ncp)[None, :]
    dist_c = r - (m * CMP_STRIDE + CMP_LEN - 1)
    assert n_cmp == ncp - 1 and (dist_c[:, m[0] == Q_BLOCK // CMP_STRIDE - 1] < 0).all()
    tile = np.arange(SLC_TILE)
    dist_n = np.stack([SLC_TILE * (1 - which) + tile[:, None] - tile[None, :] for which in range(2)])
    near = _bias_from_dist(tab_a - tab_a[:, N_BUCKETS - 1:], dist_n, dist_n >= 0)
    return dict(win=_bias_from_dist(tab_a, *band(NSA_WINDOW)),
                swa=_bias_from_dist(tab_b, *band(SWA_WINDOW)),
                cmp=_cmp_bias(_bias_from_dist(tab_a, dist_c, dist_c >= 0), t),
                near=jnp.transpose(near, (1, 0, 2, 3)))


def _pair_cols(w):
    parts = []
    for p in range(N_QBLK):
        parts += [w[..., p * HEAD_DIM:(p + 1) * HEAD_DIM],
                  w[..., (N_QBLK + p) * HEAD_DIM:(N_QBLK + p + 1) * HEAD_DIM]]
    return jnp.concatenate(parts, axis=-1)


def _prep_layer(w_in, qk_gain, w_branch):
    kv = KV_GROUPS * HEAD_DIM
    widths = [('a_q', MIX_WIDTH), ('a_kc', kv), ('a_vc', kv), ('a_ks', kv), ('a_vs', kv), ('a_kw', kv), ('a_vw', kv),
              ('a_gate', N_GATE_COLS), ('b_q', MIX_WIDTH), ('b_k', kv), ('b_v', kv),
              ('c_q', MIX_WIDTH), ('c_k', MIX_WIDTH), ('c_v', MIX_WIDTH), ('c_f', N_HEADS),
              ('merge', N_BRANCH * w_in.shape[0])]
    cols, off = {}, 0
    for name, w in widths:
        cols[name] = w_in[:, off:off + w]
        off += w
    scale = HEAD_DIM ** -0.5 * LOG2E
    tile = lambda gvec, n: jnp.tile(gvec, n)
    zeros = lambda n: jnp.zeros((n,), F32)
    ones = lambda n: jnp.ones((n,), F32)
    pieces = [(_pair_cols(cols['a_q']), tile(qk_gain[0] * scale, N_HEADS), ones(MIX_WIDTH)),
              (_pair_cols(cols['b_q']), tile(qk_gain[2] * scale, N_HEADS), ones(MIX_WIDTH)),
              (cols['c_q'], tile(qk_gain[4] * scale, N_HEADS), ones(MIX_WIDTH)),
              (cols['c_k'], tile(qk_gain[5], N_HEADS), ones(MIX_WIDTH)),
              (cols['a_ks'], tile(qk_gain[1], KV_GROUPS), ones(kv)),
              (cols['a_kw'], tile(qk_gain[1], KV_GROUPS), ones(kv)),
              (cols['b_k'], tile(qk_gain[3], KV_GROUPS), ones(kv)),
              (cols['a_vs'], zeros(kv), zeros(kv)),
              (cols['c_v'], zeros(MIX_WIDTH), zeros(MIX_WIDTH)),
              (cols['a_kc'], zeros(kv), zeros(kv)), (cols['a_vc'], zeros(kv), zeros(kv)),
              (cols['a_vw'], zeros(kv), zeros(kv)), (cols['b_v'], zeros(kv), zeros(kv))]
    w_qkv = jnp.concatenate([p[0] for p in pieces], axis=1).astype(BF16)
    n_qkv = w_qkv.shape[1]
    aux = jnp.zeros((8, n_qkv), F32)
    aux = aux.at[0].set(jnp.concatenate([p[1] for p in pieces])).at[1].set(jnp.concatenate([p[2] for p in pieces]))
    d = w_in.shape[0]
    w_gates = jnp.concatenate([cols['a_gate'], cols['c_f'],
                               jnp.zeros((d, LANES - N_GATE_COLS - N_HEADS), F32)], axis=1).astype(BF16)
    wb = jnp.stack([_pair_cols(w_branch[0].T).T, _pair_cols(w_branch[1].T).T, w_branch[2]]).astype(BF16)
    return w_qkv, aux, w_gates, cols['merge'].astype(BF16), wb


def _prep_compress(cmp_pos, cmp_w1, cmp_w2, k_gain):
    def both_groups(w):
        z = jnp.zeros_like(w)
        return jnp.concatenate([jnp.concatenate([w, z], -1), jnp.concatenate([z, w], -1)], -2)

    pos = jnp.concatenate([cmp_pos, cmp_pos], -1)
    w1 = both_groups(cmp_w1.reshape(2, CMP_LEN, HEAD_DIM, CMP_HIDDEN)).astype(BF16)
    w2 = both_groups(cmp_w2).astype(BF16)
    aux = jnp.zeros((2, 8, LANES), F32)
    aux = aux.at[0, 0].set(jnp.tile(k_gain, KV_GROUPS)).at[0, 1].set(1.0)
    return pos, w1, w2, aux


def kernel(x, rel_bias, norm_mix, norm_ffn, w_in, forget_bias, qk_gain, cmp_pos, cmp_w1, cmp_w2, sinks, w_branch, w_out, w_gate, w_up, conv_w, conv_b, w_down):
    b, t, d = x.shape
    bt = b * t
    depth = w_in.shape[0]
    tabs = _static_tables(t)
    biases = _bias_tables(rel_bias, t)
    fox_tq = min(FOX_TQ, t)
    x2 = x.reshape(bt, d)
    h = _rmsnorm(x2, norm_mix[0])
    for l in range(depth):
        w_qkv, aux, w_gates, w_merge, wb = _prep_layer(w_in[l], qk_gain[l], w_branch[l])
        qkv = _inproj_qkv(h, w_qkv, aux, tabs['bd']).reshape(N_CB, b, t, LANES)
        graw, gsig = _inproj_gates(h, w_gates)
        mg = _inproj_merge(h, w_merge, d)

        cmp_kv = _compress(qkv, *_prep_compress(cmp_pos[l], cmp_w1[l], cmp_w2[l], qk_gain[l, 1]), tabs['bd'])
        o_cmp, qa = _cmp_select(qkv, cmp_kv, biases['cmp'], tabs['ovt'], tabs['n_slc'])
        o_slc = _slc_attention(qa, qkv, tabs['key_block'], biases['near'])
        o_win = _banded_attention(qkv, CB_AQ, CB_AKW, CB_AVW, biases['win'], False)

        swa_bias = biases['swa'].at[:, :, 0].set(sinks[l][:, None] * LOG2E)
        o_b = _banded_attention(qkv, CB_BQ, CB_BK, CB_BV, swa_bias, True)

        fb_row = jnp.zeros((1, LANES), F32).at[0, N_GATE_COLS:N_GATE_COLS + N_HEADS].set(forget_bias[l])
        o_c = _fox_attention(qkv, *_fox_prep(graw, fb_row, b, t), fox_tq)

        flat = lambda o: o.reshape(N_QBLK, bt, LANES)
        x2 = _merge(flat(o_cmp), flat(o_slc), flat(o_win), flat(o_b), flat(o_c), gsig, tabs['eg'], mg, wb,
                    w_out[l].astype(BF16), x2)
        x2, *h = _ffn(x2, norm_ffn[l], norm_mix[l + 1] if l + 1 < depth else None, w_gate[l].astype(BF16),
                      w_up[l].astype(BF16), conv_w[l], conv_b[l], w_down[l].astype(BF16), t)
        h = h[0] if h else None
    return x2.reshape(b, t, d)
```
